```python
import math
import jax, jax.numpy as jnp
from jax import lax
import numpy as np

D_MODEL = 1024
BATCH = 8
SEQ = 2048
DEPTH = 4

D_PLE = 256
N_A_LAYERS = DEPTH // 2
N_B_LAYERS = DEPTH - N_A_LAYERS
N_DENSE = (DEPTH + 1) // 2
N_MOE = DEPTH // 2
EPS = 1e-6

D_RNN = D_MODEL
CONV_WIDTH = 4
LRU_BLOCKS = 16
LRU_BLOCK_W = D_RNN // LRU_BLOCKS
LRU_C = 8.0

N_HEADS = 16
N_KV_GROUPS = 2
GROUP = N_HEADS // N_KV_GROUPS
HEAD_DIM = 64
N_KV_BRANCH = 6
CMP_BLOCK = 32
CMP_STRIDE = 16
CMP_HIDDEN = 256
SLC_BLOCK = 64
N_SELECT = 8
WINDOW = 512
Q_BLOCK = 128

N_BUCKETS = 32
MAX_DISTANCE = 128

D_FF = 3072
N_EXPERTS = 8
TOP_K = 2
D_FF_EXPERT = 3584

kernel_name = 'hybrid_rglru_nsa_moe_trunk'


def rmsnorm(x, g):
    xf = x.astype(jnp.float32)
    y = xf * lax.rsqrt(jnp.mean(xf * xf, axis=-1, keepdims=True) + EPS)
    return (y * g.astype(jnp.float32)).astype(x.dtype)


def masked_softmax(logits, mask):
    l = jnp.where(mask, logits.astype(jnp.float32), -1e30)
    m = jnp.max(l, axis=-1, keepdims=True)
    e = jnp.where(mask, jnp.exp(l - m), 0.0)
    return e / jnp.maximum(jnp.sum(e, axis=-1, keepdims=True), 1e-30)


def rel_bucket(dist):
    n = jnp.maximum(dist, 0)
    max_exact = N_BUCKETS // 2
    nf = jnp.maximum(n, 1).astype(jnp.float32)
    large = max_exact + (jnp.log(nf / max_exact) / math.log(MAX_DISTANCE / max_exact) * (N_BUCKETS - max_exact)).astype(jnp.int32)
    return jnp.where(n < max_exact, n, jnp.minimum(large, N_BUCKETS - 1))


def causal_dwconv(x, w, b):
    S = x.shape[1]
    xp = jnp.pad(x, ((0, 0), (CONV_WIDTH - 1, 0), (0, 0)))
    return sum(xp[:, k:k + S] * w[k] for k in range(CONV_WIDTH)) + b


def block_diag_linear(x, w, b):
    xb = x.reshape(x.shape[:-1] + (LRU_BLOCKS, LRU_BLOCK_W))
    return jnp.einsum('bsnc,ncd->bsnd', xb, w).reshape(x.shape) + b


def rg_lru(x, w_a, b_a, w_x, b_x, lam):
    f32 = jnp.float32
    r = jax.nn.sigmoid(block_diag_linear(x, w_a, b_a)).astype(f32)
    i = jax.nn.sigmoid(block_diag_linear(x, w_x, b_x)).astype(f32)
    log_a = -LRU_C * r * jax.nn.softplus(-lam.astype(f32))
    a = jnp.exp(log_a)
    u = jnp.sqrt(-jnp.expm1(2.0 * log_a)) * (i * x.astype(f32))

    def combine(c1, c2):
        a1, b1 = c1
        a2, b2 = c2
        return a1 * a2, a2 * b1 + b2

    _, h = lax.associative_scan(combine, (a, u), axis=1)
    return h.astype(x.dtype)


def recurrent_block(xn, w_in, conv_w, conv_b, w_a, b_a, w_x, b_x, lam, w_out):
    gate, xr = jnp.split(xn @ w_in, 2, axis=-1)
    xr = causal_dwconv(xr, conv_w, conv_b)
    h = rg_lru(xr, w_a, b_a, w_x, b_x, lam)
    return (jax.nn.gelu(gate) * h) @ w_out


def swiglu(x, w_gu, w_down):
    g, u = jnp.split(x @ w_gu, 2, axis=-1)
    return (jax.nn.silu(g) * u) @ w_down


def moe_ffn(x, w_router, w_gu, w_down):
    logits = (x @ w_router).astype(jnp.float32)
    top_val, top_idx = lax.top_k(logits, TOP_K)
    top_w = jax.nn.softmax(top_val, axis=-1)
    combine = jnp.sum(jax.nn.one_hot(top_idx, N_EXPERTS, dtype=jnp.float32) * top_w[..., None], axis=-2)
    combine = combine.astype(x.dtype)
    y = jnp.zeros_like(x)
    for e in range(N_EXPERTS):
        y = y + combine[..., e:e + 1] * swiglu(x, w_gu[e], w_down[e])
    return y


def nsa_shared_kv(x, kv_norm, w_kv, cmp_pos_k, cmp_pos_v, cmp_k_w1, cmp_k_w2, cmp_v_w1, cmp_v_w2):
    B, S, _ = x.shape
    kv = (rmsnorm(x, kv_norm) @ w_kv).reshape(B, S, N_KV_BRANCH, N_KV_GROUPS, HEAD_DIM)
    k_ct, v_ct, k_slc, v_slc, k_win, v_win = [kv[:, :, j] for j in range(N_KV_BRANCH)]
    n_cmp = (S - CMP_BLOCK) // CMP_STRIDE + 1
    idx = jnp.arange(n_cmp)[:, None] * CMP_STRIDE + jnp.arange(CMP_BLOCK)[None, :]

    def compress(t, pos_emb, w1, w2):
        blk = t[:, idx] + pos_emb[:, None, :]
        blk = blk.transpose(0, 1, 3, 2, 4).reshape(B, n_cmp, N_KV_GROUPS, CMP_BLOCK * HEAD_DIM)
        return jax.nn.gelu(blk @ w1) @ w2

    k_cmp = compress(k_ct, cmp_pos_k, cmp_k_w1, cmp_k_w2)
    v_cmp = compress(v_ct, cmp_pos_v, cmp_v_w1, cmp_v_w2)
    return (k_cmp, v_cmp, k_slc, v_slc, k_win, v_win)


def nsa_attention(xn, w_q, w_o, rel_bias, k_cmp, v_cmp, k_slc, v_slc, k_win, v_win):
    B, S, _ = xn.shape
    G, R, Dh = N_KV_GROUPS, GROUP, HEAD_DIM
    dt = xn.dtype
    proj = xn @ w_q
    q = proj[..., :N_HEADS * Dh].reshape(B, S, G, R, Dh) * (Dh ** -0.5)
    gates = jax.nn.sigmoid(proj[..., N_HEADS * Dh:].reshape(B, S, 3, G, R))
    pos = jnp.arange(S, dtype=jnp.int32)
    tbl = rel_bias.reshape(N_BUCKETS, G, R)

    n_cmp = k_cmp.shape[1]
    cmp_start = jnp.arange(n_cmp, dtype=jnp.int32) * CMP_STRIDE
    dist_c = pos[:, None] - (cmp_start + CMP_BLOCK - 1)[None, :]
    bias_c = tbl[rel_bucket(dist_c)].transpose(2, 3, 0, 1)
    s_c = jnp.einsum('bsgrd,bcgd->bgrsc', q, k_cmp) + bias_c
    p_c = masked_softmax(s_c, dist_c >= 0)
    o_cmp = jnp.einsum('bgrsc,bcgd->bsgrd', p_c.astype(dt), v_cmp)

    n_slc = S // SLC_BLOCK
    n_sel = min(N_SELECT, n_slc)
    slc_start = jnp.arange(n_slc, dtype=jnp.int32) * SLC_BLOCK
    overlap = ((cmp_start[:, None] < slc_start[None, :] + SLC_BLOCK) &
               (cmp_start[:, None] + CMP_BLOCK > slc_start[None, :])).astype(jnp.float32)
    imp = jnp.einsum('bgrsc,cj->bsgj', p_c, overlap)
    q_blk = (pos // SLC_BLOCK)[:, None]
    j = jnp.arange(n_slc, dtype=jnp.int32)[None, :]
    valid = (j <= q_blk)[:, None, :]
    forced = ((j == 0) | (j == q_blk) | (j == q_blk - 1))[:, None, :]
    score = jnp.where(forced, 1e9, jnp.where(valid, imp, -1e9))
    _, sel = lax.top_k(score, n_sel)

    n_qb = S // Q_BLOCK
    kb = k_slc.reshape(B, n_slc, SLC_BLOCK, G, Dh).transpose(0, 3, 1, 2, 4)
    vb = v_slc.reshape(B, n_slc, SLC_BLOCK, G, Dh).transpose(0, 3, 1, 2, 4)
    kw_pad = jnp.pad(k_win, ((0, 0), (WINDOW, 0), (0, 0), (0, 0)))
    vw_pad = jnp.pad(v_win, ((0, 0), (WINDOW, 0), (0, 0), (0, 0)))
    bi = jnp.arange(B)[:, None, None, None]
    gi = jnp.arange(G)[None, None, :, None]
    gi5 = gi[..., None]
    offs = jnp.arange(SLC_BLOCK, dtype=jnp.int32)
    band = jnp.arange(WINDOW + Q_BLOCK, dtype=jnp.int32)
    n_sel_tok = n_sel * SLC_BLOCK

    def block_fn(args):
        qb, selb, t0 = args
        tq = t0 + jnp.arange(Q_BLOCK, dtype=jnp.int32)
        ks = kb[bi, gi, selb]
        vs = vb[bi, gi, selb]
        dist_s = tq[None, :, None, None, None] - (selb[..., None] * SLC_BLOCK + offs)
        bias_s = tbl[rel_bucket(dist_s), gi5].transpose(0, 1, 2, 5, 3, 4)
        s_s = jnp.einsum('bqgrd,bqgnkd->bqgrnk', qb, ks) + bias_s
        s_s = s_s.reshape(B, Q_BLOCK, G, R, n_sel_tok)
        mask_s = (dist_s >= 0).reshape(B, Q_BLOCK, G, 1, n_sel_tok)
        p_s = masked_softmax(s_s, mask_s)
        o_s = jnp.einsum('bqgrk,bqgkd->bqgrd', p_s.astype(dt), vs.reshape(B, Q_BLOCK, G, n_sel_tok, Dh))

        kw = lax.dynamic_slice_in_dim(kw_pad, t0, WINDOW + Q_BLOCK, axis=1)
        vw = lax.dynamic_slice_in_dim(vw_pad, t0, WINDOW + Q_BLOCK, axis=1)
        kpos = t0 - WINDOW + band
        dist_w = tq[:, None] - kpos[None, :]
        mask_w = (dist_w >= 0) & (dist_w < WINDOW) & (kpos[None, :] >= 0)
        bias_w = tbl[rel_bucket(dist_w)].transpose(0, 2, 3, 1)
        s_w = jnp.einsum('bqgrd,bkgd->bqgrk', qb, kw) + bias_w
        p_w = masked_softmax(s_w, mask_w[:, None, None, :])
        o_w = jnp.einsum('bqgrk,bkgd->bqgrd', p_w.astype(dt), vw)
        return o_s, o_w

    q_blocks = jnp.moveaxis(q.reshape(B, n_qb, Q_BLOCK, G, R, Dh), 1, 0)
    sel_blocks = jnp.moveaxis(sel.reshape(B, n_qb, Q_BLOCK, G, n_sel), 1, 0)
    t0s = jnp.arange(n_qb, dtype=jnp.int32) * Q_BLOCK
    o_s, o_w = lax.map(block_fn, (q_blocks, sel_blocks, t0s))
    o_s = jnp.moveaxis(o_s, 0, 1).reshape(B, S, G, R, Dh)
    o_w = jnp.moveaxis(o_w, 0, 1).reshape(B, S, G, R, Dh)

    o = (gates[:, :, 0, :, :, None] * o_cmp + gates[:, :, 1, :, :, None] * o_s
         + gates[:, :, 2, :, :, None] * o_w)
    return o.reshape(B, S, N_HEADS * Dh) @ w_o


def setup_inputs(seed: int = 0) -> dict:
    key = jax.random.key(seed)
    keys = iter(jax.random.split(key, 48))
    f32 = jnp.float32

    def w(shape, fan_in):
        return jax.random.normal(next(keys), shape, f32) * (fan_in ** -0.5)

    def gain(shape):
        return 1.0 + 0.05 * jax.random.normal(next(keys), shape, f32)

    def small(shape, scale=0.01):
        return scale * jax.random.normal(next(keys), shape, f32)

    lru_s = jax.random.uniform(next(keys), (N_A_LAYERS, D_RNN), f32, 0.9, 0.999) ** (1.0 / LRU_C)
    d_q = N_HEADS * HEAD_DIM + 3 * N_HEADS
    return {
        'x': jax.random.normal(next(keys), (BATCH, SEQ, D_MODEL), f32),
        'p': jax.random.normal(next(keys), (DEPTH, BATCH, SEQ, D_PLE), f32),
        'g_mix': gain((DEPTH, D_MODEL)),
        'g_ffn': gain((DEPTH, D_MODEL)),
        'g_ple': gain((DEPTH, D_MODEL)),
        'g_final': gain((D_MODEL,)),
        'rg_w_in': w((N_A_LAYERS, D_MODEL, 2 * D_RNN), D_MODEL),
        'rg_conv_w': w((N_A_LAYERS, CONV_WIDTH, D_RNN), CONV_WIDTH),
        'rg_conv_b': small((N_A_LAYERS, D_RNN)),
        'rg_w_a': w((N_A_LAYERS, LRU_BLOCKS, LRU_BLOCK_W, LRU_BLOCK_W), LRU_BLOCK_W),
        'rg_b_a': small((N_A_LAYERS, D_RNN)),
        'rg_w_x': w((N_A_LAYERS, LRU_BLOCKS, LRU_BLOCK_W, LRU_BLOCK_W), LRU_BLOCK_W),
        'rg_b_x': small((N_A_LAYERS, D_RNN)),
        'rg_lambda': jnp.log(lru_s) - jnp.log1p(-lru_s),
        'rg_w_out': w((N_A_LAYERS, D_RNN, D_MODEL), D_RNN),
        'kv_norm': gain((D_MODEL,)),
        'w_kv': w((D_MODEL, N_KV_BRANCH * N_KV_GROUPS * HEAD_DIM), D_MODEL),
        'cmp_pos_k': small((CMP_BLOCK, HEAD_DIM), 0.1),
        'cmp_pos_v': small((CMP_BLOCK, HEAD_DIM), 0.1),
        'cmp_k_w1': w((CMP_BLOCK * HEAD_DIM, CMP_HIDDEN), CMP_BLOCK * HEAD_DIM),
        'cmp_k_w2': w((CMP_HIDDEN, HEAD_DIM), CMP_HIDDEN),
        'cmp_v_w1': w((CMP_BLOCK * HEAD_DIM, CMP_HIDDEN), CMP_BLOCK * HEAD_DIM),
        'cmp_v_w2': w((CMP_HIDDEN, HEAD_DIM), CMP_HIDDEN),
        'rel_bias': small((N_BUCKETS, N_HEADS), 0.3),
        'nsa_w_q': w((N_B_LAYERS, D_MODEL, d_q), D_MODEL),
        'nsa_w_o': w((N_B_LAYERS, N_HEADS * HEAD_DIM, D_MODEL), N_HEADS * HEAD_DIM),
        'ffn_w_gu': w((N_DENSE, D_MODEL, 2 * D_FF), D_MODEL),
        'ffn_w_down': w((N_DENSE, D_FF, D_MODEL), D_FF),
        'moe_w_router': w((N_MOE, D_MODEL, N_EXPERTS), D_MODEL),
        'moe_w_gu': w((N_MOE, N_EXPERTS, D_MODEL, 2 * D_FF_EXPERT), D_MODEL),
        'moe_w_down': w((N_MOE, N_EXPERTS, D_FF_EXPERT, D_MODEL), D_FF_EXPERT),
        'ple_w_proj': w((DEPTH, D_PLE, D_MODEL), D_PLE),
        'ple_w_gate': w((DEPTH, D_MODEL, D_MODEL), D_MODEL),
    }


def reference(x, p, g_mix, g_ffn, g_ple, g_final,
              rg_w_in, rg_conv_w, rg_conv_b, rg_w_a, rg_b_a, rg_w_x, rg_b_x, rg_lambda, rg_w_out,
              kv_norm, w_kv, cmp_pos_k, cmp_pos_v, cmp_k_w1, cmp_k_w2, cmp_v_w1, cmp_v_w2, rel_bias,
              nsa_w_q, nsa_w_o, ffn_w_gu, ffn_w_down, moe_w_router, moe_w_gu, moe_w_down,
              ple_w_proj, ple_w_gate):
    shared_kv = None
    for i in range(DEPTH):
        h = rmsnorm(x, g_mix[i])
        if i < N_A_LAYERS:
            a = i
            x = x + recurrent_block(h, rg_w_in[a], rg_conv_w[a], rg_conv_b[a], rg_w_a[a], rg_b_a[a],
                                    rg_w_x[a], rg_b_x[a], rg_lambda[a], rg_w_out[a])
        else:
            b = i - N_A_LAYERS
            x = x + nsa_attention(h, nsa_w_q[b], nsa_w_o[b], rel_bias, *shared_kv)
        h = rmsnorm(x, g_ffn[i])
        if i % 2 == 0:
            x = x + swiglu(h, ffn_w_gu[i // 2], ffn_w_down[i // 2])
        else:
            x = x + moe_ffn(h, moe_w_router[i // 2], moe_w_gu[i // 2], moe_w_down[i // 2])
        ple_gate = jax.nn.sigmoid(rmsnorm(x, g_ple[i]) @ ple_w_gate[i])
        x = x + ple_gate * (p[i] @ ple_w_proj[i])
        if i == N_A_LAYERS - 1:
            shared_kv = nsa_shared_kv(x, kv_norm, w_kv, cmp_pos_k, cmp_pos_v,
                                      cmp_k_w1, cmp_k_w2, cmp_v_w1, cmp_v_w2)
    return rmsnorm(x, g_final)
```

```python
import functools
import math

import numpy as np
import jax
import jax.numpy as jnp
from jax import lax
from jax.experimental import pallas as pl
from jax.experimental.pallas import tpu as pltpu

F32 = jnp.float32
BF16 = jnp.bfloat16

EPS = 1e-6
CONV_WIDTH = 4
LRU_BLOCKS = 16
LRU_C = 8.0
N_HEADS = 16
N_KV_GROUPS = 2
GROUP = N_HEADS // N_KV_GROUPS
HEAD_DIM = 64
N_KV_BRANCH = 6
CMP_BLOCK = 32
CMP_STRIDE = 16
SLC_BLOCK = 64
N_SELECT = 8
WINDOW = 512
Q_BLOCK = 128
N_BUCKETS = 32
MAX_DISTANCE = 128
TOP_K = 2
NEAR = WINDOW + Q_BLOCK
FAR_CHUNK = 512
NEG = -1e30

VMEM_LIMIT_V7X = 56 * 1024 * 1024
ROW_TILE = 512


def _cparams(sem):
    return pltpu.CompilerParams(dimension_semantics=sem, vmem_limit_bytes=VMEM_LIMIT_V7X)


def _rms(x, g):
    return x * lax.rsqrt(jnp.mean(x * x, axis=-1, keepdims=True) + EPS) * g


def _silu(x):
    return x * jax.nn.sigmoid(x)


def _norm_mm_kernel(x_ref, g_ref, *refs, n_w, epilogue):
    w_refs, o_ref, xn_ref = refs[:n_w], refs[n_w], refs[n_w + 1]

    @pl.when(pl.program_id(1) == 0)
    def _():
        xn_ref[...] = _rms(x_ref[...], g_ref[...]).astype(BF16)

    xn = xn_ref[...]
    outs = [jnp.dot(xn, w[...], preferred_element_type=F32) for w in w_refs]
    o_ref[...] = epilogue(*outs).astype(o_ref.dtype)


def norm_matmul(x, g, w, col_blocks, n_out, tn, epilogue, out_dtype, tm=ROW_TILE):
    n, d = x.shape
    grid = (n // tm, n_out // tn)
    w_specs = [pl.BlockSpec((d, tn), functools.partial(lambda i, j, cb: (0, cb + j), cb=cb))
               for cb in col_blocks]
    return pl.pallas_call(
        functools.partial(_norm_mm_kernel, n_w=len(col_blocks), epilogue=epilogue),
        grid=grid,
        in_specs=[pl.BlockSpec((tm, d), lambda i, j: (i, 0)),
                  pl.BlockSpec((1, d), lambda i, j: (0, 0))] + w_specs,
        out_specs=pl.BlockSpec((tm, tn), lambda i, j: (i, j)),
        out_shape=jax.ShapeDtypeStruct((n, n_out), out_dtype),
        scratch_shapes=[pltpu.VMEM((tm, d), BF16)],
        compiler_params=_cparams(("parallel", "arbitrary")),
    )(x, g.reshape(1, d), *([w] * len(col_blocks)))


def _mm_res_kernel(a_ref, w_ref, res_ref, *refs, scale_col):
    if scale_col is None:
        (o_ref,) = refs
        y = jnp.dot(a_ref[...], w_ref[...], preferred_element_type=F32)
    else:
        s_ref, o_ref = refs
        y = jnp.dot(a_ref[...], w_ref[...], preferred_element_type=F32)
        y = y * s_ref[:, scale_col:scale_col + 1]
    o_ref[...] = res_ref[...] + y


def matmul_residual(a, w, res, scale=None, scale_col=None, tm=ROW_TILE, tn=512):
    n, k = a.shape
    d = w.shape[1]
    in_specs = [pl.BlockSpec((tm, k), lambda i, j: (i, 0)),
                pl.BlockSpec((k, tn), lambda i, j: (0, j)),
                pl.BlockSpec((tm, tn), lambda i, j: (i, j))]
    args = [a, w, res]
    if scale is not None:
        in_specs.append(pl.BlockSpec((tm, scale.shape[1]), lambda i, j: (i, 0)))
        args.append(scale)
    return pl.pallas_call(
        functools.partial(_mm_res_kernel, scale_col=scale_col),
        grid=(n // tm, d // tn),
        in_specs=in_specs,
        out_specs=pl.BlockSpec((tm, tn), lambda i, j: (i, j)),
        out_shape=jax.ShapeDtypeStruct((n, d), F32),
        compiler_params=_cparams(("parallel", "arbitrary")),
    )(*args)


BD = 256
SCAN_ROWS = 8


def _rg_kernel(gate_ref, xr_ref, cw_ref, cb_ref, wa_ref, ba_ref, wx_ref, bx_ref, lam_ref,
               o_ref, xe_ref, h_ref, a_ref, u_ref):
    t, c = xr_ref.shape

    @pl.when(pl.program_id(1) == 0)
    def _():
        xe_ref[0:8, :] = jnp.zeros((8, c), F32)
        h_ref[...] = jnp.zeros_like(h_ref)

    xe_ref[8:, :] = xr_ref[...]
    xc = cb_ref[...] + sum(
        cw_ref[k:k + 1, :] * xe_ref[8 - (CONV_WIDTH - 1) + k: 8 - (CONV_WIDTH - 1) + k + t, :]
        for k in range(CONV_WIDTH))
    xe_ref[0:8, :] = xr_ref[t - 8:, :]

    lam = -lam_ref[...]
    softplus_neg_lam = jnp.maximum(lam, 0.0) + jnp.log1p(jnp.exp(-jnp.abs(lam)))
    for cblk in range(c // BD):
        sl = slice(cblk * BD, (cblk + 1) * BD)
        xb = xc[:, sl]
        xb16 = xb.astype(BF16)
        r = jax.nn.sigmoid(jnp.dot(xb16, wa_ref[cblk], preferred_element_type=F32) + ba_ref[:, sl])
        i = jax.nn.sigmoid(jnp.dot(xb16, wx_ref[cblk], preferred_element_type=F32) + bx_ref[:, sl])
        log_a = -LRU_C * r * softplus_neg_lam[:, sl]
        a = jnp.exp(log_a)
        a_ref[:, sl] = a
        u_ref[:, sl] = jnp.sqrt(1.0 - a * a) * (i * xb)

    row = lax.broadcasted_iota(jnp.int32, (SCAN_ROWS, c), 0)

    def scan_tile(k, h_prev):
        rows = pl.ds(pl.multiple_of(k * SCAN_ROWS, SCAN_ROWS), SCAN_ROWS)
        a = a_ref[rows, :]
        u = u_ref[rows, :]
        for d in (1, 2, 4):
            a_s = jnp.where(row >= d, pltpu.roll(a, d, 0), 1.0)
            u_s = jnp.where(row >= d, pltpu.roll(u, d, 0), 0.0)
            u = a * u_s + u
            a = a * a_s
        h = a * h_prev + u
        u_ref[rows, :] = h
        return jnp.broadcast_to(h[SCAN_ROWS - 1:SCAN_ROWS, :], (SCAN_ROWS, c))

    h_ref[...] = lax.fori_loop(0, t // SCAN_ROWS, scan_tile, h_ref[...])
    o_ref[...] = (jax.nn.gelu(gate_ref[...]) * u_ref[...]).astype(o_ref.dtype)


def rg_lru_core(proj, batch, seq, conv_w, conv_b, wa_bd, b_a, wx_bd, b_x, lam, t_chunk=512):
    n, c2 = proj.shape
    c = c2 // 2
    nt = seq // t_chunk
    vec = lambda v: v.reshape(1, c)
    row_spec = lambda col: pl.BlockSpec((t_chunk, c), lambda b, t: (b * nt + t, col))
    const = lambda shape: pl.BlockSpec(shape, lambda b, t: (0,) * len(shape))
    return pl.pallas_call(
        _rg_kernel,
        grid=(batch, nt),
        in_specs=[row_spec(0), row_spec(1), const((CONV_WIDTH, c)), const((1, c)),
                  const(wa_bd.shape), const((1, c)), const(wx_bd.shape), const((1, c)), const((1, c))],
        out_specs=pl.BlockSpec((t_chunk, c), lambda b, t: (b * nt + t, 0)),
        out_shape=jax.ShapeDtypeStruct((n, c), BF16),
        scratch_shapes=[pltpu.VMEM((t_chunk + 8, c), F32), pltpu.VMEM((SCAN_ROWS, c), F32),
                        pltpu.VMEM((t_chunk, c), F32), pltpu.VMEM((t_chunk, c), F32)],
        compiler_params=_cparams(("parallel", "arbitrary")),
    )(proj, proj, conv_w, vec(conv_b), wa_bd, vec(b_a), wx_bd, vec(b_x), vec(lam))


def _block_diag_chunks(w):
    nb, bw, _ = w.shape
    per = BD // bw
    w = w.reshape(nb // per, per, bw, bw)
    eye = jnp.eye(per, dtype=w.dtype)
    out = jnp.einsum('cpij,pq->cpiqj', w, eye).reshape(nb // per, BD, BD)
    return out.astype(BF16)


def _ple_kernel(x_ref, g_ref, wg_ref, p_ref, wp_ref, *refs, final_norm):
    if final_norm:
        gf_ref, o_ref = refs
    else:
        (o_ref,) = refs
    x = x_ref[...]
    xn = _rms(x, g_ref[...]).astype(BF16)
    gate = jax.nn.sigmoid(jnp.dot(xn, wg_ref[...], preferred_element_type=F32))
    proj = jnp.dot(p_ref[...].astype(BF16), wp_ref[...], preferred_element_type=F32)
    y = x + gate * proj
    if final_norm:
        y = _rms(y, gf_ref[...])
    o_ref[...] = y


def ple_update(x, g, wg, p, wp, g_final=None, tm=ROW_TILE):
    n, d = x.shape
    dp = p.shape[1]
    const = lambda shape: pl.BlockSpec(shape, lambda i: (0,) * len(shape))
    in_specs = [pl.BlockSpec((tm, d), lambda i: (i, 0)), const((1, d)), const((d, d)),
                pl.BlockSpec((tm, dp), lambda i: (i, 0)), const((dp, d))]
    args = [x, g.reshape(1, d), wg, p, wp]
    if g_final is not None:
        in_specs.append(const((1, d)))
        args.append(g_final.reshape(1, d))
    return pl.pallas_call(
        functools.partial(_ple_kernel, final_norm=g_final is not None),
        grid=(n // tm,),
        in_specs=in_specs,
        out_specs=pl.BlockSpec((tm, d), lambda i: (i, 0)),
        out_shape=jax.ShapeDtypeStruct((n, d), F32),
        compiler_params=_cparams(("parallel",)),
    )(*args)


def _router_kernel(x_ref, g_ref, w_ref, o_ref):
    xn = _rms(x_ref[...], g_ref[...])
    logits = jnp.dot(xn, w_ref[...], preferred_element_type=F32, precision=lax.Precision.HIGHEST)
    ne = logits.shape[1]
    lane = lax.broadcasted_iota(jnp.int32, logits.shape, 1).astype(F32)
    m1 = jnp.max(logits, axis=1, keepdims=True)
    i1 = jnp.min(jnp.where(logits == m1, lane, float(ne)), axis=1, keepdims=True)
    rest = jnp.where(lane == i1, -jnp.inf, logits)
    m2 = jnp.max(rest, axis=1, keepdims=True)
    i2 = jnp.min(jnp.where(rest == m2, lane, float(ne)), axis=1, keepdims=True)
    e2 = jnp.exp(m2 - m1)
    denom = 1.0 + e2
    o_ref[...] = jnp.where(lane == i1, 1.0 / denom, 0.0) + jnp.where(lane == i2, e2 / denom, 0.0)


def moe_router(x, g, w_router, tm=ROW_TILE):
    n, d = x.shape
    ne = w_router.shape[1]
    return pl.pallas_call(
        _router_kernel,
        grid=(n // tm,),
        in_specs=[pl.BlockSpec((tm, d), lambda i: (i, 0)), pl.BlockSpec((1, d), lambda i: (0, 0)),
                  pl.BlockSpec((d, ne), lambda i: (0, 0))],
        out_specs=pl.BlockSpec((tm, ne), lambda i: (i, 0)),
        out_shape=jax.ShapeDtypeStruct((n, ne), F32),
        compiler_params=_cparams(("parallel",)),
    )(x, g.reshape(1, d), w_router)


def _compress_kernel(h_ref, pos_ref, w1_ref, w2_ref, o_ref):
    half = h_ref.shape[1]
    h = h_ref[...].astype(BF16)
    p_lo = jnp.dot(h, w1_ref[0:half, :], preferred_element_type=F32)
    p_hi = jnp.dot(h, w1_ref[half:, :], preferred_element_type=F32)
    nh = p_hi.shape[0]
    posb = jnp.dot(pos_ref[...].astype(BF16), w1_ref[...], preferred_element_type=F32)
    hid = p_lo + pltpu.roll(p_hi, nh - 1, 0) + posb
    o_ref[...] = jnp.dot(jax.nn.gelu(hid).astype(BF16), w2_ref[...],
                         preferred_element_type=F32).astype(o_ref.dtype)


def nsa_compress(halves, pos, w1, w2):
    bg, nh, hw = halves.shape
    hid = w1.shape[1]
    dh = w2.shape[1]
    return pl.pallas_call(
        _compress_kernel,
        grid=(bg,),
        in_specs=[pl.BlockSpec((None, nh, hw), lambda i: (i, 0, 0)),
                  pl.BlockSpec((1, 2 * hw), lambda i: (0, 0)),
                  pl.BlockSpec((2 * hw, hid), lambda i: (0, 0)),
                  pl.BlockSpec((hid, dh), lambda i: (0, 0))],
        out_specs=pl.BlockSpec((None, nh, dh), lambda i: (i, 0, 0)),
        out_shape=jax.ShapeDtypeStruct((bg, nh, dh), BF16),
        compiler_params=_cparams(("parallel",)),
    )(halves, pos.reshape(1, 2 * hw), w1, w2)


def _rel_bucket_np(dist):
    n = np.maximum(dist, 0)
    max_exact = N_BUCKETS // 2
    nf = np.maximum(n, 1).astype(np.float32)
    large = max_exact + (np.log(nf / max_exact) / np.float32(math.log(MAX_DISTANCE / max_exact))
                         * (N_BUCKETS - max_exact)).astype(np.int32)
    return np.where(n < max_exact, n, np.minimum(large, N_BUCKETS - 1)).astype(np.int32)


def _softmax_parts(s, mask):
    l = jnp.where(mask, s, NEG)
    m = jnp.max(l, axis=-1, keepdims=True)
    e = jnp.where(mask, jnp.exp(l - m), 0.0)
    return e, jnp.maximum(jnp.sum(e, axis=-1, keepdims=True), 1e-30)


def _nsa_kernel(q_ref, kc_ref, vc_ref, bc_ref, ks_ref, vs_ref, kw_ref, vw_ref, tz_ref, gates_ref,
                ovl_ref, exp_ref, o_ref, km_ref, *, n_cmp, n_slc, n_sel):
    r, tq, dh = q_ref.shape
    g = pl.program_id(1)
    qt = pl.program_id(2)
    t0 = pl.multiple_of(qt * tq, tq)
    q = q_ref[...].reshape(r * tq, dh)
    nt = (((1,), (1,)), ((), ()))

    ncp = kc_ref.shape[0]
    s_c = lax.dot_general(q, kc_ref[...], nt, preferred_element_type=F32).reshape(r, tq, ncp)
    s_c = s_c + bc_ref[...]
    qpos = t0 + lax.broadcasted_iota(jnp.int32, (tq, ncp), 0)
    cidx = lax.broadcasted_iota(jnp.int32, (tq, ncp), 1)
    mask_c = ((qpos - (cidx * CMP_STRIDE + CMP_BLOCK - 1)) >= 0) & (cidx < n_cmp)
    e_c, den_c = _softmax_parts(s_c, mask_c[None])
    p_c = e_c / den_c
    o_cmp = jnp.dot(p_c.reshape(r * tq, ncp).astype(BF16), vc_ref[...], preferred_element_type=F32)

    imp = jnp.dot(jnp.sum(p_c, axis=0), ovl_ref[...], preferred_element_type=F32,
                  precision=lax.Precision.HIGHEST)
    jblk = lax.broadcasted_iota(jnp.int32, (tq, n_slc), 1).astype(F32)
    qblk = ((t0 + lax.broadcasted_iota(jnp.int32, (tq, n_slc), 0)) // SLC_BLOCK).astype(F32)
    forced = (jblk == 0) | (jblk == qblk) | (jblk == qblk - 1)
    score = jnp.where(forced, 1e9, jnp.where(jblk <= qblk, imp, -1e9))
    sel = jnp.zeros((tq, n_slc), F32)
    for _ in range(n_sel):
        m = jnp.max(score, axis=1, keepdims=True)
        first = jnp.min(jnp.where(score == m, jblk, float(n_slc)), axis=1, keepdims=True)
        pick = jblk == first
        sel = jnp.where(pick, 1.0, sel)
        score = jnp.where(pick, -jnp.inf, score)
    km_ref[...] = jnp.dot(sel.astype(BF16), exp_ref[...], preferred_element_type=F32)

    near = tz_ref.shape[2]
    qi = lax.broadcasted_iota(jnp.int32, (tq, near), 0)
    kj = lax.broadcasted_iota(jnp.int32, (tq, near), 1)
    dist = qi + WINDOW - kj
    bias_near = tz_ref[...]
    near_rows = pl.ds(t0, near)

    s_w = lax.dot_general(q, kw_ref[near_rows, :], nt, preferred_element_type=F32).reshape(r, tq, near)
    mask_w = (dist >= 0) & (dist < WINDOW) & (t0 + kj >= WINDOW)
    e_w, den_w = _softmax_parts(s_w + bias_near, mask_w[None])
    o_win = jnp.dot(e_w.reshape(r * tq, near).astype(BF16), vw_ref[near_rows, :],
                    preferred_element_type=F32) / den_w.reshape(r * tq, 1)

    far_bias = tz_ref[:, :, 0:1]

    def far_chunk(f, carry):
        m_run, l_run, acc = carry
        start = pl.multiple_of(WINDOW + f * FAR_CHUNK, FAR_CHUNK)
        rows = pl.ds(start, FAR_CHUNK)
        s = lax.dot_general(q, ks_ref[rows, :], nt, preferred_element_type=F32).reshape(r, tq, FAR_CHUNK)
        kpos = start + lax.broadcasted_iota(jnp.int32, (tq, FAR_CHUNK), 1)
        mask = (km_ref[:, rows] > 0.5) & (kpos < t0)
        l = jnp.where(mask[None], s + far_bias, NEG)
        m_new = jnp.maximum(m_run, jnp.max(l, axis=-1, keepdims=True))
        alpha = jnp.exp(m_run - m_new)
        e = jnp.where(mask[None], jnp.exp(l - m_new), 0.0)
        l_new = alpha * l_run + jnp.sum(e, axis=-1, keepdims=True)
        pv = jnp.dot(e.reshape(r * tq, FAR_CHUNK).astype(BF16), vs_ref[rows, :],
                     preferred_element_type=F32)
        return m_new, l_new, alpha.reshape(r * tq, 1) * acc + pv

    n_far = (jnp.maximum(t0, WINDOW) - 1) // FAR_CHUNK
    init = (jnp.full((r, tq, 1), NEG, F32), jnp.zeros((r, tq, 1), F32), jnp.zeros((r * tq, dh), F32))
    m_run, l_run, acc = lax.fori_loop(0, n_far, far_chunk, init)

    s_s = lax.dot_general(q, ks_ref[near_rows, :], nt, preferred_element_type=F32).reshape(r, tq, near)
    mask_s = (km_ref[:, near_rows] > 0.5) & (dist >= 0)
    l = jnp.where(mask_s[None], s_s + bias_near, NEG)
    m_new = jnp.maximum(m_run, jnp.max(l, axis=-1, keepdims=True))
    alpha = jnp.exp(m_run - m_new)
    e = jnp.where(mask_s[None], jnp.exp(l - m_new), 0.0)
    den_s = jnp.maximum(alpha * l_run + jnp.sum(e, axis=-1, keepdims=True), 1e-30)
    acc = alpha.reshape(r * tq, 1) * acc + jnp.dot(
        e.reshape(r * tq, near).astype(BF16), vs_ref[near_rows, :], preferred_element_type=F32)
    o_slc = acc / den_s.reshape(r * tq, 1)

    gates = gates_ref[...]
    for h in range(r):
        rows = slice(h * tq, (h + 1) * tq)
        out_h = jnp.zeros((tq, dh), F32)
        for branch, o_b in enumerate((o_cmp, o_slc, o_win)):
            out_h = out_h + gates[:, branch * r + h: branch * r + h + 1] * o_b[rows, :]
        o_ref[h] = out_h.astype(o_ref.dtype)


def nsa_attention_core(q_t, k_cmp, v_cmp, bias_c, ks_pad, vs_pad, kw_pad, vw_pad, tz, gates,
                       overlap, expand, n_cmp):
    b, h, s, dh = q_t.shape
    gq = N_KV_GROUPS
    r = h // gq
    tq = Q_BLOCK
    ncp = k_cmp.shape[2]
    n_slc = s // SLC_BLOCK
    n_sel = min(N_SELECT, n_slc)
    spad = ks_pad.shape[2]
    near = tz.shape[2]
    kv_spec = lambda rows: pl.BlockSpec((None, None, rows, dh), lambda bi, gi, qi: (bi, gi, 0, 0))
    return pl.pallas_call(
        functools.partial(_nsa_kernel, n_cmp=n_cmp, n_slc=n_slc, n_sel=n_sel),
        grid=(b, gq, s // tq),
        in_specs=[pl.BlockSpec((None, r, tq, dh), lambda bi, gi, qi: (bi, gi, qi, 0)),
                  kv_spec(ncp), kv_spec(ncp),
                  pl.BlockSpec((r, tq, ncp), lambda bi, gi, qi: (gi, qi, 0)),
                  kv_spec(spad), kv_spec(spad), kv_spec(spad), kv_spec(spad),
                  pl.BlockSpec((r, tq, near), lambda bi, gi, qi: (gi, 0, 0)),
                  pl.BlockSpec((None, None, tq, gates.shape[3]), lambda bi, gi, qi: (bi, gi, qi, 0)),
                  pl.BlockSpec(overlap.shape, lambda bi, gi, qi: (0, 0)),
                  pl.BlockSpec(expand.shape, lambda bi, gi, qi: (0, 0))],
        out_specs=pl.BlockSpec((None, r, tq, dh), lambda bi, gi, qi: (bi, gi, qi, 0)),
        out_shape=jax.ShapeDtypeStruct((b, h, s, dh), BF16),
        scratch_shapes=[pltpu.VMEM((tq, spad), F32)],
        compiler_params=_cparams(("parallel", "parallel", "arbitrary")),
    )(q_t, k_cmp, v_cmp, bias_c, ks_pad, vs_pad, kw_pad, vw_pad, tz, gates, overlap, expand)


def _nsa_tables(rel_bias, seq, n_cmp, ncp):
    tq = Q_BLOCK
    qi = np.arange(tq)[:, None]
    dist_near = qi + WINDOW - np.arange(NEAR)[None, :]
    tz = rel_bias[_rel_bucket_np(dist_near)]
    s_idx = np.arange(seq)[:, None]
    dist_c = s_idx - (np.arange(ncp)[None, :] * CMP_STRIDE + CMP_BLOCK - 1)
    bias_c = rel_bias[_rel_bucket_np(dist_c)]
    n_slc = seq // SLC_BLOCK
    cs = np.arange(ncp)[:, None] * CMP_STRIDE
    ss = np.arange(n_slc)[None, :] * SLC_BLOCK
    overlap = ((cs < ss + SLC_BLOCK) & (cs + CMP_BLOCK > ss) & (np.arange(ncp)[:, None] < n_cmp))
    kpos = np.arange(WINDOW + seq)[None, :] - WINDOW
    expand = (kpos // SLC_BLOCK == np.arange(n_slc)[:, None]) & (kpos >= 0)
    return (jnp.transpose(tz, (2, 0, 1)), jnp.transpose(bias_c, (2, 0, 1)),
            jnp.asarray(overlap, F32), jnp.asarray(expand, BF16))


def _nsa_shared_kv(x, batch, seq, kv_norm, w_kv, cmp_pos_k, cmp_pos_v, k_w1, k_w2, v_w1, v_w2):
    n, d = x.shape
    gq, dh = N_KV_GROUPS, HEAD_DIM
    kv = norm_matmul(x, kv_norm, w_kv.astype(BF16), (0,), w_kv.shape[1], w_kv.shape[1],
                     lambda y: y, BF16)
    kv = kv.reshape(batch, seq, N_KV_BRANCH, gq, dh)
    per_group = lambda j: jnp.transpose(kv[:, :, j], (0, 2, 1, 3))
    n_cmp = (seq - CMP_BLOCK) // CMP_STRIDE + 1
    nhalf = seq // CMP_STRIDE

    def compress(t, pos, w1, w2):
        halves = t.reshape(batch * gq, nhalf, CMP_STRIDE * dh)
        return nsa_compress(halves, pos, w1.astype(BF16), w2.astype(BF16)).reshape(batch, gq, nhalf, dh)

    k_cmp = compress(per_group(0), cmp_pos_k, k_w1, k_w2)
    v_cmp = compress(per_group(1), cmp_pos_v, v_w1, v_w2)
    pad = lambda t: jnp.pad(t, ((0, 0), (0, 0), (WINDOW, 0), (0, 0)))
    return (k_cmp, v_cmp, pad(per_group(2)), pad(per_group(3)), pad(per_group(4)), pad(per_group(5)),
            n_cmp)


def _nsa_layer(x, batch, seq, g_mix, w_q, w_o, shared, tables):
    n, d = x.shape
    hd = N_HEADS * HEAD_DIM
    k_cmp, v_cmp, ks, vs, kw, vw, n_cmp = shared
    tz, bias_c, overlap, expand = tables
    w_q16 = w_q.astype(BF16)
    q = norm_matmul(x, g_mix, w_q16[:, :hd], (0,), hd, 512,
                    lambda y: y * (HEAD_DIM ** -0.5), BF16)
    n_gate = w_q.shape[1] - hd
    gates = norm_matmul(x, g_mix, w_q16[:, hd:], (0,), n_gate, n_gate, jax.nn.sigmoid, F32)
    q_t = jnp.transpose(q.reshape(batch, seq, N_HEADS, HEAD_DIM), (0, 2, 1, 3))
    gates = jnp.transpose(gates.reshape(batch, seq, 3, N_KV_GROUPS, GROUP), (0, 3, 1, 2, 4))
    o_t = nsa_attention_core(q_t, k_cmp, v_cmp, bias_c, ks, vs, kw, vw, tz,
                             gates.reshape(batch, N_KV_GROUPS, seq, 3 * GROUP), overlap, expand, n_cmp)
    o = jnp.transpose(o_t, (0, 2, 1, 3)).reshape(n, hd)
    return matmul_residual(o, w_o.astype(BF16), x)


def _swiglu_ffn(x, g, w_gu, w_down, res, scale=None, scale_col=None, tn=512):
    ff = w_down.shape[0]
    act = norm_matmul(x, g, w_gu, (0, ff // tn), ff, tn, lambda a, b: _silu(a) * b, BF16)
    return matmul_residual(act, w_down, res, scale, scale_col)


def _moe_ffn(x, g, w_router, w_gu, w_down):
    combine = moe_router(x, g, w_router)
    y = x
    for e in range(w_gu.shape[0]):
        y = _swiglu_ffn(x, g, w_gu[e].astype(BF16), w_down[e].astype(BF16), y, combine, e)
    return y


def kernel(x, p, g_mix, g_ffn, g_ple, g_final, rg_w_in, rg_conv_w, rg_conv_b, rg_w_a, rg_b_a, rg_w_x,
           rg_b_x, rg_lambda, rg_w_out, kv_norm, w_kv, cmp_pos_k, cmp_pos_v, cmp_k_w1, cmp_k_w2,
           cmp_v_w1, cmp_v_w2, rel_bias, nsa_w_q, nsa_w_o, ffn_w_gu, ffn_w_down, moe_w_router,
           moe_w_gu, moe_w_down, ple_w_proj, ple_w_gate):
    batch, seq, d = x.shape
    depth = p.shape[0]
    n_a = rg_w_in.shape[0]
    n = batch * seq
    x = x.reshape(n, d)
    p = p.reshape(depth, n, p.shape[-1])
    n_cmp = (seq - CMP_BLOCK) // CMP_STRIDE + 1
    tables = _nsa_tables(rel_bias, seq, n_cmp, seq // CMP_STRIDE)
    shared = None
    for i in range(depth):
        if i < n_a:
            c = rg_w_in.shape[2] // 2
            proj = norm_matmul(x, g_mix[i], rg_w_in[i].astype(BF16), (0,), 2 * c, 512, lambda y: y, F32)
            hg = rg_lru_core(proj, batch, seq, rg_conv_w[i], rg_conv_b[i], _block_diag_chunks(rg_w_a[i]),
                             rg_b_a[i], _block_diag_chunks(rg_w_x[i]), rg_b_x[i], rg_lambda[i])
            x = matmul_residual(hg, rg_w_out[i].astype(BF16), x)
        else:
            b = i - n_a
            x = _nsa_layer(x, batch, seq, g_mix[i], nsa_w_q[b], nsa_w_o[b], shared, tables)
        if i % 2 == 0:
            x = _swiglu_ffn(x, g_ffn[i], ffn_w_gu[i // 2].astype(BF16), ffn_w_down[i // 2].astype(BF16), x)
        else:
            x = _moe_ffn(x, g_ffn[i], moe_w_router[i // 2], moe_w_gu[i // 2], moe_w_down[i // 2])
        x = ple_update(x, g_ple[i], ple_w_gate[i].astype(BF16), p[i], ple_w_proj[i].astype(BF16),
                       g_final if i == depth - 1 else None)
        if i == n_a - 1:
            shared = _nsa_shared_kv(x, batch, seq, kv_norm, w_kv, cmp_pos_k, cmp_pos_v,
                                    cmp_k_w1, cmp_k_w2, cmp_v_w1, cmp_v_w2)
    return x.reshape(batch, seq, d)
```

```python
import functools
import math

import numpy as np
import jax
import jax.numpy as jnp
from jax import lax
from jax.experimental import pallas as pl
from jax.experimental.pallas import tpu as pltpu

F32 = jnp.float32
BF16 = jnp.bfloat16

EPS = 1e-6
CONV_WIDTH = 4
LRU_BLOCKS = 16
LRU_C = 8.0
N_HEADS = 16
N_KV_GROUPS = 2
GROUP = N_HEADS // N_KV_GROUPS
HEAD_DIM = 64
N_KV_BRANCH = 6
CMP_BLOCK = 32
CMP_STRIDE = 16
SLC_BLOCK = 64
N_SELECT = 8
WINDOW = 512
Q_BLOCK = 128
N_BUCKETS = 32
MAX_DISTANCE = 128
TOP_K = 2
NEAR = WINDOW + Q_BLOCK
FAR_CHUNK = 512
NEG = -1e30

VMEM_LIMIT_V7X = 56 * 1024 * 1024
ROW_TILE = 512


def _cparams(sem):
    return pltpu.CompilerParams(dimension_semantics=sem, vmem_limit_bytes=VMEM_LIMIT_V7X)


def _rms(x, g):
    return x * lax.rsqrt(jnp.mean(x * x, axis=-1, keepdims=True) + EPS) * g


def _silu(x):
    return x * jax.nn.sigmoid(x)


def _norm_mm_kernel(x_ref, g_ref, *refs, n_w, epilogue):
    w_refs, o_ref, xn_ref = refs[:n_w], refs[n_w], refs[n_w + 1]

    @pl.when(pl.program_id(1) == 0)
    def _():
        xn_ref[...] = _rms(x_ref[...], g_ref[...]).astype(BF16)

    xn = xn_ref[...]
    outs = [jnp.dot(xn, w[...], preferred_element_type=F32) for w in w_refs]
    o_ref[...] = epilogue(*outs).astype(o_ref.dtype)


def norm_matmul(x, g, w, col_blocks, n_out, tn, epilogue, out_dtype, tm=ROW_TILE):
    n, d = x.shape
    grid = (n // tm, n_out // tn)
    w_specs = [pl.BlockSpec((d, tn), functools.partial(lambda i, j, cb: (0, cb + j), cb=cb))
               for cb in col_blocks]
    return pl.pallas_call(
        functools.partial(_norm_mm_kernel, n_w=len(col_blocks), epilogue=epilogue),
        grid=grid,
        in_specs=[pl.BlockSpec((tm, d), lambda i, j: (i, 0)),
                  pl.BlockSpec((1, d), lambda i, j: (0, 0))] + w_specs,
        out_specs=pl.BlockSpec((tm, tn), lambda i, j: (i, j)),
        out_shape=jax.ShapeDtypeStruct((n, n_out), out_dtype),
        scratch_shapes=[pltpu.VMEM((tm, d), BF16)],
        compiler_params=_cparams(("parallel", "arbitrary")),
    )(x, g.reshape(1, d), *([w] * len(col_blocks)))


def _mm_res_kernel(a_ref, w_ref, res_ref, *refs, scale_col):
    if scale_col is None:
        (o_ref,) = refs
        y = jnp.dot(a_ref[...], w_ref[...], preferred_element_type=F32)
    else:
        s_ref, o_ref = refs
        y = jnp.dot(a_ref[...], w_ref[...], preferred_element_type=F32)
        y = y * s_ref[:, scale_col:scale_col + 1]
    o_ref[...] = res_ref[...] + y


def matmul_residual(a, w, res, scale=None, scale_col=None, tm=ROW_TILE, tn=512):
    n, k = a.shape
    d = w.shape[1]
    in_specs = [pl.BlockSpec((tm, k), lambda i, j: (i, 0)),
                pl.BlockSpec((k, tn), lambda i, j: (0, j)),
                pl.BlockSpec((tm, tn), lambda i, j: (i, j))]
    args = [a, w, res]
    if scale is not None:
        in_specs.append(pl.BlockSpec((tm, scale.shape[1]), lambda i, j: (i, 0)))
        args.append(scale)
    return pl.pallas_call(
        functools.partial(_mm_res_kernel, scale_col=scale_col),
        grid=(n // tm, d // tn),
        in_specs=in_specs,
        out_specs=pl.BlockSpec((tm, tn), lambda i, j: (i, j)),
        out_shape=jax.ShapeDtypeStruct((n, d), F32),
        compiler_params=_cparams(("parallel", "arbitrary")),
    )(*args)


BD = 256
SCAN_ROWS = 8


def _rg_kernel(gate_ref, xr_ref, cw_ref, cb_ref, wa_ref, ba_ref, wx_ref, bx_ref, lam_ref,
               o_ref, xe_ref, h_ref, a_ref, u_ref):
    t, c = xr_ref.shape

    @pl.when(pl.program_id(1) == 0)
    def _():
        xe_ref[0:8, :] = jnp.zeros((8, c), F32)
        h_ref[...] = jnp.zeros_like(h_ref)

    xe_ref[8:, :] = xr_ref[...]
    xc = cb_ref[...] + sum(
        cw_ref[k:k + 1, :] * xe_ref[8 - (CONV_WIDTH - 1) + k: 8 - (CONV_WIDTH - 1) + k + t, :]
        for k in range(CONV_WIDTH))
    xe_ref[0:8, :] = xr_ref[t - 8:, :]

    lam = -lam_ref[...]
    softplus_neg_lam = jnp.maximum(lam, 0.0) + jnp.log1p(jnp.exp(-jnp.abs(lam)))
    for cblk in range(c // BD):
        sl = slice(cblk * BD, (cblk + 1) * BD)
        xb = xc[:, sl]
        xb16 = xb.astype(BF16)
        r = jax.nn.sigmoid(jnp.dot(xb16, wa_ref[cblk], preferred_element_type=F32) + ba_ref[:, sl])
        i = jax.nn.sigmoid(jnp.dot(xb16, wx_ref[cblk], preferred_element_type=F32) + bx_ref[:, sl])
        log_a = -LRU_C * r * softplus_neg_lam[:, sl]
        a = jnp.exp(log_a)
        a_ref[:, sl] = a
        u_ref[:, sl] = jnp.sqrt(1.0 - a * a) * (i * xb)

    row = lax.broadcasted_iota(jnp.int32, (SCAN_ROWS, c), 0)

    def scan_tile(k, h_prev):
        rows = pl.ds(pl.multiple_of(k * SCAN_ROWS, SCAN_ROWS), SCAN_ROWS)
        a = a_ref[rows, :]
        u = u_ref[rows, :]
        for d in (1, 2, 4):
            a_s = jnp.where(row >= d, pltpu.roll(a, d, 0), 1.0)
            u_s = jnp.where(row >= d, pltpu.roll(u, d, 0), 0.0)
            u = a * u_s + u
            a = a * a_s
        h = a * h_prev + u
        u_ref[rows, :] = h
        return jnp.broadcast_to(h[SCAN_ROWS - 1:SCAN_ROWS, :], (SCAN_ROWS, c))

    h_ref[...] = lax.fori_loop(0, t // SCAN_ROWS, scan_tile, h_ref[...])
    o_ref[...] = (jax.nn.gelu(gate_ref[...]) * u_ref[...]).astype(o_ref.dtype)


def rg_lru_core(proj, batch, seq, conv_w, conv_b, wa_bd, b_a, wx_bd, b_x, lam, t_chunk=512):
    n, c2 = proj.shape
    c = c2 // 2
    nt = seq // t_chunk
    vec = lambda v: v.reshape(1, c)
    row_spec = lambda col: pl.BlockSpec((t_chunk, c), lambda b, t: (b * nt + t, col))
    const = lambda shape: pl.BlockSpec(shape, lambda b, t: (0,) * len(shape))
    return pl.pallas_call(
        _rg_kernel,
        grid=(batch, nt),
        in_specs=[row_spec(0), row_spec(1), const((CONV_WIDTH, c)), const((1, c)),
                  const(wa_bd.shape), const((1, c)), const(wx_bd.shape), const((1, c)), const((1, c))],
        out_specs=pl.BlockSpec((t_chunk, c), lambda b, t: (b * nt + t, 0)),
        out_shape=jax.ShapeDtypeStruct((n, c), BF16),
        scratch_shapes=[pltpu.VMEM((t_chunk + 8, c), F32), pltpu.VMEM((SCAN_ROWS, c), F32),
                        pltpu.VMEM((t_chunk, c), F32), pltpu.VMEM((t_chunk, c), F32)],
        compiler_params=_cparams(("parallel", "arbitrary")),
    )(proj, proj, conv_w, vec(conv_b), wa_bd, vec(b_a), wx_bd, vec(b_x), vec(lam))


def _block_diag_chunks(w):
    nb, bw, _ = w.shape
    per = BD // bw
    w = w.reshape(nb // per, per, bw, bw)
    eye = jnp.eye(per, dtype=w.dtype)
    out = jnp.einsum('cpij,pq->cpiqj', w, eye).reshape(nb // per, BD, BD)
    return out.astype(BF16)


def _ple_kernel(x_ref, g_ref, wg_ref, p_ref, wp_ref, *refs, final_norm):
    if final_norm:
        gf_ref, o_ref = refs
    else:
        (o_ref,) = refs
    x = x_ref[...]
    xn = _rms(x, g_ref[...]).astype(BF16)
    gate = jax.nn.sigmoid(jnp.dot(xn, wg_ref[...], preferred_element_type=F32))
    proj = jnp.dot(p_ref[...].astype(BF16), wp_ref[...], preferred_element_type=F32)
    y = x + gate * proj
    if final_norm:
        y = _rms(y, gf_ref[...])
    o_ref[...] = y


def ple_update(x, g, wg, p, wp, g_final=None, tm=ROW_TILE):
    n, d = x.shape
    dp = p.shape[1]
    const = lambda shape: pl.BlockSpec(shape, lambda i: (0,) * len(shape))
    in_specs = [pl.BlockSpec((tm, d), lambda i: (i, 0)), const((1, d)), const((d, d)),
                pl.BlockSpec((tm, dp), lambda i: (i, 0)), const((dp, d))]
    args = [x, g.reshape(1, d), wg, p, wp]
    if g_final is not None:
        in_specs.append(const((1, d)))
        args.append(g_final.reshape(1, d))
    return pl.pallas_call(
        functools.partial(_ple_kernel, final_norm=g_final is not None),
        grid=(n // tm,),
        in_specs=in_specs,
        out_specs=pl.BlockSpec((tm, d), lambda i: (i, 0)),
        out_shape=jax.ShapeDtypeStruct((n, d), F32),
        compiler_params=_cparams(("parallel",)),
    )(*args)


SEL_ROWS = 8


def _route_kernel(x_ref, g_ref, wt_ref, sel_ref, w_ref, cnt_ref, carry_ref):
    @pl.when(pl.program_id(0) == 0)
    def _():
        carry_ref[...] = jnp.zeros_like(carry_ref)

    xn = _rms(x_ref[...], g_ref[...])
    logits = lax.dot_general(wt_ref[...], xn, (((1,), (1,)), ((), ())), preferred_element_type=F32,
                             precision=lax.Precision.HIGHEST)
    ne, tm = logits.shape
    row = lax.broadcasted_iota(jnp.int32, (ne, tm), 0).astype(F32)
    m1 = jnp.max(logits, axis=0, keepdims=True)
    i1 = jnp.min(jnp.where(logits == m1, row, float(ne)), axis=0, keepdims=True)
    rest = jnp.where(row == i1, -jnp.inf, logits)
    m2 = jnp.max(rest, axis=0, keepdims=True)
    i2 = jnp.min(jnp.where(rest == m2, row, float(ne)), axis=0, keepdims=True)
    e2 = jnp.exp(m2 - m1)
    denom = 1.0 + e2
    ind = jnp.where((row == i1) | (row == i2), 1.0, 0.0)
    earlier = (lax.broadcasted_iota(jnp.int32, (tm, tm), 0) < lax.broadcasted_iota(jnp.int32, (tm, tm), 1))
    rank = jnp.dot(ind.astype(BF16), jnp.where(earlier, 1.0, 0.0).astype(BF16),
                   preferred_element_type=F32) + carry_ref[:, 0:1]
    r1 = jnp.sum(jnp.where(row == i1, rank, 0.0), axis=0, keepdims=True)
    r2 = jnp.sum(jnp.where(row == i2, rank, 0.0), axis=0, keepdims=True)
    carry_ref[...] = carry_ref[...] + jnp.sum(ind, axis=1, keepdims=True)
    pick = lambda vals: functools.reduce(
        lambda acc, kv: jnp.where(row == float(kv[0]), kv[1], acc), enumerate(vals), jnp.zeros((ne, tm), F32))
    sel_ref[...] = pick([i1, i2, r1, r2]).astype(jnp.int32)
    w_ref[...] = pick([1.0 / denom, e2 / denom])
    cnt_ref[...] = carry_ref[...].astype(jnp.int32)


def moe_route(x, g, w_router, tm=ROW_TILE):
    n, d = x.shape
    ne = w_router.shape[1]
    assert ne == SEL_ROWS
    return pl.pallas_call(
        _route_kernel,
        grid=(n // tm,),
        in_specs=[pl.BlockSpec((tm, d), lambda i: (i, 0)), pl.BlockSpec((1, d), lambda i: (0, 0)),
                  pl.BlockSpec((ne, d), lambda i: (0, 0))],
        out_specs=[pl.BlockSpec((ne, tm), lambda i: (0, i)), pl.BlockSpec((ne, tm), lambda i: (0, i)),
                   pl.BlockSpec((ne, 128), lambda i: (0, 0))],
        out_shape=[jax.ShapeDtypeStruct((ne, n), jnp.int32), jax.ShapeDtypeStruct((ne, n), F32),
                   jax.ShapeDtypeStruct((ne, 128), jnp.int32)],
        scratch_shapes=[pltpu.VMEM((ne, 128), F32)],
        compiler_params=_cparams(("arbitrary",)),
        name="moe_route",
    )(x, g.reshape(1, d), w_router.T)


def _slot_kernel(starts_ref, sel_ref, pos_ref):
    sel = sel_ref[...]
    start_of = lambda e_row: functools.reduce(
        lambda acc, e: jnp.where(e_row == e, starts_ref[e], acc), range(starts_ref.shape[0]),
        jnp.zeros_like(e_row))
    row = lax.broadcasted_iota(jnp.int32, sel.shape, 0)
    p1 = start_of(sel[0:1]) + sel[2:3]
    p2 = start_of(sel[1:2]) + sel[3:4]
    pos_ref[...] = jnp.where(row == 0, p1, jnp.where(row == 1, p2, 0))


def moe_slots(starts, sel, tm=2048):
    ne, n = sel.shape
    return pl.pallas_call(
        _slot_kernel,
        grid_spec=pltpu.PrefetchScalarGridSpec(
            num_scalar_prefetch=1, grid=(n // tm,),
            in_specs=[pl.BlockSpec((ne, tm), lambda i, s: (0, i))],
            out_specs=pl.BlockSpec((ne, tm), lambda i, s: (0, i))),
        out_shape=jax.ShapeDtypeStruct((ne, n), jnp.int32),
        compiler_params=_cparams(("parallel",)),
        name="moe_slots",
    )(starts, sel)


def _row_copy(src, src_row, dst, dst_row, sem):
    return pltpu.make_async_copy(src.at[pl.ds(src_row, 1)], dst.at[pl.ds(dst_row, 1)], sem)


def _dispatch_kernel(pos_ref, x_hbm, xs_in, xs_out, sem, *, n, tm):
    del xs_in
    base = pl.program_id(0) * tm

    def copies(t):
        return [_row_copy(x_hbm, base + t, xs_out, pos_ref[k * n + base + t], sem) for k in range(TOP_K)]

    def start(t, c):
        for cp in copies(t):
            cp.start()
        return c

    def wait(t, c):
        for cp in copies(t):
            cp.wait()
        return c

    lax.fori_loop(0, tm, start, 0, unroll=8)
    lax.fori_loop(0, tm, wait, 0, unroll=8)


def moe_dispatch(pos_flat, x, n_rows, tm=ROW_TILE):
    n, d = x.shape
    return pl.pallas_call(
        functools.partial(_dispatch_kernel, n=n, tm=tm),
        grid_spec=pltpu.PrefetchScalarGridSpec(
            num_scalar_prefetch=1, grid=(n // tm,),
            in_specs=[pl.BlockSpec(memory_space=pl.ANY), pl.BlockSpec(memory_space=pl.ANY)],
            out_specs=pl.BlockSpec(memory_space=pl.ANY),
            scratch_shapes=[pltpu.SemaphoreType.DMA]),
        out_shape=jax.ShapeDtypeStruct((n_rows, d), F32),
        input_output_aliases={2: 0},
        compiler_params=_cparams(("arbitrary",)),
        name="moe_dispatch",
    )(pos_flat, x, jnp.zeros((n_rows, d), F32))


def _moe_gu_kernel(te_ref, nu_ref, x_ref, g_ref, wg_ref, wu_ref, o_ref, xn_ref):
    i, j = pl.program_id(0), pl.program_id(1)

    @pl.when(i < nu_ref[0])
    def _():
        @pl.when(j == 0)
        def _():
            xn_ref[...] = _rms(x_ref[...], g_ref[...]).astype(BF16)

        xn = xn_ref[...]
        gate = jnp.dot(xn, wg_ref[...], preferred_element_type=F32)
        up = jnp.dot(xn, wu_ref[...], preferred_element_type=F32)
        o_ref[...] = (_silu(gate) * up).astype(o_ref.dtype)

    @pl.when(i >= nu_ref[0])
    def _():
        o_ref[...] = jnp.zeros_like(o_ref)


def moe_gate_up(tile_expert, n_used, xs, g, w_gu, tm=ROW_TILE, tn=512):
    p, d = xs.shape
    ff = w_gu.shape[2] // 2
    nj = ff // tn
    live = lambda i, nu: jnp.minimum(i, nu[0] - 1)
    col = lambda i, j, nu: jnp.where(i < nu[0], j, nj - 1)
    return pl.pallas_call(
        _moe_gu_kernel,
        grid_spec=pltpu.PrefetchScalarGridSpec(
            num_scalar_prefetch=2, grid=(p // tm, nj),
            in_specs=[pl.BlockSpec((tm, d), lambda i, j, te, nu: (live(i, nu), 0)),
                      pl.BlockSpec((1, d), lambda i, j, te, nu: (0, 0)),
                      pl.BlockSpec((None, d, tn), lambda i, j, te, nu: (te[i], 0, col(i, j, nu))),
                      pl.BlockSpec((None, d, tn), lambda i, j, te, nu: (te[i], 0, nj + col(i, j, nu)))],
            out_specs=pl.BlockSpec((tm, tn), lambda i, j, te, nu: (i, j)),
            scratch_shapes=[pltpu.VMEM((tm, d), BF16)]),
        out_shape=jax.ShapeDtypeStruct((p, ff), BF16),
        compiler_params=_cparams(("arbitrary", "arbitrary")),
        name="moe_gate_up",
    )(tile_expert, n_used, xs, g.reshape(1, d), w_gu, w_gu)


def _moe_down_kernel(te_ref, nu_ref, a_ref, w_ref, o_ref):
    @pl.when(pl.program_id(0) < nu_ref[0])
    def _():
        o_ref[...] = jnp.dot(a_ref[...], w_ref[...], preferred_element_type=F32)

    @pl.when(pl.program_id(0) >= nu_ref[0])
    def _():
        o_ref[...] = jnp.zeros_like(o_ref)


def moe_down(tile_expert, n_used, act, w_down, tm=ROW_TILE, tn=512):
    p, ff = act.shape
    d = w_down.shape[2]
    nj = d // tn
    live = lambda i, nu: jnp.minimum(i, nu[0] - 1)
    col = lambda i, j, nu: jnp.where(i < nu[0], j, nj - 1)
    return pl.pallas_call(
        _moe_down_kernel,
        grid_spec=pltpu.PrefetchScalarGridSpec(
            num_scalar_prefetch=2, grid=(p // tm, nj),
            in_specs=[pl.BlockSpec((tm, ff), lambda i, j, te, nu: (live(i, nu), 0)),
                      pl.BlockSpec((None, ff, tn), lambda i, j, te, nu: (te[i], 0, col(i, j, nu)))],
            out_specs=pl.BlockSpec((tm, tn), lambda i, j, te, nu: (i, j))),
        out_shape=jax.ShapeDtypeStruct((p, d), F32),
        compiler_params=_cparams(("arbitrary", "arbitrary")),
        name="moe_down",
    )(tile_expert, n_used, act, w_down)


def _combine_kernel(pos_ref, w_ref, x_ref, y_hbm, o_ref, buf_ref, sem, *, n, tm):
    base = pl.program_id(0) * tm

    def copies(t):
        return [_row_copy(y_hbm, pos_ref[k * n + base + t], buf_ref.at[k], t, sem) for k in range(TOP_K)]

    def start(t, c):
        for cp in copies(t):
            cp.start()
        return c

    def wait(t, c):
        for cp in copies(t):
            cp.wait()
        return c

    lax.fori_loop(0, tm, start, 0, unroll=8)
    diag = (lax.broadcasted_iota(jnp.int32, (tm, tm), 0) == lax.broadcasted_iota(jnp.int32, (tm, tm), 1))
    wcol = [jnp.sum(jnp.where(diag, w_ref[k:k + 1, :], 0.0), axis=1, keepdims=True) for k in range(TOP_K)]
    lax.fori_loop(0, tm, wait, 0, unroll=8)
    o_ref[...] = x_ref[...] + sum(wcol[k] * buf_ref[k] for k in range(TOP_K))


def moe_combine(pos_flat, w, x, y_sorted, tm=ROW_TILE):
    n, d = x.shape
    return pl.pallas_call(
        functools.partial(_combine_kernel, n=n, tm=tm),
        grid_spec=pltpu.PrefetchScalarGridSpec(
            num_scalar_prefetch=1, grid=(n // tm,),
            in_specs=[pl.BlockSpec((w.shape[0], tm), lambda i, pos: (0, i)),
                      pl.BlockSpec((tm, d), lambda i, pos: (i, 0)),
                      pl.BlockSpec(memory_space=pl.ANY)],
            out_specs=pl.BlockSpec((tm, d), lambda i, pos: (i, 0)),
            scratch_shapes=[pltpu.VMEM((TOP_K, tm, d), F32), pltpu.SemaphoreType.DMA]),
        out_shape=jax.ShapeDtypeStruct((n, d), F32),
        compiler_params=_cparams(("arbitrary",)),
        name="moe_combine",
    )(pos_flat, w, x, y_sorted)


def _compress_kernel(h_ref, pos_ref, w1_ref, w2_ref, o_ref):
    half = h_ref.shape[1]
    h = h_ref[...].astype(BF16)
    p_lo = jnp.dot(h, w1_ref[0:half, :], preferred_element_type=F32)
    p_hi = jnp.dot(h, w1_ref[half:, :], preferred_element_type=F32)
    nh = p_hi.shape[0]
    posb = jnp.dot(pos_ref[...].astype(BF16), w1_ref[...], preferred_element_type=F32)
    hid = p_lo + pltpu.roll(p_hi, nh - 1, 0) + posb
    o_ref[...] = jnp.dot(jax.nn.gelu(hid).astype(BF16), w2_ref[...],
                         preferred_element_type=F32).astype(o_ref.dtype)


def nsa_compress(halves, pos, w1, w2):
    bg, nh, hw = halves.shape
    hid = w1.shape[1]
    dh = w2.shape[1]
    return pl.pallas_call(
        _compress_kernel,
        grid=(bg,),
        in_specs=[pl.BlockSpec((None, nh, hw), lambda i: (i, 0, 0)),
                  pl.BlockSpec((1, 2 * hw), lambda i: (0, 0)),
                  pl.BlockSpec((2 * hw, hid), lambda i: (0, 0)),
                  pl.BlockSpec((hid, dh), lambda i: (0, 0))],
        out_specs=pl.BlockSpec((None, nh, dh), lambda i: (i, 0, 0)),
        out_shape=jax.ShapeDtypeStruct((bg, nh, dh), BF16),
        compiler_params=_cparams(("parallel",)),
    )(halves, pos.reshape(1, 2 * hw), w1, w2)


def _rel_bucket_np(dist):
    n = np.maximum(dist, 0)
    max_exact = N_BUCKETS // 2
    nf = np.maximum(n, 1).astype(np.float32)
    large = max_exact + (np.log(nf / max_exact) / np.float32(math.log(MAX_DISTANCE / max_exact))
                         * (N_BUCKETS - max_exact)).astype(np.int32)
    return np.where(n < max_exact, n, np.minimum(large, N_BUCKETS - 1)).astype(np.int32)


def _softmax_parts(s, mask):
    l = jnp.where(mask, s, NEG)
    m = jnp.max(l, axis=-1, keepdims=True)
    e = jnp.where(mask, jnp.exp(l - m), 0.0)
    return e, jnp.maximum(jnp.sum(e, axis=-1, keepdims=True), 1e-30)


def _nsa_kernel(q_ref, kc_ref, vc_ref, bc_ref, ks_ref, vs_ref, kw_ref, vw_ref, tz_ref, gates_ref,
                ovl_ref, exp_ref, o_ref, km_ref, *, n_cmp, n_slc, n_sel):
    r, tq, dh = q_ref.shape
    g = pl.program_id(1)
    qt = pl.program_id(2)
    t0 = pl.multiple_of(qt * tq, tq)
    q = q_ref[...].reshape(r * tq, dh)
    nt = (((1,), (1,)), ((), ()))

    ncp = kc_ref.shape[0]
    s_c = lax.dot_general(q, kc_ref[...], nt, preferred_element_type=F32).reshape(r, tq, ncp)
    s_c = s_c + bc_ref[...]
    qpos = t0 + lax.broadcasted_iota(jnp.int32, (tq, ncp), 0)
    cidx = lax.broadcasted_iota(jnp.int32, (tq, ncp), 1)
    mask_c = ((qpos - (cidx * CMP_STRIDE + CMP_BLOCK - 1)) >= 0) & (cidx < n_cmp)
    e_c, den_c = _softmax_parts(s_c, mask_c[None])
    p_c = e_c / den_c
    o_cmp = jnp.dot(p_c.reshape(r * tq, ncp).astype(BF16), vc_ref[...], preferred_element_type=F32)

    imp = jnp.dot(jnp.sum(p_c, axis=0), ovl_ref[...], preferred_element_type=F32,
                  precision=lax.Precision.HIGHEST)
    jblk = lax.broadcasted_iota(jnp.int32, (tq, n_slc), 1).astype(F32)
    qblk = ((t0 + lax.broadcasted_iota(jnp.int32, (tq, n_slc), 0)) // SLC_BLOCK).astype(F32)
    forced = (jblk == 0) | (jblk == qblk) | (jblk == qblk - 1)
    score = jnp.where(forced, 1e9, jnp.where(jblk <= qblk, imp, -1e9))
    sel = jnp.zeros((tq, n_slc), F32)
    for _ in range(n_sel):
        m = jnp.max(score, axis=1, keepdims=True)
        first = jnp.min(jnp.where(score == m, jblk, float(n_slc)), axis=1, keepdims=True)
        pick = jblk == first
        sel = jnp.where(pick, 1.0, sel)
        score = jnp.where(pick, -jnp.inf, score)
    km_ref[...] = jnp.dot(sel.astype(BF16), exp_ref[...], preferred_element_type=F32)

    near = tz_ref.shape[2]
    qi = lax.broadcasted_iota(jnp.int32, (tq, near), 0)
    kj = lax.broadcasted_iota(jnp.int32, (tq, near), 1)
    dist = qi + WINDOW - kj
    bias_near = tz_ref[...]
    near_rows = pl.ds(t0, near)

    s_w = lax.dot_general(q, kw_ref[near_rows, :], nt, preferred_element_type=F32).reshape(r, tq, near)
    mask_w = (dist >= 0) & (dist < WINDOW) & (t0 + kj >= WINDOW)
    e_w, den_w = _softmax_parts(s_w + bias_near, mask_w[None])
    o_win = jnp.dot(e_w.reshape(r * tq, near).astype(BF16), vw_ref[near_rows, :],
                    preferred_element_type=F32) / den_w.reshape(r * tq, 1)

    far_bias = tz_ref[:, :, 0:1]

    def far_chunk(f, carry):
        m_run, l_run, acc = carry
        start = pl.multiple_of(WINDOW + f * FAR_CHUNK, FAR_CHUNK)
        rows = pl.ds(start, FAR_CHUNK)
        s = lax.dot_general(q, ks_ref[rows, :], nt, preferred_element_type=F32).reshape(r, tq, FAR_CHUNK)
        kpos = start + lax.broadcasted_iota(jnp.int32, (tq, FAR_CHUNK), 1)
        mask = (km_ref[:, rows] > 0.5) & (kpos < t0)
        l = jnp.where(mask[None], s + far_bias, NEG)
        m_new = jnp.maximum(m_run, jnp.max(l, axis=-1, keepdims=True))
        alpha = jnp.exp(m_run - m_new)
        e = jnp.where(mask[None], jnp.exp(l - m_new), 0.0)
        l_new = alpha * l_run + jnp.sum(e, axis=-1, keepdims=True)
        pv = jnp.dot(e.reshape(r * tq, FAR_CHUNK).astype(BF16), vs_ref[rows, :],
                     preferred_element_type=F32)
        return m_new, l_new, alpha.reshape(r * tq, 1) * acc + pv

    n_far = (jnp.maximum(t0, WINDOW) - 1) // FAR_CHUNK
    init = (jnp.full((r, tq, 1), NEG, F32), jnp.zeros((r, tq, 1), F32), jnp.zeros((r * tq, dh), F32))
    m_run, l_run, acc = lax.fori_loop(0, n_far, far_chunk, init)

    s_s = lax.dot_general(q, ks_ref[near_rows, :], nt, preferred_element_type=F32).reshape(r, tq, near)
    mask_s = (km_ref[:, near_rows] > 0.5) & (dist >= 0)
    l = jnp.where(mask_s[None], s_s + bias_near, NEG)
    m_new = jnp.maximum(m_run, jnp.max(l, axis=-1, keepdims=True))
    alpha = jnp.exp(m_run - m_new)
    e = jnp.where(mask_s[None], jnp.exp(l - m_new), 0.0)
    den_s = jnp.maximum(alpha * l_run + jnp.sum(e, axis=-1, keepdims=True), 1e-30)
    acc = alpha.reshape(r * tq, 1) * acc + jnp.dot(
        e.reshape(r * tq, near).astype(BF16), vs_ref[near_rows, :], preferred_element_type=F32)
    o_slc = acc / den_s.reshape(r * tq, 1)

    gates = gates_ref[...]
    for h in range(r):
        rows = slice(h * tq, (h + 1) * tq)
        out_h = jnp.zeros((tq, dh), F32)
        for branch, o_b in enumerate((o_cmp, o_slc, o_win)):
            out_h = out_h + gates[:, branch * r + h: branch * r + h + 1] * o_b[rows, :]
        o_ref[h] = out_h.astype(o_ref.dtype)


def nsa_attention_core(q_t, k_cmp, v_cmp, bias_c, ks_pad, vs_pad, kw_pad, vw_pad, tz, gates,
                       overlap, expand, n_cmp):
    b, h, s, dh = q_t.shape
    gq = N_KV_GROUPS
    r = h // gq
    tq = Q_BLOCK
    ncp = k_cmp.shape[2]
    n_slc = s // SLC_BLOCK
    n_sel = min(N_SELECT, n_slc)
    spad = ks_pad.shape[2]
    near = tz.shape[2]
    kv_spec = lambda rows: pl.BlockSpec((None, None, rows, dh), lambda bi, gi, qi: (bi, gi, 0, 0))
    return pl.pallas_call(
        functools.partial(_nsa_kernel, n_cmp=n_cmp, n_slc=n_slc, n_sel=n_sel),
        grid=(b, gq, s // tq),
        in_specs=[pl.BlockSpec((None, r, tq, dh), lambda bi, gi, qi: (bi, gi, qi, 0)),
                  kv_spec(ncp), kv_spec(ncp),
                  pl.BlockSpec((r, tq, ncp), lambda bi, gi, qi: (gi, qi, 0)),
                  kv_spec(spad), kv_spec(spad), kv_spec(spad), kv_spec(spad),
                  pl.BlockSpec((r, tq, near), lambda bi, gi, qi: (gi, 0, 0)),
                  pl.BlockSpec((None, None, tq, gates.shape[3]), lambda bi, gi, qi: (bi, gi, qi, 0)),
                  pl.BlockSpec(overlap.shape, lambda bi, gi, qi: (0, 0)),
                  pl.BlockSpec(expand.shape, lambda bi, gi, qi: (0, 0))],
        out_specs=pl.BlockSpec((None, r, tq, dh), lambda bi, gi, qi: (bi, gi, qi, 0)),
        out_shape=jax.ShapeDtypeStruct((b, h, s, dh), BF16),
        scratch_shapes=[pltpu.VMEM((tq, spad), F32)],
        compiler_params=_cparams(("parallel", "parallel", "arbitrary")),
    )(q_t, k_cmp, v_cmp, bias_c, ks_pad, vs_pad, kw_pad, vw_pad, tz, gates, overlap, expand)


def _nsa_tables(rel_bias, seq, n_cmp, ncp):
    tq = Q_BLOCK
    qi = np.arange(tq)[:, None]
    dist_near = qi + WINDOW - np.arange(NEAR)[None, :]
    tz = rel_bias[_rel_bucket_np(dist_near)]
    s_idx = np.arange(seq)[:, None]
    dist_c = s_idx - (np.arange(ncp)[None, :] * CMP_STRIDE + CMP_BLOCK - 1)
    bias_c = rel_bias[_rel_bucket_np(dist_c)]
    n_slc = seq // SLC_BLOCK
    cs = np.arange(ncp)[:, None] * CMP_STRIDE
    ss = np.arange(n_slc)[None, :] * SLC_BLOCK
    overlap = ((cs < ss + SLC_BLOCK) & (cs + CMP_BLOCK > ss) & (np.arange(ncp)[:, None] < n_cmp))
    kpos = np.arange(WINDOW + seq)[None, :] - WINDOW
    expand = (kpos // SLC_BLOCK == np.arange(n_slc)[:, None]) & (kpos >= 0)
    return (jnp.transpose(tz, (2, 0, 1)), jnp.transpose(bias_c, (2, 0, 1)),
            jnp.asarray(overlap, F32), jnp.asarray(expand, BF16))


def _nsa_shared_kv(x, batch, seq, kv_norm, w_kv, cmp_pos_k, cmp_pos_v, k_w1, k_w2, v_w1, v_w2):
    n, d = x.shape
    gq, dh = N_KV_GROUPS, HEAD_DIM
    kv = norm_matmul(x, kv_norm, w_kv.astype(BF16), (0,), w_kv.shape[1], w_kv.shape[1],
                     lambda y: y, BF16)
    kv = kv.reshape(batch, seq, N_KV_BRANCH, gq, dh)
    per_group = lambda j: jnp.transpose(kv[:, :, j], (0, 2, 1, 3))
    n_cmp = (seq - CMP_BLOCK) // CMP_STRIDE + 1
    nhalf = seq // CMP_STRIDE

    def compress(t, pos, w1, w2):
        halves = t.reshape(batch * gq, nhalf, CMP_STRIDE * dh)
        return nsa_compress(halves, pos, w1.astype(BF16), w2.astype(BF16)).reshape(batch, gq, nhalf, dh)

    k_cmp = compress(per_group(0), cmp_pos_k, k_w1, k_w2)
    v_cmp = compress(per_group(1), cmp_pos_v, v_w1, v_w2)
    pad = lambda t: jnp.pad(t, ((0, 0), (0, 0), (WINDOW, 0), (0, 0)))
    return (k_cmp, v_cmp, pad(per_group(2)), pad(per_group(3)), pad(per_group(4)), pad(per_group(5)),
            n_cmp)


def _nsa_layer(x, batch, seq, g_mix, w_q, w_o, shared, tables):
    n, d = x.shape
    hd = N_HEADS * HEAD_DIM
    k_cmp, v_cmp, ks, vs, kw, vw, n_cmp = shared
    tz, bias_c, overlap, expand = tables
    w_q16 = w_q.astype(BF16)
    q = norm_matmul(x, g_mix, w_q16[:, :hd], (0,), hd, 512,
                    lambda y: y * (HEAD_DIM ** -0.5), BF16)
    n_gate = w_q.shape[1] - hd
    gates = norm_matmul(x, g_mix, w_q16[:, hd:], (0,), n_gate, n_gate, jax.nn.sigmoid, F32)
    q_t = jnp.transpose(q.reshape(batch, seq, N_HEADS, HEAD_DIM), (0, 2, 1, 3))
    gates = jnp.transpose(gates.reshape(batch, seq, 3, N_KV_GROUPS, GROUP), (0, 3, 1, 2, 4))
    o_t = nsa_attention_core(q_t, k_cmp, v_cmp, bias_c, ks, vs, kw, vw, tz,
                             gates.reshape(batch, N_KV_GROUPS, seq, 3 * GROUP), overlap, expand, n_cmp)
    o = jnp.transpose(o_t, (0, 2, 1, 3)).reshape(n, hd)
    return matmul_residual(o, w_o.astype(BF16), x)


def _swiglu_ffn(x, g, w_gu, w_down, res, scale=None, scale_col=None, tn=512):
    ff = w_down.shape[0]
    act = norm_matmul(x, g, w_gu, (0, ff // tn), ff, tn, lambda a, b: _silu(a) * b, BF16)
    return matmul_residual(act, w_down, res, scale, scale_col)


def _moe_ffn(x, g, w_router, w_gu, w_down, tm=ROW_TILE):
    n, d = x.shape
    ne = w_router.shape[1]
    sel, w, counts = moe_route(x, g, w_router)
    padded = (counts[:, 0] + tm - 1) // tm * tm
    ends = jnp.cumsum(padded)
    n_tiles = (TOP_K * n) // tm + ne
    n_used = (ends[-1] // tm).astype(jnp.int32).reshape(1)
    tile_expert = jnp.searchsorted(ends, jnp.arange(n_tiles, dtype=jnp.int32) * tm, side='right')
    tile_expert = jnp.minimum(tile_expert, tile_expert[n_used[0] - 1]).astype(jnp.int32)
    pos = moe_slots((ends - padded).astype(jnp.int32), sel)
    pos_flat = pos[:TOP_K].reshape(TOP_K * n)
    xs = moe_dispatch(pos_flat, x, n_tiles * tm)
    act = moe_gate_up(tile_expert, n_used, xs, g, w_gu.astype(BF16))
    y_sorted = moe_down(tile_expert, n_used, act, w_down.astype(BF16))
    return moe_combine(pos_flat, w, x, y_sorted)


def kernel(x, p, g_mix, g_ffn, g_ple, g_final, rg_w_in, rg_conv_w, rg_conv_b, rg_w_a, rg_b_a, rg_w_x,
           rg_b_x, rg_lambda, rg_w_out, kv_norm, w_kv, cmp_pos_k, cmp_pos_v, cmp_k_w1, cmp_k_w2,
           cmp_v_w1, cmp_v_w2, rel_bias, nsa_w_q, nsa_w_o, ffn_w_gu, ffn_w_down, moe_w_router,
           moe_w_gu, moe_w_down, ple_w_proj, ple_w_gate):
    batch, seq, d = x.shape
    depth = p.shape[0]
    n_a = rg_w_in.shape[0]
    n = batch * seq
    x = x.reshape(n, d)
    p = p.reshape(depth, n, p.shape[-1])
    n_cmp = (seq - CMP_BLOCK) // CMP_STRIDE + 1
    tables = _nsa_tables(rel_bias, seq, n_cmp, seq // CMP_STRIDE)
    shared = None
    for i in range(depth):
        if i < n_a:
            c = rg_w_in.shape[2] // 2
            proj = norm_matmul(x, g_mix[i], rg_w_in[i].astype(BF16), (0,), 2 * c, 512, lambda y: y, F32)
            hg = rg_lru_core(proj, batch, seq, rg_conv_w[i], rg_conv_b[i], _block_diag_chunks(rg_w_a[i]),
                             rg_b_a[i], _block_diag_chunks(rg_w_x[i]), rg_b_x[i], rg_lambda[i])
            x = matmul_residual(hg, rg_w_out[i].astype(BF16), x)
        else:
            b = i - n_a
            x = _nsa_layer(x, batch, seq, g_mix[i], nsa_w_q[b], nsa_w_o[b], shared, tables)
        if i % 2 == 0:
            x = _swiglu_ffn(x, g_ffn[i], ffn_w_gu[i // 2].astype(BF16), ffn_w_down[i // 2].astype(BF16), x)
        else:
            x = _moe_ffn(x, g_ffn[i], moe_w_router[i // 2], moe_w_gu[i // 2], moe_w_down[i // 2])
        x = ple_update(x, g_ple[i], ple_w_gate[i].astype(BF16), p[i], ple_w_proj[i].astype(BF16),
                       g_final if i == depth - 1 else None)
        if i == n_a - 1:
            shared = _nsa_shared_kv(x, batch, seq, kv_norm, w_kv, cmp_pos_k, cmp_pos_v,
                                    cmp_k_w1, cmp_k_w2, cmp_v_w1, cmp_v_w2)
    return x.reshape(batch, seq, d)
```

```python
import functools
import math

import numpy as np
import jax
import jax.numpy as jnp
from jax import lax
from jax.experimental import pallas as pl
from jax.experimental.pallas import tpu as pltpu

F32 = jnp.float32
BF16 = jnp.bfloat16

EPS = 1e-6
CONV_WIDTH = 4
LRU_BLOCKS = 16
LRU_C = 8.0
N_HEADS = 16
N_KV_GROUPS = 2
GROUP = N_HEADS // N_KV_GROUPS
HEAD_DIM = 64
N_KV_BRANCH = 6
CMP_BLOCK = 32
CMP_STRIDE = 16
SLC_BLOCK = 64
N_SELECT = 8
WINDOW = 512
Q_BLOCK = 128
N_BUCKETS = 32
MAX_DISTANCE = 128
TOP_K = 2
NEAR = WINDOW + Q_BLOCK
FAR_CHUNK = 512
NEG = -1e30

VMEM_LIMIT_V7X = 56 * 1024 * 1024
ROW_TILE = 512


def _cparams(sem):
    return pltpu.CompilerParams(dimension_semantics=sem, vmem_limit_bytes=VMEM_LIMIT_V7X)


def _rms(x, g):
    return x * lax.rsqrt(jnp.mean(x * x, axis=-1, keepdims=True) + EPS) * g


def _silu(x):
    return x * jax.nn.sigmoid(x)


def _norm_mm_kernel(x_ref, g_ref, *refs, n_w, epilogue):
    w_refs, o_ref, xn_ref = refs[:n_w], refs[n_w], refs[n_w + 1]

    @pl.when(pl.program_id(1) == 0)
    def _():
        xn_ref[...] = _rms(x_ref[...], g_ref[...]).astype(BF16)

    xn = xn_ref[...]
    outs = [jnp.dot(xn, w[...], preferred_element_type=F32) for w in w_refs]
    o_ref[...] = epilogue(*outs).astype(o_ref.dtype)


def norm_matmul(x, g, w, col_blocks, n_out, tn, epilogue, out_dtype, tm=ROW_TILE):
    n, d = x.shape
    grid = (n // tm, n_out // tn)
    w_specs = [pl.BlockSpec((d, tn), functools.partial(lambda i, j, cb: (0, cb + j), cb=cb))
               for cb in col_blocks]
    return pl.pallas_call(
        functools.partial(_norm_mm_kernel, n_w=len(col_blocks), epilogue=epilogue),
        grid=grid,
        in_specs=[pl.BlockSpec((tm, d), lambda i, j: (i, 0)),
                  pl.BlockSpec((1, d), lambda i, j: (0, 0))] + w_specs,
        out_specs=pl.BlockSpec((tm, tn), lambda i, j: (i, j)),
        out_shape=jax.ShapeDtypeStruct((n, n_out), out_dtype),
        scratch_shapes=[pltpu.VMEM((tm, d), BF16)],
        compiler_params=_cparams(("parallel", "arbitrary")),
    )(x, g.reshape(1, d), *([w] * len(col_blocks)))


def _mm_res_kernel(a_ref, w_ref, res_ref, *refs, scale_col):
    if scale_col is None:
        (o_ref,) = refs
        y = jnp.dot(a_ref[...], w_ref[...], preferred_element_type=F32)
    else:
        s_ref, o_ref = refs
        y = jnp.dot(a_ref[...], w_ref[...], preferred_element_type=F32)
        y = y * s_ref[:, scale_col:scale_col + 1]
    o_ref[...] = res_ref[...] + y


def matmul_residual(a, w, res, scale=None, scale_col=None, tm=ROW_TILE, tn=1024):
    n, k = a.shape
    d = w.shape[1]
    in_specs = [pl.BlockSpec((tm, k), lambda i, j: (i, 0)),
                pl.BlockSpec((k, tn), lambda i, j: (0, j)),
                pl.BlockSpec((tm, tn), lambda i, j: (i, j))]
    args = [a, w, res]
    if scale is not None:
        in_specs.append(pl.BlockSpec((tm, scale.shape[1]), lambda i, j: (i, 0)))
        args.append(scale)
    return pl.pallas_call(
        functools.partial(_mm_res_kernel, scale_col=scale_col),
        grid=(n // tm, d // tn),
        in_specs=in_specs,
        out_specs=pl.BlockSpec((tm, tn), lambda i, j: (i, j)),
        out_shape=jax.ShapeDtypeStruct((n, d), F32),
        compiler_params=_cparams(("parallel", "arbitrary")),
    )(*args)


BD = 256
SCAN_ROWS = 8


def _rg_kernel(gate_ref, xr_ref, cw_ref, cb_ref, wa_ref, ba_ref, wx_ref, bx_ref, lam_ref,
               o_ref, xe_ref, h_ref, a_ref, u_ref):
    t, c = xr_ref.shape

    @pl.when(pl.program_id(1) == 0)
    def _():
        xe_ref[0:8, :] = jnp.zeros((8, c), F32)
        h_ref[...] = jnp.zeros_like(h_ref)

    xe_ref[8:, :] = xr_ref[...]
    xc = cb_ref[...] + sum(
        cw_ref[k:k + 1, :] * xe_ref[8 - (CONV_WIDTH - 1) + k: 8 - (CONV_WIDTH - 1) + k + t, :]
        for k in range(CONV_WIDTH))
    xe_ref[0:8, :] = xr_ref[t - 8:, :]

    lam = -lam_ref[...]
    softplus_neg_lam = jnp.maximum(lam, 0.0) + jnp.log1p(jnp.exp(-jnp.abs(lam)))
    for cblk in range(c // BD):
        sl = slice(cblk * BD, (cblk + 1) * BD)
        xb = xc[:, sl]
        xb16 = xb.astype(BF16)
        r = jax.nn.sigmoid(jnp.dot(xb16, wa_ref[cblk], preferred_element_type=F32) + ba_ref[:, sl])
        i = jax.nn.sigmoid(jnp.dot(xb16, wx_ref[cblk], preferred_element_type=F32) + bx_ref[:, sl])
        log_a = -LRU_C * r * softplus_neg_lam[:, sl]
        a = jnp.exp(log_a)
        a_ref[:, sl] = a
        u_ref[:, sl] = jnp.sqrt(1.0 - a * a) * (i * xb)

    row = lax.broadcasted_iota(jnp.int32, (SCAN_ROWS, c), 0)

    def scan_tile(k, h_prev):
        rows = pl.ds(pl.multiple_of(k * SCAN_ROWS, SCAN_ROWS), SCAN_ROWS)
        a = a_ref[rows, :]
        u = u_ref[rows, :]
        for d in (1, 2, 4):
            a_s = jnp.where(row >= d, pltpu.roll(a, d, 0), 1.0)
            u_s = jnp.where(row >= d, pltpu.roll(u, d, 0), 0.0)
            u = a * u_s + u
            a = a * a_s
        h = a * h_prev + u
        u_ref[rows, :] = h
        return jnp.broadcast_to(h[SCAN_ROWS - 1:SCAN_ROWS, :], (SCAN_ROWS, c))

    h_ref[...] = lax.fori_loop(0, t // SCAN_ROWS, scan_tile, h_ref[...])
    o_ref[...] = (jax.nn.gelu(gate_ref[...]) * u_ref[...]).astype(o_ref.dtype)


def rg_lru_core(proj, batch, seq, conv_w, conv_b, wa_bd, b_a, wx_bd, b_x, lam, t_chunk=512):
    n, c2 = proj.shape
    c = c2 // 2
    nt = seq // t_chunk
    vec = lambda v: v.reshape(1, c)
    row_spec = lambda col: pl.BlockSpec((t_chunk, c), lambda b, t: (b * nt + t, col))
    const = lambda shape: pl.BlockSpec(shape, lambda b, t: (0,) * len(shape))
    return pl.pallas_call(
        _rg_kernel,
        grid=(batch, nt),
        in_specs=[row_spec(0), row_spec(1), const((CONV_WIDTH, c)), const((1, c)),
                  const(wa_bd.shape), const((1, c)), const(wx_bd.shape), const((1, c)), const((1, c))],
        out_specs=pl.BlockSpec((t_chunk, c), lambda b, t: (b * nt + t, 0)),
        out_shape=jax.ShapeDtypeStruct((n, c), BF16),
        scratch_shapes=[pltpu.VMEM((t_chunk + 8, c), F32), pltpu.VMEM((SCAN_ROWS, c), F32),
                        pltpu.VMEM((t_chunk, c), F32), pltpu.VMEM((t_chunk, c), F32)],
        compiler_params=_cparams(("parallel", "arbitrary")),
    )(proj, proj, conv_w, vec(conv_b), wa_bd, vec(b_a), wx_bd, vec(b_x), vec(lam))


def _block_diag_chunks(w):
    nb, bw, _ = w.shape
    per = BD // bw
    w = w.reshape(nb // per, per, bw, bw)
    eye = jnp.eye(per, dtype=w.dtype)
    out = jnp.einsum('cpij,pq->cpiqj', w, eye).reshape(nb // per, BD, BD)
    return out.astype(BF16)


def _ple_kernel(x_ref, g_ref, wg_ref, p_ref, wp_ref, *refs, final_norm):
    if final_norm:
        gf_ref, o_ref = refs
    else:
        (o_ref,) = refs
    x = x_ref[...]
    xn = _rms(x, g_ref[...]).astype(BF16)
    gate = jax.nn.sigmoid(jnp.dot(xn, wg_ref[...], preferred_element_type=F32))
    proj = jnp.dot(p_ref[...].astype(BF16), wp_ref[...], preferred_element_type=F32)
    y = x + gate * proj
    if final_norm:
        y = _rms(y, gf_ref[...])
    o_ref[...] = y


def ple_update(x, g, wg, p, wp, g_final=None, tm=ROW_TILE):
    n, d = x.shape
    dp = p.shape[1]
    const = lambda shape: pl.BlockSpec(shape, lambda i: (0,) * len(shape))
    in_specs = [pl.BlockSpec((tm, d), lambda i: (i, 0)), const((1, d)), const((d, d)),
                pl.BlockSpec((tm, dp), lambda i: (i, 0)), const((dp, d))]
    args = [x, g.reshape(1, d), wg, p, wp]
    if g_final is not None:
        in_specs.append(const((1, d)))
        args.append(g_final.reshape(1, d))
    return pl.pallas_call(
        functools.partial(_ple_kernel, final_norm=g_final is not None),
        grid=(n // tm,),
        in_specs=in_specs,
        out_specs=pl.BlockSpec((tm, d), lambda i: (i, 0)),
        out_shape=jax.ShapeDtypeStruct((n, d), F32),
        compiler_params=_cparams(("parallel",)),
    )(*args)


SEL_ROWS = 8


def _route_kernel(x_ref, g_ref, wt_ref, sel_ref, w_ref, cnt_ref, carry_ref):
    @pl.when(pl.program_id(0) == 0)
    def _():
        carry_ref[...] = jnp.zeros_like(carry_ref)

    xn = _rms(x_ref[...], g_ref[...])
    logits = lax.dot_general(wt_ref[...], xn, (((1,), (1,)), ((), ())), preferred_element_type=F32,
                             precision=lax.Precision.HIGHEST)
    ne, tm = logits.shape
    row = lax.broadcasted_iota(jnp.int32, (ne, tm), 0).astype(F32)
    m1 = jnp.max(logits, axis=0, keepdims=True)
    i1 = jnp.min(jnp.where(logits == m1, row, float(ne)), axis=0, keepdims=True)
    rest = jnp.where(row == i1, -jnp.inf, logits)
    m2 = jnp.max(rest, axis=0, keepdims=True)
    i2 = jnp.min(jnp.where(rest == m2, row, float(ne)), axis=0, keepdims=True)
    e2 = jnp.exp(m2 - m1)
    denom = 1.0 + e2
    ind = jnp.where((row == i1) | (row == i2), 1.0, 0.0)
    earlier = (lax.broadcasted_iota(jnp.int32, (tm, tm), 0) < lax.broadcasted_iota(jnp.int32, (tm, tm), 1))
    rank = jnp.dot(ind.astype(BF16), jnp.where(earlier, 1.0, 0.0).astype(BF16),
                   preferred_element_type=F32) + carry_ref[:, 0:1]
    r1 = jnp.sum(jnp.where(row == i1, rank, 0.0), axis=0, keepdims=True)
    r2 = jnp.sum(jnp.where(row == i2, rank, 0.0), axis=0, keepdims=True)
    carry_ref[...] = carry_ref[...] + jnp.sum(ind, axis=1, keepdims=True)
    pick = lambda vals: functools.reduce(
        lambda acc, kv: jnp.where(row == float(kv[0]), kv[1], acc), enumerate(vals), jnp.zeros((ne, tm), F32))
    sel_ref[...] = pick([i1, i2, r1, r2]).astype(jnp.int32)
    w_ref[...] = pick([1.0 / denom, e2 / denom])
    cnt_ref[...] = carry_ref[...].astype(jnp.int32)


def moe_route(x, g, w_router, tm=ROW_TILE):
    n, d = x.shape
    ne = w_router.shape[1]
    assert ne == SEL_ROWS
    return pl.pallas_call(
        _route_kernel,
        grid=(n // tm,),
        in_specs=[pl.BlockSpec((tm, d), lambda i: (i, 0)), pl.BlockSpec((1, d), lambda i: (0, 0)),
                  pl.BlockSpec((ne, d), lambda i: (0, 0))],
        out_specs=[pl.BlockSpec((ne, tm), lambda i: (0, i)), pl.BlockSpec((ne, tm), lambda i: (0, i)),
                   pl.BlockSpec((ne, 128), lambda i: (0, 0))],
        out_shape=[jax.ShapeDtypeStruct((ne, n), jnp.int32), jax.ShapeDtypeStruct((ne, n), F32),
                   jax.ShapeDtypeStruct((ne, 128), jnp.int32)],
        scratch_shapes=[pltpu.VMEM((ne, 128), F32)],
        compiler_params=_cparams(("arbitrary",)),
        name="moe_route",
    )(x, g.reshape(1, d), w_router.T)


def _slot_kernel(starts_ref, sel_ref, pos_ref):
    sel = sel_ref[...]
    start_of = lambda e_row: functools.reduce(
        lambda acc, e: jnp.where(e_row == e, starts_ref[e], acc), range(starts_ref.shape[0]),
        jnp.zeros_like(e_row))
    row = lax.broadcasted_iota(jnp.int32, sel.shape, 0)
    p1 = start_of(sel[0:1]) + sel[2:3]
    p2 = start_of(sel[1:2]) + sel[3:4]
    pos_ref[...] = jnp.where(row == 0, p1, jnp.where(row == 1, p2, 0))


def moe_slots(starts, sel, tm=2048):
    ne, n = sel.shape
    return pl.pallas_call(
        _slot_kernel,
        grid_spec=pltpu.PrefetchScalarGridSpec(
            num_scalar_prefetch=1, grid=(n // tm,),
            in_specs=[pl.BlockSpec((ne, tm), lambda i, s: (0, i))],
            out_specs=pl.BlockSpec((ne, tm), lambda i, s: (0, i))),
        out_shape=jax.ShapeDtypeStruct((ne, n), jnp.int32),
        compiler_params=_cparams(("parallel",)),
        name="moe_slots",
    )(starts, sel)


def _row_copy(src, src_row, dst, dst_row, sem):
    return pltpu.make_async_copy(src.at[pl.ds(src_row, 1)], dst.at[pl.ds(dst_row, 1)], sem)


def _inverse_kernel(pos_ref, inv_ref, *, n):
    def clear(s, c):
        inv_ref[s] = 0
        return c

    def fill(t, c):
        for k in range(TOP_K):
            inv_ref[pos_ref[k * n + t]] = t
        return c

    lax.fori_loop(0, inv_ref.shape[0], clear, 0, unroll=8)
    lax.fori_loop(0, n, fill, 0, unroll=8)


def moe_inverse(pos_flat, n, n_rows):
    return pl.pallas_call(
        functools.partial(_inverse_kernel, n=n),
        in_specs=[pl.BlockSpec(memory_space=pltpu.SMEM)],
        out_specs=pl.BlockSpec(memory_space=pltpu.SMEM),
        out_shape=jax.ShapeDtypeStruct((n_rows,), jnp.int32),
        name="moe_inverse",
    )(pos_flat)


def _experts_kernel(te_ref, nu_ref, inv_ref, x_hbm, g_ref, wgu_ref, wd_ref, o_ref, xbuf_ref, act_ref, sems,
                    *, tm, tn):
    del te_ref
    i = pl.program_id(0)
    n_used = nu_ref[0]
    ff = wd_ref.shape[0]

    def gather(tile, slot, op):
        def body(t, c):
            cp = _row_copy(x_hbm, inv_ref[tile * tm + t], xbuf_ref.at[slot], t, sems.at[slot])
            cp.start() if op == "start" else cp.wait()
            return c
        lax.fori_loop(0, tm, body, 0, unroll=8)

    @pl.when(i < n_used)
    def _():
        slot = lax.rem(i, 2)

        @pl.when(i == 0)
        def _():
            gather(0, 0, "start")

        @pl.when(i + 1 < n_used)
        def _():
            gather(i + 1, 1 - slot, "start")

        gather(i, slot, "wait")
        xn = _rms(xbuf_ref[slot], g_ref[...]).astype(BF16)
        for c in range(ff // tn):
            gate = jnp.dot(xn, wgu_ref[:, c * tn:(c + 1) * tn], preferred_element_type=F32)
            up = jnp.dot(xn, wgu_ref[:, ff + c * tn:ff + (c + 1) * tn], preferred_element_type=F32)
            act_ref[:, c * tn:(c + 1) * tn] = (_silu(gate) * up).astype(BF16)
        o_ref[...] = jnp.dot(act_ref[...], wd_ref[...], preferred_element_type=F32)

    @pl.when(i >= n_used)
    def _():
        o_ref[...] = jnp.zeros_like(o_ref)


def moe_experts(tile_expert, n_used, inv, x, g, w_gu, w_down, tm=ROW_TILE, tn=512):
    n, d = x.shape
    p = inv.shape[0]
    ff = w_down.shape[1]
    whole = lambda shape: pl.BlockSpec((None,) + shape, lambda i, te, nu, inv: (te[i], 0, 0),
                                       pipeline_mode=pl.Buffered(1))
    return pl.pallas_call(
        functools.partial(_experts_kernel, tm=tm, tn=min(tn, ff)),
        grid_spec=pltpu.PrefetchScalarGridSpec(
            num_scalar_prefetch=3, grid=(p // tm,),
            in_specs=[pl.BlockSpec(memory_space=pl.ANY),
                      pl.BlockSpec((1, d), lambda i, te, nu, inv: (0, 0)),
                      whole((d, 2 * ff)), whole((ff, d))],
            out_specs=pl.BlockSpec((tm, d), lambda i, te, nu, inv: (i, 0)),
            scratch_shapes=[pltpu.VMEM((2, tm, d), F32), pltpu.VMEM((tm, ff), BF16),
                            pltpu.SemaphoreType.DMA((2,))]),
        out_shape=jax.ShapeDtypeStruct((p, d), F32),
        compiler_params=_cparams(("arbitrary",)),
        name="moe_experts",
    )(tile_expert, n_used, inv, x, g.reshape(1, d), w_gu, w_down)


def _combine_kernel(pos_ref, w_ref, x_ref, y_hbm, o_ref, buf_ref, sem, *, n, tm):
    base = pl.program_id(0) * tm

    def copies(t):
        return [_row_copy(y_hbm, pos_ref[k * n + base + t], buf_ref.at[k], t, sem) for k in range(TOP_K)]

    def start(t, c):
        for cp in copies(t):
            cp.start()
        return c

    def wait(t, c):
        for cp in copies(t):
            cp.wait()
        return c

    lax.fori_loop(0, tm, start, 0, unroll=8)
    diag = (lax.broadcasted_iota(jnp.int32, (tm, tm), 0) == lax.broadcasted_iota(jnp.int32, (tm, tm), 1))
    wcol = [jnp.sum(jnp.where(diag, w_ref[k:k + 1, :], 0.0), axis=1, keepdims=True) for k in range(TOP_K)]
    lax.fori_loop(0, tm, wait, 0, unroll=8)
    o_ref[...] = x_ref[...] + sum(wcol[k] * buf_ref[k] for k in range(TOP_K))


def moe_combine(pos_flat, w, x, y_sorted, tm=ROW_TILE):
    n, d = x.shape
    return pl.pallas_call(
        functools.partial(_combine_kernel, n=n, tm=tm),
        grid_spec=pltpu.PrefetchScalarGridSpec(
            num_scalar_prefetch=1, grid=(n // tm,),
            in_specs=[pl.BlockSpec((w.shape[0], tm), lambda i, pos: (0, i)),
                      pl.BlockSpec((tm, d), lambda i, pos: (i, 0)),
                      pl.BlockSpec(memory_space=pl.ANY)],
            out_specs=pl.BlockSpec((tm, d), lambda i, pos: (i, 0)),
            scratch_shapes=[pltpu.VMEM((TOP_K, tm, d), F32), pltpu.SemaphoreType.DMA]),
        out_shape=jax.ShapeDtypeStruct((n, d), F32),
        compiler_params=_cparams(("arbitrary",)),
        name="moe_combine",
    )(pos_flat, w, x, y_sorted)


def _compress_kernel(h_ref, pos_ref, w1_ref, w2_ref, o_ref):
    half = h_ref.shape[1]
    h = h_ref[...].astype(BF16)
    p_lo = jnp.dot(h, w1_ref[0:half, :], preferred_element_type=F32)
    p_hi = jnp.dot(h, w1_ref[half:, :], preferred_element_type=F32)
    nh = p_hi.shape[0]
    posb = jnp.dot(pos_ref[...].astype(BF16), w1_ref[...], preferred_element_type=F32)
    hid = p_lo + pltpu.roll(p_hi, nh - 1, 0) + posb
    o_ref[...] = jnp.dot(jax.nn.gelu(hid).astype(BF16), w2_ref[...],
                         preferred_element_type=F32).astype(o_ref.dtype)


def nsa_compress(halves, pos, w1, w2):
    bg, nh, hw = halves.shape
    hid = w1.shape[1]
    dh = w2.shape[1]
    return pl.pallas_call(
        _compress_kernel,
        grid=(bg,),
        in_specs=[pl.BlockSpec((None, nh, hw), lambda i: (i, 0, 0)),
                  pl.BlockSpec((1, 2 * hw), lambda i: (0, 0)),
                  pl.BlockSpec((2 * hw, hid), lambda i: (0, 0)),
                  pl.BlockSpec((hid, dh), lambda i: (0, 0))],
        out_specs=pl.BlockSpec((None, nh, dh), lambda i: (i, 0, 0)),
        out_shape=jax.ShapeDtypeStruct((bg, nh, dh), BF16),
        compiler_params=_cparams(("parallel",)),
    )(halves, pos.reshape(1, 2 * hw), w1, w2)


def _rel_bucket_np(dist):
    n = np.maximum(dist, 0)
    max_exact = N_BUCKETS // 2
    nf = np.maximum(n, 1).astype(np.float32)
    large = max_exact + (np.log(nf / max_exact) / np.float32(math.log(MAX_DISTANCE / max_exact))
                         * (N_BUCKETS - max_exact)).astype(np.int32)
    return np.where(n < max_exact, n, np.minimum(large, N_BUCKETS - 1)).astype(np.int32)


def _softmax_parts(s, mask):
    l = jnp.where(mask, s, NEG)
    m = jnp.max(l, axis=-1, keepdims=True)
    e = jnp.where(mask, jnp.exp(l - m), 0.0)
    return e, jnp.maximum(jnp.sum(e, axis=-1, keepdims=True), 1e-30)


def _nsa_kernel(q_ref, kc_ref, vc_ref, bc_ref, ks_ref, vs_ref, kw_ref, vw_ref, tz_ref, gates_ref,
                ovl_ref, exp_ref, o_ref, km_ref, *, n_cmp, n_slc, n_sel):
    r, tq, dh = q_ref.shape
    g = pl.program_id(1)
    qt = pl.program_id(2)
    t0 = pl.multiple_of(qt * tq, tq)
    q = q_ref[...].reshape(r * tq, dh)
    nt = (((1,), (1,)), ((), ()))

    ncp = kc_ref.shape[0]
    s_c = lax.dot_general(q, kc_ref[...], nt, preferred_element_type=F32).reshape(r, tq, ncp)
    s_c = s_c + bc_ref[...]
    qpos = t0 + lax.broadcasted_iota(jnp.int32, (tq, ncp), 0)
    cidx = lax.broadcasted_iota(jnp.int32, (tq, ncp), 1)
    mask_c = ((qpos - (cidx * CMP_STRIDE + CMP_BLOCK - 1)) >= 0) & (cidx < n_cmp)
    e_c, den_c = _softmax_parts(s_c, mask_c[None])
    p_c = e_c / den_c
    o_cmp = jnp.dot(p_c.reshape(r * tq, ncp).astype(BF16), vc_ref[...], preferred_element_type=F32)

    imp = jnp.dot(jnp.sum(p_c, axis=0), ovl_ref[...], preferred_element_type=F32,
                  precision=lax.Precision.HIGHEST)
    jblk = lax.broadcasted_iota(jnp.int32, (tq, n_slc), 1).astype(F32)
    qblk = ((t0 + lax.broadcasted_iota(jnp.int32, (tq, n_slc), 0)) // SLC_BLOCK).astype(F32)
    forced = (jblk == 0) | (jblk == qblk) | (jblk == qblk - 1)
    score = jnp.where(forced, 1e9, jnp.where(jblk <= qblk, imp, -1e9))
    sel = jnp.zeros((tq, n_slc), F32)
    for _ in range(n_sel):
        m = jnp.max(score, axis=1, keepdims=True)
        first = jnp.min(jnp.where(score == m, jblk, float(n_slc)), axis=1, keepdims=True)
        pick = jblk == first
        sel = jnp.where(pick, 1.0, sel)
        score = jnp.where(pick, -jnp.inf, score)
    km_ref[...] = jnp.dot(sel.astype(BF16), exp_ref[...], preferred_element_type=F32)

    near = tz_ref.shape[2]
    qi = lax.broadcasted_iota(jnp.int32, (tq, near), 0)
    kj = lax.broadcasted_iota(jnp.int32, (tq, near), 1)
    dist = qi + WINDOW - kj
    bias_near = tz_ref[...]
    near_rows = pl.ds(t0, near)

    s_w = lax.dot_general(q, kw_ref[near_rows, :], nt, preferred_element_type=F32).reshape(r, tq, near)
    mask_w = (dist >= 0) & (dist < WINDOW) & (t0 + kj >= WINDOW)
    e_w, den_w = _softmax_parts(s_w + bias_near, mask_w[None])
    o_win = jnp.dot(e_w.reshape(r * tq, near).astype(BF16), vw_ref[near_rows, :],
                    preferred_element_type=F32) / den_w.reshape(r * tq, 1)

    far_bias = tz_ref[:, :, 0:1]

    def far_chunk(f, carry):
        m_run, l_run, acc = carry
        start = pl.multiple_of(WINDOW + f * FAR_CHUNK, FAR_CHUNK)
        rows = pl.ds(start, FAR_CHUNK)
        s = lax.dot_general(q, ks_ref[rows, :], nt, preferred_element_type=F32).reshape(r, tq, FAR_CHUNK)
        kpos = start + lax.broadcasted_iota(jnp.int32, (tq, FAR_CHUNK), 1)
        mask = (km_ref[:, rows] > 0.5) & (kpos < t0)
        l = jnp.where(mask[None], s + far_bias, NEG)
        m_new = jnp.maximum(m_run, jnp.max(l, axis=-1, keepdims=True))
        alpha = jnp.exp(m_run - m_new)
        e = jnp.where(mask[None], jnp.exp(l - m_new), 0.0)
        l_new = alpha * l_run + jnp.sum(e, axis=-1, keepdims=True)
        pv = jnp.dot(e.reshape(r * tq, FAR_CHUNK).astype(BF16), vs_ref[rows, :],
                     preferred_element_type=F32)
        return m_new, l_new, alpha.reshape(r * tq, 1) * acc + pv

    n_far = (jnp.maximum(t0, WINDOW) - 1) // FAR_CHUNK
    init = (jnp.full((r, tq, 1), NEG, F32), jnp.zeros((r, tq, 1), F32), jnp.zeros((r * tq, dh), F32))
    m_run, l_run, acc = lax.fori_loop(0, n_far, far_chunk, init)

    s_s = lax.dot_general(q, ks_ref[near_rows, :], nt, preferred_element_type=F32).reshape(r, tq, near)
    mask_s = (km_ref[:, near_rows] > 0.5) & (dist >= 0)
    l = jnp.where(mask_s[None], s_s + bias_near, NEG)
    m_new = jnp.maximum(m_run, jnp.max(l, axis=-1, keepdims=True))
    alpha = jnp.exp(m_run - m_new)
    e = jnp.where(mask_s[None], jnp.exp(l - m_new), 0.0)
    den_s = jnp.maximum(alpha * l_run + jnp.sum(e, axis=-1, keepdims=True), 1e-30)
    acc = alpha.reshape(r * tq, 1) * acc + jnp.dot(
        e.reshape(r * tq, near).astype(BF16), vs_ref[near_rows, :], preferred_element_type=F32)
    o_slc = acc / den_s.reshape(r * tq, 1)

    gates = gates_ref[...]
    for h in range(r):
        rows = slice(h * tq, (h + 1) * tq)
        out_h = jnp.zeros((tq, dh), F32)
        for branch, o_b in enumerate((o_cmp, o_slc, o_win)):
            out_h = out_h + gates[:, branch * r + h: branch * r + h + 1] * o_b[rows, :]
        o_ref[h] = out_h.astype(o_ref.dtype)


def nsa_attention_core(q_t, k_cmp, v_cmp, bias_c, ks_pad, vs_pad, kw_pad, vw_pad, tz, gates,
                       overlap, expand, n_cmp):
    b, h, s, dh = q_t.shape
    gq = N_KV_GROUPS
    r = h // gq
    tq = Q_BLOCK
    ncp = k_cmp.shape[2]
    n_slc = s // SLC_BLOCK
    n_sel = min(N_SELECT, n_slc)
    spad = ks_pad.shape[2]
    near = tz.shape[2]
    kv_spec = lambda rows: pl.BlockSpec((None, None, rows, dh), lambda bi, gi, qi: (bi, gi, 0, 0))
    return pl.pallas_call(
        functools.partial(_nsa_kernel, n_cmp=n_cmp, n_slc=n_slc, n_sel=n_sel),
        grid=(b, gq, s // tq),
        in_specs=[pl.BlockSpec((None, r, tq, dh), lambda bi, gi, qi: (bi, gi, qi, 0)),
                  kv_spec(ncp), kv_spec(ncp),
                  pl.BlockSpec((r, tq, ncp), lambda bi, gi, qi: (gi, qi, 0)),
                  kv_spec(spad), kv_spec(spad), kv_spec(spad), kv_spec(spad),
                  pl.BlockSpec((r, tq, near), lambda bi, gi, qi: (gi, 0, 0)),
                  pl.BlockSpec((None, None, tq, gates.shape[3]), lambda bi, gi, qi: (bi, gi, qi, 0)),
                  pl.BlockSpec(overlap.shape, lambda bi, gi, qi: (0, 0)),
                  pl.BlockSpec(expand.shape, lambda bi, gi, qi: (0, 0))],
        out_specs=pl.BlockSpec((None, r, tq, dh), lambda bi, gi, qi: (bi, gi, qi, 0)),
        out_shape=jax.ShapeDtypeStruct((b, h, s, dh), BF16),
        scratch_shapes=[pltpu.VMEM((tq, spad), F32)],
        compiler_params=_cparams(("parallel", "parallel", "arbitrary")),
    )(q_t, k_cmp, v_cmp, bias_c, ks_pad, vs_pad, kw_pad, vw_pad, tz, gates, overlap, expand)


def _bias_table_kernel(rb_ref, bkt_near_ref, bkt_cmp_ref, tz_ref, bc_ref, *, rows):
    h = pl.program_id(0)

    def lookup(bkt):
        acc = jnp.zeros(bkt.shape, F32)
        for k in range(N_BUCKETS):
            acc = jnp.where(bkt == k, rb_ref[k * N_HEADS + h], acc)
        return acc

    def chunk(src, dst):
        def body(c, carry):
            sl = pl.ds(pl.multiple_of(c * rows, rows), rows)
            dst[sl, :] = lookup(src[sl, :])
            return carry
        lax.fori_loop(0, src.shape[0] // rows, body, 0)

    chunk(bkt_near_ref, tz_ref)
    chunk(bkt_cmp_ref, bc_ref)


def _nsa_tables(rel_bias, seq, n_cmp, ncp):
    tq = Q_BLOCK
    dist_near = np.arange(tq)[:, None] + WINDOW - np.arange(NEAR)[None, :]
    dist_c = np.arange(seq)[:, None] - (np.arange(ncp)[None, :] * CMP_STRIDE + CMP_BLOCK - 1)
    bkt_near = jnp.asarray(_rel_bucket_np(dist_near))
    bkt_c = jnp.asarray(_rel_bucket_np(dist_c))
    rows = 32
    tz, bias_c = pl.pallas_call(
        functools.partial(_bias_table_kernel, rows=rows),
        grid_spec=pltpu.PrefetchScalarGridSpec(
            num_scalar_prefetch=1, grid=(N_HEADS,),
            in_specs=[pl.BlockSpec((tq, NEAR), lambda h, rb: (0, 0)),
                      pl.BlockSpec((seq, ncp), lambda h, rb: (0, 0))],
            out_specs=[pl.BlockSpec((None, tq, NEAR), lambda h, rb: (h, 0, 0)),
                       pl.BlockSpec((None, seq, ncp), lambda h, rb: (h, 0, 0))]),
        out_shape=[jax.ShapeDtypeStruct((N_HEADS, tq, NEAR), F32),
                   jax.ShapeDtypeStruct((N_HEADS, seq, ncp), F32)],
        compiler_params=_cparams(("parallel",)),
        name="nsa_bias_tables",
    )(rel_bias.reshape(N_BUCKETS * N_HEADS), bkt_near, bkt_c)
    n_slc = seq // SLC_BLOCK
    cs = np.arange(ncp)[:, None] * CMP_STRIDE
    ss = np.arange(n_slc)[None, :] * SLC_BLOCK
    overlap = ((cs < ss + SLC_BLOCK) & (cs + CMP_BLOCK > ss) & (np.arange(ncp)[:, None] < n_cmp))
    kpos = np.arange(WINDOW + seq)[None, :] - WINDOW
    expand = (kpos // SLC_BLOCK == np.arange(n_slc)[:, None]) & (kpos >= 0)
    return tz, bias_c, jnp.asarray(overlap, F32), jnp.asarray(expand, BF16)


def _nsa_shared_kv(x, batch, seq, kv_norm, w_kv, cmp_pos_k, cmp_pos_v, k_w1, k_w2, v_w1, v_w2):
    n, d = x.shape
    gq, dh = N_KV_GROUPS, HEAD_DIM
    kv = norm_matmul(x, kv_norm, w_kv.astype(BF16), (0,), w_kv.shape[1], w_kv.shape[1],
                     lambda y: y, BF16)
    kv = kv.reshape(batch, seq, N_KV_BRANCH, gq, dh)
    per_group = lambda j: jnp.transpose(kv[:, :, j], (0, 2, 1, 3))
    n_cmp = (seq - CMP_BLOCK) // CMP_STRIDE + 1
    nhalf = seq // CMP_STRIDE

    def compress(t, pos, w1, w2):
        halves = t.reshape(batch * gq, nhalf, CMP_STRIDE * dh)
        return nsa_compress(halves, pos, w1.astype(BF16), w2.astype(BF16)).reshape(batch, gq, nhalf, dh)

    k_cmp = compress(per_group(0), cmp_pos_k, k_w1, k_w2)
    v_cmp = compress(per_group(1), cmp_pos_v, v_w1, v_w2)
    pad = lambda t: jnp.pad(t, ((0, 0), (0, 0), (WINDOW, 0), (0, 0)))
    return (k_cmp, v_cmp, pad(per_group(2)), pad(per_group(3)), pad(per_group(4)), pad(per_group(5)),
            n_cmp)


def _nsa_layer(x, batch, seq, g_mix, w_q, w_o, shared, tables):
    n, d = x.shape
    hd = N_HEADS * HEAD_DIM
    k_cmp, v_cmp, ks, vs, kw, vw, n_cmp = shared
    tz, bias_c, overlap, expand = tables
    w_q16 = w_q.astype(BF16)
    q = norm_matmul(x, g_mix, w_q16[:, :hd], (0,), hd, hd,
                    lambda y: y * (HEAD_DIM ** -0.5), BF16)
    n_gate = w_q.shape[1] - hd
    gates = norm_matmul(x, g_mix, w_q16[:, hd:], (0,), n_gate, n_gate, jax.nn.sigmoid, F32)
    q_t = jnp.transpose(q.reshape(batch, seq, N_HEADS, HEAD_DIM), (0, 2, 1, 3))
    gates = jnp.transpose(gates.reshape(batch, seq, 3, N_KV_GROUPS, GROUP), (0, 3, 1, 2, 4))
    o_t = nsa_attention_core(q_t, k_cmp, v_cmp, bias_c, ks, vs, kw, vw, tz,
                             gates.reshape(batch, N_KV_GROUPS, seq, 3 * GROUP), overlap, expand, n_cmp)
    o = jnp.transpose(o_t, (0, 2, 1, 3)).reshape(n, hd)
    return matmul_residual(o, w_o.astype(BF16), x)


def _ffn_kernel(x_ref, g_ref, wgu_ref, wd_ref, o_ref, act_ref, *, tn):
    x = x_ref[...]
    xn = _rms(x, g_ref[...]).astype(BF16)
    ff = wd_ref.shape[0]
    for c in range(ff // tn):
        gate = jnp.dot(xn, wgu_ref[:, c * tn:(c + 1) * tn], preferred_element_type=F32)
        up = jnp.dot(xn, wgu_ref[:, ff + c * tn:ff + (c + 1) * tn], preferred_element_type=F32)
        act_ref[:, c * tn:(c + 1) * tn] = (_silu(gate) * up).astype(BF16)
    o_ref[...] = x + jnp.dot(act_ref[...], wd_ref[...], preferred_element_type=F32)


def swiglu_ffn(x, g, w_gu, w_down, tm=ROW_TILE, tn=512):
    n, d = x.shape
    ff = w_down.shape[0]
    whole = lambda shape: pl.BlockSpec(shape, lambda i: (0, 0), pipeline_mode=pl.Buffered(1))
    return pl.pallas_call(
        functools.partial(_ffn_kernel, tn=min(tn, ff)),
        grid=(n // tm,),
        in_specs=[pl.BlockSpec((tm, d), lambda i: (i, 0)), pl.BlockSpec((1, d), lambda i: (0, 0)),
                  whole((d, 2 * ff)), whole((ff, d))],
        out_specs=pl.BlockSpec((tm, d), lambda i: (i, 0)),
        out_shape=jax.ShapeDtypeStruct((n, d), F32),
        scratch_shapes=[pltpu.VMEM((tm, ff), BF16)],
        compiler_params=_cparams(("parallel",)),
        name="swiglu_ffn",
    )(x, g.reshape(1, d), w_gu, w_down)


def _moe_ffn(x, g, w_router, w_gu, w_down, tm=ROW_TILE):
    n, d = x.shape
    ne = w_router.shape[1]
    sel, w, counts = moe_route(x, g, w_router)
    padded = (counts[:, 0] + tm - 1) // tm * tm
    ends = jnp.cumsum(padded)
    n_tiles = (TOP_K * n) // tm + ne
    n_used = (ends[-1] // tm).astype(jnp.int32).reshape(1)
    tile_expert = jnp.searchsorted(ends, jnp.arange(n_tiles, dtype=jnp.int32) * tm, side='right')
    tile_expert = jnp.minimum(tile_expert, tile_expert[jnp.maximum(n_used[0] - 1, 0)]).astype(jnp.int32)
    pos = moe_slots((ends - padded).astype(jnp.int32), sel)
    pos_flat = pos[:TOP_K].reshape(TOP_K * n)
    inv = moe_inverse(pos_flat, n, n_tiles * tm)
    y_sorted = moe_experts(tile_expert, n_used, inv, x, g, w_gu.astype(BF16), w_down.astype(BF16))
    return moe_combine(pos_flat, w, x, y_sorted)


def kernel(x, p, g_mix, g_ffn, g_ple, g_final, rg_w_in, rg_conv_w, rg_conv_b, rg_w_a, rg_b_a, rg_w_x,
           rg_b_x, rg_lambda, rg_w_out, kv_norm, w_kv, cmp_pos_k, cmp_pos_v, cmp_k_w1, cmp_k_w2,
           cmp_v_w1, cmp_v_w2, rel_bias, nsa_w_q, nsa_w_o, ffn_w_gu, ffn_w_down, moe_w_router,
           moe_w_gu, moe_w_down, ple_w_proj, ple_w_gate):
    batch, seq, d = x.shape
    depth = p.shape[0]
    n_a = rg_w_in.shape[0]
    n = batch * seq
    x = x.reshape(n, d)
    p = p.reshape(depth, n, p.shape[-1])
    n_cmp = (seq - CMP_BLOCK) // CMP_STRIDE + 1
    tables = _nsa_tables(rel_bias, seq, n_cmp, seq // CMP_STRIDE)
    shared = None
    for i in range(depth):
        if i < n_a:
            c = rg_w_in.shape[2] // 2
            proj = norm_matmul(x, g_mix[i], rg_w_in[i].astype(BF16), (0,), 2 * c, 2 * c, lambda y: y, F32)
            hg = rg_lru_core(proj, batch, seq, rg_conv_w[i], rg_conv_b[i], _block_diag_chunks(rg_w_a[i]),
                             rg_b_a[i], _block_diag_chunks(rg_w_x[i]), rg_b_x[i], rg_lambda[i])
            x = matmul_residual(hg, rg_w_out[i].astype(BF16), x)
        else:
            b = i - n_a
            x = _nsa_layer(x, batch, seq, g_mix[i], nsa_w_q[b], nsa_w_o[b], shared, tables)
        if i % 2 == 0:
            x = swiglu_ffn(x, g_ffn[i], ffn_w_gu[i // 2].astype(BF16), ffn_w_down[i // 2].astype(BF16))
        else:
            x = _moe_ffn(x, g_ffn[i], moe_w_router[i // 2], moe_w_gu[i // 2], moe_w_down[i // 2])
        x = ple_update(x, g_ple[i], ple_w_gate[i].astype(BF16), p[i], ple_w_proj[i].astype(BF16),
                       g_final if i == depth - 1 else None)
        if i == n_a - 1:
            shared = _nsa_shared_kv(x, batch, seq, kv_norm, w_kv, cmp_pos_k, cmp_pos_v,
                                    cmp_k_w1, cmp_k_w2, cmp_v_w1, cmp_v_w2)
    return x.reshape(batch, seq, d)
```

```python
import functools
import math

import numpy as np
import jax
import jax.numpy as jnp
from jax import lax
from jax.experimental import pallas as pl
from jax.experimental.pallas import tpu as pltpu

F32 = jnp.float32
BF16 = jnp.bfloat16

EPS = 1e-6
CONV_WIDTH = 4
LRU_BLOCKS = 16
LRU_C = 8.0
N_HEADS = 16
N_KV_GROUPS = 2
GROUP = N_HEADS // N_KV_GROUPS
HEAD_DIM = 64
N_KV_BRANCH = 6
CMP_BLOCK = 32
CMP_STRIDE = 16
SLC_BLOCK = 64
N_SELECT = 8
WINDOW = 512
Q_BLOCK = 128
N_BUCKETS = 32
MAX_DISTANCE = 128
TOP_K = 2
NEAR = WINDOW + Q_BLOCK
FAR_CHUNK = 512
NEG = -1e30

VMEM_LIMIT_V7X = 56 * 1024 * 1024
ROW_TILE = 512


def _cparams(sem):
    return pltpu.CompilerParams(dimension_semantics=sem, vmem_limit_bytes=VMEM_LIMIT_V7X)


def _rms(x, g):
    return x * lax.rsqrt(jnp.mean(x * x, axis=-1, keepdims=True) + EPS) * g


def _silu(x):
    return x * jax.nn.sigmoid(x)


def _norm_mm_kernel(x_ref, g_ref, *refs, n_w, epilogue):
    w_refs, o_ref, xn_ref = refs[:n_w], refs[n_w], refs[n_w + 1]

    @pl.when(pl.program_id(1) == 0)
    def _():
        xn_ref[...] = _rms(x_ref[...], g_ref[...]).astype(BF16)

    xn = xn_ref[...]
    outs = [jnp.dot(xn, w[...], preferred_element_type=F32) for w in w_refs]
    o_ref[...] = epilogue(*outs).astype(o_ref.dtype)


def norm_matmul(x, g, w, col_blocks, n_out, tn, epilogue, out_dtype, tm=ROW_TILE):
    n, d = x.shape
    grid = (n // tm, n_out // tn)
    w_specs = [pl.BlockSpec((d, tn), functools.partial(lambda i, j, cb: (0, cb + j), cb=cb))
               for cb in col_blocks]
    return pl.pallas_call(
        functools.partial(_norm_mm_kernel, n_w=len(col_blocks), epilogue=epilogue),
        grid=grid,
        in_specs=[pl.BlockSpec((tm, d), lambda i, j: (i, 0)),
                  pl.BlockSpec((1, d), lambda i, j: (0, 0))] + w_specs,
        out_specs=pl.BlockSpec((tm, tn), lambda i, j: (i, j)),
        out_shape=jax.ShapeDtypeStruct((n, n_out), out_dtype),
        scratch_shapes=[pltpu.VMEM((tm, d), BF16)],
        compiler_params=_cparams(("parallel", "arbitrary")),
    )(x, g.reshape(1, d), *([w] * len(col_blocks)))


def _mm_res_kernel(a_ref, w_ref, res_ref, *refs, scale_col):
    if scale_col is None:
        (o_ref,) = refs
        y = jnp.dot(a_ref[...], w_ref[...], preferred_element_type=F32)
    else:
        s_ref, o_ref = refs
        y = jnp.dot(a_ref[...], w_ref[...], preferred_element_type=F32)
        y = y * s_ref[:, scale_col:scale_col + 1]
    o_ref[...] = res_ref[...] + y


def matmul_residual(a, w, res, scale=None, scale_col=None, tm=ROW_TILE, tn=1024):
    n, k = a.shape
    d = w.shape[1]
    in_specs = [pl.BlockSpec((tm, k), lambda i, j: (i, 0)),
                pl.BlockSpec((k, tn), lambda i, j: (0, j)),
                pl.BlockSpec((tm, tn), lambda i, j: (i, j))]
    args = [a, w, res]
    if scale is not None:
        in_specs.append(pl.BlockSpec((tm, scale.shape[1]), lambda i, j: (i, 0)))
        args.append(scale)
    return pl.pallas_call(
        functools.partial(_mm_res_kernel, scale_col=scale_col),
        grid=(n // tm, d // tn),
        in_specs=in_specs,
        out_specs=pl.BlockSpec((tm, tn), lambda i, j: (i, j)),
        out_shape=jax.ShapeDtypeStruct((n, d), F32),
        compiler_params=_cparams(("parallel", "arbitrary")),
    )(*args)


BD = 256
SCAN_ROWS = 8


def _rg_kernel(gate_ref, xr_ref, cw_ref, cb_ref, wa_ref, ba_ref, wx_ref, bx_ref, lam_ref,
               o_ref, xe_ref, h_ref, a_ref, u_ref):
    t, c = xr_ref.shape

    @pl.when(pl.program_id(1) == 0)
    def _():
        xe_ref[0:8, :] = jnp.zeros((8, c), F32)
        h_ref[...] = jnp.zeros_like(h_ref)

    xe_ref[8:, :] = xr_ref[...]
    xc = cb_ref[...] + sum(
        cw_ref[k:k + 1, :] * xe_ref[8 - (CONV_WIDTH - 1) + k: 8 - (CONV_WIDTH - 1) + k + t, :]
        for k in range(CONV_WIDTH))
    xe_ref[0:8, :] = xr_ref[t - 8:, :]

    lam = -lam_ref[...]
    softplus_neg_lam = jnp.maximum(lam, 0.0) + jnp.log1p(jnp.exp(-jnp.abs(lam)))
    for cblk in range(c // BD):
        sl = slice(cblk * BD, (cblk + 1) * BD)
        xb = xc[:, sl]
        xb16 = xb.astype(BF16)
        r = jax.nn.sigmoid(jnp.dot(xb16, wa_ref[cblk], preferred_element_type=F32) + ba_ref[:, sl])
        i = jax.nn.sigmoid(jnp.dot(xb16, wx_ref[cblk], preferred_element_type=F32) + bx_ref[:, sl])
        log_a = -LRU_C * r * softplus_neg_lam[:, sl]
        a = jnp.exp(log_a)
        a_ref[:, sl] = a
        u_ref[:, sl] = jnp.sqrt(1.0 - a * a) * (i * xb)

    row = lax.broadcasted_iota(jnp.int32, (SCAN_ROWS, c), 0)

    def scan_tile(k, h_prev):
        rows = pl.ds(pl.multiple_of(k * SCAN_ROWS, SCAN_ROWS), SCAN_ROWS)
        a = a_ref[rows, :]
        u = u_ref[rows, :]
        for d in (1, 2, 4):
            a_s = jnp.where(row >= d, pltpu.roll(a, d, 0), 1.0)
            u_s = jnp.where(row >= d, pltpu.roll(u, d, 0), 0.0)
            u = a * u_s + u
            a = a * a_s
        h = a * h_prev + u
        u_ref[rows, :] = h
        return jnp.broadcast_to(h[SCAN_ROWS - 1:SCAN_ROWS, :], (SCAN_ROWS, c))

    h_ref[...] = lax.fori_loop(0, t // SCAN_ROWS, scan_tile, h_ref[...])
    o_ref[...] = (jax.nn.gelu(gate_ref[...]) * u_ref[...]).astype(o_ref.dtype)


def rg_lru_core(proj, batch, seq, conv_w, conv_b, wa_bd, b_a, wx_bd, b_x, lam, t_chunk=512):
    n, c2 = proj.shape
    c = c2 // 2
    nt = seq // t_chunk
    vec = lambda v: v.reshape(1, c)
    row_spec = lambda col: pl.BlockSpec((t_chunk, c), lambda b, t: (b * nt + t, col))
    const = lambda shape: pl.BlockSpec(shape, lambda b, t: (0,) * len(shape))
    return pl.pallas_call(
        _rg_kernel,
        grid=(batch, nt),
        in_specs=[row_spec(0), row_spec(1), const((CONV_WIDTH, c)), const((1, c)),
                  const(wa_bd.shape), const((1, c)), const(wx_bd.shape), const((1, c)), const((1, c))],
        out_specs=pl.BlockSpec((t_chunk, c), lambda b, t: (b * nt + t, 0)),
        out_shape=jax.ShapeDtypeStruct((n, c), BF16),
        scratch_shapes=[pltpu.VMEM((t_chunk + 8, c), F32), pltpu.VMEM((SCAN_ROWS, c), F32),
                        pltpu.VMEM((t_chunk, c), F32), pltpu.VMEM((t_chunk, c), F32)],
        compiler_params=_cparams(("parallel", "arbitrary")),
    )(proj, proj, conv_w, vec(conv_b), wa_bd, vec(b_a), wx_bd, vec(b_x), vec(lam))


def _block_diag_chunks(w):
    nb, bw, _ = w.shape
    per = BD // bw
    w = w.reshape(nb // per, per, bw, bw)
    eye = jnp.eye(per, dtype=w.dtype)
    out = jnp.einsum('cpij,pq->cpiqj', w, eye).reshape(nb // per, BD, BD)
    return out.astype(BF16)


def _ple_kernel(x_ref, g_ref, wg_ref, p_ref, wp_ref, *refs, final_norm):
    if final_norm:
        gf_ref, o_ref = refs
    else:
        (o_ref,) = refs
    x = x_ref[...]
    xn = _rms(x, g_ref[...]).astype(BF16)
    gate = jax.nn.sigmoid(jnp.dot(xn, wg_ref[...], preferred_element_type=F32))
    proj = jnp.dot(p_ref[...].astype(BF16), wp_ref[...], preferred_element_type=F32)
    y = x + gate * proj
    if final_norm:
        y = _rms(y, gf_ref[...])
    o_ref[...] = y


def ple_update(x, g, wg, p, wp, g_final=None, tm=ROW_TILE):
    n, d = x.shape
    dp = p.shape[1]
    const = lambda shape: pl.BlockSpec(shape, lambda i: (0,) * len(shape))
    in_specs = [pl.BlockSpec((tm, d), lambda i: (i, 0)), const((1, d)), const((d, d)),
                pl.BlockSpec((tm, dp), lambda i: (i, 0)), const((dp, d))]
    args = [x, g.reshape(1, d), wg, p, wp]
    if g_final is not None:
        in_specs.append(const((1, d)))
        args.append(g_final.reshape(1, d))
    return pl.pallas_call(
        functools.partial(_ple_kernel, final_norm=g_final is not None),
        grid=(n // tm,),
        in_specs=in_specs,
        out_specs=pl.BlockSpec((tm, d), lambda i: (i, 0)),
        out_shape=jax.ShapeDtypeStruct((n, d), F32),
        compiler_params=_cparams(("parallel",)),
    )(*args)


SEL_ROWS = 8


def _route_kernel(x_ref, g_ref, wt_ref, sel_ref, w_ref, cnt_ref, carry_ref):
    @pl.when(pl.program_id(0) == 0)
    def _():
        carry_ref[...] = jnp.zeros_like(carry_ref)

    xn = _rms(x_ref[...], g_ref[...])
    logits = lax.dot_general(wt_ref[...], xn, (((1,), (1,)), ((), ())), preferred_element_type=F32,
                             precision=lax.Precision.HIGHEST)
    ne, tm = logits.shape
    row = lax.broadcasted_iota(jnp.int32, (ne, tm), 0).astype(F32)
    m1 = jnp.max(logits, axis=0, keepdims=True)
    i1 = jnp.min(jnp.where(logits == m1, row, float(ne)), axis=0, keepdims=True)
    rest = jnp.where(row == i1, -jnp.inf, logits)
    m2 = jnp.max(rest, axis=0, keepdims=True)
    i2 = jnp.min(jnp.where(rest == m2, row, float(ne)), axis=0, keepdims=True)
    e2 = jnp.exp(m2 - m1)
    denom = 1.0 + e2
    ind = jnp.where((row == i1) | (row == i2), 1.0, 0.0)
    earlier = (lax.broadcasted_iota(jnp.int32, (tm, tm), 0) < lax.broadcasted_iota(jnp.int32, (tm, tm), 1))
    rank = jnp.dot(ind.astype(BF16), jnp.where(earlier, 1.0, 0.0).astype(BF16),
                   preferred_element_type=F32) + carry_ref[:, 0:1]
    r1 = jnp.sum(jnp.where(row == i1, rank, 0.0), axis=0, keepdims=True)
    r2 = jnp.sum(jnp.where(row == i2, rank, 0.0), axis=0, keepdims=True)
    carry_ref[...] = carry_ref[...] + jnp.sum(ind, axis=1, keepdims=True)
    pick = lambda vals: functools.reduce(
        lambda acc, kv: jnp.where(row == float(kv[0]), kv[1], acc), enumerate(vals), jnp.zeros((ne, tm), F32))
    sel_ref[...] = pick([i1, i2, r1, r2]).astype(jnp.int32)
    w_ref[...] = pick([1.0 / denom, e2 / denom])
    cnt_ref[...] = carry_ref[...].astype(jnp.int32)


def moe_route(x, g, w_router, tm=ROW_TILE):
    n, d = x.shape
    ne = w_router.shape[1]
    assert ne == SEL_ROWS
    return pl.pallas_call(
        _route_kernel,
        grid=(n // tm,),
        in_specs=[pl.BlockSpec((tm, d), lambda i: (i, 0)), pl.BlockSpec((1, d), lambda i: (0, 0)),
                  pl.BlockSpec((ne, d), lambda i: (0, 0))],
        out_specs=[pl.BlockSpec((ne, tm), lambda i: (0, i)), pl.BlockSpec((ne, tm), lambda i: (0, i)),
                   pl.BlockSpec((ne, 128), lambda i: (0, 0))],
        out_shape=[jax.ShapeDtypeStruct((ne, n), jnp.int32), jax.ShapeDtypeStruct((ne, n), F32),
                   jax.ShapeDtypeStruct((ne, 128), jnp.int32)],
        scratch_shapes=[pltpu.VMEM((ne, 128), F32)],
        compiler_params=_cparams(("arbitrary",)),
        name="moe_route",
    )(x, g.reshape(1, d), w_router.T)


def _slot_kernel(starts_ref, sel_ref, pos_ref):
    sel = sel_ref[...]
    start_of = lambda e_row: functools.reduce(
        lambda acc, e: jnp.where(e_row == e, starts_ref[e], acc), range(starts_ref.shape[0]),
        jnp.zeros_like(e_row))
    row = lax.broadcasted_iota(jnp.int32, sel.shape, 0)
    p1 = start_of(sel[0:1]) + sel[2:3]
    p2 = start_of(sel[1:2]) + sel[3:4]
    pos_ref[...] = jnp.where(row == 0, p1, jnp.where(row == 1, p2, 0))


def moe_slots(starts, sel, tm=2048):
    ne, n = sel.shape
    return pl.pallas_call(
        _slot_kernel,
        grid_spec=pltpu.PrefetchScalarGridSpec(
            num_scalar_prefetch=1, grid=(n // tm,),
            in_specs=[pl.BlockSpec((ne, tm), lambda i, s: (0, i))],
            out_specs=pl.BlockSpec((ne, tm), lambda i, s: (0, i))),
        out_shape=jax.ShapeDtypeStruct((ne, n), jnp.int32),
        compiler_params=_cparams(("parallel",)),
        name="moe_slots",
    )(starts, sel)


def _row_copy(src, src_row, dst, dst_row, sem):
    return pltpu.make_async_copy(src.at[pl.ds(src_row, 1)], dst.at[pl.ds(dst_row, 1)], sem)


def _inverse_kernel(pos_ref, inv_ref, *, n):
    def clear(s, c):
        inv_ref[s] = 0
        return c

    def fill(t, c):
        for k in range(TOP_K):
            inv_ref[pos_ref[k * n + t]] = t
        return c

    lax.fori_loop(0, inv_ref.shape[0], clear, 0, unroll=8)
    lax.fori_loop(0, n, fill, 0, unroll=8)


def moe_inverse(pos_flat, n, n_rows):
    return pl.pallas_call(
        functools.partial(_inverse_kernel, n=n),
        in_specs=[pl.BlockSpec(memory_space=pltpu.SMEM)],
        out_specs=pl.BlockSpec(memory_space=pltpu.SMEM),
        out_shape=jax.ShapeDtypeStruct((n_rows,), jnp.int32),
        name="moe_inverse",
    )(pos_flat)


def _experts_kernel(te_ref, nu_ref, inv_ref, x_hbm, g_ref, wgu_ref, wd_ref, o_ref, xbuf_ref, act_ref, sems,
                    *, tm, tn):
    del te_ref
    i = pl.program_id(0)
    n_used = nu_ref[0]
    ff = wd_ref.shape[0]

    def start_gather(tile, slot):
        def body(t, c):
            _row_copy(x_hbm, inv_ref[tile * tm + t], xbuf_ref.at[slot], t, sems.at[slot]).start()
            return c
        lax.fori_loop(0, tm, body, 0, unroll=8)

    def wait_gather(slot):
        pltpu.make_async_copy(x_hbm.at[pl.ds(0, tm)], xbuf_ref.at[slot], sems.at[slot]).wait()

    @pl.when(i < n_used)
    def _():
        slot = lax.rem(i, 2)

        @pl.when(i == 0)
        def _():
            start_gather(0, 0)

        @pl.when(i + 1 < n_used)
        def _():
            start_gather(i + 1, 1 - slot)

        wait_gather(slot)
        xn = _rms(xbuf_ref[slot], g_ref[...]).astype(BF16)
        for c in range(ff // tn):
            gate = jnp.dot(xn, wgu_ref[:, c * tn:(c + 1) * tn], preferred_element_type=F32)
            up = jnp.dot(xn, wgu_ref[:, ff + c * tn:ff + (c + 1) * tn], preferred_element_type=F32)
            act_ref[:, c * tn:(c + 1) * tn] = (_silu(gate) * up).astype(BF16)
        o_ref[...] = jnp.dot(act_ref[...], wd_ref[...], preferred_element_type=F32)

    @pl.when(i >= n_used)
    def _():
        o_ref[...] = jnp.zeros_like(o_ref)


def moe_experts(tile_expert, n_used, inv, x, g, w_gu, w_down, tm=ROW_TILE, tn=512):
    n, d = x.shape
    p = inv.shape[0]
    ff = w_down.shape[1]
    whole = lambda shape: pl.BlockSpec((None,) + shape, lambda i, te, nu, inv: (te[i], 0, 0),
                                       pipeline_mode=pl.Buffered(1))
    return pl.pallas_call(
        functools.partial(_experts_kernel, tm=tm, tn=min(tn, ff)),
        grid_spec=pltpu.PrefetchScalarGridSpec(
            num_scalar_prefetch=3, grid=(p // tm,),
            in_specs=[pl.BlockSpec(memory_space=pl.ANY),
                      pl.BlockSpec((1, d), lambda i, te, nu, inv: (0, 0)),
                      whole((d, 2 * ff)), whole((ff, d))],
            out_specs=pl.BlockSpec((tm, d), lambda i, te, nu, inv: (i, 0)),
            scratch_shapes=[pltpu.VMEM((2, tm, d), F32), pltpu.VMEM((tm, ff), BF16),
                            pltpu.SemaphoreType.DMA((2,))]),
        out_shape=jax.ShapeDtypeStruct((p, d), F32),
        compiler_params=_cparams(("arbitrary",)),
        name="moe_experts",
    )(tile_expert, n_used, inv, x, g.reshape(1, d), w_gu, w_down)


def _combine_kernel(pos_ref, w_ref, x_ref, y_hbm, o_ref, buf_ref, sem, *, n, tm):
    base = pl.program_id(0) * tm

    def start(t, c):
        for k in range(TOP_K):
            _row_copy(y_hbm, pos_ref[k * n + base + t], buf_ref.at[k], t, sem).start()
        return c

    lax.fori_loop(0, tm, start, 0, unroll=8)
    diag = (lax.broadcasted_iota(jnp.int32, (tm, tm), 0) == lax.broadcasted_iota(jnp.int32, (tm, tm), 1))
    wcol = [jnp.sum(jnp.where(diag, w_ref[k:k + 1, :], 0.0), axis=1, keepdims=True) for k in range(TOP_K)]
    for k in range(TOP_K):
        pltpu.make_async_copy(y_hbm.at[pl.ds(0, tm)], buf_ref.at[k], sem).wait()
    o_ref[...] = x_ref[...] + sum(wcol[k] * buf_ref[k] for k in range(TOP_K))


def moe_combine(pos_flat, w, x, y_sorted, tm=ROW_TILE):
    n, d = x.shape
    return pl.pallas_call(
        functools.partial(_combine_kernel, n=n, tm=tm),
        grid_spec=pltpu.PrefetchScalarGridSpec(
            num_scalar_prefetch=1, grid=(n // tm,),
            in_specs=[pl.BlockSpec((w.shape[0], tm), lambda i, pos: (0, i)),
                      pl.BlockSpec((tm, d), lambda i, pos: (i, 0)),
                      pl.BlockSpec(memory_space=pl.ANY)],
            out_specs=pl.BlockSpec((tm, d), lambda i, pos: (i, 0)),
            scratch_shapes=[pltpu.VMEM((TOP_K, tm, d), F32), pltpu.SemaphoreType.DMA]),
        out_shape=jax.ShapeDtypeStruct((n, d), F32),
        compiler_params=_cparams(("arbitrary",)),
        name="moe_combine",
    )(pos_flat, w, x, y_sorted)


def _compress_kernel(h_ref, pos_ref, w1_ref, w2_ref, o_ref):
    half = h_ref.shape[1]
    h = h_ref[...].astype(BF16)
    p_lo = jnp.dot(h, w1_ref[0:half, :], preferred_element_type=F32)
    p_hi = jnp.dot(h, w1_ref[half:, :], preferred_element_type=F32)
    nh = p_hi.shape[0]
    posb = jnp.dot(pos_ref[...].astype(BF16), w1_ref[...], preferred_element_type=F32)
    hid = p_lo + pltpu.roll(p_hi, nh - 1, 0) + posb
    o_ref[...] = jnp.dot(jax.nn.gelu(hid).astype(BF16), w2_ref[...],
                         preferred_element_type=F32).astype(o_ref.dtype)


def nsa_compress(halves, pos, w1, w2):
    bg, nh, hw = halves.shape
    hid = w1.shape[1]
    dh = w2.shape[1]
    return pl.pallas_call(
        _compress_kernel,
        grid=(bg,),
        in_specs=[pl.BlockSpec((None, nh, hw), lambda i: (i, 0, 0)),
                  pl.BlockSpec((1, 2 * hw), lambda i: (0, 0)),
                  pl.BlockSpec((2 * hw, hid), lambda i: (0, 0)),
                  pl.BlockSpec((hid, dh), lambda i: (0, 0))],
        out_specs=pl.BlockSpec((None, nh, dh), lambda i: (i, 0, 0)),
        out_shape=jax.ShapeDtypeStruct((bg, nh, dh), BF16),
        compiler_params=_cparams(("parallel",)),
    )(halves, pos.reshape(1, 2 * hw), w1, w2)


def _rel_bucket_np(dist):
    n = np.maximum(dist, 0)
    max_exact = N_BUCKETS // 2
    nf = np.maximum(n, 1).astype(np.float32)
    large = max_exact + (np.log(nf / max_exact) / np.float32(math.log(MAX_DISTANCE / max_exact))
                         * (N_BUCKETS - max_exact)).astype(np.int32)
    return np.where(n < max_exact, n, np.minimum(large, N_BUCKETS - 1)).astype(np.int32)


def _nsa_kernel(q_ref, gt_ref, kc_ref, vct_ref, bct_ref, ks_ref, kw_ref, vst_ref, vwt_ref, tzw_ref, tzc_ref,
                ovlt_ref, expt_ref, o_ref, km_scr, kmf_scr, *, n_cmp, n_sel):
    tq = q_ref.shape[0]
    dh = HEAD_DIM
    r = q_ref.shape[1] // dh
    near = tzw_ref.shape[0]
    n_slc = ovlt_ref.shape[0]
    ncp = kc_ref.shape[0]
    g = pl.program_id(1)
    t0 = pl.multiple_of(pl.program_id(2) * tq, tq)
    near_keys = pl.ds(t0, near)
    heads = lambda blocks: jnp.concatenate(blocks, axis=1)
    per_head = lambda x, h: x[:, h * tq:(h + 1) * tq]

    qt = q_ref[...].astype(F32).T.astype(BF16)
    zero = jnp.zeros((dh, r * tq), BF16)
    qh = heads([qt[h * dh:(h + 1) * dh, :] for h in range(r)])
    qz = jnp.where(g == 0, jnp.concatenate([qh, zero], axis=0), jnp.concatenate([zero, qh], axis=0))
    gate_t = gt_ref[...].T

    def gate(branch):
        rows = [gate_t[branch * N_HEADS + gi * r:branch * N_HEADS + (gi + 1) * r, :] for gi in range(N_KV_GROUPS)]
        mine = jnp.where(g == 0, rows[0], rows[1])
        return heads([mine[h:h + 1, :] for h in range(r)])

    cblk = lax.broadcasted_iota(jnp.int32, (ncp, r * tq), 0)
    qpos = t0 + (lax.broadcasted_iota(jnp.int32, (ncp, r * tq), 1) & (tq - 1))
    mask_c = (qpos - (cblk * CMP_STRIDE + CMP_BLOCK - 1) >= 0) & (cblk < n_cmp)
    s = jnp.dot(kc_ref[...], qz, preferred_element_type=F32) + bct_ref[...]
    l = jnp.where(mask_c, s, NEG)
    e = jnp.where(mask_c, jnp.exp(l - jnp.max(l, axis=0, keepdims=True)), 0.0)
    p = e / jnp.maximum(jnp.sum(e, axis=0, keepdims=True), 1e-30)
    out = gate(0) * jnp.dot(vct_ref[...], p.astype(BF16), preferred_element_type=F32)
    psum = functools.reduce(lambda acc, h: acc + per_head(p, h), range(1, r), per_head(p, 0))

    key_exists = t0 + lax.broadcasted_iota(jnp.int32, (near, r * tq), 0) >= WINDOW
    s = jnp.dot(kw_ref[near_keys, :], qz, preferred_element_type=F32)
    l = jnp.where(key_exists, s + tzw_ref[...], NEG)
    e = jnp.exp(l - jnp.max(l, axis=0, keepdims=True))
    o_w = jnp.dot(vwt_ref[:, near_keys], e.astype(BF16), preferred_element_type=F32)
    out = out + gate(2) * (o_w / jnp.sum(e, axis=0, keepdims=True))

    imp = jnp.dot(ovlt_ref[...], psum, preferred_element_type=F32, precision=lax.Precision.HIGHEST)
    jblk = lax.broadcasted_iota(jnp.int32, (n_slc, tq), 0).astype(F32)
    qblk = ((t0 + lax.broadcasted_iota(jnp.int32, (n_slc, tq), 1)) // SLC_BLOCK).astype(F32)
    forced = (jblk == 0) | (jblk == qblk) | (jblk == qblk - 1)
    score = jnp.where(forced, 1e9, jnp.where(jblk <= qblk, imp, -1e9))
    sel = jnp.zeros((n_slc, tq), F32)
    for _ in range(n_sel):
        m = jnp.max(score, axis=0, keepdims=True)
        first = jnp.min(jnp.where(score == m, jblk, float(n_slc)), axis=0, keepdims=True)
        pick = jblk == first
        sel = jnp.where(pick, 1.0, sel)
        score = jnp.where(pick, -jnp.inf, score)
    member = jnp.dot(expt_ref[...], sel.astype(BF16), preferred_element_type=F32) > 0.5
    km_scr[...] = jnp.where(member, 0.0, NEG)
    key_row = lax.broadcasted_iota(jnp.int32, member.shape, 0)
    kmf_scr[...] = jnp.where(member & (key_row < t0), 0.0, NEG)
    all_heads = lambda x: heads([x] * r)

    l = (jnp.dot(ks_ref[near_keys, :], qz, preferred_element_type=F32) + tzc_ref[...]
         + all_heads(km_scr[near_keys, :]))
    m = jnp.max(l, axis=0, keepdims=True)
    e = jnp.exp(l - m)
    den = jnp.sum(e, axis=0, keepdims=True)
    acc = jnp.dot(vst_ref[:, near_keys], e.astype(BF16), preferred_element_type=F32)
    far_bias = tzc_ref[0:1, :]

    def far_chunk(f, mda):
        m, den, acc = mda
        keys = pl.ds(pl.multiple_of(WINDOW + f * FAR_CHUNK, FAR_CHUNK), FAR_CHUNK)
        l = (jnp.dot(ks_ref[keys, :], qz, preferred_element_type=F32) + far_bias
             + all_heads(kmf_scr[keys, :]))
        m_new = jnp.maximum(m, jnp.max(l, axis=0, keepdims=True))
        alpha = jnp.exp(m - m_new)
        e = jnp.exp(l - m_new)
        pv = jnp.dot(vst_ref[:, keys], e.astype(BF16), preferred_element_type=F32)
        return m_new, alpha * den + jnp.sum(e, axis=0, keepdims=True), alpha * acc + pv

    n_far = (jnp.maximum(t0, WINDOW) - 1) // FAR_CHUNK
    m, den, acc = lax.fori_loop(0, n_far, far_chunk, (m, den, acc))
    out = out + gate(1) * (acc / den)

    o_ref[...] = jnp.concatenate([per_head(out, h) for h in range(r)], axis=0).T.astype(o_ref.dtype)


def nsa_attention_core(q, gates_pad, kc2, vct, bct, kv_pad, kvt_pad, tzw, tzc, ovlt, expt, batch, seq, n_cmp):
    n, hd = q.shape
    gq, dh, tq = N_KV_GROUPS, HEAD_DIM, Q_BLOCK
    r = N_HEADS // gq
    nqt = seq // tq
    spad = kv_pad.shape[1]
    ncp = kc2.shape[1]
    n_slc = seq // SLC_BLOCK
    near = tzw.shape[1]
    row_tile = lambda b, g, t: (b * nqt + t, g)
    key_cols = lambda branch: pl.BlockSpec((None, spad, gq * dh), lambda b, g, t: (b, 0, branch))
    val_rows = lambda branch: pl.BlockSpec((dh, spad), lambda b, g, t: (branch * gq + g, b))
    table = pl.BlockSpec((None, near, r * tq), lambda b, g, t: (g, 0, 0))
    const = lambda a: pl.BlockSpec(a.shape, lambda b, g, t: (0,) * a.ndim)
    return pl.pallas_call(
        functools.partial(_nsa_kernel, n_cmp=n_cmp, n_sel=min(N_SELECT, n_slc)),
        grid=(batch, gq, nqt),
        in_specs=[pl.BlockSpec((tq, r * dh), row_tile),
                  pl.BlockSpec((tq, gates_pad.shape[1]), lambda b, g, t: (b * nqt + t, 0)),
                  pl.BlockSpec((None, ncp, gq * dh), lambda b, g, t: (b, 0, 0)),
                  pl.BlockSpec((None, None, dh, ncp), lambda b, g, t: (b, g, 0, 0)),
                  pl.BlockSpec((None, ncp, r * tq), lambda b, g, t: (g, t, 0)),
                  key_cols(2), key_cols(4), val_rows(3), val_rows(5), table, table,
                  const(ovlt), const(expt)],
        out_specs=pl.BlockSpec((tq, r * dh), row_tile),
        out_shape=jax.ShapeDtypeStruct((n, hd), BF16),
        scratch_shapes=[pltpu.VMEM((spad, tq), F32), pltpu.VMEM((spad, tq), F32)],
        compiler_params=_cparams(("parallel", "parallel", "parallel")),
        name="nsa_attention",
    )(q, gates_pad, kc2, vct, bct, kv_pad, kv_pad, kvt_pad, kvt_pad, tzw, tzc, ovlt, expt)


def _bias_table_kernel(rb_ref, bkt_win_ref, bkt_causal_ref, bkt_cmp_ref, tzw_ref, tzc_ref, bct_ref, *, rows):
    h = pl.program_id(0)

    def lookup(bkt):
        acc = jnp.full(bkt.shape, NEG, F32)
        for k in range(N_BUCKETS):
            acc = jnp.where(bkt == k, rb_ref[k * N_HEADS + h], acc)
        return acc

    for src, dst in ((bkt_win_ref, tzw_ref), (bkt_causal_ref, tzc_ref), (bkt_cmp_ref, bct_ref)):
        def body(c, carry, src=src, dst=dst):
            sl = pl.ds(pl.multiple_of(c * rows, rows), rows)
            dst[sl, :] = lookup(src[sl, :])
            return carry
        lax.fori_loop(0, src.shape[0] // rows, body, 0)


def _nsa_tables(rel_bias, seq, n_cmp, ncp):
    tq = Q_BLOCK
    r = N_HEADS // N_KV_GROUPS
    dist_near = np.arange(tq)[None, :] + WINDOW - np.arange(NEAR)[:, None]
    bkt_near = _rel_bucket_np(dist_near)
    bkt_win = np.where((dist_near >= 0) & (dist_near < WINDOW), bkt_near, N_BUCKETS).astype(np.int32)
    bkt_causal = np.where(dist_near >= 0, bkt_near, N_BUCKETS).astype(np.int32)
    dist_c = np.arange(seq)[None, :] - (np.arange(ncp)[:, None] * CMP_STRIDE + CMP_BLOCK - 1)
    bkt_c = _rel_bucket_np(dist_c).reshape(ncp, seq // tq, tq).transpose(1, 0, 2).reshape(-1, tq)
    maps = [jnp.asarray(m) for m in (bkt_win, bkt_causal, bkt_c)]
    whole = lambda m: pl.BlockSpec(m.shape, lambda h, rb: (0, 0))
    per_head = lambda m: pl.BlockSpec((None, m.shape[0], tq), lambda h, rb: (h // r, 0, h % r))
    tzw, tzc, bct = pl.pallas_call(
        functools.partial(_bias_table_kernel, rows=8),
        grid_spec=pltpu.PrefetchScalarGridSpec(
            num_scalar_prefetch=1, grid=(N_HEADS,),
            in_specs=[whole(m) for m in maps], out_specs=[per_head(m) for m in maps]),
        out_shape=[jax.ShapeDtypeStruct((N_KV_GROUPS, m.shape[0], r * tq), F32) for m in maps],
        compiler_params=_cparams(("parallel",)),
        name="nsa_bias_tables",
    )(rel_bias.reshape(N_BUCKETS * N_HEADS), *maps)
    n_slc = seq // SLC_BLOCK
    cs = np.arange(ncp)[None, :] * CMP_STRIDE
    ss = np.arange(n_slc)[:, None] * SLC_BLOCK
    overlap_t = (cs < ss + SLC_BLOCK) & (cs + CMP_BLOCK > ss) & (np.arange(ncp)[None, :] < n_cmp)
    kpos = np.arange(WINDOW + seq)[:, None] - WINDOW
    expand_t = (kpos // SLC_BLOCK == np.arange(n_slc)[None, :]) & (kpos >= 0)
    return tzw, tzc, bct, jnp.asarray(overlap_t, F32), jnp.asarray(expand_t, BF16)


def _kv_kernel(x_ref, g_ref, w_ref, wt_ref, kv_ref, kvt_ref):
    xn = _rms(x_ref[...], g_ref[...]).astype(BF16)
    kv_ref[...] = jnp.dot(xn, w_ref[...], preferred_element_type=F32).astype(kv_ref.dtype)
    kvt_ref[...] = lax.dot_general(wt_ref[...], xn, (((1,), (1,)), ((), ())),
                                   preferred_element_type=F32).astype(kvt_ref.dtype)


def kv_projection(x, g, w_kv, tm=ROW_TILE):
    n, d = x.shape
    f = w_kv.shape[1]
    w16 = w_kv.astype(BF16)
    return pl.pallas_call(
        _kv_kernel,
        grid=(n // tm,),
        in_specs=[pl.BlockSpec((tm, d), lambda i: (i, 0)), pl.BlockSpec((1, d), lambda i: (0, 0)),
                  pl.BlockSpec((d, f), lambda i: (0, 0)), pl.BlockSpec((f, d), lambda i: (0, 0))],
        out_specs=[pl.BlockSpec((tm, f), lambda i: (i, 0)), pl.BlockSpec((f, tm), lambda i: (0, i))],
        out_shape=[jax.ShapeDtypeStruct((n, f), BF16), jax.ShapeDtypeStruct((f, n), BF16)],
        compiler_params=_cparams(("parallel",)),
        name="kv_projection",
    )(x, g.reshape(1, d), w16, w16.T)


def _nsa_shared_kv(x, batch, seq, kv_norm, w_kv, cmp_pos_k, cmp_pos_v, k_w1, k_w2, v_w1, v_w2):
    n, d = x.shape
    gq, dh = N_KV_GROUPS, HEAD_DIM
    f = w_kv.shape[1]
    kv, kvt = kv_projection(x, kv_norm, w_kv)
    kv5 = kv.reshape(batch, seq, N_KV_BRANCH, gq, dh)
    per_group = lambda j: jnp.transpose(kv5[:, :, j], (0, 2, 1, 3))
    n_cmp = (seq - CMP_BLOCK) // CMP_STRIDE + 1
    nhalf = seq // CMP_STRIDE

    def compress(t, pos, w1, w2):
        halves = t.reshape(batch * gq, nhalf, CMP_STRIDE * dh)
        return nsa_compress(halves, pos, w1.astype(BF16), w2.astype(BF16)).reshape(batch, gq, nhalf, dh)

    k_cmp = compress(per_group(0), cmp_pos_k, k_w1, k_w2)
    v_cmp = compress(per_group(1), cmp_pos_v, v_w1, v_w2)
    kc2 = jnp.transpose(k_cmp, (0, 2, 1, 3)).reshape(batch, nhalf, gq * dh)
    vct = jnp.transpose(v_cmp, (0, 1, 3, 2))
    kv_pad = jnp.pad(kv.reshape(batch, seq, f), ((0, 0), (WINDOW, 0), (0, 0)))
    kvt_pad = jnp.pad(kvt.reshape(f, batch, seq), ((0, 0), (0, 0), (WINDOW, 0))).reshape(f, -1)
    return kc2, vct, kv_pad, kvt_pad, n_cmp


def _nsa_layer(x, batch, seq, g_mix, w_q, w_o, shared, tables):
    n, d = x.shape
    hd = N_HEADS * HEAD_DIM
    kc2, vct, kv_pad, kvt_pad, n_cmp = shared
    tzw, tzc, bct, ovlt, expt = tables
    w_q16 = w_q.astype(BF16)
    q = norm_matmul(x, g_mix, w_q16[:, :hd], (0,), hd, hd, lambda y: y * (HEAD_DIM ** -0.5), BF16)
    w_gate = jnp.pad(w_q16[:, hd:], ((0, 0), (0, 128 - (w_q.shape[1] - hd))))
    gates_pad = norm_matmul(x, g_mix, w_gate, (0,), 128, 128, jax.nn.sigmoid, F32)
    o = nsa_attention_core(q, gates_pad, kc2, vct, bct, kv_pad, kvt_pad, tzw, tzc, ovlt, expt,
                           batch, seq, n_cmp)
    return matmul_residual(o, w_o.astype(BF16), x)


def _ffn_kernel(x_ref, g_ref, wgu_ref, wd_ref, o_ref, act_ref, *, tn):
    x = x_ref[...]
    xn = _rms(x, g_ref[...]).astype(BF16)
    ff = wd_ref.shape[0]
    for c in range(ff // tn):
        gate = jnp.dot(xn, wgu_ref[:, c * tn:(c + 1) * tn], preferred_element_type=F32)
        up = jnp.dot(xn, wgu_ref[:, ff + c * tn:ff + (c + 1) * tn], preferred_element_type=F32)
        act_ref[:, c * tn:(c + 1) * tn] = (_silu(gate) * up).astype(BF16)
    o_ref[...] = x + jnp.dot(act_ref[...], wd_ref[...], preferred_element_type=F32)


def swiglu_ffn(x, g, w_gu, w_down, tm=ROW_TILE, tn=512):
    n, d = x.shape
    ff = w_down.shape[0]
    whole = lambda shape: pl.BlockSpec(shape, lambda i: (0, 0), pipeline_mode=pl.Buffered(1))
    return pl.pallas_call(
        functools.partial(_ffn_kernel, tn=min(tn, ff)),
        grid=(n // tm,),
        in_specs=[pl.BlockSpec((tm, d), lambda i: (i, 0)), pl.BlockSpec((1, d), lambda i: (0, 0)),
                  whole((d, 2 * ff)), whole((ff, d))],
        out_specs=pl.BlockSpec((tm, d), lambda i: (i, 0)),
        out_shape=jax.ShapeDtypeStruct((n, d), F32),
        scratch_shapes=[pltpu.VMEM((tm, ff), BF16)],
        compiler_params=_cparams(("parallel",)),
        name="swiglu_ffn",
    )(x, g.reshape(1, d), w_gu, w_down)


def _moe_ffn(x, g, w_router, w_gu, w_down, tm=ROW_TILE):
    n, d = x.shape
    ne = w_router.shape[1]
    sel, w, counts = moe_route(x, g, w_router)
    padded = (counts[:, 0] + tm - 1) // tm * tm
    ends = jnp.cumsum(padded)
    n_tiles = (TOP_K * n) // tm + ne
    n_used = (ends[-1] // tm).astype(jnp.int32).reshape(1)
    tile_expert = jnp.searchsorted(ends, jnp.arange(n_tiles, dtype=jnp.int32) * tm, side='right')
    tile_expert = jnp.minimum(tile_expert, tile_expert[jnp.maximum(n_used[0] - 1, 0)]).astype(jnp.int32)
    pos = moe_slots((ends - padded).astype(jnp.int32), sel)
    pos_flat = pos[:TOP_K].reshape(TOP_K * n)
    inv = moe_inverse(pos_flat, n, n_tiles * tm)
    y_sorted = moe_experts(tile_expert, n_used, inv, x, g, w_gu.astype(BF16), w_down.astype(BF16))
    return moe_combine(pos_flat, w, x, y_sorted)


def kernel(x, p, g_mix, g_ffn, g_ple, g_final, rg_w_in, rg_conv_w, rg_conv_b, rg_w_a, rg_b_a, rg_w_x,
           rg_b_x, rg_lambda, rg_w_out, kv_norm, w_kv, cmp_pos_k, cmp_pos_v, cmp_k_w1, cmp_k_w2,
           cmp_v_w1, cmp_v_w2, rel_bias, nsa_w_q, nsa_w_o, ffn_w_gu, ffn_w_down, moe_w_router,
           moe_w_gu, moe_w_down, ple_w_proj, ple_w_gate):
    batch, seq, d = x.shape
    depth = p.shape[0]
    n_a = rg_w_in.shape[0]
    n = batch * seq
    x = x.reshape(n, d)
    p = p.reshape(depth, n, p.shape[-1])
    n_cmp = (seq - CMP_BLOCK) // CMP_STRIDE + 1
    tables = _nsa_tables(rel_bias, seq, n_cmp, seq // CMP_STRIDE)
    shared = None
    for i in range(depth):
        if i < n_a:
            c = rg_w_in.shape[2] // 2
            proj = norm_matmul(x, g_mix[i], rg_w_in[i].astype(BF16), (0,), 2 * c, 2 * c, lambda y: y, F32)
            hg = rg_lru_core(proj, batch, seq, rg_conv_w[i], rg_conv_b[i], _block_diag_chunks(rg_w_a[i]),
                             rg_b_a[i], _block_diag_chunks(rg_w_x[i]), rg_b_x[i], rg_lambda[i])
            x = matmul_residual(hg, rg_w_out[i].astype(BF16), x)
        else:
            b = i - n_a
            x = _nsa_layer(x, batch, seq, g_mix[i], nsa_w_q[b], nsa_w_o[b], shared, tables)
        if i % 2 == 0:
            x = swiglu_ffn(x, g_ffn[i], ffn_w_gu[i // 2].astype(BF16), ffn_w_down[i // 2].astype(BF16))
        else:
            x = _moe_ffn(x, g_ffn[i], moe_w_router[i // 2], moe_w_gu[i // 2], moe_w_down[i // 2])
        x = ple_update(x, g_ple[i], ple_w_gate[i].astype(BF16), p[i], ple_w_proj[i].astype(BF16),
                       g_final if i == depth - 1 else None)
        if i == n_a - 1:
            shared = _nsa_shared_kv(x, batch, seq, kv_norm, w_kv, cmp_pos_k, cmp_pos_v,
                                    cmp_k_w1, cmp_k_w2, cmp_v_w1, cmp_v_w2)
    return x.reshape(batch, seq, d)
```

```python
import functools
import math

import numpy as np
import jax
import jax.numpy as jnp
from jax import lax
from jax.experimental import pallas as pl
from jax.experimental.pallas import tpu as pltpu

F32 = jnp.float32
BF16 = jnp.bfloat16

EPS = 1e-6
CONV_WIDTH = 4
LRU_BLOCKS = 16
LRU_C = 8.0
N_HEADS = 16
N_KV_GROUPS = 2
GROUP = N_HEADS // N_KV_GROUPS
HEAD_DIM = 64
N_KV_BRANCH = 6
CMP_BLOCK = 32
CMP_STRIDE = 16
SLC_BLOCK = 64
N_SELECT = 8
WINDOW = 512
Q_BLOCK = 128
N_BUCKETS = 32
MAX_DISTANCE = 128
TOP_K = 2
NEAR = WINDOW + Q_BLOCK
FAR_CHUNK = 512
NEG = -1e30

VMEM_LIMIT_V7X = 56 * 1024 * 1024
ROW_TILE = 512


def _cparams(sem):
    return pltpu.CompilerParams(dimension_semantics=sem, vmem_limit_bytes=VMEM_LIMIT_V7X)


def _rms(x, g):
    return x * lax.rsqrt(jnp.mean(x * x, axis=-1, keepdims=True) + EPS) * g


def _silu(x):
    return x * jax.nn.sigmoid(x)


def _norm_mm_kernel(x_ref, g_ref, *refs, n_w, epilogue):
    w_refs, o_ref, xn_ref = refs[:n_w], refs[n_w], refs[n_w + 1]

    @pl.when(pl.program_id(1) == 0)
    def _():
        xn_ref[...] = _rms(x_ref[...], g_ref[...]).astype(BF16)

    xn = xn_ref[...]
    outs = [jnp.dot(xn, w[...], preferred_element_type=F32) for w in w_refs]
    o_ref[...] = epilogue(*outs).astype(o_ref.dtype)


def norm_matmul(x, g, w, col_blocks, n_out, tn, epilogue, out_dtype, tm=ROW_TILE):
    n, d = x.shape
    grid = (n // tm, n_out // tn)
    w_specs = [pl.BlockSpec((d, tn), functools.partial(lambda i, j, cb: (0, cb + j), cb=cb))
               for cb in col_blocks]
    return pl.pallas_call(
        functools.partial(_norm_mm_kernel, n_w=len(col_blocks), epilogue=epilogue),
        grid=grid,
        in_specs=[pl.BlockSpec((tm, d), lambda i, j: (i, 0)),
                  pl.BlockSpec((1, d), lambda i, j: (0, 0))] + w_specs,
        out_specs=pl.BlockSpec((tm, tn), lambda i, j: (i, j)),
        out_shape=jax.ShapeDtypeStruct((n, n_out), out_dtype),
        scratch_shapes=[pltpu.VMEM((tm, d), BF16)],
        compiler_params=_cparams(("parallel", "arbitrary")),
    )(x, g.reshape(1, d), *([w] * len(col_blocks)))


def _mm_res_kernel(a_ref, w_ref, res_ref, *refs, scale_col):
    if scale_col is None:
        (o_ref,) = refs
        y = jnp.dot(a_ref[...], w_ref[...], preferred_element_type=F32)
    else:
        s_ref, o_ref = refs
        y = jnp.dot(a_ref[...], w_ref[...], preferred_element_type=F32)
        y = y * s_ref[:, scale_col:scale_col + 1]
    o_ref[...] = res_ref[...] + y


def matmul_residual(a, w, res, scale=None, scale_col=None, tm=ROW_TILE, tn=1024):
    n, k = a.shape
    d = w.shape[1]
    in_specs = [pl.BlockSpec((tm, k), lambda i, j: (i, 0)),
                pl.BlockSpec((k, tn), lambda i, j: (0, j)),
                pl.BlockSpec((tm, tn), lambda i, j: (i, j))]
    args = [a, w, res]
    if scale is not None:
        in_specs.append(pl.BlockSpec((tm, scale.shape[1]), lambda i, j: (i, 0)))
        args.append(scale)
    return pl.pallas_call(
        functools.partial(_mm_res_kernel, scale_col=scale_col),
        grid=(n // tm, d // tn),
        in_specs=in_specs,
        out_specs=pl.BlockSpec((tm, tn), lambda i, j: (i, j)),
        out_shape=jax.ShapeDtypeStruct((n, d), F32),
        compiler_params=_cparams(("parallel", "arbitrary")),
    )(*args)


BD = 256
SCAN_ROWS = 8


def _rg_kernel(gate_ref, xr_ref, cw_ref, cb_ref, wa_ref, ba_ref, wx_ref, bx_ref, lam_ref,
               o_ref, xe_ref, h_ref, a_ref, u_ref):
    t, c = xr_ref.shape

    @pl.when(pl.program_id(1) == 0)
    def _():
        xe_ref[0:8, :] = jnp.zeros((8, c), F32)
        h_ref[...] = jnp.zeros_like(h_ref)

    xe_ref[8:, :] = xr_ref[...]
    xc = cb_ref[...] + sum(
        cw_ref[k:k + 1, :] * xe_ref[8 - (CONV_WIDTH - 1) + k: 8 - (CONV_WIDTH - 1) + k + t, :]
        for k in range(CONV_WIDTH))
    xe_ref[0:8, :] = xr_ref[t - 8:, :]

    lam = -lam_ref[...]
    softplus_neg_lam = jnp.maximum(lam, 0.0) + jnp.log1p(jnp.exp(-jnp.abs(lam)))
    for cblk in range(c // BD):
        sl = slice(cblk * BD, (cblk + 1) * BD)
        xb = xc[:, sl]
        xb16 = xb.astype(BF16)
        r = jax.nn.sigmoid(jnp.dot(xb16, wa_ref[cblk], preferred_element_type=F32) + ba_ref[:, sl])
        i = jax.nn.sigmoid(jnp.dot(xb16, wx_ref[cblk], preferred_element_type=F32) + bx_ref[:, sl])
        log_a = -LRU_C * r * softplus_neg_lam[:, sl]
        a = jnp.exp(log_a)
        a_ref[:, sl] = a
        u_ref[:, sl] = jnp.sqrt(1.0 - a * a) * (i * xb)

    row = lax.broadcasted_iota(jnp.int32, (SCAN_ROWS, c), 0)

    def scan_tile(k, h_prev):
        rows = pl.ds(pl.multiple_of(k * SCAN_ROWS, SCAN_ROWS), SCAN_ROWS)
        a = a_ref[rows, :]
        u = u_ref[rows, :]
        for d in (1, 2, 4):
            a_s = jnp.where(row >= d, pltpu.roll(a, d, 0), 1.0)
            u_s = jnp.where(row >= d, pltpu.roll(u, d, 0), 0.0)
            u = a * u_s + u
            a = a * a_s
        h = a * h_prev + u
        u_ref[rows, :] = h
        return jnp.broadcast_to(h[SCAN_ROWS - 1:SCAN_ROWS, :], (SCAN_ROWS, c))

    h_ref[...] = lax.fori_loop(0, t // SCAN_ROWS, scan_tile, h_ref[...])
    o_ref[...] = (jax.nn.gelu(gate_ref[...]) * u_ref[...]).astype(o_ref.dtype)


def rg_lru_core(proj, batch, seq, conv_w, conv_b, wa_bd, b_a, wx_bd, b_x, lam, t_chunk=512):
    n, c2 = proj.shape
    c = c2 // 2
    nt = seq // t_chunk
    vec = lambda v: v.reshape(1, c)
    row_spec = lambda col: pl.BlockSpec((t_chunk, c), lambda b, t: (b * nt + t, col))
    const = lambda shape: pl.BlockSpec(shape, lambda b, t: (0,) * len(shape))
    return pl.pallas_call(
        _rg_kernel,
        grid=(batch, nt),
        in_specs=[row_spec(0), row_spec(1), const((CONV_WIDTH, c)), const((1, c)),
                  const(wa_bd.shape), const((1, c)), const(wx_bd.shape), const((1, c)), const((1, c))],
        out_specs=pl.BlockSpec((t_chunk, c), lambda b, t: (b * nt + t, 0)),
        out_shape=jax.ShapeDtypeStruct((n, c), BF16),
        scratch_shapes=[pltpu.VMEM((t_chunk + 8, c), F32), pltpu.VMEM((SCAN_ROWS, c), F32),
                        pltpu.VMEM((t_chunk, c), F32), pltpu.VMEM((t_chunk, c), F32)],
        compiler_params=_cparams(("parallel", "arbitrary")),
    )(proj, proj, conv_w, vec(conv_b), wa_bd, vec(b_a), wx_bd, vec(b_x), vec(lam))


def _block_diag_chunks(w):
    nb, bw, _ = w.shape
    per = BD // bw
    w = w.reshape(nb // per, per, bw, bw)
    eye = jnp.eye(per, dtype=w.dtype)
    out = jnp.einsum('cpij,pq->cpiqj', w, eye).reshape(nb // per, BD, BD)
    return out.astype(BF16)


def _ple(x, g_ref, wg_ref, p_ref, wp_ref):
    xn = _rms(x, g_ref[...]).astype(BF16)
    gate = jax.nn.sigmoid(jnp.dot(xn, wg_ref[...], preferred_element_type=F32))
    proj = jnp.dot(p_ref[...].astype(BF16), wp_ref[...], preferred_element_type=F32)
    return x + gate * proj


SEL_ROWS = 8


def _route_kernel(x_ref, g_ref, wt_ref, sel_ref, w_ref, cnt_ref, carry_ref):
    @pl.when(pl.program_id(0) == 0)
    def _():
        carry_ref[...] = jnp.zeros_like(carry_ref)

    xn = _rms(x_ref[...], g_ref[...])
    logits = lax.dot_general(wt_ref[...], xn, (((1,), (1,)), ((), ())), preferred_element_type=F32,
                             precision=lax.Precision.HIGHEST)
    ne, tm = logits.shape
    row = lax.broadcasted_iota(jnp.int32, (ne, tm), 0).astype(F32)
    m1 = jnp.max(logits, axis=0, keepdims=True)
    i1 = jnp.min(jnp.where(logits == m1, row, float(ne)), axis=0, keepdims=True)
    rest = jnp.where(row == i1, -jnp.inf, logits)
    m2 = jnp.max(rest, axis=0, keepdims=True)
    i2 = jnp.min(jnp.where(rest == m2, row, float(ne)), axis=0, keepdims=True)
    e2 = jnp.exp(m2 - m1)
    denom = 1.0 + e2
    ind = jnp.where((row == i1) | (row == i2), 1.0, 0.0)
    earlier = (lax.broadcasted_iota(jnp.int32, (tm, tm), 0) < lax.broadcasted_iota(jnp.int32, (tm, tm), 1))
    rank = jnp.dot(ind.astype(BF16), jnp.where(earlier, 1.0, 0.0).astype(BF16),
                   preferred_element_type=F32) + carry_ref[:, 0:1]
    r1 = jnp.sum(jnp.where(row == i1, rank, 0.0), axis=0, keepdims=True)
    r2 = jnp.sum(jnp.where(row == i2, rank, 0.0), axis=0, keepdims=True)
    carry_ref[...] = carry_ref[...] + jnp.sum(ind, axis=1, keepdims=True)
    pick = lambda vals: functools.reduce(
        lambda acc, kv: jnp.where(row == float(kv[0]), kv[1], acc), enumerate(vals), jnp.zeros((ne, tm), F32))
    sel_ref[...] = pick([i1, i2, r1, r2]).astype(jnp.int32)
    w_ref[...] = pick([1.0 / denom, e2 / denom])
    cnt_ref[...] = carry_ref[...].astype(jnp.int32)


def moe_route(x, g, w_router, tm=ROW_TILE):
    n, d = x.shape
    ne = w_router.shape[1]
    assert ne == SEL_ROWS
    return pl.pallas_call(
        _route_kernel,
        grid=(n // tm,),
        in_specs=[pl.BlockSpec((tm, d), lambda i: (i, 0)), pl.BlockSpec((1, d), lambda i: (0, 0)),
                  pl.BlockSpec((ne, d), lambda i: (0, 0))],
        out_specs=[pl.BlockSpec((ne, tm), lambda i: (0, i)), pl.BlockSpec((ne, tm), lambda i: (0, i)),
                   pl.BlockSpec((ne, 128), lambda i: (0, 0))],
        out_shape=[jax.ShapeDtypeStruct((ne, n), jnp.int32), jax.ShapeDtypeStruct((ne, n), F32),
                   jax.ShapeDtypeStruct((ne, 128), jnp.int32)],
        scratch_shapes=[pltpu.VMEM((ne, 128), F32)],
        compiler_params=_cparams(("arbitrary",)),
        name="moe_route",
    )(x, g.reshape(1, d), w_router.T)


def _slot_kernel(starts_ref, sel_ref, pos_ref):
    sel = sel_ref[...]
    start_of = lambda e_row: functools.reduce(
        lambda acc, e: jnp.where(e_row == e, starts_ref[e], acc), range(starts_ref.shape[0]),
        jnp.zeros_like(e_row))
    row = lax.broadcasted_iota(jnp.int32, sel.shape, 0)
    p1 = start_of(sel[0:1]) + sel[2:3]
    p2 = start_of(sel[1:2]) + sel[3:4]
    pos_ref[...] = jnp.where(row == 0, p1, jnp.where(row == 1, p2, 0))


def moe_slots(starts, sel, tm=2048):
    ne, n = sel.shape
    return pl.pallas_call(
        _slot_kernel,
        grid_spec=pltpu.PrefetchScalarGridSpec(
            num_scalar_prefetch=1, grid=(n // tm,),
            in_specs=[pl.BlockSpec((ne, tm), lambda i, s: (0, i))],
            out_specs=pl.BlockSpec((ne, tm), lambda i, s: (0, i))),
        out_shape=jax.ShapeDtypeStruct((ne, n), jnp.int32),
        compiler_params=_cparams(("parallel",)),
        name="moe_slots",
    )(starts, sel)


def _row_copy(src, src_row, dst, dst_row, sem):
    return pltpu.make_async_copy(src.at[pl.ds(src_row, 1)], dst.at[pl.ds(dst_row, 1)], sem)


def _inverse_kernel(pos_ref, pad_ref, inv_ref, *, n):
    def clear(s, c):
        inv_ref[s] = 0
        return c

    def fill(t, c):
        for k in range(TOP_K):
            inv_ref[pos_ref[k * n + t]] = t
        return c

    n_ranges = pad_ref.shape[0] // 2
    for e in range(n_ranges):
        lax.fori_loop(pad_ref[e], pad_ref[n_ranges + e], clear, 0)
    lax.fori_loop(0, n, fill, 0, unroll=8)


def moe_inverse(pos_flat, pad_bounds, n, n_rows):
    return pl.pallas_call(
        functools.partial(_inverse_kernel, n=n),
        in_specs=[pl.BlockSpec(memory_space=pltpu.SMEM), pl.BlockSpec(memory_space=pltpu.SMEM)],
        out_specs=pl.BlockSpec(memory_space=pltpu.SMEM),
        out_shape=jax.ShapeDtypeStruct((n_rows,), jnp.int32),
        name="moe_inverse",
    )(pos_flat, pad_bounds)


def _experts_kernel(te_ref, nu_ref, inv_ref, x_hbm, g_ref, wgu_hbm, wd_hbm, o_ref, xbuf_ref, act_ref, wgu_ref,
                    wd_ref, stage_gu, stage_d, sems, wsems, *, tm, tn):
    i = pl.program_id(0)
    n_used = nu_ref[0]
    ff = wd_ref.shape[0]

    def load_expert(e):
        def stream(n_chunks, src_of, stage, dst_of):
            copy = lambda c, slot: pltpu.make_async_copy(src_of(c), stage.at[slot], wsems.at[slot])
            copy(0, 0).start()

            def body(c, carry):
                slot = lax.rem(c, 2)

                @pl.when(c + 1 < n_chunks)
                def _():
                    copy(c + 1, 1 - slot).start()

                copy(c, slot).wait()
                dst_of(c)[...] = stage[slot].astype(BF16)
                return carry

            lax.fori_loop(0, n_chunks, body, 0)

        cols = lambda c: pl.ds(pl.multiple_of(c * tn, tn), tn)
        stream(2 * ff // tn, lambda c: wgu_hbm.at[e, :, cols(c)], stage_gu, lambda c: wgu_ref.at[:, cols(c)])
        stream(ff // tn, lambda c: wd_hbm.at[e, cols(c), :], stage_d, lambda c: wd_ref.at[cols(c), :])

    def start_gather(tile, slot):
        def body(t, c):
            _row_copy(x_hbm, inv_ref[tile * tm + t], xbuf_ref.at[slot], t, sems.at[slot]).start()
            return c
        lax.fori_loop(0, tm, body, 0, unroll=8)

    def wait_gather(slot):
        pltpu.make_async_copy(x_hbm.at[pl.ds(0, tm)], xbuf_ref.at[slot], sems.at[slot]).wait()

    @pl.when(i < n_used)
    def _():
        slot = lax.rem(i, 2)

        @pl.when(i == 0)
        def _():
            start_gather(0, 0)

        @pl.when(i + 1 < n_used)
        def _():
            start_gather(i + 1, 1 - slot)

        @pl.when((i == 0) | (te_ref[i] != te_ref[jnp.maximum(i - 1, 0)]))
        def _():
            load_expert(te_ref[i])

        wait_gather(slot)
        xn = _rms(xbuf_ref[slot], g_ref[...]).astype(BF16)
        for c in range(ff // tn):
            gate = jnp.dot(xn, wgu_ref[:, c * tn:(c + 1) * tn], preferred_element_type=F32)
            up = jnp.dot(xn, wgu_ref[:, ff + c * tn:ff + (c + 1) * tn], preferred_element_type=F32)
            act_ref[:, c * tn:(c + 1) * tn] = (_silu(gate) * up).astype(BF16)
        o_ref[...] = jnp.dot(act_ref[...], wd_ref[...], preferred_element_type=F32)

    @pl.when(i >= n_used)
    def _():
        o_ref[...] = jnp.zeros_like(o_ref)


def moe_experts(tile_expert, n_used, inv, x, g, w_gu, w_down, tm=ROW_TILE, tn=512):
    n, d = x.shape
    p = inv.shape[0]
    ff = w_down.shape[1]
    tn = min(tn, ff)
    hbm = pl.BlockSpec(memory_space=pl.ANY)
    return pl.pallas_call(
        functools.partial(_experts_kernel, tm=tm, tn=tn),
        grid_spec=pltpu.PrefetchScalarGridSpec(
            num_scalar_prefetch=3, grid=(p // tm,),
            in_specs=[hbm, pl.BlockSpec((1, d), lambda i, te, nu, inv: (0, 0)), hbm, hbm],
            out_specs=pl.BlockSpec((tm, d), lambda i, te, nu, inv: (i, 0)),
            scratch_shapes=[pltpu.VMEM((2, tm, d), F32), pltpu.VMEM((tm, ff), BF16),
                            pltpu.VMEM((d, 2 * ff), BF16), pltpu.VMEM((ff, d), BF16),
                            pltpu.VMEM((2, d, tn), w_gu.dtype), pltpu.VMEM((2, tn, d), w_down.dtype),
                            pltpu.SemaphoreType.DMA((2,)), pltpu.SemaphoreType.DMA((2,))]),
        out_shape=jax.ShapeDtypeStruct((p, d), F32),
        compiler_params=_cparams(("arbitrary",)),
        name="moe_experts",
    )(tile_expert, n_used, inv, x, g.reshape(1, d), w_gu, w_down)


def _combine_kernel(pos_ref, w_ref, x_ref, y_hbm, gp_ref, wg_ref, p_ref, wp_ref, *refs, n, tm, final_norm):
    if final_norm:
        gf_ref, o_ref, buf_ref, sem = refs
    else:
        o_ref, buf_ref, sem = refs
    base = pl.program_id(0) * tm

    def start(t, c):
        for k in range(TOP_K):
            _row_copy(y_hbm, pos_ref[k * n + base + t], buf_ref.at[k], t, sem).start()
        return c

    lax.fori_loop(0, tm, start, 0, unroll=8)
    diag = (lax.broadcasted_iota(jnp.int32, (tm, tm), 0) == lax.broadcasted_iota(jnp.int32, (tm, tm), 1))
    wcol = [jnp.sum(jnp.where(diag, w_ref[k:k + 1, :], 0.0), axis=1, keepdims=True) for k in range(TOP_K)]
    for k in range(TOP_K):
        pltpu.make_async_copy(y_hbm.at[pl.ds(0, tm)], buf_ref.at[k], sem).wait()
    y = _ple(x_ref[...] + sum(wcol[k] * buf_ref[k] for k in range(TOP_K)), gp_ref, wg_ref, p_ref, wp_ref)
    o_ref[...] = _rms(y, gf_ref[...]) if final_norm else y


def moe_combine(pos_flat, w, x, y_sorted, g_ple, wg_ple, p, wp_ple, g_final=None, tm=ROW_TILE):
    n, d = x.shape
    dp = p.shape[1]
    const = lambda shape: pl.BlockSpec(shape, lambda i, pos: (0, 0), pipeline_mode=pl.Buffered(1))
    in_specs = [pl.BlockSpec((w.shape[0], tm), lambda i, pos: (0, i)),
                pl.BlockSpec((tm, d), lambda i, pos: (i, 0)),
                pl.BlockSpec(memory_space=pl.ANY),
                const((1, d)), const((d, d)), pl.BlockSpec((tm, dp), lambda i, pos: (i, 0)), const((dp, d))]
    args = [pos_flat, w, x, y_sorted, g_ple.reshape(1, d), wg_ple, p, wp_ple]
    if g_final is not None:
        in_specs.append(const((1, d)))
        args.append(g_final.reshape(1, d))
    return pl.pallas_call(
        functools.partial(_combine_kernel, n=n, tm=tm, final_norm=g_final is not None),
        grid_spec=pltpu.PrefetchScalarGridSpec(
            num_scalar_prefetch=1, grid=(n // tm,),
            in_specs=in_specs,
            out_specs=pl.BlockSpec((tm, d), lambda i, pos: (i, 0)),
            scratch_shapes=[pltpu.VMEM((TOP_K, tm, d), F32), pltpu.SemaphoreType.DMA]),
        out_shape=jax.ShapeDtypeStruct((n, d), F32),
        compiler_params=_cparams(("arbitrary",)),
        name="moe_combine",
    )(*args)


def _compress_kernel(h_ref, pos_ref, w1_ref, w2_ref, o_ref):
    half = h_ref.shape[1]
    h = h_ref[...].astype(BF16)
    p_lo = jnp.dot(h, w1_ref[0:half, :], preferred_element_type=F32)
    p_hi = jnp.dot(h, w1_ref[half:, :], preferred_element_type=F32)
    nh = p_hi.shape[0]
    posb = jnp.dot(pos_ref[...].astype(BF16), w1_ref[...], preferred_element_type=F32)
    hid = p_lo + pltpu.roll(p_hi, nh - 1, 0) + posb
    o_ref[...] = jnp.dot(jax.nn.gelu(hid).astype(BF16), w2_ref[...],
                         preferred_element_type=F32).astype(o_ref.dtype)


def nsa_compress(halves, pos, w1, w2):
    bg, nh, hw = halves.shape
    hid = w1.shape[1]
    dh = w2.shape[1]
    return pl.pallas_call(
        _compress_kernel,
        grid=(bg,),
        in_specs=[pl.BlockSpec((None, nh, hw), lambda i: (i, 0, 0)),
                  pl.BlockSpec((1, 2 * hw), lambda i: (0, 0)),
                  pl.BlockSpec((2 * hw, hid), lambda i: (0, 0)),
                  pl.BlockSpec((hid, dh), lambda i: (0, 0))],
        out_specs=pl.BlockSpec((None, nh, dh), lambda i: (i, 0, 0)),
        out_shape=jax.ShapeDtypeStruct((bg, nh, dh), BF16),
        compiler_params=_cparams(("parallel",)),
    )(halves, pos.reshape(1, 2 * hw), w1, w2)


def _rel_bucket_np(dist):
    n = np.maximum(dist, 0)
    max_exact = N_BUCKETS // 2
    nf = np.maximum(n, 1).astype(np.float32)
    large = max_exact + (np.log(nf / max_exact) / np.float32(math.log(MAX_DISTANCE / max_exact))
                         * (N_BUCKETS - max_exact)).astype(np.int32)
    return np.where(n < max_exact, n, np.minimum(large, N_BUCKETS - 1)).astype(np.int32)


def _nsa_kernel(q_ref, gt_ref, kc_ref, vct_ref, bct_ref, ks_ref, kw_ref, vst_ref, vwt_ref, tzw_ref, tzc_ref,
                ovlt_ref, expt_ref, o_ref, km_scr, kmf_scr, *, n_cmp, n_sel):
    tq = q_ref.shape[0]
    dh = HEAD_DIM
    r = q_ref.shape[1] // dh
    near = tzw_ref.shape[0]
    n_slc = ovlt_ref.shape[0]
    ncp = kc_ref.shape[0]
    g = pl.program_id(1)
    t0 = pl.multiple_of(pl.program_id(2) * tq, tq)
    near_keys = pl.ds(t0, near)
    heads = lambda blocks: jnp.concatenate(blocks, axis=1)
    per_head = lambda x, h: x[:, h * tq:(h + 1) * tq]

    qt = q_ref[...].astype(F32).T.astype(BF16)
    zero = jnp.zeros((dh, r * tq), BF16)
    qh = heads([qt[h * dh:(h + 1) * dh, :] for h in range(r)])
    qz = jnp.where(g == 0, jnp.concatenate([qh, zero], axis=0), jnp.concatenate([zero, qh], axis=0))
    gate_t = gt_ref[...].T

    def gate(branch):
        rows = [gate_t[branch * N_HEADS + gi * r:branch * N_HEADS + (gi + 1) * r, :] for gi in range(N_KV_GROUPS)]
        mine = jnp.where(g == 0, rows[0], rows[1])
        return heads([mine[h:h + 1, :] for h in range(r)])

    cblk = lax.broadcasted_iota(jnp.int32, (ncp, r * tq), 0)
    qpos = t0 + (lax.broadcasted_iota(jnp.int32, (ncp, r * tq), 1) & (tq - 1))
    mask_c = (qpos - (cblk * CMP_STRIDE + CMP_BLOCK - 1) >= 0) & (cblk < n_cmp)
    s = jnp.dot(kc_ref[...], qz, preferred_element_type=F32) + bct_ref[...]
    l = jnp.where(mask_c, s, NEG)
    e = jnp.where(mask_c, jnp.exp(l - jnp.max(l, axis=0, keepdims=True)), 0.0)
    p = e / jnp.maximum(jnp.sum(e, axis=0, keepdims=True), 1e-30)
    out = gate(0) * jnp.dot(vct_ref[...], p.astype(BF16), preferred_element_type=F32)
    psum = functools.reduce(lambda acc, h: acc + per_head(p, h), range(1, r), per_head(p, 0))

    key_exists = t0 + lax.broadcasted_iota(jnp.int32, (near, r * tq), 0) >= WINDOW
    s = jnp.dot(kw_ref[near_keys, :], qz, preferred_element_type=F32)
    l = jnp.where(key_exists, s + tzw_ref[...], NEG)
    e = jnp.exp(l - jnp.max(l, axis=0, keepdims=True))
    o_w = jnp.dot(vwt_ref[:, near_keys], e.astype(BF16), preferred_element_type=F32)
    out = out + gate(2) * (o_w / jnp.sum(e, axis=0, keepdims=True))

    imp = jnp.dot(ovlt_ref[...], psum, preferred_element_type=F32, precision=lax.Precision.HIGHEST)
    jblk = lax.broadcasted_iota(jnp.int32, (n_slc, tq), 0).astype(F32)
    qblk = ((t0 + lax.broadcasted_iota(jnp.int32, (n_slc, tq), 1)) // SLC_BLOCK).astype(F32)
    forced = (jblk == 0) | (jblk == qblk) | (jblk == qblk - 1)
    score = jnp.where(forced, 1e9, jnp.where(jblk <= qblk, imp, -1e9))
    sel = jnp.zeros((n_slc, tq), F32)
    for _ in range(n_sel):
        m = jnp.max(score, axis=0, keepdims=True)
        first = jnp.min(jnp.where(score == m, jblk, float(n_slc)), axis=0, keepdims=True)
        pick = jblk == first
        sel = jnp.where(pick, 1.0, sel)
        score = jnp.where(pick, -jnp.inf, score)
    member = jnp.dot(expt_ref[...], sel.astype(BF16), preferred_element_type=F32) > 0.5
    km_scr[...] = jnp.where(member, 0.0, NEG)
    key_row = lax.broadcasted_iota(jnp.int32, member.shape, 0)
    kmf_scr[...] = jnp.where(member & (key_row < t0), 0.0, NEG)
    all_heads = lambda x: heads([x] * r)

    l = (jnp.dot(ks_ref[near_keys, :], qz, preferred_element_type=F32) + tzc_ref[...]
         + all_heads(km_scr[near_keys, :]))
    m = jnp.max(l, axis=0, keepdims=True)
    e = jnp.exp(l - m)
    den = jnp.sum(e, axis=0, keepdims=True)
    acc = jnp.dot(vst_ref[:, near_keys], e.astype(BF16), preferred_element_type=F32)
    far_bias = tzc_ref[0:1, :]

    def far_chunk(f, mda):
        m, den, acc = mda
        keys = pl.ds(pl.multiple_of(WINDOW + f * FAR_CHUNK, FAR_CHUNK), FAR_CHUNK)
        l = (jnp.dot(ks_ref[keys, :], qz, preferred_element_type=F32) + far_bias
             + all_heads(kmf_scr[keys, :]))
        m_new = jnp.maximum(m, jnp.max(l, axis=0, keepdims=True))
        alpha = jnp.exp(m - m_new)
        e = jnp.exp(l - m_new)
        pv = jnp.dot(vst_ref[:, keys], e.astype(BF16), preferred_element_type=F32)
        return m_new, alpha * den + jnp.sum(e, axis=0, keepdims=True), alpha * acc + pv

    n_far = (jnp.maximum(t0, WINDOW) - 1) // FAR_CHUNK
    m, den, acc = lax.fori_loop(0, n_far, far_chunk, (m, den, acc))
    out = out + gate(1) * (acc / den)

    o_ref[...] = jnp.concatenate([per_head(out, h) for h in range(r)], axis=0).T.astype(o_ref.dtype)


def nsa_attention_core(q, gates_pad, kc2, vct, bct, kv_pad, kvt_pad, tzw, tzc, ovlt, expt, batch, seq, n_cmp):
    n, hd = q.shape
    gq, dh, tq = N_KV_GROUPS, HEAD_DIM, Q_BLOCK
    r = N_HEADS // gq
    nqt = seq // tq
    spad = kv_pad.shape[1]
    ncp = kc2.shape[1]
    n_slc = seq // SLC_BLOCK
    near = tzw.shape[1]
    row_tile = lambda b, g, t: (b * nqt + t, g)
    key_cols = lambda branch: pl.BlockSpec((None, spad, gq * dh), lambda b, g, t: (b, 0, branch))
    val_rows = lambda branch: pl.BlockSpec((dh, spad), lambda b, g, t: (branch * gq + g, b))
    table = pl.BlockSpec((None, near, r * tq), lambda b, g, t: (g, 0, 0))
    const = lambda a: pl.BlockSpec(a.shape, lambda b, g, t: (0,) * a.ndim)
    return pl.pallas_call(
        functools.partial(_nsa_kernel, n_cmp=n_cmp, n_sel=min(N_SELECT, n_slc)),
        grid=(batch, gq, nqt),
        in_specs=[pl.BlockSpec((tq, r * dh), row_tile),
                  pl.BlockSpec((tq, gates_pad.shape[1]), lambda b, g, t: (b * nqt + t, 0)),
                  pl.BlockSpec((None, ncp, gq * dh), lambda b, g, t: (b, 0, 0)),
                  pl.BlockSpec((None, None, dh, ncp), lambda b, g, t: (b, g, 0, 0)),
                  pl.BlockSpec((None, ncp, r * tq), lambda b, g, t: (g, t, 0)),
                  key_cols(2), key_cols(4), val_rows(3), val_rows(5), table, table,
                  const(ovlt), const(expt)],
        out_specs=pl.BlockSpec((tq, r * dh), row_tile),
        out_shape=jax.ShapeDtypeStruct((n, hd), BF16),
        scratch_shapes=[pltpu.VMEM((spad, tq), F32), pltpu.VMEM((spad, tq), F32)],
        compiler_params=_cparams(("parallel", "parallel", "parallel")),
        name="nsa_attention",
    )(q, gates_pad, kc2, vct, bct, kv_pad, kv_pad, kvt_pad, kvt_pad, tzw, tzc, ovlt, expt)


def _bias_table_kernel(rb_ref, bkt_win_ref, bkt_causal_ref, bkt_cmp_ref, tzw_ref, tzc_ref, bct_ref, *, rows):
    h = pl.program_id(0)

    def lookup(bkt):
        acc = jnp.full(bkt.shape, NEG, F32)
        for k in range(N_BUCKETS):
            acc = jnp.where(bkt == k, rb_ref[k * N_HEADS + h], acc)
        return acc

    for src, dst in ((bkt_win_ref, tzw_ref), (bkt_causal_ref, tzc_ref), (bkt_cmp_ref, bct_ref)):
        def body(c, carry, src=src, dst=dst):
            sl = pl.ds(pl.multiple_of(c * rows, rows), rows)
            dst[sl, :] = lookup(src[sl, :])
            return carry
        lax.fori_loop(0, src.shape[0] // rows, body, 0)


def _nsa_tables(rel_bias, seq, n_cmp, ncp):
    tq = Q_BLOCK
    r = N_HEADS // N_KV_GROUPS
    dist_near = np.arange(tq)[None, :] + WINDOW - np.arange(NEAR)[:, None]
    bkt_near = _rel_bucket_np(dist_near)
    bkt_win = np.where((dist_near >= 0) & (dist_near < WINDOW), bkt_near, N_BUCKETS).astype(np.int32)
    bkt_causal = np.where(dist_near >= 0, bkt_near, N_BUCKETS).astype(np.int32)
    dist_c = np.arange(seq)[None, :] - (np.arange(ncp)[:, None] * CMP_STRIDE + CMP_BLOCK - 1)
    bkt_c = _rel_bucket_np(dist_c).reshape(ncp, seq // tq, tq).transpose(1, 0, 2).reshape(-1, tq)
    maps = [jnp.asarray(m) for m in (bkt_win, bkt_causal, bkt_c)]
    whole = lambda m: pl.BlockSpec(m.shape, lambda h, rb: (0, 0))
    per_head = lambda m: pl.BlockSpec((None, m.shape[0], tq), lambda h, rb: (h // r, 0, h % r))
    tzw, tzc, bct = pl.pallas_call(
        functools.partial(_bias_table_kernel, rows=64),
        grid_spec=pltpu.PrefetchScalarGridSpec(
            num_scalar_prefetch=1, grid=(N_HEADS,),
            in_specs=[whole(m) for m in maps], out_specs=[per_head(m) for m in maps]),
        out_shape=[jax.ShapeDtypeStruct((N_KV_GROUPS, m.shape[0], r * tq), F32) for m in maps],
        compiler_params=_cparams(("parallel",)),
        name="nsa_bias_tables",
    )(rel_bias.reshape(N_BUCKETS * N_HEADS), *maps)
    n_slc = seq // SLC_BLOCK
    cs = np.arange(ncp)[None, :] * CMP_STRIDE
    ss = np.arange(n_slc)[:, None] * SLC_BLOCK
    overlap_t = (cs < ss + SLC_BLOCK) & (cs + CMP_BLOCK > ss) & (np.arange(ncp)[None, :] < n_cmp)
    kpos = np.arange(WINDOW + seq)[:, None] - WINDOW
    expand_t = (kpos // SLC_BLOCK == np.arange(n_slc)[None, :]) & (kpos >= 0)
    return tzw, tzc, bct, jnp.asarray(overlap_t, F32), jnp.asarray(expand_t, BF16)


def _kv_kernel(x_ref, g_ref, w_ref, wt_ref, kv_ref, kvt_ref):
    xn = _rms(x_ref[...], g_ref[...]).astype(BF16)
    kv_ref[...] = jnp.dot(xn, w_ref[...], preferred_element_type=F32).astype(kv_ref.dtype)
    kvt_ref[...] = lax.dot_general(wt_ref[...], xn, (((1,), (1,)), ((), ())),
                                   preferred_element_type=F32).astype(kvt_ref.dtype)


def kv_projection(x, g, w_kv, tm=ROW_TILE):
    n, d = x.shape
    f = w_kv.shape[1]
    w16 = w_kv.astype(BF16)
    return pl.pallas_call(
        _kv_kernel,
        grid=(n // tm,),
        in_specs=[pl.BlockSpec((tm, d), lambda i: (i, 0)), pl.BlockSpec((1, d), lambda i: (0, 0)),
                  pl.BlockSpec((d, f), lambda i: (0, 0)), pl.BlockSpec((f, d), lambda i: (0, 0))],
        out_specs=[pl.BlockSpec((tm, f), lambda i: (i, 0)), pl.BlockSpec((f, tm), lambda i: (0, i))],
        out_shape=[jax.ShapeDtypeStruct((n, f), BF16), jax.ShapeDtypeStruct((f, n), BF16)],
        compiler_params=_cparams(("parallel",)),
        name="kv_projection",
    )(x, g.reshape(1, d), w16, w16.T)


def _nsa_shared_kv(x, batch, seq, kv_norm, w_kv, cmp_pos_k, cmp_pos_v, k_w1, k_w2, v_w1, v_w2):
    n, d = x.shape
    gq, dh = N_KV_GROUPS, HEAD_DIM
    f = w_kv.shape[1]
    kv, kvt = kv_projection(x, kv_norm, w_kv)
    kv5 = kv.reshape(batch, seq, N_KV_BRANCH, gq, dh)
    per_group = lambda j: jnp.transpose(kv5[:, :, j], (0, 2, 1, 3))
    n_cmp = (seq - CMP_BLOCK) // CMP_STRIDE + 1
    nhalf = seq // CMP_STRIDE

    def compress(t, pos, w1, w2):
        halves = t.reshape(batch * gq, nhalf, CMP_STRIDE * dh)
        return nsa_compress(halves, pos, w1.astype(BF16), w2.astype(BF16)).reshape(batch, gq, nhalf, dh)

    k_cmp = compress(per_group(0), cmp_pos_k, k_w1, k_w2)
    v_cmp = compress(per_group(1), cmp_pos_v, v_w1, v_w2)
    kc2 = jnp.transpose(k_cmp, (0, 2, 1, 3)).reshape(batch, nhalf, gq * dh)
    vct = jnp.transpose(v_cmp, (0, 1, 3, 2))
    kv_pad = jnp.pad(kv.reshape(batch, seq, f), ((0, 0), (WINDOW, 0), (0, 0)))
    kvt_pad = jnp.pad(kvt.reshape(f, batch, seq), ((0, 0), (0, 0), (WINDOW, 0))).reshape(f, -1)
    return kc2, vct, kv_pad, kvt_pad, n_cmp


def _q_kernel(x_ref, g_ref, wq_ref, wg_ref, q_ref, gates_ref):
    xn = _rms(x_ref[...], g_ref[...]).astype(BF16)
    q = jnp.dot(xn, wq_ref[...], preferred_element_type=F32) * (HEAD_DIM ** -0.5)
    q_ref[...] = q.astype(q_ref.dtype)
    gates_ref[...] = jax.nn.sigmoid(jnp.dot(xn, wg_ref[...], preferred_element_type=F32))


def q_projection(x, g, w_q, tm=ROW_TILE):
    n, d = x.shape
    hd = N_HEADS * HEAD_DIM
    w16 = w_q.astype(BF16)
    w_gate = jnp.pad(w16[:, hd:], ((0, 0), (0, 128 - (w_q.shape[1] - hd))))
    return pl.pallas_call(
        _q_kernel,
        grid=(n // tm,),
        in_specs=[pl.BlockSpec((tm, d), lambda i: (i, 0)), pl.BlockSpec((1, d), lambda i: (0, 0)),
                  pl.BlockSpec((d, hd), lambda i: (0, 0)), pl.BlockSpec((d, 128), lambda i: (0, 0))],
        out_specs=[pl.BlockSpec((tm, hd), lambda i: (i, 0)), pl.BlockSpec((tm, 128), lambda i: (i, 0))],
        out_shape=[jax.ShapeDtypeStruct((n, hd), BF16), jax.ShapeDtypeStruct((n, 128), F32)],
        compiler_params=_cparams(("parallel",)),
        name="q_projection",
    )(x, g.reshape(1, d), w16[:, :hd], w_gate)


def _nsa_mix(x, batch, seq, g_mix, w_q, shared, tables):
    kc2, vct, kv_pad, kvt_pad, n_cmp = shared
    tzw, tzc, bct, ovlt, expt = tables
    q, gates_pad = q_projection(x, g_mix, w_q)
    return nsa_attention_core(q, gates_pad, kc2, vct, bct, kv_pad, kvt_pad, tzw, tzc, ovlt, expt,
                              batch, seq, n_cmp)


def _dense_tail_kernel(a_ref, wo_ref, x_ref, g_ref, wgu_ref, wd_ref, gp_ref, wg_ref, p_ref, wp_ref, o_ref,
                       act_ref, *, tn):
    x = x_ref[...] + jnp.dot(a_ref[...], wo_ref[...], preferred_element_type=F32)
    xn = _rms(x, g_ref[...]).astype(BF16)
    ff = wd_ref.shape[0]
    for c in range(ff // tn):
        gate = jnp.dot(xn, wgu_ref[:, c * tn:(c + 1) * tn], preferred_element_type=F32)
        up = jnp.dot(xn, wgu_ref[:, ff + c * tn:ff + (c + 1) * tn], preferred_element_type=F32)
        act_ref[:, c * tn:(c + 1) * tn] = (_silu(gate) * up).astype(BF16)
    x = x + jnp.dot(act_ref[...], wd_ref[...], preferred_element_type=F32)
    o_ref[...] = _ple(x, gp_ref, wg_ref, p_ref, wp_ref)


def dense_layer_tail(a, w_o, x, g_ffn, w_gu, w_down, g_ple, wg_ple, p, wp_ple, tm=ROW_TILE, tn=512):
    n, d = x.shape
    k = a.shape[1]
    ff = w_down.shape[0]
    dp = p.shape[1]
    whole = lambda shape: pl.BlockSpec(shape, lambda i: (0, 0), pipeline_mode=pl.Buffered(1))
    rows = lambda width: pl.BlockSpec((tm, width), lambda i: (i, 0))
    return pl.pallas_call(
        functools.partial(_dense_tail_kernel, tn=min(tn, ff)),
        grid=(n // tm,),
        in_specs=[rows(k), whole((k, d)), rows(d), whole((1, d)), whole((d, 2 * ff)), whole((ff, d)),
                  whole((1, d)), whole((d, d)), rows(dp), whole((dp, d))],
        out_specs=rows(d),
        out_shape=jax.ShapeDtypeStruct((n, d), F32),
        scratch_shapes=[pltpu.VMEM((tm, ff), BF16)],
        compiler_params=_cparams(("parallel",)),
        name="dense_layer_tail",
    )(a, w_o, x, g_ffn.reshape(1, d), w_gu, w_down, g_ple.reshape(1, d), wg_ple, p, wp_ple)


def _moe_ffn(x, g, w_router, w_gu_all, w_down_all, layer, ple, tm=ROW_TILE):
    n, d = x.shape
    ne = w_router.shape[1]
    sel, w, counts = moe_route(x, g, w_router)
    padded = (counts[:, 0] + tm - 1) // tm * tm
    ends = jnp.cumsum(padded)
    n_tiles = (TOP_K * n) // tm + ne
    n_used = (ends[-1] // tm).astype(jnp.int32).reshape(1)
    tile_expert = jnp.searchsorted(ends, jnp.arange(n_tiles, dtype=jnp.int32) * tm, side='right')
    tile_expert = jnp.minimum(tile_expert, tile_expert[jnp.maximum(n_used[0] - 1, 0)]).astype(jnp.int32)
    starts = (ends - padded).astype(jnp.int32)
    pos = moe_slots(starts, sel)
    pos_flat = pos[:TOP_K].reshape(TOP_K * n)
    total = jnp.full((1,), n_tiles * tm, jnp.int32)
    unrouted = jnp.concatenate([starts + counts[:, 0], ends[-1:], ends, total]).astype(jnp.int32)
    inv = moe_inverse(pos_flat, unrouted, n, n_tiles * tm)
    flat = lambda w_all: w_all.reshape((-1,) + w_all.shape[2:])
    y_sorted = moe_experts(tile_expert + layer * ne, n_used, inv, x, g, flat(w_gu_all), flat(w_down_all))
    return moe_combine(pos_flat, w, x, y_sorted, *ple)


def kernel(x, p, g_mix, g_ffn, g_ple, g_final, rg_w_in, rg_conv_w, rg_conv_b, rg_w_a, rg_b_a, rg_w_x,
           rg_b_x, rg_lambda, rg_w_out, kv_norm, w_kv, cmp_pos_k, cmp_pos_v, cmp_k_w1, cmp_k_w2,
           cmp_v_w1, cmp_v_w2, rel_bias, nsa_w_q, nsa_w_o, ffn_w_gu, ffn_w_down, moe_w_router,
           moe_w_gu, moe_w_down, ple_w_proj, ple_w_gate):
    batch, seq, d = x.shape
    depth = p.shape[0]
    n_a = rg_w_in.shape[0]
    n = batch * seq
    x = x.reshape(n, d)
    p = p.reshape(depth, n, p.shape[-1])
    n_cmp = (seq - CMP_BLOCK) // CMP_STRIDE + 1
    tables = _nsa_tables(rel_bias, seq, n_cmp, seq // CMP_STRIDE)
    shared = None
    for i in range(depth):
        if i < n_a:
            c = rg_w_in.shape[2] // 2
            proj = norm_matmul(x, g_mix[i], rg_w_in[i].astype(BF16), (0,), 2 * c, 2 * c, lambda y: y, F32)
            mix = rg_lru_core(proj, batch, seq, rg_conv_w[i], rg_conv_b[i], _block_diag_chunks(rg_w_a[i]),
                              rg_b_a[i], _block_diag_chunks(rg_w_x[i]), rg_b_x[i], rg_lambda[i])
            w_o = rg_w_out[i].astype(BF16)
        else:
            mix = _nsa_mix(x, batch, seq, g_mix[i], nsa_w_q[i - n_a], shared, tables)
            w_o = nsa_w_o[i - n_a].astype(BF16)
        ple = (g_ple[i], ple_w_gate[i].astype(BF16), p[i], ple_w_proj[i].astype(BF16))
        if i % 2 == 0:
            assert i < depth - 1
            x = dense_layer_tail(mix, w_o, x, g_ffn[i], ffn_w_gu[i // 2].astype(BF16),
                                 ffn_w_down[i // 2].astype(BF16), *ple)
        else:
            x = matmul_residual(mix, w_o, x)
            x = _moe_ffn(x, g_ffn[i], moe_w_router[i // 2], moe_w_gu, moe_w_down, i // 2,
                         ple + ((g_final,) if i == depth - 1 else ()))
        if i == n_a - 1:
            shared = _nsa_shared_kv(x, batch, seq, kv_norm, w_kv, cmp_pos_k, cmp_pos_v,
                                    cmp_k_w1, cmp_k_w2, cmp_v_w1, cmp_v_w2)
    return x.reshape(batch, seq, d)
```

```python
import functools
import math

import numpy as np
import jax
import jax.numpy as jnp
from jax import lax
from jax.experimental import pallas as pl
from jax.experimental.pallas import tpu as pltpu

F32 = jnp.float32
BF16 = jnp.bfloat16

EPS = 1e-6
CONV_WIDTH = 4
LRU_BLOCKS = 16
LRU_C = 8.0
N_HEADS = 16
N_KV_GROUPS = 2
GROUP = N_HEADS // N_KV_GROUPS
HEAD_DIM = 64
N_KV_BRANCH = 6
CMP_BLOCK = 32
CMP_STRIDE = 16
SLC_BLOCK = 64
N_SELECT = 8
WINDOW = 512
Q_BLOCK = 128
N_BUCKETS = 32
MAX_DISTANCE = 128
TOP_K = 2
NEAR = WINDOW + Q_BLOCK
FAR_CHUNK = 512
NEG = -1e30

VMEM_LIMIT_V7X = 56 * 1024 * 1024
ROW_TILE = 512


def _cparams(sem):
    return pltpu.CompilerParams(dimension_semantics=sem, vmem_limit_bytes=VMEM_LIMIT_V7X)


def _rms(x, g):
    return x * lax.rsqrt(jnp.mean(x * x, axis=-1, keepdims=True) + EPS) * g


def _silu(x):
    return x * jax.nn.sigmoid(x)


def _norm_mm_kernel(x_ref, g_ref, *refs, n_w, epilogue):
    w_refs, o_ref, xn_ref = refs[:n_w], refs[n_w], refs[n_w + 1]

    @pl.when(pl.program_id(1) == 0)
    def _():
        xn_ref[...] = _rms(x_ref[...], g_ref[...]).astype(BF16)

    xn = xn_ref[...]
    outs = [jnp.dot(xn, w[...], preferred_element_type=F32) for w in w_refs]
    o_ref[...] = epilogue(*outs).astype(o_ref.dtype)


def norm_matmul(x, g, w, col_blocks, n_out, tn, epilogue, out_dtype, tm=ROW_TILE):
    n, d = x.shape
    grid = (n // tm, n_out // tn)
    w_specs = [pl.BlockSpec((d, tn), functools.partial(lambda i, j, cb: (0, cb + j), cb=cb))
               for cb in col_blocks]
    return pl.pallas_call(
        functools.partial(_norm_mm_kernel, n_w=len(col_blocks), epilogue=epilogue),
        grid=grid,
        in_specs=[pl.BlockSpec((tm, d), lambda i, j: (i, 0)),
                  pl.BlockSpec((1, d), lambda i, j: (0, 0))] + w_specs,
        out_specs=pl.BlockSpec((tm, tn), lambda i, j: (i, j)),
        out_shape=jax.ShapeDtypeStruct((n, n_out), out_dtype),
        scratch_shapes=[pltpu.VMEM((tm, d), BF16)],
        compiler_params=_cparams(("parallel", "arbitrary")),
    )(x, g.reshape(1, d), *([w] * len(col_blocks)))


def _mm_res_kernel(a_ref, w_ref, res_ref, *refs, scale_col):
    if scale_col is None:
        (o_ref,) = refs
        y = jnp.dot(a_ref[...], w_ref[...], preferred_element_type=F32)
    else:
        s_ref, o_ref = refs
        y = jnp.dot(a_ref[...], w_ref[...], preferred_element_type=F32)
        y = y * s_ref[:, scale_col:scale_col + 1]
    o_ref[...] = res_ref[...] + y


def matmul_residual(a, w, res, scale=None, scale_col=None, tm=ROW_TILE, tn=1024):
    n, k = a.shape
    d = w.shape[1]
    in_specs = [pl.BlockSpec((tm, k), lambda i, j: (i, 0)),
                pl.BlockSpec((k, tn), lambda i, j: (0, j)),
                pl.BlockSpec((tm, tn), lambda i, j: (i, j))]
    args = [a, w, res]
    if scale is not None:
        in_specs.append(pl.BlockSpec((tm, scale.shape[1]), lambda i, j: (i, 0)))
        args.append(scale)
    return pl.pallas_call(
        functools.partial(_mm_res_kernel, scale_col=scale_col),
        grid=(n // tm, d // tn),
        in_specs=in_specs,
        out_specs=pl.BlockSpec((tm, tn), lambda i, j: (i, j)),
        out_shape=jax.ShapeDtypeStruct((n, d), F32),
        compiler_params=_cparams(("parallel", "arbitrary")),
    )(*args)


BD = 256
SCAN_ROWS = 8


def _rg_kernel(gate_ref, xr_ref, cw_ref, cb_ref, wa_ref, ba_ref, wx_ref, bx_ref, lam_ref,
               o_ref, xe_ref, h_ref, a_ref, u_ref):
    t, c = xr_ref.shape

    @pl.when(pl.program_id(1) == 0)
    def _():
        xe_ref[0:8, :] = jnp.zeros((8, c), F32)
        h_ref[...] = jnp.zeros_like(h_ref)

    xe_ref[8:, :] = xr_ref[...]
    xc = cb_ref[...] + sum(
        cw_ref[k:k + 1, :] * xe_ref[8 - (CONV_WIDTH - 1) + k: 8 - (CONV_WIDTH - 1) + k + t, :]
        for k in range(CONV_WIDTH))
    xe_ref[0:8, :] = xr_ref[t - 8:, :]

    lam = -lam_ref[...]
    softplus_neg_lam = jnp.maximum(lam, 0.0) + jnp.log1p(jnp.exp(-jnp.abs(lam)))
    for cblk in range(c // BD):
        sl = slice(cblk * BD, (cblk + 1) * BD)
        xb = xc[:, sl]
        xb16 = xb.astype(BF16)
        r = jax.nn.sigmoid(jnp.dot(xb16, wa_ref[cblk], preferred_element_type=F32) + ba_ref[:, sl])
        i = jax.nn.sigmoid(jnp.dot(xb16, wx_ref[cblk], preferred_element_type=F32) + bx_ref[:, sl])
        log_a = -LRU_C * r * softplus_neg_lam[:, sl]
        a = jnp.exp(log_a)
        a_ref[:, sl] = a
        u_ref[:, sl] = jnp.sqrt(1.0 - a * a) * (i * xb)

    row = lax.broadcasted_iota(jnp.int32, (SCAN_ROWS, c), 0)

    def scan_tile(k, h_prev):
        rows = pl.ds(pl.multiple_of(k * SCAN_ROWS, SCAN_ROWS), SCAN_ROWS)
        a = a_ref[rows, :]
        u = u_ref[rows, :]
        for d in (1, 2, 4):
            a_s = jnp.where(row >= d, pltpu.roll(a, d, 0), 1.0)
            u_s = jnp.where(row >= d, pltpu.roll(u, d, 0), 0.0)
            u = a * u_s + u
            a = a * a_s
        h = a * h_prev + u
        u_ref[rows, :] = h
        return jnp.broadcast_to(h[SCAN_ROWS - 1:SCAN_ROWS, :], (SCAN_ROWS, c))

    h_ref[...] = lax.fori_loop(0, t // SCAN_ROWS, scan_tile, h_ref[...])
    o_ref[...] = (jax.nn.gelu(gate_ref[...]) * u_ref[...]).astype(o_ref.dtype)


def rg_lru_core(proj, batch, seq, conv_w, conv_b, wa_bd, b_a, wx_bd, b_x, lam, t_chunk=512):
    n, c2 = proj.shape
    c = c2 // 2
    nt = seq // t_chunk
    vec = lambda v: v.reshape(1, c)
    row_spec = lambda col: pl.BlockSpec((t_chunk, c), lambda b, t: (b * nt + t, col))
    const = lambda shape: pl.BlockSpec(shape, lambda b, t: (0,) * len(shape))
    return pl.pallas_call(
        _rg_kernel,
        grid=(batch, nt),
        in_specs=[row_spec(0), row_spec(1), const((CONV_WIDTH, c)), const((1, c)),
                  const(wa_bd.shape), const((1, c)), const(wx_bd.shape), const((1, c)), const((1, c))],
        out_specs=pl.BlockSpec((t_chunk, c), lambda b, t: (b * nt + t, 0)),
        out_shape=jax.ShapeDtypeStruct((n, c), BF16),
        scratch_shapes=[pltpu.VMEM((t_chunk + 8, c), F32), pltpu.VMEM((SCAN_ROWS, c), F32),
                        pltpu.VMEM((t_chunk, c), F32), pltpu.VMEM((t_chunk, c), F32)],
        compiler_params=_cparams(("parallel", "arbitrary")),
    )(proj, proj, conv_w, vec(conv_b), wa_bd, vec(b_a), wx_bd, vec(b_x), vec(lam))


def _block_diag_chunks(w):
    nb, bw, _ = w.shape
    per = BD // bw
    w = w.reshape(nb // per, per, bw, bw)
    eye = jnp.eye(per, dtype=w.dtype)
    out = jnp.einsum('cpij,pq->cpiqj', w, eye).reshape(nb // per, BD, BD)
    return out.astype(BF16)


def _ple(x, g_ref, wg_ref, p_ref, wp_ref):
    xn = _rms(x, g_ref[...]).astype(BF16)
    gate = jax.nn.sigmoid(jnp.dot(xn, wg_ref[...], preferred_element_type=F32))
    proj = jnp.dot(p_ref[...].astype(BF16), wp_ref[...], preferred_element_type=F32)
    return x + gate * proj


SEL_ROWS = 8


def _route_kernel(x_ref, g_ref, wt_ref, sel_ref, w_ref, cnt_ref, carry_ref):
    @pl.when(pl.program_id(0) == 0)
    def _():
        carry_ref[...] = jnp.zeros_like(carry_ref)

    xn = _rms(x_ref[...], g_ref[...])
    logits = lax.dot_general(wt_ref[...], xn, (((1,), (1,)), ((), ())), preferred_element_type=F32,
                             precision=lax.Precision.HIGHEST)
    ne, tm = logits.shape
    row = lax.broadcasted_iota(jnp.int32, (ne, tm), 0).astype(F32)
    m1 = jnp.max(logits, axis=0, keepdims=True)
    i1 = jnp.min(jnp.where(logits == m1, row, float(ne)), axis=0, keepdims=True)
    rest = jnp.where(row == i1, -jnp.inf, logits)
    m2 = jnp.max(rest, axis=0, keepdims=True)
    i2 = jnp.min(jnp.where(rest == m2, row, float(ne)), axis=0, keepdims=True)
    e2 = jnp.exp(m2 - m1)
    denom = 1.0 + e2
    ind = jnp.where((row == i1) | (row == i2), 1.0, 0.0)
    earlier = (lax.broadcasted_iota(jnp.int32, (tm, tm), 0) < lax.broadcasted_iota(jnp.int32, (tm, tm), 1))
    rank = jnp.dot(ind.astype(BF16), jnp.where(earlier, 1.0, 0.0).astype(BF16),
                   preferred_element_type=F32) + carry_ref[:, 0:1]
    r1 = jnp.sum(jnp.where(row == i1, rank, 0.0), axis=0, keepdims=True)
    r2 = jnp.sum(jnp.where(row == i2, rank, 0.0), axis=0, keepdims=True)
    carry_ref[...] = carry_ref[...] + jnp.sum(ind, axis=1, keepdims=True)
    pick = lambda vals: functools.reduce(
        lambda acc, kv: jnp.where(row == float(kv[0]), kv[1], acc), enumerate(vals), jnp.zeros((ne, tm), F32))
    sel_ref[...] = pick([i1, i2, r1, r2]).astype(jnp.int32)
    w_ref[...] = pick([1.0 / denom, e2 / denom])
    cnt_ref[...] = carry_ref[...].astype(jnp.int32)


def moe_route(x, g, w_router, tm=ROW_TILE):
    n, d = x.shape
    ne = w_router.shape[1]
    assert ne == SEL_ROWS
    return pl.pallas_call(
        _route_kernel,
        grid=(n // tm,),
        in_specs=[pl.BlockSpec((tm, d), lambda i: (i, 0)), pl.BlockSpec((1, d), lambda i: (0, 0)),
                  pl.BlockSpec((ne, d), lambda i: (0, 0))],
        out_specs=[pl.BlockSpec((ne, tm), lambda i: (0, i)), pl.BlockSpec((ne, tm), lambda i: (0, i)),
                   pl.BlockSpec((ne, 128), lambda i: (0, 0))],
        out_shape=[jax.ShapeDtypeStruct((ne, n), jnp.int32), jax.ShapeDtypeStruct((ne, n), F32),
                   jax.ShapeDtypeStruct((ne, 128), jnp.int32)],
        scratch_shapes=[pltpu.VMEM((ne, 128), F32)],
        compiler_params=_cparams(("arbitrary",)),
        name="moe_route",
    )(x, g.reshape(1, d), w_router.T)


def _slot_kernel(starts_ref, sel_ref, pos_ref):
    sel = sel_ref[...]
    start_of = lambda e_row: functools.reduce(
        lambda acc, e: jnp.where(e_row == e, starts_ref[e], acc), range(starts_ref.shape[0]),
        jnp.zeros_like(e_row))
    row = lax.broadcasted_iota(jnp.int32, sel.shape, 0)
    p1 = start_of(sel[0:1]) + sel[2:3]
    p2 = start_of(sel[1:2]) + sel[3:4]
    pos_ref[...] = jnp.where(row == 0, p1, jnp.where(row == 1, p2, 0))


def moe_slots(starts, sel, tm=2048):
    ne, n = sel.shape
    return pl.pallas_call(
        _slot_kernel,
        grid_spec=pltpu.PrefetchScalarGridSpec(
            num_scalar_prefetch=1, grid=(n // tm,),
            in_specs=[pl.BlockSpec((ne, tm), lambda i, s: (0, i))],
            out_specs=pl.BlockSpec((ne, tm), lambda i, s: (0, i))),
        out_shape=jax.ShapeDtypeStruct((ne, n), jnp.int32),
        compiler_params=_cparams(("parallel",)),
        name="moe_slots",
    )(starts, sel)


def _row_copy(src, src_row, dst, dst_row, sem):
    return pltpu.make_async_copy(src.at[pl.ds(src_row, 1)], dst.at[pl.ds(dst_row, 1)], sem)


def _inverse_kernel(pos_ref, pad_ref, inv_ref, *, n):
    def clear(s, c):
        inv_ref[s] = 0
        return c

    def fill(t, c):
        for k in range(TOP_K):
            inv_ref[pos_ref[k * n + t]] = t
        return c

    n_ranges = pad_ref.shape[0] // 2
    for e in range(n_ranges):
        lax.fori_loop(pad_ref[e], pad_ref[n_ranges + e], clear, 0)
    lax.fori_loop(0, n, fill, 0, unroll=8)


def moe_inverse(pos_flat, pad_bounds, n, n_rows):
    return pl.pallas_call(
        functools.partial(_inverse_kernel, n=n),
        in_specs=[pl.BlockSpec(memory_space=pltpu.SMEM), pl.BlockSpec(memory_space=pltpu.SMEM)],
        out_specs=pl.BlockSpec(memory_space=pltpu.SMEM),
        out_shape=jax.ShapeDtypeStruct((n_rows,), jnp.int32),
        name="moe_inverse",
    )(pos_flat, pad_bounds)


def _experts_kernel(te_ref, nu_ref, inv_ref, x_hbm, g_ref, wgu_hbm, wd_hbm, o_ref, xbuf_ref, act_ref, wgu_ref,
                    wd_ref, stage_gu, stage_d, sems, wsems, *, tm, tn):
    i = pl.program_id(0)
    n_used = nu_ref[0]
    ff = wd_ref.shape[0]

    def load_expert(e):
        def stream(n_chunks, src_of, stage, dst_of):
            copy = lambda c, slot: pltpu.make_async_copy(src_of(c), stage.at[slot], wsems.at[slot])
            copy(0, 0).start()

            def body(c, carry):
                slot = lax.rem(c, 2)

                @pl.when(c + 1 < n_chunks)
                def _():
                    copy(c + 1, 1 - slot).start()

                copy(c, slot).wait()
                dst_of(c)[...] = stage[slot].astype(BF16)
                return carry

            lax.fori_loop(0, n_chunks, body, 0)

        cols = lambda c: pl.ds(pl.multiple_of(c * tn, tn), tn)
        stream(2 * ff // tn, lambda c: wgu_hbm.at[e, :, cols(c)], stage_gu, lambda c: wgu_ref.at[:, cols(c)])
        stream(ff // tn, lambda c: wd_hbm.at[e, cols(c), :], stage_d, lambda c: wd_ref.at[cols(c), :])

    def start_gather(tile, slot):
        def body(t, c):
            _row_copy(x_hbm, inv_ref[tile * tm + t], xbuf_ref.at[slot], t, sems.at[slot]).start()
            return c
        lax.fori_loop(0, tm, body, 0, unroll=8)

    def wait_gather(slot):
        pltpu.make_async_copy(x_hbm.at[pl.ds(0, tm)], xbuf_ref.at[slot], sems.at[slot]).wait()

    last = pl.num_programs(0) - 1
    slot = lax.rem(i, 2)

    @pl.when(i == 0)
    def _():
        start_gather(0, 0)

    @pl.when(i < n_used)
    def _():
        @pl.when((i == 0) | (te_ref[i] != te_ref[jnp.maximum(i - 1, 0)]))
        def _():
            load_expert(te_ref[i])

        wait_gather(slot)
        xn = _rms(xbuf_ref[slot], g_ref[...]).astype(BF16)
        n_chunks = ff // tn
        nxt = jnp.minimum(i + 1, last) * tm
        for c in range(n_chunks):
            for t in range(c * tm // n_chunks, (c + 1) * tm // n_chunks):
                _row_copy(x_hbm, inv_ref[nxt + t], xbuf_ref.at[1 - slot], t, sems.at[1 - slot]).start()
            gate = jnp.dot(xn, wgu_ref[:, c * tn:(c + 1) * tn], preferred_element_type=F32)
            up = jnp.dot(xn, wgu_ref[:, ff + c * tn:ff + (c + 1) * tn], preferred_element_type=F32)
            act_ref[:, c * tn:(c + 1) * tn] = (_silu(gate) * up).astype(BF16)
        o_ref[...] = jnp.dot(act_ref[...], wd_ref[...], preferred_element_type=F32)

        @pl.when(i == last)
        def _():
            wait_gather(1 - slot)

    @pl.when(i >= n_used)
    def _():
        @pl.when(i == n_used)
        def _():
            wait_gather(slot)

        o_ref[...] = jnp.zeros_like(o_ref)


def moe_experts(tile_expert, n_used, inv, x, g, w_gu, w_down, tm=ROW_TILE, tn=512):
    n, d = x.shape
    p = inv.shape[0]
    ff = w_down.shape[1]
    tn = min(tn, ff)
    hbm = pl.BlockSpec(memory_space=pl.ANY)
    return pl.pallas_call(
        functools.partial(_experts_kernel, tm=tm, tn=tn),
        grid_spec=pltpu.PrefetchScalarGridSpec(
            num_scalar_prefetch=3, grid=(p // tm,),
            in_specs=[hbm, pl.BlockSpec((1, d), lambda i, te, nu, inv: (0, 0)), hbm, hbm],
            out_specs=pl.BlockSpec((tm, d), lambda i, te, nu, inv: (i, 0)),
            scratch_shapes=[pltpu.VMEM((2, tm, d), F32), pltpu.VMEM((tm, ff), BF16),
                            pltpu.VMEM((d, 2 * ff), BF16), pltpu.VMEM((ff, d), BF16),
                            pltpu.VMEM((2, d, tn), w_gu.dtype), pltpu.VMEM((2, tn, d), w_down.dtype),
                            pltpu.SemaphoreType.DMA((2,)), pltpu.SemaphoreType.DMA((2,))]),
        out_shape=jax.ShapeDtypeStruct((p, d), F32),
        compiler_params=_cparams(("arbitrary",)),
        name="moe_experts",
    )(tile_expert, n_used, inv, x, g.reshape(1, d), w_gu, w_down)


def _combine_kernel(pos_ref, w_ref, x_ref, y_hbm, gp_ref, wg_ref, p_ref, wp_ref, *refs, n, tm, final_norm):
    if final_norm:
        gf_ref, o_ref, buf_ref, sem = refs
    else:
        o_ref, buf_ref, sem = refs
    base = pl.program_id(0) * tm

    def start(t, c):
        for k in range(TOP_K):
            _row_copy(y_hbm, pos_ref[k * n + base + t], buf_ref.at[k], t, sem).start()
        return c

    lax.fori_loop(0, tm, start, 0, unroll=8)
    diag = (lax.broadcasted_iota(jnp.int32, (tm, tm), 0) == lax.broadcasted_iota(jnp.int32, (tm, tm), 1))
    wcol = [jnp.sum(jnp.where(diag, w_ref[k:k + 1, :], 0.0), axis=1, keepdims=True) for k in range(TOP_K)]
    for k in range(TOP_K):
        pltpu.make_async_copy(y_hbm.at[pl.ds(0, tm)], buf_ref.at[k], sem).wait()
    y = _ple(x_ref[...] + sum(wcol[k] * buf_ref[k] for k in range(TOP_K)), gp_ref, wg_ref, p_ref, wp_ref)
    o_ref[...] = _rms(y, gf_ref[...]) if final_norm else y


def moe_combine(pos_flat, w, x, y_sorted, g_ple, wg_ple, p, wp_ple, g_final=None, tm=ROW_TILE):
    n, d = x.shape
    dp = p.shape[1]
    const = lambda shape: pl.BlockSpec(shape, lambda i, pos: (0, 0), pipeline_mode=pl.Buffered(1))
    in_specs = [pl.BlockSpec((w.shape[0], tm), lambda i, pos: (0, i)),
                pl.BlockSpec((tm, d), lambda i, pos: (i, 0)),
                pl.BlockSpec(memory_space=pl.ANY),
                const((1, d)), const((d, d)), pl.BlockSpec((tm, dp), lambda i, pos: (i, 0)), const((dp, d))]
    args = [pos_flat, w, x, y_sorted, g_ple.reshape(1, d), wg_ple, p, wp_ple]
    if g_final is not None:
        in_specs.append(const((1, d)))
        args.append(g_final.reshape(1, d))
    return pl.pallas_call(
        functools.partial(_combine_kernel, n=n, tm=tm, final_norm=g_final is not None),
        grid_spec=pltpu.PrefetchScalarGridSpec(
            num_scalar_prefetch=1, grid=(n // tm,),
            in_specs=in_specs,
            out_specs=pl.BlockSpec((tm, d), lambda i, pos: (i, 0)),
            scratch_shapes=[pltpu.VMEM((TOP_K, tm, d), F32), pltpu.SemaphoreType.DMA]),
        out_shape=jax.ShapeDtypeStruct((n, d), F32),
        compiler_params=_cparams(("arbitrary",)),
        name="moe_combine",
    )(*args)


def _compress_kernel(h_ref, pos_ref, w1_ref, w2_ref, o_ref):
    half = h_ref.shape[1]
    h = h_ref[...].astype(BF16)
    p_lo = jnp.dot(h, w1_ref[0:half, :], preferred_element_type=F32)
    p_hi = jnp.dot(h, w1_ref[half:, :], preferred_element_type=F32)
    nh = p_hi.shape[0]
    posb = jnp.dot(pos_ref[...].astype(BF16), w1_ref[...], preferred_element_type=F32)
    hid = p_lo + pltpu.roll(p_hi, nh - 1, 0) + posb
    o_ref[...] = jnp.dot(jax.nn.gelu(hid).astype(BF16), w2_ref[...],
                         preferred_element_type=F32).astype(o_ref.dtype)


def nsa_compress(halves, pos, w1, w2):
    bg, nh, hw = halves.shape
    hid = w1.shape[1]
    dh = w2.shape[1]
    return pl.pallas_call(
        _compress_kernel,
        grid=(bg,),
        in_specs=[pl.BlockSpec((None, nh, hw), lambda i: (i, 0, 0)),
                  pl.BlockSpec((1, 2 * hw), lambda i: (0, 0)),
                  pl.BlockSpec((2 * hw, hid), lambda i: (0, 0)),
                  pl.BlockSpec((hid, dh), lambda i: (0, 0))],
        out_specs=pl.BlockSpec((None, nh, dh), lambda i: (i, 0, 0)),
        out_shape=jax.ShapeDtypeStruct((bg, nh, dh), BF16),
        compiler_params=_cparams(("parallel",)),
    )(halves, pos.reshape(1, 2 * hw), w1, w2)


def _rel_bucket_np(dist):
    n = np.maximum(dist, 0)
    max_exact = N_BUCKETS // 2
    nf = np.maximum(n, 1).astype(np.float32)
    large = max_exact + (np.log(nf / max_exact) / np.float32(math.log(MAX_DISTANCE / max_exact))
                         * (N_BUCKETS - max_exact)).astype(np.int32)
    return np.where(n < max_exact, n, np.minimum(large, N_BUCKETS - 1)).astype(np.int32)


def _nsa_kernel(q_ref, gt_ref, kc_ref, vct_ref, bct_ref, ks_ref, kw_ref, vst_ref, vwt_ref, tzw_ref, tzc_ref,
                ovlt_ref, expt_ref, o_ref, *, n_cmp, n_sel):
    tq = q_ref.shape[0]
    dh = HEAD_DIM
    r = q_ref.shape[1] // dh
    near = tzw_ref.shape[0]
    n_slc = ovlt_ref.shape[0]
    ncp = kc_ref.shape[0]
    g = pl.program_id(1)
    t0 = pl.multiple_of(pl.program_id(2) * tq, tq)
    near_keys = pl.ds(t0, near)
    heads = lambda blocks: jnp.concatenate(blocks, axis=1)
    per_head = lambda x, h: x[:, h * tq:(h + 1) * tq]

    qt = q_ref[...].astype(F32).T.astype(BF16)
    zero = jnp.zeros((dh, r * tq), BF16)
    qh = heads([qt[h * dh:(h + 1) * dh, :] for h in range(r)])
    qz = jnp.where(g == 0, jnp.concatenate([qh, zero], axis=0), jnp.concatenate([zero, qh], axis=0))
    gate_t = gt_ref[...].T

    def gate(branch):
        rows = [gate_t[branch * N_HEADS + gi * r:branch * N_HEADS + (gi + 1) * r, :] for gi in range(N_KV_GROUPS)]
        mine = jnp.where(g == 0, rows[0], rows[1])
        return heads([mine[h:h + 1, :] for h in range(r)])

    cblk = lax.broadcasted_iota(jnp.int32, (ncp, r * tq), 0)
    qpos = t0 + (lax.broadcasted_iota(jnp.int32, (ncp, r * tq), 1) & (tq - 1))
    mask_c = (qpos - (cblk * CMP_STRIDE + CMP_BLOCK - 1) >= 0) & (cblk < n_cmp)
    s = jnp.dot(kc_ref[...], qz, preferred_element_type=F32) + bct_ref[...]
    l = jnp.where(mask_c, s, NEG)
    e = jnp.where(mask_c, jnp.exp(l - jnp.max(l, axis=0, keepdims=True)), 0.0)
    p = e / jnp.maximum(jnp.sum(e, axis=0, keepdims=True), 1e-30)
    out = gate(0) * jnp.dot(vct_ref[...], p.astype(BF16), preferred_element_type=F32)
    psum = functools.reduce(lambda acc, h: acc + per_head(p, h), range(1, r), per_head(p, 0))

    key_exists = t0 + lax.broadcasted_iota(jnp.int32, (near, r * tq), 0) >= WINDOW
    s = jnp.dot(kw_ref[near_keys, :], qz, preferred_element_type=F32)
    l = jnp.where(key_exists, s + tzw_ref[...], NEG)
    e = jnp.exp(l - jnp.max(l, axis=0, keepdims=True))
    o_w = jnp.dot(vwt_ref[:, near_keys], e.astype(BF16), preferred_element_type=F32)
    out = out + gate(2) * (o_w / jnp.sum(e, axis=0, keepdims=True))

    imp = jnp.dot(ovlt_ref[...], psum, preferred_element_type=F32, precision=lax.Precision.HIGHEST)
    jblk = lax.broadcasted_iota(jnp.int32, (n_slc, tq), 0).astype(F32)
    qblk = ((t0 + lax.broadcasted_iota(jnp.int32, (n_slc, tq), 1)) // SLC_BLOCK).astype(F32)
    forced = (jblk == 0) | (jblk == qblk) | (jblk == qblk - 1)
    score = jnp.where(forced, 1e9, jnp.where(jblk <= qblk, imp, -1e9))
    sel = jnp.zeros((n_slc, tq), F32)
    for _ in range(n_sel):
        m = jnp.max(score, axis=0, keepdims=True)
        first = jnp.min(jnp.where(score == m, jblk, float(n_slc)), axis=0, keepdims=True)
        pick = jblk == first
        sel = jnp.where(pick, 1.0, sel)
        score = jnp.where(pick, -jnp.inf, score)
    sel16 = sel.astype(BF16)
    all_heads = lambda x: heads([x] * r)

    def member_mask(keys, also=True):
        member = jnp.dot(expt_ref[keys, :], sel16, preferred_element_type=F32) > 0.5
        return all_heads(jnp.where(member & also, 0.0, NEG))

    l = jnp.dot(ks_ref[near_keys, :], qz, preferred_element_type=F32) + tzc_ref[...] + member_mask(near_keys)
    m = jnp.max(l, axis=0, keepdims=True)
    e = jnp.exp(l - m)
    den = jnp.sum(e, axis=0, keepdims=True)
    acc = jnp.dot(vst_ref[:, near_keys], e.astype(BF16), preferred_element_type=F32)
    far_bias = tzc_ref[0:1, :]

    def far_chunk(f, mda):
        m, den, acc = mda
        first = pl.multiple_of(WINDOW + f * FAR_CHUNK, FAR_CHUNK)
        keys = pl.ds(first, FAR_CHUNK)
        before_near = first + lax.broadcasted_iota(jnp.int32, (FAR_CHUNK, tq), 0) < t0
        l = jnp.dot(ks_ref[keys, :], qz, preferred_element_type=F32) + far_bias + member_mask(keys, before_near)
        m_new = jnp.maximum(m, jnp.max(l, axis=0, keepdims=True))
        alpha = jnp.exp(m - m_new)
        e = jnp.exp(l - m_new)
        pv = jnp.dot(vst_ref[:, keys], e.astype(BF16), preferred_element_type=F32)
        return m_new, alpha * den + jnp.sum(e, axis=0, keepdims=True), alpha * acc + pv

    n_far = (jnp.maximum(t0, WINDOW) - 1) // FAR_CHUNK
    m, den, acc = lax.fori_loop(0, n_far, far_chunk, (m, den, acc))
    out = out + gate(1) * (acc / den)

    o_ref[...] = jnp.concatenate([per_head(out, h) for h in range(r)], axis=0).T.astype(o_ref.dtype)


def nsa_attention_core(q, gates_pad, kc2, vct, bct, kv_pad, kvt_pad, tzw, tzc, ovlt, expt, batch, seq, n_cmp):
    n, hd = q.shape
    gq, dh, tq = N_KV_GROUPS, HEAD_DIM, Q_BLOCK
    r = N_HEADS // gq
    nqt = seq // tq
    spad = kv_pad.shape[1]
    ncp = kc2.shape[1]
    n_slc = seq // SLC_BLOCK
    near = tzw.shape[1]
    row_tile = lambda b, g, t: (b * nqt + t, g)
    key_cols = lambda branch: pl.BlockSpec((None, spad, gq * dh), lambda b, g, t: (b, 0, branch))
    val_rows = lambda branch: pl.BlockSpec((dh, spad), lambda b, g, t: (branch * gq + g, b))
    table = pl.BlockSpec((None, near, r * tq), lambda b, g, t: (g, 0, 0))
    const = lambda a: pl.BlockSpec(a.shape, lambda b, g, t: (0,) * a.ndim)
    return pl.pallas_call(
        functools.partial(_nsa_kernel, n_cmp=n_cmp, n_sel=min(N_SELECT, n_slc)),
        grid=(batch, gq, nqt),
        in_specs=[pl.BlockSpec((tq, r * dh), row_tile),
                  pl.BlockSpec((tq, gates_pad.shape[1]), lambda b, g, t: (b * nqt + t, 0)),
                  pl.BlockSpec((None, ncp, gq * dh), lambda b, g, t: (b, 0, 0)),
                  pl.BlockSpec((None, None, dh, ncp), lambda b, g, t: (b, g, 0, 0)),
                  pl.BlockSpec((None, ncp, r * tq), lambda b, g, t: (g, t, 0)),
                  key_cols(2), key_cols(4), val_rows(3), val_rows(5), table, table,
                  const(ovlt), const(expt)],
        out_specs=pl.BlockSpec((tq, r * dh), row_tile),
        out_shape=jax.ShapeDtypeStruct((n, hd), BF16),
        compiler_params=_cparams(("parallel", "parallel", "parallel")),
        name="nsa_attention",
    )(q, gates_pad, kc2, vct, bct, kv_pad, kv_pad, kvt_pad, kvt_pad, tzw, tzc, ovlt, expt)


def _bias_table_kernel(rb_ref, bkt_win_ref, bkt_causal_ref, bkt_cmp_ref, tzw_ref, tzc_ref, bct_ref, *, rows):
    h = pl.program_id(0)

    def lookup(bkt):
        acc = jnp.full(bkt.shape, NEG, F32)
        for k in range(N_BUCKETS):
            acc = jnp.where(bkt == k, rb_ref[k * N_HEADS + h], acc)
        return acc

    for src, dst in ((bkt_win_ref, tzw_ref), (bkt_causal_ref, tzc_ref), (bkt_cmp_ref, bct_ref)):
        def body(c, carry, src=src, dst=dst):
            sl = pl.ds(pl.multiple_of(c * rows, rows), rows)
            dst[sl, :] = lookup(src[sl, :])
            return carry
        lax.fori_loop(0, src.shape[0] // rows, body, 0)


def _nsa_tables(rel_bias, seq, n_cmp, ncp):
    tq = Q_BLOCK
    r = N_HEADS // N_KV_GROUPS
    dist_near = np.arange(tq)[None, :] + WINDOW - np.arange(NEAR)[:, None]
    bkt_near = _rel_bucket_np(dist_near)
    bkt_win = np.where((dist_near >= 0) & (dist_near < WINDOW), bkt_near, N_BUCKETS).astype(np.int32)
    bkt_causal = np.where(dist_near >= 0, bkt_near, N_BUCKETS).astype(np.int32)
    dist_c = np.arange(seq)[None, :] - (np.arange(ncp)[:, None] * CMP_STRIDE + CMP_BLOCK - 1)
    bkt_c = _rel_bucket_np(dist_c).reshape(ncp, seq // tq, tq).transpose(1, 0, 2).reshape(-1, tq)
    maps = [jnp.asarray(m) for m in (bkt_win, bkt_causal, bkt_c)]
    whole = lambda m: pl.BlockSpec(m.shape, lambda h, rb: (0, 0))
    per_head = lambda m: pl.BlockSpec((None, m.shape[0], tq), lambda h, rb: (h // r, 0, h % r))
    tzw, tzc, bct = pl.pallas_call(
        functools.partial(_bias_table_kernel, rows=64),
        grid_spec=pltpu.PrefetchScalarGridSpec(
            num_scalar_prefetch=1, grid=(N_HEADS,),
            in_specs=[whole(m) for m in maps], out_specs=[per_head(m) for m in maps]),
        out_shape=[jax.ShapeDtypeStruct((N_KV_GROUPS, m.shape[0], r * tq), F32) for m in maps],
        compiler_params=_cparams(("parallel",)),
        name="nsa_bias_tables",
    )(rel_bias.reshape(N_BUCKETS * N_HEADS), *maps)
    n_slc = seq // SLC_BLOCK
    cs = np.arange(ncp)[None, :] * CMP_STRIDE
    ss = np.arange(n_slc)[:, None] * SLC_BLOCK
    overlap_t = (cs < ss + SLC_BLOCK) & (cs + CMP_BLOCK > ss) & (np.arange(ncp)[None, :] < n_cmp)
    kpos = np.arange(WINDOW + seq)[:, None] - WINDOW
    expand_t = (kpos // SLC_BLOCK == np.arange(n_slc)[None, :]) & (kpos >= 0)
    return tzw, tzc, bct, jnp.asarray(overlap_t, F32), jnp.asarray(expand_t, BF16)


def _kv_kernel(x_ref, g_ref, w_ref, wt_ref, kv_ref, kvt_ref):
    xn = _rms(x_ref[...], g_ref[...]).astype(BF16)
    kv_ref[...] = jnp.dot(xn, w_ref[...], preferred_element_type=F32).astype(kv_ref.dtype)
    kvt_ref[...] = lax.dot_general(wt_ref[...], xn, (((1,), (1,)), ((), ())),
                                   preferred_element_type=F32).astype(kvt_ref.dtype)


def kv_projection(x, g, w_kv, tm=ROW_TILE):
    n, d = x.shape
    f = w_kv.shape[1]
    w16 = w_kv.astype(BF16)
    return pl.pallas_call(
        _kv_kernel,
        grid=(n // tm,),
        in_specs=[pl.BlockSpec((tm, d), lambda i: (i, 0)), pl.BlockSpec((1, d), lambda i: (0, 0)),
                  pl.BlockSpec((d, f), lambda i: (0, 0)), pl.BlockSpec((f, d), lambda i: (0, 0))],
        out_specs=[pl.BlockSpec((tm, f), lambda i: (i, 0)), pl.BlockSpec((f, tm), lambda i: (0, i))],
        out_shape=[jax.ShapeDtypeStruct((n, f), BF16), jax.ShapeDtypeStruct((f, n), BF16)],
        compiler_params=_cparams(("parallel",)),
        name="kv_projection",
    )(x, g.reshape(1, d), w16, w16.T)


def _nsa_shared_kv(x, batch, seq, kv_norm, w_kv, cmp_pos_k, cmp_pos_v, k_w1, k_w2, v_w1, v_w2):
    n, d = x.shape
    gq, dh = N_KV_GROUPS, HEAD_DIM
    f = w_kv.shape[1]
    kv, kvt = kv_projection(x, kv_norm, w_kv)
    kv5 = kv.reshape(batch, seq, N_KV_BRANCH, gq, dh)
    per_group = lambda j: jnp.transpose(kv5[:, :, j], (0, 2, 1, 3))
    n_cmp = (seq - CMP_BLOCK) // CMP_STRIDE + 1
    nhalf = seq // CMP_STRIDE

    def compress(t, pos, w1, w2):
        halves = t.reshape(batch * gq, nhalf, CMP_STRIDE * dh)
        return nsa_compress(halves, pos, w1.astype(BF16), w2.astype(BF16)).reshape(batch, gq, nhalf, dh)

    k_cmp = compress(per_group(0), cmp_pos_k, k_w1, k_w2)
    v_cmp = compress(per_group(1), cmp_pos_v, v_w1, v_w2)
    kc2 = jnp.transpose(k_cmp, (0, 2, 1, 3)).reshape(batch, nhalf, gq * dh)
    vct = jnp.transpose(v_cmp, (0, 1, 3, 2))
    kv_pad = jnp.pad(kv.reshape(batch, seq, f), ((0, 0), (WINDOW, 0), (0, 0)))
    kvt_pad = jnp.pad(kvt.reshape(f, batch, seq), ((0, 0), (0, 0), (WINDOW, 0))).reshape(f, -1)
    return kc2, vct, kv_pad, kvt_pad, n_cmp


def _q_kernel(x_ref, g_ref, wq_ref, wg_ref, q_ref, gates_ref):
    xn = _rms(x_ref[...], g_ref[...]).astype(BF16)
    q = jnp.dot(xn, wq_ref[...], preferred_element_type=F32) * (HEAD_DIM ** -0.5)
    q_ref[...] = q.astype(q_ref.dtype)
    gates_ref[...] = jax.nn.sigmoid(jnp.dot(xn, wg_ref[...], preferred_element_type=F32))


def q_projection(x, g, w_q, tm=ROW_TILE):
    n, d = x.shape
    hd = N_HEADS * HEAD_DIM
    w16 = w_q.astype(BF16)
    w_gate = jnp.pad(w16[:, hd:], ((0, 0), (0, 128 - (w_q.shape[1] - hd))))
    return pl.pallas_call(
        _q_kernel,
        grid=(n // tm,),
        in_specs=[pl.BlockSpec((tm, d), lambda i: (i, 0)), pl.BlockSpec((1, d), lambda i: (0, 0)),
                  pl.BlockSpec((d, hd), lambda i: (0, 0)), pl.BlockSpec((d, 128), lambda i: (0, 0))],
        out_specs=[pl.BlockSpec((tm, hd), lambda i: (i, 0)), pl.BlockSpec((tm, 128), lambda i: (i, 0))],
        out_shape=[jax.ShapeDtypeStruct((n, hd), BF16), jax.ShapeDtypeStruct((n, 128), F32)],
        compiler_params=_cparams(("parallel",)),
        name="q_projection",
    )(x, g.reshape(1, d), w16[:, :hd], w_gate)


def _nsa_mix(x, batch, seq, g_mix, w_q, shared, tables):
    kc2, vct, kv_pad, kvt_pad, n_cmp = shared
    tzw, tzc, bct, ovlt, expt = tables
    q, gates_pad = q_projection(x, g_mix, w_q)
    return nsa_attention_core(q, gates_pad, kc2, vct, bct, kv_pad, kvt_pad, tzw, tzc, ovlt, expt,
                              batch, seq, n_cmp)


def _dense_tail_kernel(a_ref, wo_ref, x_ref, g_ref, wgu_ref, wd_ref, gp_ref, wg_ref, p_ref, wp_ref, o_ref,
                       act_ref, *, tn):
    x = x_ref[...] + jnp.dot(a_ref[...], wo_ref[...], preferred_element_type=F32)
    xn = _rms(x, g_ref[...]).astype(BF16)
    ff = wd_ref.shape[0]
    for c in range(ff // tn):
        gate = jnp.dot(xn, wgu_ref[:, c * tn:(c + 1) * tn], preferred_element_type=F32)
        up = jnp.dot(xn, wgu_ref[:, ff + c * tn:ff + (c + 1) * tn], preferred_element_type=F32)
        act_ref[:, c * tn:(c + 1) * tn] = (_silu(gate) * up).astype(BF16)
    x = x + jnp.dot(act_ref[...], wd_ref[...], preferred_element_type=F32)
    o_ref[...] = _ple(x, gp_ref, wg_ref, p_ref, wp_ref)


def dense_layer_tail(a, w_o, x, g_ffn, w_gu, w_down, g_ple, wg_ple, p, wp_ple, tm=ROW_TILE, tn=512):
    n, d = x.shape
    k = a.shape[1]
    ff = w_down.shape[0]
    dp = p.shape[1]
    whole = lambda shape: pl.BlockSpec(shape, lambda i: (0, 0), pipeline_mode=pl.Buffered(1))
    rows = lambda width: pl.BlockSpec((tm, width), lambda i: (i, 0))
    return pl.pallas_call(
        functools.partial(_dense_tail_kernel, tn=min(tn, ff)),
        grid=(n // tm,),
        in_specs=[rows(k), whole((k, d)), rows(d), whole((1, d)), whole((d, 2 * ff)), whole((ff, d)),
                  whole((1, d)), whole((d, d)), rows(dp), whole((dp, d))],
        out_specs=rows(d),
        out_shape=jax.ShapeDtypeStruct((n, d), F32),
        scratch_shapes=[pltpu.VMEM((tm, ff), BF16)],
        compiler_params=_cparams(("parallel",)),
        name="dense_layer_tail",
    )(a, w_o, x, g_ffn.reshape(1, d), w_gu, w_down, g_ple.reshape(1, d), wg_ple, p, wp_ple)


def _moe_ffn(x, g, w_router, w_gu_all, w_down_all, layer, ple, tm=ROW_TILE):
    n, d = x.shape
    ne = w_router.shape[1]
    sel, w, counts = moe_route(x, g, w_router)
    padded = (counts[:, 0] + tm - 1) // tm * tm
    ends = jnp.cumsum(padded)
    n_tiles = (TOP_K * n) // tm + ne
    n_used = (ends[-1] // tm).astype(jnp.int32).reshape(1)
    tile_expert = jnp.searchsorted(ends, jnp.arange(n_tiles, dtype=jnp.int32) * tm, side='right')
    tile_expert = jnp.minimum(tile_expert, tile_expert[jnp.maximum(n_used[0] - 1, 0)]).astype(jnp.int32)
    starts = (ends - padded).astype(jnp.int32)
    pos = moe_slots(starts, sel)
    pos_flat = pos[:TOP_K].reshape(TOP_K * n)
    total = jnp.full((1,), n_tiles * tm, jnp.int32)
    unrouted = jnp.concatenate([starts + counts[:, 0], ends[-1:], ends, total]).astype(jnp.int32)
    inv = moe_inverse(pos_flat, unrouted, n, n_tiles * tm)
    flat = lambda w_all: w_all.reshape((-1,) + w_all.shape[2:])
    y_sorted = moe_experts(tile_expert + layer * ne, n_used, inv, x, g, flat(w_gu_all), flat(w_down_all))
    return moe_combine(pos_flat, w, x, y_sorted, *ple)


def kernel(x, p, g_mix, g_ffn, g_ple, g_final, rg_w_in, rg_conv_w, rg_conv_b, rg_w_a, rg_b_a, rg_w_x,
           rg_b_x, rg_lambda, rg_w_out, kv_norm, w_kv, cmp_pos_k, cmp_pos_v, cmp_k_w1, cmp_k_w2,
           cmp_v_w1, cmp_v_w2, rel_bias, nsa_w_q, nsa_w_o, ffn_w_gu, ffn_w_down, moe_w_router,
           moe_w_gu, moe_w_down, ple_w_proj, ple_w_gate):
    batch, seq, d = x.shape
    depth = p.shape[0]
    n_a = rg_w_in.shape[0]
    n = batch * seq
    x = x.reshape(n, d)
    p = p.reshape(depth, n, p.shape[-1])
    n_cmp = (seq - CMP_BLOCK) // CMP_STRIDE + 1
    tables = _nsa_tables(rel_bias, seq, n_cmp, seq // CMP_STRIDE)
    shared = None
    for i in range(depth):
        if i < n_a:
            c = rg_w_in.shape[2] // 2
            proj = norm_matmul(x, g_mix[i], rg_w_in[i].astype(BF16), (0,), 2 * c, 2 * c, lambda y: y, F32)
            mix = rg_lru_core(proj, batch, seq, rg_conv_w[i], rg_conv_b[i], _block_diag_chunks(rg_w_a[i]),
                              rg_b_a[i], _block_diag_chunks(rg_w_x[i]), rg_b_x[i], rg_lambda[i])
            w_o = rg_w_out[i].astype(BF16)
        else:
            mix = _nsa_mix(x, batch, seq, g_mix[i], nsa_w_q[i - n_a], shared, tables)
            w_o = nsa_w_o[i - n_a].astype(BF16)
        ple = (g_ple[i], ple_w_gate[i].astype(BF16), p[i], ple_w_proj[i].astype(BF16))
        if i % 2 == 0:
            assert i < depth - 1
            x = dense_layer_tail(mix, w_o, x, g_ffn[i], ffn_w_gu[i // 2].astype(BF16),
                                 ffn_w_down[i // 2].astype(BF16), *ple)
        else:
            x = matmul_residual(mix, w_o, x)
            x = _moe_ffn(x, g_ffn[i], moe_w_router[i // 2], moe_w_gu, moe_w_down, i // 2,
                         ple + ((g_final,) if i == depth - 1 else ()))
        if i == n_a - 1:
            shared = _nsa_shared_kv(x, batch, seq, kv_norm, w_kv, cmp_pos_k, cmp_pos_v,
                                    cmp_k_w1, cmp_k_w2, cmp_v_w1, cmp_v_w2)
    return x.reshape(batch, seq, d)
```

```python
import functools
import math

import numpy as np
import jax
import jax.numpy as jnp
from jax import lax
from jax.experimental import pallas as pl
from jax.experimental.pallas import tpu as pltpu

F32 = jnp.float32
BF16 = jnp.bfloat16

EPS = 1e-6
CONV_WIDTH = 4
LRU_BLOCKS = 16
LRU_C = 8.0
N_HEADS = 16
N_KV_GROUPS = 2
GROUP = N_HEADS // N_KV_GROUPS
HEAD_DIM = 64
N_KV_BRANCH = 6
CMP_BLOCK = 32
CMP_STRIDE = 16
SLC_BLOCK = 64
N_SELECT = 8
WINDOW = 512
Q_BLOCK = 128
N_BUCKETS = 32
MAX_DISTANCE = 128
TOP_K = 2
NEAR = WINDOW + Q_BLOCK
FAR_CHUNK = 512
NEG = -1e30
LOG2_E = math.log2(math.e)

VMEM_LIMIT_V7X = 56 * 1024 * 1024
ROW_TILE = 512


def _cparams(sem):
    return pltpu.CompilerParams(dimension_semantics=sem, vmem_limit_bytes=VMEM_LIMIT_V7X)


def _rms(x, g):
    return x * lax.rsqrt(jnp.mean(x * x, axis=-1, keepdims=True) + EPS) * g


def _silu(x):
    return x * jax.nn.sigmoid(x)


def _mm_res_kernel(a_ref, w_ref, res_ref, *refs, scale_col):
    if scale_col is None:
        (o_ref,) = refs
        y = jnp.dot(a_ref[...], w_ref[...], preferred_element_type=F32)
    else:
        s_ref, o_ref = refs
        y = jnp.dot(a_ref[...], w_ref[...], preferred_element_type=F32)
        y = y * s_ref[:, scale_col:scale_col + 1]
    o_ref[...] = res_ref[...] + y


def matmul_residual(a, w, res, scale=None, scale_col=None, tm=ROW_TILE, tn=1024):
    n, k = a.shape
    d = w.shape[1]
    in_specs = [pl.BlockSpec((tm, k), lambda i, j: (i, 0)),
                pl.BlockSpec((k, tn), lambda i, j: (0, j)),
                pl.BlockSpec((tm, tn), lambda i, j: (i, j))]
    args = [a, w, res]
    if scale is not None:
        in_specs.append(pl.BlockSpec((tm, scale.shape[1]), lambda i, j: (i, 0)))
        args.append(scale)
    return pl.pallas_call(
        functools.partial(_mm_res_kernel, scale_col=scale_col),
        grid=(n // tm, d // tn),
        in_specs=in_specs,
        out_specs=pl.BlockSpec((tm, tn), lambda i, j: (i, j)),
        out_shape=jax.ShapeDtypeStruct((n, d), F32),
        compiler_params=_cparams(("parallel", "arbitrary")),
    )(*args)


BD = 256
SCAN_ROWS = 8


def _rg_kernel(x_ref, g_ref, win_ref, cw_ref, cb_ref, wa_ref, ba_ref, wx_ref, bx_ref, lam_ref,
               o_ref, xe_ref, h_ref, a_ref, u_ref, xn_ref):
    t, c = o_ref.shape

    @pl.when(pl.program_id(1) == 0)
    def _():
        xe_ref[0:8, :] = jnp.zeros((8, c), F32)
        h_ref[...] = jnp.zeros_like(h_ref)

    xn_ref[...] = _rms(x_ref[...], g_ref[...]).astype(BF16)
    xe_ref[8:, :] = jnp.dot(xn_ref[...], win_ref[:, c:], preferred_element_type=F32)
    xc = cb_ref[...] + sum(
        cw_ref[k:k + 1, :] * xe_ref[8 - (CONV_WIDTH - 1) + k: 8 - (CONV_WIDTH - 1) + k + t, :]
        for k in range(CONV_WIDTH))
    xe_ref[0:8, :] = xe_ref[t:t + 8, :]

    lam = -lam_ref[...]
    softplus_neg_lam = jnp.maximum(lam, 0.0) + jnp.log1p(jnp.exp(-jnp.abs(lam)))
    for cblk in range(c // BD):
        sl = slice(cblk * BD, (cblk + 1) * BD)
        xb = xc[:, sl]
        xb16 = xb.astype(BF16)
        r = jax.nn.sigmoid(jnp.dot(xb16, wa_ref[cblk], preferred_element_type=F32) + ba_ref[:, sl])
        i = jax.nn.sigmoid(jnp.dot(xb16, wx_ref[cblk], preferred_element_type=F32) + bx_ref[:, sl])
        log_a = -LRU_C * r * softplus_neg_lam[:, sl]
        a = jnp.exp(log_a)
        a_ref[:, sl] = a
        u_ref[:, sl] = jnp.sqrt(1.0 - a * a) * (i * xb)

    row = lax.broadcasted_iota(jnp.int32, (SCAN_ROWS, c), 0)

    def scan_tile(k, h_prev):
        rows = pl.ds(pl.multiple_of(k * SCAN_ROWS, SCAN_ROWS), SCAN_ROWS)
        a = a_ref[rows, :]
        u = u_ref[rows, :]
        for d in (1, 2, 4):
            a_s = jnp.where(row >= d, pltpu.roll(a, d, 0), 1.0)
            u_s = jnp.where(row >= d, pltpu.roll(u, d, 0), 0.0)
            u = a * u_s + u
            a = a * a_s
        h = a * h_prev + u
        u_ref[rows, :] = h
        return jnp.broadcast_to(h[SCAN_ROWS - 1:SCAN_ROWS, :], (SCAN_ROWS, c))

    h_ref[...] = lax.fori_loop(0, t // SCAN_ROWS, scan_tile, h_ref[...])
    gate = jnp.dot(xn_ref[...], win_ref[:, :c], preferred_element_type=F32)
    o_ref[...] = (jax.nn.gelu(gate) * u_ref[...]).astype(o_ref.dtype)


def rg_lru_core(x, g, w_in, batch, seq, conv_w, conv_b, wa_bd, b_a, wx_bd, b_x, lam, t_chunk=512):
    n, d = x.shape
    c = w_in.shape[1] // 2
    nt = seq // t_chunk
    vec = lambda v: v.reshape(1, -1)
    rows = lambda width: pl.BlockSpec((t_chunk, width), lambda b, t: (b * nt + t, 0))
    const = lambda shape: pl.BlockSpec(shape, lambda b, t: (0,) * len(shape))
    return pl.pallas_call(
        _rg_kernel,
        grid=(batch, nt),
        in_specs=[rows(d), const((1, d)), const(w_in.shape), const((CONV_WIDTH, c)), const((1, c)),
                  const(wa_bd.shape), const((1, c)), const(wx_bd.shape), const((1, c)), const((1, c))],
        out_specs=rows(c),
        out_shape=jax.ShapeDtypeStruct((n, c), BF16),
        scratch_shapes=[pltpu.VMEM((t_chunk + 8, c), F32), pltpu.VMEM((SCAN_ROWS, c), F32),
                        pltpu.VMEM((t_chunk, c), F32), pltpu.VMEM((t_chunk, c), F32),
                        pltpu.VMEM((t_chunk, d), BF16)],
        compiler_params=_cparams(("parallel", "arbitrary")),
        name="rg_lru_core",
    )(x, vec(g), w_in, conv_w, vec(conv_b), wa_bd, vec(b_a), wx_bd, vec(b_x), vec(lam))


def _block_diag_chunks(w):
    nb, bw, _ = w.shape
    per = BD // bw
    w = w.reshape(nb // per, per, bw, bw)
    eye = jnp.eye(per, dtype=w.dtype)
    out = jnp.einsum('cpij,pq->cpiqj', w, eye).reshape(nb // per, BD, BD)
    return out.astype(BF16)


def _ple(x, g_ref, wg_ref, p_ref, wp_ref):
    xn = _rms(x, g_ref[...]).astype(BF16)
    gate = jax.nn.sigmoid(jnp.dot(xn, wg_ref[...], preferred_element_type=F32))
    proj = jnp.dot(p_ref[...].astype(BF16), wp_ref[...], preferred_element_type=F32)
    return x + gate * proj


SEL_ROWS = 8


def _route_kernel(x_ref, g_ref, wt_ref, sel_ref, w_ref, cnt_ref, carry_ref):
    @pl.when(pl.program_id(0) == 0)
    def _():
        carry_ref[...] = jnp.zeros_like(carry_ref)

    xn = _rms(x_ref[...], g_ref[...])
    logits = lax.dot_general(wt_ref[...], xn, (((1,), (1,)), ((), ())), preferred_element_type=F32,
                             precision=lax.Precision.HIGHEST)
    ne, tm = logits.shape
    row = lax.broadcasted_iota(jnp.int32, (ne, tm), 0).astype(F32)
    m1 = jnp.max(logits, axis=0, keepdims=True)
    i1 = jnp.min(jnp.where(logits == m1, row, float(ne)), axis=0, keepdims=True)
    rest = jnp.where(row == i1, -jnp.inf, logits)
    m2 = jnp.max(rest, axis=0, keepdims=True)
    i2 = jnp.min(jnp.where(rest == m2, row, float(ne)), axis=0, keepdims=True)
    e2 = jnp.exp(m2 - m1)
    denom = 1.0 + e2
    ind = jnp.where((row == i1) | (row == i2), 1.0, 0.0)
    earlier = (lax.broadcasted_iota(jnp.int32, (tm, tm), 0) < lax.broadcasted_iota(jnp.int32, (tm, tm), 1))
    rank = jnp.dot(ind.astype(BF16), jnp.where(earlier, 1.0, 0.0).astype(BF16),
                   preferred_element_type=F32) + carry_ref[:, 0:1]
    r1 = jnp.sum(jnp.where(row == i1, rank, 0.0), axis=0, keepdims=True)
    r2 = jnp.sum(jnp.where(row == i2, rank, 0.0), axis=0, keepdims=True)
    carry_ref[...] = carry_ref[...] + jnp.sum(ind, axis=1, keepdims=True)
    pick = lambda vals: functools.reduce(
        lambda acc, kv: jnp.where(row == float(kv[0]), kv[1], acc), enumerate(vals), jnp.zeros((ne, tm), F32))
    sel_ref[...] = pick([i1, i2, r1, r2]).astype(jnp.int32)
    w_ref[...] = pick([1.0 / denom, e2 / denom])
    cnt_ref[...] = carry_ref[...].astype(jnp.int32)


def moe_route(x, g, w_router, tm=ROW_TILE):
    n, d = x.shape
    ne = w_router.shape[1]
    assert ne == SEL_ROWS
    return pl.pallas_call(
        _route_kernel,
        grid=(n // tm,),
        in_specs=[pl.BlockSpec((tm, d), lambda i: (i, 0)), pl.BlockSpec((1, d), lambda i: (0, 0)),
                  pl.BlockSpec((ne, d), lambda i: (0, 0))],
        out_specs=[pl.BlockSpec((ne, tm), lambda i: (0, i)), pl.BlockSpec((ne, tm), lambda i: (0, i)),
                   pl.BlockSpec((ne, 128), lambda i: (0, 0))],
        out_shape=[jax.ShapeDtypeStruct((ne, n), jnp.int32), jax.ShapeDtypeStruct((ne, n), F32),
                   jax.ShapeDtypeStruct((ne, 128), jnp.int32)],
        scratch_shapes=[pltpu.VMEM((ne, 128), F32)],
        compiler_params=_cparams(("arbitrary",)),
        name="moe_route",
    )(x, g.reshape(1, d), w_router.T)


def _slot_kernel(starts_ref, sel_ref, pos_ref):
    sel = sel_ref[...]
    start_of = lambda e_row: functools.reduce(
        lambda acc, e: jnp.where(e_row == e, starts_ref[e], acc), range(starts_ref.shape[0]),
        jnp.zeros_like(e_row))
    row = lax.broadcasted_iota(jnp.int32, sel.shape, 0)
    p1 = start_of(sel[0:1]) + sel[2:3]
    p2 = start_of(sel[1:2]) + sel[3:4]
    pos_ref[...] = jnp.where(row == 0, p1, jnp.where(row == 1, p2, 0))


def moe_slots(starts, sel, tm=2048):
    ne, n = sel.shape
    return pl.pallas_call(
        _slot_kernel,
        grid_spec=pltpu.PrefetchScalarGridSpec(
            num_scalar_prefetch=1, grid=(n // tm,),
            in_specs=[pl.BlockSpec((ne, tm), lambda i, s: (0, i))],
            out_specs=pl.BlockSpec((ne, tm), lambda i, s: (0, i))),
        out_shape=jax.ShapeDtypeStruct((ne, n), jnp.int32),
        compiler_params=_cparams(("parallel",)),
        name="moe_slots",
    )(starts, sel)


def _row_copy(src, src_row, dst, dst_row, sem):
    return pltpu.make_async_copy(src.at[pl.ds(src_row, 1)], dst.at[pl.ds(dst_row, 1)], sem)


def _inverse_kernel(pos_ref, pad_ref, inv_ref, *, n):
    def clear(s, c):
        inv_ref[s] = 0
        return c

    def fill(t, c):
        for k in range(TOP_K):
            inv_ref[pos_ref[k * n + t]] = t
        return c

    n_ranges = pad_ref.shape[0] // 2
    for e in range(n_ranges):
        lax.fori_loop(pad_ref[e], pad_ref[n_ranges + e], clear, 0)
    lax.fori_loop(0, n, fill, 0, unroll=8)


def moe_inverse(pos_flat, pad_bounds, n, n_rows):
    return pl.pallas_call(
        functools.partial(_inverse_kernel, n=n),
        in_specs=[pl.BlockSpec(memory_space=pltpu.SMEM), pl.BlockSpec(memory_space=pltpu.SMEM)],
        out_specs=pl.BlockSpec(memory_space=pltpu.SMEM),
        out_shape=jax.ShapeDtypeStruct((n_rows,), jnp.int32),
        name="moe_inverse",
    )(pos_flat, pad_bounds)


def _experts_kernel(te_ref, nu_ref, inv_ref, x_hbm, g_ref, wgu_hbm, wd_hbm, o_ref, xbuf_ref, act_ref, wgu_ref,
                    wd_ref, stage_gu, stage_d, sems, wsems, *, tm, tn):
    i = pl.program_id(0)
    n_used = nu_ref[0]
    ff = wd_ref.shape[0]

    def load_expert(e):
        def stream(n_chunks, src_of, stage, dst_of):
            copy = lambda c, slot: pltpu.make_async_copy(src_of(c), stage.at[slot], wsems.at[slot])
            copy(0, 0).start()

            def body(c, carry):
                slot = lax.rem(c, 2)

                @pl.when(c + 1 < n_chunks)
                def _():
                    copy(c + 1, 1 - slot).start()

                copy(c, slot).wait()
                dst_of(c)[...] = stage[slot].astype(BF16)
                return carry

            lax.fori_loop(0, n_chunks, body, 0)

        cols = lambda c: pl.ds(pl.multiple_of(c * tn, tn), tn)
        stream(2 * ff // tn, lambda c: wgu_hbm.at[e, :, cols(c)], stage_gu, lambda c: wgu_ref.at[:, cols(c)])
        stream(ff // tn, lambda c: wd_hbm.at[e, cols(c), :], stage_d, lambda c: wd_ref.at[cols(c), :])

    def start_gather(tile, slot):
        def body(t, c):
            _row_copy(x_hbm, inv_ref[tile * tm + t], xbuf_ref.at[slot], t, sems.at[slot]).start()
            return c
        lax.fori_loop(0, tm, body, 0, unroll=8)

    def wait_gather(slot):
        pltpu.make_async_copy(x_hbm.at[pl.ds(0, tm)], xbuf_ref.at[slot], sems.at[slot]).wait()

    last = pl.num_programs(0) - 1
    slot = lax.rem(i, 2)

    @pl.when(i == 0)
    def _():
        start_gather(0, 0)

    @pl.when(i < n_used)
    def _():
        @pl.when((i == 0) | (te_ref[i] != te_ref[jnp.maximum(i - 1, 0)]))
        def _():
            load_expert(te_ref[i])

        wait_gather(slot)
        xn = _rms(xbuf_ref[slot], g_ref[...]).astype(BF16)
        n_chunks = ff // tn
        nxt = jnp.minimum(i + 1, last) * tm
        for c in range(n_chunks):
            for t in range(c * tm // n_chunks, (c + 1) * tm // n_chunks):
                _row_copy(x_hbm, inv_ref[nxt + t], xbuf_ref.at[1 - slot], t, sems.at[1 - slot]).start()
            gate = jnp.dot(xn, wgu_ref[:, c * tn:(c + 1) * tn], preferred_element_type=F32)
            up = jnp.dot(xn, wgu_ref[:, ff + c * tn:ff + (c + 1) * tn], preferred_element_type=F32)
            act_ref[:, c * tn:(c + 1) * tn] = (_silu(gate) * up).astype(BF16)
        o_ref[...] = jnp.dot(act_ref[...], wd_ref[...], preferred_element_type=F32)

        @pl.when(i == last)
        def _():
            wait_gather(1 - slot)

    @pl.when(i >= n_used)
    def _():
        @pl.when(i == n_used)
        def _():
            wait_gather(slot)

        o_ref[...] = jnp.zeros_like(o_ref)


def moe_experts(tile_expert, n_used, inv, x, g, w_gu, w_down, tm=ROW_TILE, tn=512):
    n, d = x.shape
    p = inv.shape[0]
    ff = w_down.shape[1]
    tn = min(tn, ff)
    hbm = pl.BlockSpec(memory_space=pl.ANY)
    return pl.pallas_call(
        functools.partial(_experts_kernel, tm=tm, tn=tn),
        grid_spec=pltpu.PrefetchScalarGridSpec(
            num_scalar_prefetch=3, grid=(p // tm,),
            in_specs=[hbm, pl.BlockSpec((1, d), lambda i, te, nu, inv: (0, 0)), hbm, hbm],
            out_specs=pl.BlockSpec((tm, d), lambda i, te, nu, inv: (i, 0)),
            scratch_shapes=[pltpu.VMEM((2, tm, d), F32), pltpu.VMEM((tm, ff), BF16),
                            pltpu.VMEM((d, 2 * ff), BF16), pltpu.VMEM((ff, d), BF16),
                            pltpu.VMEM((2, d, tn), w_gu.dtype), pltpu.VMEM((2, tn, d), w_down.dtype),
                            pltpu.SemaphoreType.DMA((2,)), pltpu.SemaphoreType.DMA((2,))]),
        out_shape=jax.ShapeDtypeStruct((p, d), F32),
        compiler_params=_cparams(("arbitrary",)),
        name="moe_experts",
    )(tile_expert, n_used, inv, x, g.reshape(1, d), w_gu, w_down)


def _combine_kernel(pos_ref, w_ref, x_ref, y_hbm, gp_ref, wg_ref, p_ref, wp_ref, *refs, n, tm, final_norm):
    if final_norm:
        gf_ref, o_ref, buf_ref, sem = refs
    else:
        o_ref, buf_ref, sem = refs
    base = pl.program_id(0) * tm

    def start(t, c):
        for k in range(TOP_K):
            _row_copy(y_hbm, pos_ref[k * n + base + t], buf_ref.at[k], t, sem).start()
        return c

    lax.fori_loop(0, tm, start, 0, unroll=8)
    diag = (lax.broadcasted_iota(jnp.int32, (tm, tm), 0) == lax.broadcasted_iota(jnp.int32, (tm, tm), 1))
    wcol = [jnp.sum(jnp.where(diag, w_ref[k:k + 1, :], 0.0), axis=1, keepdims=True) for k in range(TOP_K)]
    for k in range(TOP_K):
        pltpu.make_async_copy(y_hbm.at[pl.ds(0, tm)], buf_ref.at[k], sem).wait()
    y = _ple(x_ref[...] + sum(wcol[k] * buf_ref[k] for k in range(TOP_K)), gp_ref, wg_ref, p_ref, wp_ref)
    o_ref[...] = _rms(y, gf_ref[...]) if final_norm else y


def moe_combine(pos_flat, w, x, y_sorted, g_ple, wg_ple, p, wp_ple, g_final=None, tm=ROW_TILE):
    n, d = x.shape
    dp = p.shape[1]
    const = lambda shape: pl.BlockSpec(shape, lambda i, pos: (0, 0), pipeline_mode=pl.Buffered(1))
    in_specs = [pl.BlockSpec((w.shape[0], tm), lambda i, pos: (0, i)),
                pl.BlockSpec((tm, d), lambda i, pos: (i, 0)),
                pl.BlockSpec(memory_space=pl.ANY),
                const((1, d)), const((d, d)), pl.BlockSpec((tm, dp), lambda i, pos: (i, 0)), const((dp, d))]
    args = [pos_flat, w, x, y_sorted, g_ple.reshape(1, d), wg_ple, p, wp_ple]
    if g_final is not None:
        in_specs.append(const((1, d)))
        args.append(g_final.reshape(1, d))
    return pl.pallas_call(
        functools.partial(_combine_kernel, n=n, tm=tm, final_norm=g_final is not None),
        grid_spec=pltpu.PrefetchScalarGridSpec(
            num_scalar_prefetch=1, grid=(n // tm,),
            in_specs=in_specs,
            out_specs=pl.BlockSpec((tm, d), lambda i, pos: (i, 0)),
            scratch_shapes=[pltpu.VMEM((TOP_K, tm, d), F32), pltpu.SemaphoreType.DMA]),
        out_shape=jax.ShapeDtypeStruct((n, d), F32),
        compiler_params=_cparams(("arbitrary",)),
        name="moe_combine",
    )(*args)


def _compress_kernel(h_ref, pos_ref, w1_ref, w2_ref, o_ref):
    half = h_ref.shape[1]
    h = h_ref[...].astype(BF16)
    p_lo = jnp.dot(h, w1_ref[0:half, :], preferred_element_type=F32)
    p_hi = jnp.dot(h, w1_ref[half:, :], preferred_element_type=F32)
    nh = p_hi.shape[0]
    posb = jnp.dot(pos_ref[...].astype(BF16), w1_ref[...], preferred_element_type=F32)
    hid = p_lo + pltpu.roll(p_hi, nh - 1, 0) + posb
    o_ref[...] = jnp.dot(jax.nn.gelu(hid).astype(BF16), w2_ref[...],
                         preferred_element_type=F32).astype(o_ref.dtype)


def nsa_compress(halves, pos, w1, w2):
    bg, nh, hw = halves.shape
    hid = w1.shape[1]
    dh = w2.shape[1]
    return pl.pallas_call(
        _compress_kernel,
        grid=(bg,),
        in_specs=[pl.BlockSpec((None, nh, hw), lambda i: (i, 0, 0)),
                  pl.BlockSpec((1, 2 * hw), lambda i: (0, 0)),
                  pl.BlockSpec((2 * hw, hid), lambda i: (0, 0)),
                  pl.BlockSpec((hid, dh), lambda i: (0, 0))],
        out_specs=pl.BlockSpec((None, nh, dh), lambda i: (i, 0, 0)),
        out_shape=jax.ShapeDtypeStruct((bg, nh, dh), BF16),
        compiler_params=_cparams(("parallel",)),
    )(halves, pos.reshape(1, 2 * hw), w1, w2)


def _rel_bucket_np(dist):
    n = np.maximum(dist, 0)
    max_exact = N_BUCKETS // 2
    nf = np.maximum(n, 1).astype(np.float32)
    large = max_exact + (np.log(nf / max_exact) / np.float32(math.log(MAX_DISTANCE / max_exact))
                         * (N_BUCKETS - max_exact)).astype(np.int32)
    return np.where(n < max_exact, n, np.minimum(large, N_BUCKETS - 1)).astype(np.int32)


def _nsa_kernel(q_ref, gt_ref, kc_ref, vct_ref, bct_ref, ks_ref, kw_ref, vst_ref, vwt_ref, tzw_ref, tzc_ref,
                ovlt_ref, expt_ref, o_ref, *, n_cmp, n_sel):
    tq = q_ref.shape[0]
    dh = HEAD_DIM
    r = q_ref.shape[1] // dh
    near = tzw_ref.shape[0]
    n_slc = ovlt_ref.shape[0]
    ncp = kc_ref.shape[0]
    g = pl.program_id(1)
    t0 = pl.multiple_of(pl.program_id(2) * tq, tq)
    near_keys = pl.ds(t0, near)
    heads = lambda blocks: jnp.concatenate(blocks, axis=1)
    per_head = lambda x, h: x[:, h * tq:(h + 1) * tq]

    qt = q_ref[...].astype(F32).T.astype(BF16)
    zero = jnp.zeros((dh, r * tq), BF16)
    qh = heads([qt[h * dh:(h + 1) * dh, :] for h in range(r)])
    qz = jnp.where(g == 0, jnp.concatenate([qh, zero], axis=0), jnp.concatenate([zero, qh], axis=0))
    gate_t = gt_ref[...].T

    def gate(branch):
        rows = [gate_t[branch * N_HEADS + gi * r:branch * N_HEADS + (gi + 1) * r, :] for gi in range(N_KV_GROUPS)]
        mine = jnp.where(g == 0, rows[0], rows[1])
        return heads([mine[h:h + 1, :] for h in range(r)])

    cblk = lax.broadcasted_iota(jnp.int32, (ncp, r * tq), 0)
    qpos = t0 + (lax.broadcasted_iota(jnp.int32, (ncp, r * tq), 1) & (tq - 1))
    mask_c = (qpos - (cblk * CMP_STRIDE + CMP_BLOCK - 1) >= 0) & (cblk < n_cmp)
    s = jnp.dot(kc_ref[...], qz, preferred_element_type=F32) + bct_ref[...]
    l = jnp.where(mask_c, s, NEG)
    e = jnp.where(mask_c, jnp.exp2(l - jnp.max(l, axis=0, keepdims=True)), 0.0)
    p = e / jnp.maximum(jnp.sum(e, axis=0, keepdims=True), 1e-30)
    out = gate(0) * jnp.dot(vct_ref[...], p.astype(BF16), preferred_element_type=F32)
    psum = functools.reduce(lambda acc, h: acc + per_head(p, h), range(1, r), per_head(p, 0))

    key_exists = t0 + lax.broadcasted_iota(jnp.int32, (near, r * tq), 0) >= WINDOW
    s = jnp.dot(kw_ref[near_keys, :], qz, preferred_element_type=F32)
    l = jnp.where(key_exists, s + tzw_ref[...], NEG)
    e = jnp.exp2(l - jnp.max(l, axis=0, keepdims=True))
    o_w = jnp.dot(vwt_ref[:, near_keys], e.astype(BF16), preferred_element_type=F32)
    out = out + gate(2) * (o_w / jnp.sum(e, axis=0, keepdims=True))

    imp = jnp.dot(ovlt_ref[...], psum, preferred_element_type=F32, precision=lax.Precision.HIGHEST)
    jblk = lax.broadcasted_iota(jnp.int32, (n_slc, tq), 0).astype(F32)
    qblk = ((t0 + lax.broadcasted_iota(jnp.int32, (n_slc, tq), 1)) // SLC_BLOCK).astype(F32)
    forced = (jblk == 0) | (jblk == qblk) | (jblk == qblk - 1)
    score = jnp.where(forced, 1e9, jnp.where(jblk <= qblk, imp, -1e9))
    sel = jnp.zeros((n_slc, tq), F32)
    for _ in range(n_sel):
        m = jnp.max(score, axis=0, keepdims=True)
        first = jnp.min(jnp.where(score == m, jblk, float(n_slc)), axis=0, keepdims=True)
        pick = jblk == first
        sel = jnp.where(pick, 1.0, sel)
        score = jnp.where(pick, -jnp.inf, score)
    sel16 = sel.astype(BF16)
    all_heads = lambda x: heads([x] * r)

    def member_mask(keys, also=True):
        member = jnp.dot(expt_ref[keys, :], sel16, preferred_element_type=F32) > 0.5
        return all_heads(jnp.where(member & also, 0.0, NEG))

    l = jnp.dot(ks_ref[near_keys, :], qz, preferred_element_type=F32) + tzc_ref[...] + member_mask(near_keys)
    m = jnp.max(l, axis=0, keepdims=True)
    e = jnp.exp2(l - m)
    den = jnp.sum(e, axis=0, keepdims=True)
    acc = jnp.dot(vst_ref[:, near_keys], e.astype(BF16), preferred_element_type=F32)
    far_bias = tzc_ref[0:1, :]

    def far_chunk(f, mda):
        m, den, acc = mda
        first = pl.multiple_of(WINDOW + f * FAR_CHUNK, FAR_CHUNK)
        keys = pl.ds(first, FAR_CHUNK)
        before_near = first + lax.broadcasted_iota(jnp.int32, (FAR_CHUNK, tq), 0) < t0
        l = jnp.dot(ks_ref[keys, :], qz, preferred_element_type=F32) + far_bias + member_mask(keys, before_near)
        m_new = jnp.maximum(m, jnp.max(l, axis=0, keepdims=True))
        alpha = jnp.exp2(m - m_new)
        e = jnp.exp2(l - m_new)
        pv = jnp.dot(vst_ref[:, keys], e.astype(BF16), preferred_element_type=F32)
        return m_new, alpha * den + jnp.sum(e, axis=0, keepdims=True), alpha * acc + pv

    n_far = (jnp.maximum(t0, WINDOW) - 1) // FAR_CHUNK
    m, den, acc = lax.fori_loop(0, n_far, far_chunk, (m, den, acc))
    out = out + gate(1) * (acc / den)

    o_ref[...] = jnp.concatenate([per_head(out, h) for h in range(r)], axis=0).T.astype(o_ref.dtype)


def nsa_attention_core(q, gates_pad, kc2, vct, bct, kv_pad, kvt_pad, tzw, tzc, ovlt, expt, batch, seq, n_cmp):
    n, hd = q.shape
    gq, dh, tq = N_KV_GROUPS, HEAD_DIM, Q_BLOCK
    r = N_HEADS // gq
    nqt = seq // tq
    spad = kv_pad.shape[1]
    ncp = kc2.shape[1]
    n_slc = seq // SLC_BLOCK
    near = tzw.shape[1]
    row_tile = lambda b, g, t: (b * nqt + t, g)
    key_cols = lambda branch: pl.BlockSpec((None, spad, gq * dh), lambda b, g, t: (b, 0, branch))
    val_rows = lambda branch: pl.BlockSpec((dh, spad), lambda b, g, t: (branch * gq + g, b))
    table = pl.BlockSpec((None, near, r * tq), lambda b, g, t: (g, 0, 0))
    const = lambda a: pl.BlockSpec(a.shape, lambda b, g, t: (0,) * a.ndim)
    return pl.pallas_call(
        functools.partial(_nsa_kernel, n_cmp=n_cmp, n_sel=min(N_SELECT, n_slc)),
        grid=(batch, gq, nqt),
        in_specs=[pl.BlockSpec((tq, r * dh), row_tile),
                  pl.BlockSpec((tq, gates_pad.shape[1]), lambda b, g, t: (b * nqt + t, 0)),
                  pl.BlockSpec((None, ncp, gq * dh), lambda b, g, t: (b, 0, 0)),
                  pl.BlockSpec((None, None, dh, ncp), lambda b, g, t: (b, g, 0, 0)),
                  pl.BlockSpec((None, ncp, r * tq), lambda b, g, t: (g, t, 0)),
                  key_cols(2), key_cols(4), val_rows(3), val_rows(5), table, table,
                  const(ovlt), const(expt)],
        out_specs=pl.BlockSpec((tq, r * dh), row_tile),
        out_shape=jax.ShapeDtypeStruct((n, hd), BF16),
        compiler_params=_cparams(("parallel", "parallel", "parallel")),
        name="nsa_attention",
    )(q, gates_pad, kc2, vct, bct, kv_pad, kv_pad, kvt_pad, kvt_pad, tzw, tzc, ovlt, expt)


def _bias_table_kernel(rb_ref, bkt_win_ref, bkt_causal_ref, bkt_cmp_ref, tzw_ref, tzc_ref, bct_ref, *, rows):
    h = pl.program_id(0)

    def lookup(bkt):
        acc = jnp.full(bkt.shape, NEG, F32)
        for k in range(N_BUCKETS):
            acc = jnp.where(bkt == k, rb_ref[k * N_HEADS + h] * LOG2_E, acc)
        return acc

    for src, dst in ((bkt_win_ref, tzw_ref), (bkt_causal_ref, tzc_ref), (bkt_cmp_ref, bct_ref)):
        def body(c, carry, src=src, dst=dst):
            sl = pl.ds(pl.multiple_of(c * rows, rows), rows)
            dst[sl, :] = lookup(src[sl, :])
            return carry
        lax.fori_loop(0, src.shape[0] // rows, body, 0)


def _nsa_tables(rel_bias, seq, n_cmp, ncp):
    tq = Q_BLOCK
    r = N_HEADS // N_KV_GROUPS
    dist_near = np.arange(tq)[None, :] + WINDOW - np.arange(NEAR)[:, None]
    bkt_near = _rel_bucket_np(dist_near)
    bkt_win = np.where((dist_near >= 0) & (dist_near < WINDOW), bkt_near, N_BUCKETS).astype(np.int32)
    bkt_causal = np.where(dist_near >= 0, bkt_near, N_BUCKETS).astype(np.int32)
    dist_c = np.arange(seq)[None, :] - (np.arange(ncp)[:, None] * CMP_STRIDE + CMP_BLOCK - 1)
    bkt_c = _rel_bucket_np(dist_c).reshape(ncp, seq // tq, tq).transpose(1, 0, 2).reshape(-1, tq)
    maps = [jnp.asarray(m) for m in (bkt_win, bkt_causal, bkt_c)]
    whole = lambda m: pl.BlockSpec(m.shape, lambda h, rb: (0, 0))
    per_head = lambda m: pl.BlockSpec((None, m.shape[0], tq), lambda h, rb: (h // r, 0, h % r))
    tzw, tzc, bct = pl.pallas_call(
        functools.partial(_bias_table_kernel, rows=64),
        grid_spec=pltpu.PrefetchScalarGridSpec(
            num_scalar_prefetch=1, grid=(N_HEADS,),
            in_specs=[whole(m) for m in maps], out_specs=[per_head(m) for m in maps]),
        out_shape=[jax.ShapeDtypeStruct((N_KV_GROUPS, m.shape[0], r * tq), F32) for m in maps],
        compiler_params=_cparams(("parallel",)),
        name="nsa_bias_tables",
    )(rel_bias.reshape(N_BUCKETS * N_HEADS), *maps)
    n_slc = seq // SLC_BLOCK
    cs = np.arange(ncp)[None, :] * CMP_STRIDE
    ss = np.arange(n_slc)[:, None] * SLC_BLOCK
    overlap_t = (cs < ss + SLC_BLOCK) & (cs + CMP_BLOCK > ss) & (np.arange(ncp)[None, :] < n_cmp)
    kpos = np.arange(WINDOW + seq)[:, None] - WINDOW
    expand_t = (kpos // SLC_BLOCK == np.arange(n_slc)[None, :]) & (kpos >= 0)
    return tzw, tzc, bct, jnp.asarray(overlap_t, F32), jnp.asarray(expand_t, BF16)


def _kv_kernel(x_ref, g_ref, w_ref, wt_ref, kv_ref, kvt_ref):
    @pl.when(pl.program_id(1) == 0)
    def _():
        kv_ref[...] = jnp.zeros_like(kv_ref)
        kvt_ref[...] = jnp.zeros_like(kvt_ref)

    @pl.when(pl.program_id(1) > 0)
    def _():
        xn = _rms(x_ref[...], g_ref[...]).astype(BF16)
        kv_ref[...] = jnp.dot(xn, w_ref[...], preferred_element_type=F32).astype(kv_ref.dtype)
        kvt_ref[...] = lax.dot_general(wt_ref[...], xn, (((1,), (1,)), ((), ())),
                                       preferred_element_type=F32).astype(kvt_ref.dtype)


def kv_projection(x, g, w_kv, batch, seq):
    n, d = x.shape
    f = w_kv.shape[1]
    tm = WINDOW
    nt = seq // tm
    rows = batch * (nt + 1) * tm
    w16 = w_kv.astype(BF16)
    const = lambda shape: pl.BlockSpec(shape, lambda b, t: (0, 0))
    return pl.pallas_call(
        _kv_kernel,
        grid=(batch, nt + 1),
        in_specs=[pl.BlockSpec((tm, d), lambda b, t: (b * nt + jnp.maximum(t - 1, 0), 0)), const((1, d)),
                  const((d, f)), const((f, d))],
        out_specs=[pl.BlockSpec((tm, f), lambda b, t: (b * (nt + 1) + t, 0)),
                   pl.BlockSpec((f, tm), lambda b, t: (0, b * (nt + 1) + t))],
        out_shape=[jax.ShapeDtypeStruct((rows, f), BF16), jax.ShapeDtypeStruct((f, rows), BF16)],
        compiler_params=_cparams(("parallel", "arbitrary")),
        name="kv_projection",
    )(x, g.reshape(1, d), w16, w16.T)


def _nsa_shared_kv(x, batch, seq, kv_norm, w_kv, cmp_pos_k, cmp_pos_v, k_w1, k_w2, v_w1, v_w2):
    n, d = x.shape
    gq, dh = N_KV_GROUPS, HEAD_DIM
    f = w_kv.shape[1]
    kv_pad, kvt_pad = kv_projection(x, kv_norm, w_kv, batch, seq)
    kv_pad = kv_pad.reshape(batch, WINDOW + seq, f)
    kv5 = kv_pad[:, WINDOW:].reshape(batch, seq, N_KV_BRANCH, gq, dh)
    per_group = lambda j: jnp.transpose(kv5[:, :, j], (0, 2, 1, 3))
    n_cmp = (seq - CMP_BLOCK) // CMP_STRIDE + 1
    nhalf = seq // CMP_STRIDE

    def compress(t, pos, w1, w2):
        halves = t.reshape(batch * gq, nhalf, CMP_STRIDE * dh)
        return nsa_compress(halves, pos, w1.astype(BF16), w2.astype(BF16)).reshape(batch, gq, nhalf, dh)

    k_cmp = compress(per_group(0), cmp_pos_k, k_w1, k_w2)
    v_cmp = compress(per_group(1), cmp_pos_v, v_w1, v_w2)
    kc2 = jnp.transpose(k_cmp, (0, 2, 1, 3)).reshape(batch, nhalf, gq * dh)
    vct = jnp.transpose(v_cmp, (0, 1, 3, 2))
    return kc2, vct, kv_pad, kvt_pad, n_cmp


def _q_kernel(x_ref, g_ref, wq_ref, wg_ref, q_ref, gates_ref):
    xn = _rms(x_ref[...], g_ref[...]).astype(BF16)
    q = jnp.dot(xn, wq_ref[...], preferred_element_type=F32) * (HEAD_DIM ** -0.5 * LOG2_E)
    q_ref[...] = q.astype(q_ref.dtype)
    gates_ref[...] = jax.nn.sigmoid(jnp.dot(xn, wg_ref[...], preferred_element_type=F32))


def q_projection(x, g, w_q, tm=ROW_TILE):
    n, d = x.shape
    hd = N_HEADS * HEAD_DIM
    w16 = w_q.astype(BF16)
    w_gate = jnp.pad(w16[:, hd:], ((0, 0), (0, 128 - (w_q.shape[1] - hd))))
    return pl.pallas_call(
        _q_kernel,
        grid=(n // tm,),
        in_specs=[pl.BlockSpec((tm, d), lambda i: (i, 0)), pl.BlockSpec((1, d), lambda i: (0, 0)),
                  pl.BlockSpec((d, hd), lambda i: (0, 0)), pl.BlockSpec((d, 128), lambda i: (0, 0))],
        out_specs=[pl.BlockSpec((tm, hd), lambda i: (i, 0)), pl.BlockSpec((tm, 128), lambda i: (i, 0))],
        out_shape=[jax.ShapeDtypeStruct((n, hd), BF16), jax.ShapeDtypeStruct((n, 128), F32)],
        compiler_params=_cparams(("parallel",)),
        name="q_projection",
    )(x, g.reshape(1, d), w16[:, :hd], w_gate)


def _nsa_mix(x, batch, seq, g_mix, w_q, shared, tables):
    kc2, vct, kv_pad, kvt_pad, n_cmp = shared
    tzw, tzc, bct, ovlt, expt = tables
    q, gates_pad = q_projection(x, g_mix, w_q)
    return nsa_attention_core(q, gates_pad, kc2, vct, bct, kv_pad, kvt_pad, tzw, tzc, ovlt, expt,
                              batch, seq, n_cmp)


def _dense_tail_kernel(a_ref, wo_ref, x_ref, g_ref, wgu_ref, wd_ref, gp_ref, wg_ref, p_ref, wp_ref, o_ref,
                       act_ref, *, tn):
    x = x_ref[...] + jnp.dot(a_ref[...], wo_ref[...], preferred_element_type=F32)
    xn = _rms(x, g_ref[...]).astype(BF16)
    ff = wd_ref.shape[0]
    for c in range(ff // tn):
        gate = jnp.dot(xn, wgu_ref[:, c * tn:(c + 1) * tn], preferred_element_type=F32)
        up = jnp.dot(xn, wgu_ref[:, ff + c * tn:ff + (c + 1) * tn], preferred_element_type=F32)
        act_ref[:, c * tn:(c + 1) * tn] = (_silu(gate) * up).astype(BF16)
    x = x + jnp.dot(act_ref[...], wd_ref[...], preferred_element_type=F32)
    o_ref[...] = _ple(x, gp_ref, wg_ref, p_ref, wp_ref)


def dense_layer_tail(a, w_o, x, g_ffn, w_gu, w_down, g_ple, wg_ple, p, wp_ple, tm=ROW_TILE, tn=512):
    n, d = x.shape
    k = a.shape[1]
    ff = w_down.shape[0]
    dp = p.shape[1]
    whole = lambda shape: pl.BlockSpec(shape, lambda i: (0, 0), pipeline_mode=pl.Buffered(1))
    rows = lambda width: pl.BlockSpec((tm, width), lambda i: (i, 0))
    return pl.pallas_call(
        functools.partial(_dense_tail_kernel, tn=min(tn, ff)),
        grid=(n // tm,),
        in_specs=[rows(k), whole((k, d)), rows(d), whole((1, d)), whole((d, 2 * ff)), whole((ff, d)),
                  whole((1, d)), whole((d, d)), rows(dp), whole((dp, d))],
        out_specs=rows(d),
        out_shape=jax.ShapeDtypeStruct((n, d), F32),
        scratch_shapes=[pltpu.VMEM((tm, ff), BF16)],
        compiler_params=_cparams(("parallel",)),
        name="dense_layer_tail",
    )(a, w_o, x, g_ffn.reshape(1, d), w_gu, w_down, g_ple.reshape(1, d), wg_ple, p, wp_ple)


def _moe_ffn(x, g, w_router, w_gu_all, w_down_all, layer, ple, tm=ROW_TILE):
    n, d = x.shape
    ne = w_router.shape[1]
    sel, w, counts = moe_route(x, g, w_router)
    padded = (counts[:, 0] + tm - 1) // tm * tm
    ends = jnp.cumsum(padded)
    n_tiles = (TOP_K * n) // tm + ne
    n_used = (ends[-1] // tm).astype(jnp.int32).reshape(1)
    tile_first_row = jnp.arange(n_tiles, dtype=jnp.int32) * tm
    tile_expert = jnp.sum(ends[None, :] <= tile_first_row[:, None], axis=1)
    tile_expert = jnp.minimum(tile_expert, tile_expert[jnp.maximum(n_used[0] - 1, 0)]).astype(jnp.int32)
    starts = (ends - padded).astype(jnp.int32)
    pos = moe_slots(starts, sel)
    pos_flat = pos[:TOP_K].reshape(TOP_K * n)
    total = jnp.full((1,), n_tiles * tm, jnp.int32)
    unrouted = jnp.concatenate([starts + counts[:, 0], ends[-1:], ends, total]).astype(jnp.int32)
    inv = moe_inverse(pos_flat, unrouted, n, n_tiles * tm)
    flat = lambda w_all: w_all.reshape((-1,) + w_all.shape[2:])
    y_sorted = moe_experts(tile_expert + layer * ne, n_used, inv, x, g, flat(w_gu_all), flat(w_down_all))
    return moe_combine(pos_flat, w, x, y_sorted, *ple)


def kernel(x, p, g_mix, g_ffn, g_ple, g_final, rg_w_in, rg_conv_w, rg_conv_b, rg_w_a, rg_b_a, rg_w_x,
           rg_b_x, rg_lambda, rg_w_out, kv_norm, w_kv, cmp_pos_k, cmp_pos_v, cmp_k_w1, cmp_k_w2,
           cmp_v_w1, cmp_v_w2, rel_bias, nsa_w_q, nsa_w_o, ffn_w_gu, ffn_w_down, moe_w_router,
           moe_w_gu, moe_w_down, ple_w_proj, ple_w_gate):
    batch, seq, d = x.shape
    depth = p.shape[0]
    n_a = rg_w_in.shape[0]
    n = batch * seq
    x = x.reshape(n, d)
    p = p.reshape(depth, n, p.shape[-1])
    n_cmp = (seq - CMP_BLOCK) // CMP_STRIDE + 1
    tables = _nsa_tables(rel_bias, seq, n_cmp, seq // CMP_STRIDE)
    shared = None
    for i in range(depth):
        if i < n_a:
            mix = rg_lru_core(x, g_mix[i], rg_w_in[i].astype(BF16), batch, seq, rg_conv_w[i], rg_conv_b[i],
                              _block_diag_chunks(rg_w_a[i]), rg_b_a[i], _block_diag_chunks(rg_w_x[i]),
                              rg_b_x[i], rg_lambda[i])
            w_o = rg_w_out[i].astype(BF16)
        else:
            mix = _nsa_mix(x, batch, seq, g_mix[i], nsa_w_q[i - n_a], shared, tables)
            w_o = nsa_w_o[i - n_a].astype(BF16)
        ple = (g_ple[i], ple_w_gate[i].astype(BF16), p[i], ple_w_proj[i].astype(BF16))
        if i % 2 == 0:
            assert i < depth - 1
            x = dense_layer_tail(mix, w_o, x, g_ffn[i], ffn_w_gu[i // 2].astype(BF16),
                                 ffn_w_down[i // 2].astype(BF16), *ple)
        else:
            x = matmul_residual(mix, w_o, x)
            x = _moe_ffn(x, g_ffn[i], moe_w_router[i // 2], moe_w_gu, moe_w_down, i // 2,
                         ple + ((g_final,) if i == depth - 1 else ()))
        if i == n_a - 1:
            shared = _nsa_shared_kv(x, batch, seq, kv_norm, w_kv, cmp_pos_k, cmp_pos_v,
                                    cmp_k_w1, cmp_k_w2, cmp_v_w1, cmp_v_w2)
    return x.reshape(batch, seq, d)
```

```python
import functools
import math

import numpy as np
import jax
import jax.numpy as jnp
from jax import lax
from jax.experimental import pallas as pl
from jax.experimental.pallas import tpu as pltpu

F32 = jnp.float32
BF16 = jnp.bfloat16

EPS = 1e-6
CONV_WIDTH = 4
LRU_BLOCKS = 16
LRU_C = 8.0
N_HEADS = 16
N_KV_GROUPS = 2
GROUP = N_HEADS // N_KV_GROUPS
HEAD_DIM = 64
N_KV_BRANCH = 6
CMP_BLOCK = 32
CMP_STRIDE = 16
SLC_BLOCK = 64
N_SELECT = 8
WINDOW = 512
Q_BLOCK = 128
N_BUCKETS = 32
MAX_DISTANCE = 128
TOP_K = 2
NEAR = WINDOW + Q_BLOCK
FAR_CHUNK = 512
NEG = -1e30
LOG2_E = math.log2(math.e)
PACKED_ROWS_BF16 = 16

VMEM_LIMIT_V7X = 56 * 1024 * 1024
ROW_TILE = 512


def _cparams(sem):
    return pltpu.CompilerParams(dimension_semantics=sem, vmem_limit_bytes=VMEM_LIMIT_V7X)


def _rms(x, g):
    return x * lax.rsqrt(jnp.mean(x * x, axis=-1, keepdims=True) + EPS) * g


def _silu(x):
    return x * jax.nn.sigmoid(x)


def _mm_res_kernel(a_ref, w_ref, res_ref, *refs, scale_col):
    if scale_col is None:
        (o_ref,) = refs
        y = jnp.dot(a_ref[...], w_ref[...], preferred_element_type=F32)
    else:
        s_ref, o_ref = refs
        y = jnp.dot(a_ref[...], w_ref[...], preferred_element_type=F32)
        y = y * s_ref[:, scale_col:scale_col + 1]
    o_ref[...] = res_ref[...] + y


def matmul_residual(a, w, res, scale=None, scale_col=None, tm=ROW_TILE, tn=1024):
    n, k = a.shape
    d = w.shape[1]
    in_specs = [pl.BlockSpec((tm, k), lambda i, j: (i, 0)),
                pl.BlockSpec((k, tn), lambda i, j: (0, j)),
                pl.BlockSpec((tm, tn), lambda i, j: (i, j))]
    args = [a, w, res]
    if scale is not None:
        in_specs.append(pl.BlockSpec((tm, scale.shape[1]), lambda i, j: (i, 0)))
        args.append(scale)
    return pl.pallas_call(
        functools.partial(_mm_res_kernel, scale_col=scale_col),
        grid=(n // tm, d // tn),
        in_specs=in_specs,
        out_specs=pl.BlockSpec((tm, tn), lambda i, j: (i, j)),
        out_shape=jax.ShapeDtypeStruct((n, d), F32),
        compiler_params=_cparams(("parallel", "arbitrary")),
    )(*args)


BD = 256
SCAN_ROWS = 8


def _rg_kernel(x_ref, g_ref, win_ref, cw_ref, cb_ref, wa_ref, ba_ref, wx_ref, bx_ref, lam_ref,
               o_ref, xe_ref, h_ref, a_ref, u_ref, xn_ref):
    t, c = o_ref.shape

    @pl.when(pl.program_id(1) == 0)
    def _():
        xe_ref[0:8, :] = jnp.zeros((8, c), F32)
        h_ref[...] = jnp.zeros_like(h_ref)

    xn_ref[...] = _rms(x_ref[...], g_ref[...]).astype(BF16)
    xe_ref[8:, :] = jnp.dot(xn_ref[...], win_ref[:, c:], preferred_element_type=F32)
    xc = cb_ref[...] + sum(
        cw_ref[k:k + 1, :] * xe_ref[8 - (CONV_WIDTH - 1) + k: 8 - (CONV_WIDTH - 1) + k + t, :]
        for k in range(CONV_WIDTH))
    xe_ref[0:8, :] = xe_ref[t:t + 8, :]

    lam = -lam_ref[...]
    softplus_neg_lam = jnp.maximum(lam, 0.0) + jnp.log1p(jnp.exp(-jnp.abs(lam)))
    for cblk in range(c // BD):
        sl = slice(cblk * BD, (cblk + 1) * BD)
        xb = xc[:, sl]
        xb16 = xb.astype(BF16)
        r = jax.nn.sigmoid(jnp.dot(xb16, wa_ref[cblk], preferred_element_type=F32) + ba_ref[:, sl])
        i = jax.nn.sigmoid(jnp.dot(xb16, wx_ref[cblk], preferred_element_type=F32) + bx_ref[:, sl])
        log_a = -LRU_C * r * softplus_neg_lam[:, sl]
        a = jnp.exp(log_a)
        a_ref[:, sl] = a
        u_ref[:, sl] = jnp.sqrt(1.0 - a * a) * (i * xb)

    row = lax.broadcasted_iota(jnp.int32, (SCAN_ROWS, c), 0)

    def scan_tile(k, h_prev):
        rows = pl.ds(pl.multiple_of(k * SCAN_ROWS, SCAN_ROWS), SCAN_ROWS)
        a = a_ref[rows, :]
        u = u_ref[rows, :]
        for d in (1, 2, 4):
            a_s = jnp.where(row >= d, pltpu.roll(a, d, 0), 1.0)
            u_s = jnp.where(row >= d, pltpu.roll(u, d, 0), 0.0)
            u = a * u_s + u
            a = a * a_s
        h = a * h_prev + u
        u_ref[rows, :] = h
        return jnp.broadcast_to(h[SCAN_ROWS - 1:SCAN_ROWS, :], (SCAN_ROWS, c))

    h_ref[...] = lax.fori_loop(0, t // SCAN_ROWS, scan_tile, h_ref[...])
    gate = jnp.dot(xn_ref[...], win_ref[:, :c], preferred_element_type=F32)
    o_ref[...] = (jax.nn.gelu(gate) * u_ref[...]).astype(o_ref.dtype)


def rg_lru_core(x, g, w_in, batch, seq, conv_w, conv_b, wa_bd, b_a, wx_bd, b_x, lam, t_chunk=512):
    n, d = x.shape
    c = w_in.shape[1] // 2
    nt = seq // t_chunk
    vec = lambda v: v.reshape(1, -1)
    rows = lambda width: pl.BlockSpec((t_chunk, width), lambda b, t: (b * nt + t, 0))
    const = lambda shape: pl.BlockSpec(shape, lambda b, t: (0,) * len(shape))
    return pl.pallas_call(
        _rg_kernel,
        grid=(batch, nt),
        in_specs=[rows(d), const((1, d)), const(w_in.shape), const((CONV_WIDTH, c)), const((1, c)),
                  const(wa_bd.shape), const((1, c)), const(wx_bd.shape), const((1, c)), const((1, c))],
        out_specs=rows(c),
        out_shape=jax.ShapeDtypeStruct((n, c), BF16),
        scratch_shapes=[pltpu.VMEM((t_chunk + 8, c), F32), pltpu.VMEM((SCAN_ROWS, c), F32),
                        pltpu.VMEM((t_chunk, c), F32), pltpu.VMEM((t_chunk, c), F32),
                        pltpu.VMEM((t_chunk, d), BF16)],
        compiler_params=_cparams(("parallel", "arbitrary")),
        name="rg_lru_core",
    )(x, vec(g), w_in, conv_w, vec(conv_b), wa_bd, vec(b_a), wx_bd, vec(b_x), vec(lam))


def _block_diag_chunks(w):
    nb, bw, _ = w.shape
    per = BD // bw
    w = w.reshape(nb // per, per, bw, bw)
    eye = jnp.eye(per, dtype=w.dtype)
    out = jnp.einsum('cpij,pq->cpiqj', w, eye).reshape(nb // per, BD, BD)
    return out.astype(BF16)


def _ple(x, g_ref, wg_ref, p_ref, wp_ref):
    xn = _rms(x, g_ref[...]).astype(BF16)
    gate = jax.nn.sigmoid(jnp.dot(xn, wg_ref[...], preferred_element_type=F32))
    proj = jnp.dot(p_ref[...].astype(BF16), wp_ref[...], preferred_element_type=F32)
    return x + gate * proj


SEL_ROWS = 8


def _route_kernel(x_ref, g_ref, wt_ref, sel_ref, w_ref, cnt_ref, carry_ref):
    @pl.when(pl.program_id(0) == 0)
    def _():
        carry_ref[...] = jnp.zeros_like(carry_ref)

    xn = _rms(x_ref[...], g_ref[...])
    logits = lax.dot_general(wt_ref[...], xn, (((1,), (1,)), ((), ())), preferred_element_type=F32,
                             precision=lax.Precision.HIGHEST)
    ne, tm = logits.shape
    row = lax.broadcasted_iota(jnp.int32, (ne, tm), 0).astype(F32)
    m1 = jnp.max(logits, axis=0, keepdims=True)
    i1 = jnp.min(jnp.where(logits == m1, row, float(ne)), axis=0, keepdims=True)
    rest = jnp.where(row == i1, -jnp.inf, logits)
    m2 = jnp.max(rest, axis=0, keepdims=True)
    i2 = jnp.min(jnp.where(rest == m2, row, float(ne)), axis=0, keepdims=True)
    e2 = jnp.exp(m2 - m1)
    denom = 1.0 + e2
    ind = jnp.where((row == i1) | (row == i2), 1.0, 0.0)
    earlier = (lax.broadcasted_iota(jnp.int32, (tm, tm), 0) < lax.broadcasted_iota(jnp.int32, (tm, tm), 1))
    rank = jnp.dot(ind.astype(BF16), jnp.where(earlier, 1.0, 0.0).astype(BF16),
                   preferred_element_type=F32) + carry_ref[:, 0:1]
    r1 = jnp.sum(jnp.where(row == i1, rank, 0.0), axis=0, keepdims=True)
    r2 = jnp.sum(jnp.where(row == i2, rank, 0.0), axis=0, keepdims=True)
    carry_ref[...] = carry_ref[...] + jnp.sum(ind, axis=1, keepdims=True)
    pick = lambda vals: functools.reduce(
        lambda acc, kv: jnp.where(row == float(kv[0]), kv[1], acc), enumerate(vals), jnp.zeros((ne, tm), F32))
    sel_ref[...] = pick([i1, i2, r1, r2]).astype(jnp.int32)
    w_ref[...] = pick([1.0 / denom, e2 / denom])
    cnt_ref[...] = carry_ref[...].astype(jnp.int32)


def moe_route(x, g, w_router, tm=ROW_TILE):
    n, d = x.shape
    ne = w_router.shape[1]
    assert ne == SEL_ROWS
    return pl.pallas_call(
        _route_kernel,
        grid=(n // tm,),
        in_specs=[pl.BlockSpec((tm, d), lambda i: (i, 0)), pl.BlockSpec((1, d), lambda i: (0, 0)),
                  pl.BlockSpec((ne, d), lambda i: (0, 0))],
        out_specs=[pl.BlockSpec((ne, tm), lambda i: (0, i)), pl.BlockSpec((ne, tm), lambda i: (0, i)),
                   pl.BlockSpec((ne, 128), lambda i: (0, 0))],
        out_shape=[jax.ShapeDtypeStruct((ne, n), jnp.int32), jax.ShapeDtypeStruct((ne, n), F32),
                   jax.ShapeDtypeStruct((ne, 128), jnp.int32)],
        scratch_shapes=[pltpu.VMEM((ne, 128), F32)],
        compiler_params=_cparams(("arbitrary",)),
        name="moe_route",
    )(x, g.reshape(1, d), w_router.T)


def _slot_kernel(starts_ref, sel_ref, pos_ref):
    sel = sel_ref[...]
    start_of = lambda e_row: functools.reduce(
        lambda acc, e: jnp.where(e_row == e, starts_ref[e], acc), range(starts_ref.shape[0]),
        jnp.zeros_like(e_row))
    row = lax.broadcasted_iota(jnp.int32, sel.shape, 0)
    p1 = start_of(sel[0:1]) + sel[2:3]
    p2 = start_of(sel[1:2]) + sel[3:4]
    pos_ref[...] = jnp.where(row == 0, p1, jnp.where(row == 1, p2, 0))


def moe_slots(starts, sel, tm=2048):
    ne, n = sel.shape
    return pl.pallas_call(
        _slot_kernel,
        grid_spec=pltpu.PrefetchScalarGridSpec(
            num_scalar_prefetch=1, grid=(n // tm,),
            in_specs=[pl.BlockSpec((ne, tm), lambda i, s: (0, i))],
            out_specs=pl.BlockSpec((ne, tm), lambda i, s: (0, i))),
        out_shape=jax.ShapeDtypeStruct((ne, n), jnp.int32),
        compiler_params=_cparams(("parallel",)),
        name="moe_slots",
    )(starts, sel)


def _row_copy(src, src_row, dst, dst_row, sem):
    return pltpu.make_async_copy(src.at[pl.ds(src_row, 1)], dst.at[pl.ds(dst_row, 1)], sem)


def _inverse_kernel(pos_ref, pad_ref, inv_ref, *, n):
    def clear(s, c):
        inv_ref[s] = 0
        return c

    def fill(t, c):
        for k in range(TOP_K):
            inv_ref[pos_ref[k * n + t]] = t
        return c

    n_ranges = pad_ref.shape[0] // 2
    for e in range(n_ranges):
        lax.fori_loop(pad_ref[e], pad_ref[n_ranges + e], clear, 0)
    lax.fori_loop(0, n, fill, 0, unroll=8)


def moe_inverse(pos_flat, pad_bounds, n, n_rows):
    return pl.pallas_call(
        functools.partial(_inverse_kernel, n=n),
        in_specs=[pl.BlockSpec(memory_space=pltpu.SMEM), pl.BlockSpec(memory_space=pltpu.SMEM)],
        out_specs=pl.BlockSpec(memory_space=pltpu.SMEM),
        out_shape=jax.ShapeDtypeStruct((n_rows,), jnp.int32),
        name="moe_inverse",
    )(pos_flat, pad_bounds)


def _experts_kernel(te_ref, nu_ref, inv_ref, x_hbm, g_ref, wgu_hbm, wd_hbm, o_ref, xbuf_ref, act_ref, wgu_ref,
                    wd_ref, stage_gu, stage_d, sems, wsems, dsems, *, tm, tn):
    i = pl.program_id(0)
    n_used = nu_ref[0]
    ff = wd_ref.shape[0]
    n_chunks = ff // tn

    def gu_chunk(j):
        col = (j % 2) * ff + (j // 2) * tn
        return slice(col, col + tn)

    def gu_copy(e, j):
        slot = j % stage_gu.shape[0]
        return pltpu.make_async_copy(wgu_hbm.at[e, :, gu_chunk(j)], stage_gu.at[slot], wsems.at[slot])

    def down_copy(e, c):
        slot = c % stage_d.shape[0]
        return pltpu.make_async_copy(wd_hbm.at[e, c * tn:(c + 1) * tn, :], stage_d.at[slot], dsems.at[slot])

    def start_gather(tile, slot):
        def body(t, c):
            _row_copy(x_hbm, inv_ref[tile * tm + t], xbuf_ref.at[slot], t, sems.at[slot]).start()
            return c
        lax.fori_loop(0, tm, body, 0, unroll=8)

    def wait_gather(slot):
        pltpu.make_async_copy(x_hbm.at[pl.ds(0, tm)], xbuf_ref.at[slot], sems.at[slot]).wait()

    last = pl.num_programs(0) - 1
    slot = lax.rem(i, 2)

    @pl.when(i == 0)
    def _():
        start_gather(0, 0)

    def tile(stream):
        e = te_ref[i]
        if stream:
            for j in range(min(stage_gu.shape[0], 2 * n_chunks)):
                gu_copy(e, j).start()
            for c in range(min(stage_d.shape[0], n_chunks)):
                down_copy(e, c).start()
        wait_gather(slot)
        xn = _rms(xbuf_ref[slot], g_ref[...]).astype(BF16)
        nxt = jnp.minimum(i + 1, last) * tm
        for c in range(n_chunks):
            if stream:
                for j in (2 * c, 2 * c + 1):
                    gu_copy(e, j).wait()
                    wgu_ref[:, gu_chunk(j)] = stage_gu[j % stage_gu.shape[0]].astype(BF16)
                    if j + stage_gu.shape[0] < 2 * n_chunks:
                        gu_copy(e, j + stage_gu.shape[0]).start()
            for t in range(c * tm // n_chunks, (c + 1) * tm // n_chunks):
                _row_copy(x_hbm, inv_ref[nxt + t], xbuf_ref.at[1 - slot], t, sems.at[1 - slot]).start()
            rows = slice(c * tn, (c + 1) * tn)
            gate = jnp.dot(xn, wgu_ref[:, rows], preferred_element_type=F32)
            up = jnp.dot(xn, wgu_ref[:, ff + c * tn:ff + (c + 1) * tn], preferred_element_type=F32)
            act_ref[:, rows] = (_silu(gate) * up).astype(BF16)
            if stream:
                down_copy(e, c).wait()
                wd_ref[rows, :] = stage_d[c % stage_d.shape[0]].astype(BF16)
                if c + stage_d.shape[0] < n_chunks:
                    down_copy(e, c + stage_d.shape[0]).start()
                part = jnp.dot(act_ref[:, rows], wd_ref[rows, :], preferred_element_type=F32)
                o_ref[...] = part if c == 0 else o_ref[...] + part
        if not stream:
            o_ref[...] = jnp.dot(act_ref[...], wd_ref[...], preferred_element_type=F32)

        @pl.when(i == last)
        def _():
            wait_gather(1 - slot)

    new_expert = (i == 0) | (te_ref[i] != te_ref[jnp.maximum(i - 1, 0)])

    @pl.when((i < n_used) & new_expert)
    def _():
        tile(stream=True)

    @pl.when((i < n_used) & jnp.logical_not(new_expert))
    def _():
        tile(stream=False)

    @pl.when(i >= n_used)
    def _():
        @pl.when(i == n_used)
        def _():
            wait_gather(slot)

        o_ref[...] = jnp.zeros_like(o_ref)


def moe_experts(tile_expert, n_used, inv, x, g, w_gu, w_down, tm=ROW_TILE, tn=512):
    n, d = x.shape
    p = inv.shape[0]
    ff = w_down.shape[1]
    tn = min(tn, ff)
    hbm = pl.BlockSpec(memory_space=pl.ANY)
    return pl.pallas_call(
        functools.partial(_experts_kernel, tm=tm, tn=tn),
        grid_spec=pltpu.PrefetchScalarGridSpec(
            num_scalar_prefetch=3, grid=(p // tm,),
            in_specs=[hbm, pl.BlockSpec((1, d), lambda i, te, nu, inv: (0, 0)), hbm, hbm],
            out_specs=pl.BlockSpec((tm, d), lambda i, te, nu, inv: (i, 0)),
            scratch_shapes=[pltpu.VMEM((2, tm, d), F32), pltpu.VMEM((tm, ff), BF16),
                            pltpu.VMEM((d, 2 * ff), BF16), pltpu.VMEM((ff, d), BF16),
                            pltpu.VMEM((4, d, tn), w_gu.dtype), pltpu.VMEM((2, tn, d), w_down.dtype),
                            pltpu.SemaphoreType.DMA((2,)), pltpu.SemaphoreType.DMA((4,)),
                            pltpu.SemaphoreType.DMA((2,))]),
        out_shape=jax.ShapeDtypeStruct((p, d), F32),
        compiler_params=_cparams(("arbitrary",)),
        name="moe_experts",
    )(tile_expert, n_used, inv, x, g.reshape(1, d), w_gu, w_down)


def _combine_kernel(pos_ref, w_ref, x_ref, y_hbm, gp_ref, wg_ref, p_ref, wp_ref, *refs, n, tm, final_norm):
    if final_norm:
        gf_ref, o_ref, buf_ref, sem = refs
    else:
        o_ref, buf_ref, sem = refs
    base = pl.program_id(0) * tm

    def start(t, c):
        for k in range(TOP_K):
            _row_copy(y_hbm, pos_ref[k * n + base + t], buf_ref.at[k], t, sem).start()
        return c

    lax.fori_loop(0, tm, start, 0, unroll=8)
    diag = (lax.broadcasted_iota(jnp.int32, (tm, tm), 0) == lax.broadcasted_iota(jnp.int32, (tm, tm), 1))
    wcol = [jnp.sum(jnp.where(diag, w_ref[k:k + 1, :], 0.0), axis=1, keepdims=True) for k in range(TOP_K)]
    for k in range(TOP_K):
        pltpu.make_async_copy(y_hbm.at[pl.ds(0, tm)], buf_ref.at[k], sem).wait()
    y = _ple(x_ref[...] + sum(wcol[k] * buf_ref[k] for k in range(TOP_K)), gp_ref, wg_ref, p_ref, wp_ref)
    o_ref[...] = _rms(y, gf_ref[...]) if final_norm else y


def moe_combine(pos_flat, w, x, y_sorted, g_ple, wg_ple, p, wp_ple, g_final=None, tm=ROW_TILE):
    n, d = x.shape
    dp = p.shape[1]
    const = lambda shape: pl.BlockSpec(shape, lambda i, pos: (0, 0), pipeline_mode=pl.Buffered(1))
    in_specs = [pl.BlockSpec((w.shape[0], tm), lambda i, pos: (0, i)),
                pl.BlockSpec((tm, d), lambda i, pos: (i, 0)),
                pl.BlockSpec(memory_space=pl.ANY),
                const((1, d)), const((d, d)), pl.BlockSpec((tm, dp), lambda i, pos: (i, 0)), const((dp, d))]
    args = [pos_flat, w, x, y_sorted, g_ple.reshape(1, d), wg_ple, p, wp_ple]
    if g_final is not None:
        in_specs.append(const((1, d)))
        args.append(g_final.reshape(1, d))
    return pl.pallas_call(
        functools.partial(_combine_kernel, n=n, tm=tm, final_norm=g_final is not None),
        grid_spec=pltpu.PrefetchScalarGridSpec(
            num_scalar_prefetch=1, grid=(n // tm,),
            in_specs=in_specs,
            out_specs=pl.BlockSpec((tm, d), lambda i, pos: (i, 0)),
            scratch_shapes=[pltpu.VMEM((TOP_K, tm, d), F32), pltpu.SemaphoreType.DMA]),
        out_shape=jax.ShapeDtypeStruct((n, d), F32),
        compiler_params=_cparams(("arbitrary",)),
        name="moe_combine",
    )(*args)


def _compress_kernel(h_ref, pos_ref, w1_ref, w2_ref, o_ref):
    half = h_ref.shape[1]
    h = h_ref[...].astype(BF16)
    p_lo = jnp.dot(h, w1_ref[0:half, :], preferred_element_type=F32)
    p_hi = jnp.dot(h, w1_ref[half:, :], preferred_element_type=F32)
    nh = p_hi.shape[0]
    posb = jnp.dot(pos_ref[...].astype(BF16), w1_ref[...], preferred_element_type=F32)
    hid = p_lo + pltpu.roll(p_hi, nh - 1, 0) + posb
    o_ref[...] = jnp.dot(jax.nn.gelu(hid).astype(BF16), w2_ref[...],
                         preferred_element_type=F32).astype(o_ref.dtype)


def nsa_compress(halves, pos, w1, w2):
    bg, nh, hw = halves.shape
    hid = w1.shape[1]
    dh = w2.shape[1]
    return pl.pallas_call(
        _compress_kernel,
        grid=(bg,),
        in_specs=[pl.BlockSpec((None, nh, hw), lambda i: (i, 0, 0)),
                  pl.BlockSpec((1, 2 * hw), lambda i: (0, 0)),
                  pl.BlockSpec((2 * hw, hid), lambda i: (0, 0)),
                  pl.BlockSpec((hid, dh), lambda i: (0, 0))],
        out_specs=pl.BlockSpec((None, nh, dh), lambda i: (i, 0, 0)),
        out_shape=jax.ShapeDtypeStruct((bg, nh, dh), BF16),
        compiler_params=_cparams(("parallel",)),
    )(halves, pos.reshape(1, 2 * hw), w1, w2)


def _rel_bucket_np(dist):
    n = np.maximum(dist, 0)
    max_exact = N_BUCKETS // 2
    nf = np.maximum(n, 1).astype(np.float32)
    large = max_exact + (np.log(nf / max_exact) / np.float32(math.log(MAX_DISTANCE / max_exact))
                         * (N_BUCKETS - max_exact)).astype(np.int32)
    return np.where(n < max_exact, n, np.minimum(large, N_BUCKETS - 1)).astype(np.int32)


def _nsa_kernel(q_ref, gt_ref, kc_ref, vct_ref, bct_ref, ks_ref, kw_ref, vst_ref, vwt_ref, tzw_ref, tzc_ref,
                ovlt_ref, expt_ref, o_ref, *, n_cmp, n_sel):
    tq = q_ref.shape[0]
    dh = HEAD_DIM
    r = q_ref.shape[1] // dh
    near = tzw_ref.shape[0]
    n_slc = ovlt_ref.shape[0]
    ncp = kc_ref.shape[0]
    g = pl.program_id(1)
    t0 = pl.multiple_of(pl.program_id(2) * tq, tq)
    near_keys = pl.ds(t0, near)
    heads = lambda blocks: jnp.concatenate(blocks, axis=1)
    per_head = lambda x, h: x[:, h * tq:(h + 1) * tq]

    qt = q_ref[...].astype(F32).T.astype(BF16)
    zero = jnp.zeros((dh, r * tq), BF16)
    qh = heads([qt[h * dh:(h + 1) * dh, :] for h in range(r)])
    qz = jnp.where(g == 0, jnp.concatenate([qh, zero], axis=0), jnp.concatenate([zero, qh], axis=0))
    gate_t = gt_ref[...].T

    def gate(branch):
        rows = [gate_t[branch * N_HEADS + gi * r:branch * N_HEADS + (gi + 1) * r, :] for gi in range(N_KV_GROUPS)]
        mine = jnp.where(g == 0, rows[0], rows[1])
        return heads([mine[h:h + 1, :] for h in range(r)])

    cblk = lax.broadcasted_iota(jnp.int32, (ncp, r * tq), 0)
    qpos = t0 + (lax.broadcasted_iota(jnp.int32, (ncp, r * tq), 1) & (tq - 1))
    mask_c = (qpos - (cblk * CMP_STRIDE + CMP_BLOCK - 1) >= 0) & (cblk < n_cmp)
    s = jnp.dot(kc_ref[...], qz, preferred_element_type=F32) + bct_ref[...]
    l = jnp.where(mask_c, s, NEG)
    e = jnp.where(mask_c, jnp.exp2(l - jnp.max(l, axis=0, keepdims=True)), 0.0)
    p = e / jnp.maximum(jnp.sum(e, axis=0, keepdims=True), 1e-30)
    out = gate(0) * jnp.dot(vct_ref[...], p.astype(BF16), preferred_element_type=F32)
    psum = functools.reduce(lambda acc, h: acc + per_head(p, h), range(1, r), per_head(p, 0))

    def pv_and_sum(vt_ref, keys, e):
        ones = jnp.ones((PACKED_ROWS_BF16, keys.size), BF16)
        return jnp.dot(jnp.concatenate([vt_ref[:, keys], ones], axis=0), e.astype(BF16),
                       preferred_element_type=F32)

    key_exists = t0 + lax.broadcasted_iota(jnp.int32, (near, r * tq), 0) >= WINDOW
    s = jnp.dot(kw_ref[near_keys, :], qz, preferred_element_type=F32)
    l = jnp.where(key_exists, s + tzw_ref[...], NEG)
    e = jnp.exp2(l - jnp.max(l, axis=0, keepdims=True))
    o_w = pv_and_sum(vwt_ref, near_keys, e)
    out = out + gate(2) * (o_w[:dh] / o_w[dh:dh + 1])

    imp = jnp.dot(ovlt_ref[...], psum, preferred_element_type=F32, precision=lax.Precision.HIGHEST)
    jblk = lax.broadcasted_iota(jnp.int32, (n_slc, tq), 0).astype(F32)
    qblk = ((t0 + lax.broadcasted_iota(jnp.int32, (n_slc, tq), 1)) // SLC_BLOCK).astype(F32)
    forced = (jblk == 0) | (jblk == qblk) | (jblk == qblk - 1)
    score = jnp.where(forced, 1e9, jnp.where(jblk <= qblk, imp, -1e9))
    sel = jnp.zeros((n_slc, tq), F32)
    for _ in range(n_sel):
        m = jnp.max(score, axis=0, keepdims=True)
        first = jnp.min(jnp.where(score == m, jblk, float(n_slc)), axis=0, keepdims=True)
        pick = jblk == first
        sel = jnp.where(pick, 1.0, sel)
        score = jnp.where(pick, -jnp.inf, score)
    sel16 = sel.astype(BF16)
    all_heads = lambda x: heads([x] * r)

    def member_mask(keys, also=True):
        member = jnp.dot(expt_ref[keys, :], sel16, preferred_element_type=F32) > 0.5
        return all_heads(jnp.where(member & also, 0.0, NEG))

    l = jnp.dot(ks_ref[near_keys, :], qz, preferred_element_type=F32) + tzc_ref[...] + member_mask(near_keys)
    m = jnp.max(l, axis=0, keepdims=True)
    acc = pv_and_sum(vst_ref, near_keys, jnp.exp2(l - m))
    far_bias = tzc_ref[0:1, :]

    def far_chunk(f, ma):
        m, acc = ma
        first = pl.multiple_of(WINDOW + f * FAR_CHUNK, FAR_CHUNK)
        keys = pl.ds(first, FAR_CHUNK)
        before_near = first + lax.broadcasted_iota(jnp.int32, (FAR_CHUNK, tq), 0) < t0
        l = jnp.dot(ks_ref[keys, :], qz, preferred_element_type=F32) + far_bias + member_mask(keys, before_near)
        m_new = jnp.maximum(m, jnp.max(l, axis=0, keepdims=True))
        return m_new, jnp.exp2(m - m_new) * acc + pv_and_sum(vst_ref, keys, jnp.exp2(l - m_new))

    n_far = (jnp.maximum(t0, WINDOW) - 1) // FAR_CHUNK
    m, acc = lax.fori_loop(0, n_far, far_chunk, (m, acc))
    out = out + gate(1) * (acc[:dh] / acc[dh:dh + 1])

    o_ref[...] = jnp.concatenate([per_head(out, h) for h in range(r)], axis=0).T.astype(o_ref.dtype)


def nsa_attention_core(q, gates_pad, kc2, vct, bct, kv_pad, kvt_pad, tzw, tzc, ovlt, expt, batch, seq, n_cmp):
    n, hd = q.shape
    gq, dh, tq = N_KV_GROUPS, HEAD_DIM, Q_BLOCK
    r = N_HEADS // gq
    nqt = seq // tq
    spad = kv_pad.shape[1]
    ncp = kc2.shape[1]
    n_slc = seq // SLC_BLOCK
    near = tzw.shape[1]
    row_tile = lambda b, g, t: (b * nqt + t, g)
    key_cols = lambda branch: pl.BlockSpec((None, spad, gq * dh), lambda b, g, t: (b, 0, branch))
    val_rows = lambda branch: pl.BlockSpec((dh, spad), lambda b, g, t: (branch * gq + g, b))
    table = pl.BlockSpec((None, near, r * tq), lambda b, g, t: (g, 0, 0))
    const = lambda a: pl.BlockSpec(a.shape, lambda b, g, t: (0,) * a.ndim)
    return pl.pallas_call(
        functools.partial(_nsa_kernel, n_cmp=n_cmp, n_sel=min(N_SELECT, n_slc)),
        grid=(batch, gq, nqt),
        in_specs=[pl.BlockSpec((tq, r * dh), row_tile),
                  pl.BlockSpec((tq, gates_pad.shape[1]), lambda b, g, t: (b * nqt + t, 0)),
                  pl.BlockSpec((None, ncp, gq * dh), lambda b, g, t: (b, 0, 0)),
                  pl.BlockSpec((None, None, dh, ncp), lambda b, g, t: (b, g, 0, 0)),
                  pl.BlockSpec((None, ncp, r * tq), lambda b, g, t: (g, t, 0)),
                  key_cols(2), key_cols(4), val_rows(3), val_rows(5), table, table,
                  const(ovlt), const(expt)],
        out_specs=pl.BlockSpec((tq, r * dh), row_tile),
        out_shape=jax.ShapeDtypeStruct((n, hd), BF16),
        compiler_params=_cparams(("parallel", "parallel", "parallel")),
        name="nsa_attention",
    )(q, gates_pad, kc2, vct, bct, kv_pad, kv_pad, kvt_pad, kvt_pad, tzw, tzc, ovlt, expt)


def _bias_table_kernel(rb_ref, bkt_win_ref, bkt_causal_ref, bkt_cmp_ref, tzw_ref, tzc_ref, bct_ref, *, rows):
    h = pl.program_id(0)

    def lookup(bkt):
        acc = jnp.full(bkt.shape, NEG, F32)
        for k in range(N_BUCKETS):
            acc = jnp.where(bkt == k, rb_ref[k * N_HEADS + h] * LOG2_E, acc)
        return acc

    for src, dst in ((bkt_win_ref, tzw_ref), (bkt_causal_ref, tzc_ref), (bkt_cmp_ref, bct_ref)):
        def body(c, carry, src=src, dst=dst):
            sl = pl.ds(pl.multiple_of(c * rows, rows), rows)
            dst[sl, :] = lookup(src[sl, :])
            return carry
        lax.fori_loop(0, src.shape[0] // rows, body, 0)


def _nsa_tables(rel_bias, seq, n_cmp, ncp):
    tq = Q_BLOCK
    r = N_HEADS // N_KV_GROUPS
    dist_near = np.arange(tq)[None, :] + WINDOW - np.arange(NEAR)[:, None]
    bkt_near = _rel_bucket_np(dist_near)
    bkt_win = np.where((dist_near >= 0) & (dist_near < WINDOW), bkt_near, N_BUCKETS).astype(np.int32)
    bkt_causal = np.where(dist_near >= 0, bkt_near, N_BUCKETS).astype(np.int32)
    dist_c = np.arange(seq)[None, :] - (np.arange(ncp)[:, None] * CMP_STRIDE + CMP_BLOCK - 1)
    bkt_c = _rel_bucket_np(dist_c).reshape(ncp, seq // tq, tq).transpose(1, 0, 2).reshape(-1, tq)
    maps = [jnp.asarray(m) for m in (bkt_win, bkt_causal, bkt_c)]
    whole = lambda m: pl.BlockSpec(m.shape, lambda h, rb: (0, 0))
    per_head = lambda m: pl.BlockSpec((None, m.shape[0], tq), lambda h, rb: (h // r, 0, h % r))
    tzw, tzc, bct = pl.pallas_call(
        functools.partial(_bias_table_kernel, rows=64),
        grid_spec=pltpu.PrefetchScalarGridSpec(
            num_scalar_prefetch=1, grid=(N_HEADS,),
            in_specs=[whole(m) for m in maps], out_specs=[per_head(m) for m in maps]),
        out_shape=[jax.ShapeDtypeStruct((N_KV_GROUPS, m.shape[0], r * tq), F32) for m in maps],
        compiler_params=_cparams(("parallel",)),
        name="nsa_bias_tables",
    )(rel_bias.reshape(N_BUCKETS * N_HEADS), *maps)
    n_slc = seq // SLC_BLOCK
    cs = np.arange(ncp)[None, :] * CMP_STRIDE
    ss = np.arange(n_slc)[:, None] * SLC_BLOCK
    overlap_t = (cs < ss + SLC_BLOCK) & (cs + CMP_BLOCK > ss) & (np.arange(ncp)[None, :] < n_cmp)
    kpos = np.arange(WINDOW + seq)[:, None] - WINDOW
    expand_t = (kpos // SLC_BLOCK == np.arange(n_slc)[None, :]) & (kpos >= 0)
    return tzw, tzc, bct, jnp.asarray(overlap_t, F32), jnp.asarray(expand_t, BF16)


def _kv_kernel(x_ref, g_ref, w_ref, wt_ref, kv_ref, kvt_ref):
    @pl.when(pl.program_id(1) == 0)
    def _():
        kv_ref[...] = jnp.zeros_like(kv_ref)
        kvt_ref[...] = jnp.zeros_like(kvt_ref)

    @pl.when(pl.program_id(1) > 0)
    def _():
        xn = _rms(x_ref[...], g_ref[...]).astype(BF16)
        kv_ref[...] = jnp.dot(xn, w_ref[...], preferred_element_type=F32).astype(kv_ref.dtype)
        kvt_ref[...] = lax.dot_general(wt_ref[...], xn, (((1,), (1,)), ((), ())),
                                       preferred_element_type=F32).astype(kvt_ref.dtype)


def kv_projection(x, g, w_kv, batch, seq):
    n, d = x.shape
    f = w_kv.shape[1]
    tm = WINDOW
    nt = seq // tm
    rows = batch * (nt + 1) * tm
    w16 = w_kv.astype(BF16)
    const = lambda shape: pl.BlockSpec(shape, lambda b, t: (0, 0))
    return pl.pallas_call(
        _kv_kernel,
        grid=(batch, nt + 1),
        in_specs=[pl.BlockSpec((tm, d), lambda b, t: (b * nt + jnp.maximum(t - 1, 0), 0)), const((1, d)),
                  const((d, f)), const((f, d))],
        out_specs=[pl.BlockSpec((tm, f), lambda b, t: (b * (nt + 1) + t, 0)),
                   pl.BlockSpec((f, tm), lambda b, t: (0, b * (nt + 1) + t))],
        out_shape=[jax.ShapeDtypeStruct((rows, f), BF16), jax.ShapeDtypeStruct((f, rows), BF16)],
        compiler_params=_cparams(("parallel", "arbitrary")),
        name="kv_projection",
    )(x, g.reshape(1, d), w16, w16.T)


def _nsa_shared_kv(x, batch, seq, kv_norm, w_kv, cmp_pos_k, cmp_pos_v, k_w1, k_w2, v_w1, v_w2):
    n, d = x.shape
    gq, dh = N_KV_GROUPS, HEAD_DIM
    f = w_kv.shape[1]
    kv_pad, kvt_pad = kv_projection(x, kv_norm, w_kv, batch, seq)
    kv_pad = kv_pad.reshape(batch, WINDOW + seq, f)
    kv5 = kv_pad[:, WINDOW:].reshape(batch, seq, N_KV_BRANCH, gq, dh)
    per_group = lambda j: jnp.transpose(kv5[:, :, j], (0, 2, 1, 3))
    n_cmp = (seq - CMP_BLOCK) // CMP_STRIDE + 1
    nhalf = seq // CMP_STRIDE

    def compress(t, pos, w1, w2):
        halves = t.reshape(batch * gq, nhalf, CMP_STRIDE * dh)
        return nsa_compress(halves, pos, w1.astype(BF16), w2.astype(BF16)).reshape(batch, gq, nhalf, dh)

    k_cmp = compress(per_group(0), cmp_pos_k, k_w1, k_w2)
    v_cmp = compress(per_group(1), cmp_pos_v, v_w1, v_w2)
    kc2 = jnp.transpose(k_cmp, (0, 2, 1, 3)).reshape(batch, nhalf, gq * dh)
    vct = jnp.transpose(v_cmp, (0, 1, 3, 2))
    return kc2, vct, kv_pad, kvt_pad, n_cmp


def _q_kernel(x_ref, g_ref, wq_ref, wg_ref, q_ref, gates_ref):
    xn = _rms(x_ref[...], g_ref[...]).astype(BF16)
    q = jnp.dot(xn, wq_ref[...], preferred_element_type=F32) * (HEAD_DIM ** -0.5 * LOG2_E)
    q_ref[...] = q.astype(q_ref.dtype)
    gates_ref[...] = jax.nn.sigmoid(jnp.dot(xn, wg_ref[...], preferred_element_type=F32))


def q_projection(x, g, w_q, tm=ROW_TILE):
    n, d = x.shape
    hd = N_HEADS * HEAD_DIM
    w16 = w_q.astype(BF16)
    w_gate = jnp.pad(w16[:, hd:], ((0, 0), (0, 128 - (w_q.shape[1] - hd))))
    return pl.pallas_call(
        _q_kernel,
        grid=(n // tm,),
        in_specs=[pl.BlockSpec((tm, d), lambda i: (i, 0)), pl.BlockSpec((1, d), lambda i: (0, 0)),
                  pl.BlockSpec((d, hd), lambda i: (0, 0)), pl.BlockSpec((d, 128), lambda i: (0, 0))],
        out_specs=[pl.BlockSpec((tm, hd), lambda i: (i, 0)), pl.BlockSpec((tm, 128), lambda i: (i, 0))],
        out_shape=[jax.ShapeDtypeStruct((n, hd), BF16), jax.ShapeDtypeStruct((n, 128), F32)],
        compiler_params=_cparams(("parallel",)),
        name="q_projection",
    )(x, g.reshape(1, d), w16[:, :hd], w_gate)


def _nsa_mix(x, batch, seq, g_mix, w_q, shared, tables):
    kc2, vct, kv_pad, kvt_pad, n_cmp = shared
    tzw, tzc, bct, ovlt, expt = tables
    q, gates_pad = q_projection(x, g_mix, w_q)
    return nsa_attention_core(q, gates_pad, kc2, vct, bct, kv_pad, kvt_pad, tzw, tzc, ovlt, expt,
                              batch, seq, n_cmp)


def _dense_tail_kernel(a_ref, wo_ref, x_ref, g_ref, wgu_ref, wd_ref, gp_ref, wg_ref, p_ref, wp_ref, o_ref,
                       act_ref, *, tn):
    x = x_ref[...] + jnp.dot(a_ref[...], wo_ref[...], preferred_element_type=F32)
    xn = _rms(x, g_ref[...]).astype(BF16)
    ff = wd_ref.shape[0]
    for c in range(ff // tn):
        gate = jnp.dot(xn, wgu_ref[:, c * tn:(c + 1) * tn], preferred_element_type=F32)
        up = jnp.dot(xn, wgu_ref[:, ff + c * tn:ff + (c + 1) * tn], preferred_element_type=F32)
        act_ref[:, c * tn:(c + 1) * tn] = (_silu(gate) * up).astype(BF16)
    x = x + jnp.dot(act_ref[...], wd_ref[...], preferred_element_type=F32)
    o_ref[...] = _ple(x, gp_ref, wg_ref, p_ref, wp_ref)


def dense_layer_tail(a, w_o, x, g_ffn, w_gu, w_down, g_ple, wg_ple, p, wp_ple, tm=ROW_TILE, tn=512):
    n, d = x.shape
    k = a.shape[1]
    ff = w_down.shape[0]
    dp = p.shape[1]
    whole = lambda shape: pl.BlockSpec(shape, lambda i: (0, 0), pipeline_mode=pl.Buffered(1))
    rows = lambda width: pl.BlockSpec((tm, width), lambda i: (i, 0))
    return pl.pallas_call(
        functools.partial(_dense_tail_kernel, tn=min(tn, ff)),
        grid=(n // tm,),
        in_specs=[rows(k), whole((k, d)), rows(d), whole((1, d)), whole((d, 2 * ff)), whole((ff, d)),
                  whole((1, d)), whole((d, d)), rows(dp), whole((dp, d))],
        out_specs=rows(d),
        out_shape=jax.ShapeDtypeStruct((n, d), F32),
        scratch_shapes=[pltpu.VMEM((tm, ff), BF16)],
        compiler_params=_cparams(("parallel",)),
        name="dense_layer_tail",
    )(a, w_o, x, g_ffn.reshape(1, d), w_gu, w_down, g_ple.reshape(1, d), wg_ple, p, wp_ple)


def _moe_ffn(x, g, w_router, w_gu_all, w_down_all, layer, ple, tm=ROW_TILE):
    n, d = x.shape
    ne = w_router.shape[1]
    sel, w, counts = moe_route(x, g, w_router)
    padded = (counts[:, 0] + tm - 1) // tm * tm
    ends = jnp.cumsum(padded)
    n_tiles = (TOP_K * n) // tm + ne
    n_used = (ends[-1] // tm).astype(jnp.int32).reshape(1)
    tile_first_row = jnp.arange(n_tiles, dtype=jnp.int32) * tm
    tile_expert = jnp.sum(ends[None, :] <= tile_first_row[:, None], axis=1)
    tile_expert = jnp.minimum(tile_expert, tile_expert[jnp.maximum(n_used[0] - 1, 0)]).astype(jnp.int32)
    starts = (ends - padded).astype(jnp.int32)
    pos = moe_slots(starts, sel)
    pos_flat = pos[:TOP_K].reshape(TOP_K * n)
    total = jnp.full((1,), n_tiles * tm, jnp.int32)
    unrouted = jnp.concatenate([starts + counts[:, 0], ends[-1:], ends, total]).astype(jnp.int32)
    inv = moe_inverse(pos_flat, unrouted, n, n_tiles * tm)
    flat = lambda w_all: w_all.reshape((-1,) + w_all.shape[2:])
    y_sorted = moe_experts(tile_expert + layer * ne, n_used, inv, x, g, flat(w_gu_all), flat(w_down_all))
    return moe_combine(pos_flat, w, x, y_sorted, *ple)


def kernel(x, p, g_mix, g_ffn, g_ple, g_final, rg_w_in, rg_conv_w, rg_conv_b, rg_w_a, rg_b_a, rg_w_x,
           rg_b_x, rg_lambda, rg_w_out, kv_norm, w_kv, cmp_pos_k, cmp_pos_v, cmp_k_w1, cmp_k_w2,
           cmp_v_w1, cmp_v_w2, rel_bias, nsa_w_q, nsa_w_o, ffn_w_gu, ffn_w_down, moe_w_router,
           moe_w_gu, moe_w_down, ple_w_proj, ple_w_gate):
    batch, seq, d = x.shape
    depth = p.shape[0]
    n_a = rg_w_in.shape[0]
    n = batch * seq
    x = x.reshape(n, d)
    p = p.reshape(depth, n, p.shape[-1])
    n_cmp = (seq - CMP_BLOCK) // CMP_STRIDE + 1
    tables = _nsa_tables(rel_bias, seq, n_cmp, seq // CMP_STRIDE)
    shared = None
    for i in range(depth):
        if i < n_a:
            mix = rg_lru_core(x, g_mix[i], rg_w_in[i].astype(BF16), batch, seq, rg_conv_w[i], rg_conv_b[i],
                              _block_diag_chunks(rg_w_a[i]), rg_b_a[i], _block_diag_chunks(rg_w_x[i]),
                              rg_b_x[i], rg_lambda[i])
            w_o = rg_w_out[i].astype(BF16)
        else:
            mix = _nsa_mix(x, batch, seq, g_mix[i], nsa_w_q[i - n_a], shared, tables)
            w_o = nsa_w_o[i - n_a].astype(BF16)
        ple = (g_ple[i], ple_w_gate[i].astype(BF16), p[i], ple_w_proj[i].astype(BF16))
        if i % 2 == 0:
            assert i < depth - 1
            x = dense_layer_tail(mix, w_o, x, g_ffn[i], ffn_w_gu[i // 2].astype(BF16),
                                 ffn_w_down[i // 2].astype(BF16), *ple)
        else:
            x = matmul_residual(mix, w_o, x)
            x = _moe_ffn(x, g_ffn[i], moe_w_router[i // 2], moe_w_gu, moe_w_down, i // 2,
                         ple + ((g_final,) if i == depth - 1 else ()))
        if i == n_a - 1:
            shared = _nsa_shared_kv(x, batch, seq, kv_norm, w_kv, cmp_pos_k, cmp_pos_v,
                                    cmp_k_w1, cmp_k_w2, cmp_v_w1, cmp_v_w2)
    return x.reshape(batch, seq, d)
```

```python
import functools
import math

import numpy as np
import jax
import jax.numpy as jnp
from jax import lax
from jax.experimental import pallas as pl
from jax.experimental.pallas import tpu as pltpu

F32 = jnp.float32
BF16 = jnp.bfloat16

EPS = 1e-6
CONV_WIDTH = 4
LRU_BLOCKS = 16
LRU_C = 8.0
N_HEADS = 16
N_KV_GROUPS = 2
GROUP = N_HEADS // N_KV_GROUPS
HEAD_DIM = 64
N_KV_BRANCH = 6
CMP_BLOCK = 32
CMP_STRIDE = 16
SLC_BLOCK = 64
N_SELECT = 8
WINDOW = 512
Q_BLOCK = 128
N_BUCKETS = 32
MAX_DISTANCE = 128
TOP_K = 2
NEAR = WINDOW + Q_BLOCK
FAR_CHUNK = 512
NEG = -1e30
LOG2_E = math.log2(math.e)
PACKED_ROWS_BF16 = 16

VMEM_LIMIT_V7X = 56 * 1024 * 1024
ROW_TILE = 512


def _cparams(sem):
    return pltpu.CompilerParams(dimension_semantics=sem, vmem_limit_bytes=VMEM_LIMIT_V7X)


def _rms(x, g):
    return x * lax.rsqrt(jnp.mean(x * x, axis=-1, keepdims=True) + EPS) * g


def _silu(x):
    return x * jax.nn.sigmoid(x)


BD = 256
SCAN_ROWS = 8


def _rg_kernel(x_ref, g_ref, win_ref, cw_ref, cb_ref, wa_ref, ba_ref, wx_ref, bx_ref, lam_ref,
               o_ref, xe_ref, h_ref, a_ref, u_ref, xn_ref):
    t, c = o_ref.shape

    @pl.when(pl.program_id(1) == 0)
    def _():
        xe_ref[0:8, :] = jnp.zeros((8, c), F32)
        h_ref[...] = jnp.zeros_like(h_ref)

    xn_ref[...] = _rms(x_ref[...], g_ref[...]).astype(BF16)
    xe_ref[8:, :] = jnp.dot(xn_ref[...], win_ref[:, c:], preferred_element_type=F32)
    xc = cb_ref[...] + sum(
        cw_ref[k:k + 1, :] * xe_ref[8 - (CONV_WIDTH - 1) + k: 8 - (CONV_WIDTH - 1) + k + t, :]
        for k in range(CONV_WIDTH))
    xe_ref[0:8, :] = xe_ref[t:t + 8, :]

    lam = -lam_ref[...]
    softplus_neg_lam = jnp.maximum(lam, 0.0) + jnp.log1p(jnp.exp(-jnp.abs(lam)))
    for cblk in range(c // BD):
        sl = slice(cblk * BD, (cblk + 1) * BD)
        xb = xc[:, sl]
        xb16 = xb.astype(BF16)
        r = jax.nn.sigmoid(jnp.dot(xb16, wa_ref[cblk], preferred_element_type=F32) + ba_ref[:, sl])
        i = jax.nn.sigmoid(jnp.dot(xb16, wx_ref[cblk], preferred_element_type=F32) + bx_ref[:, sl])
        log_a = -LRU_C * r * softplus_neg_lam[:, sl]
        a = jnp.exp(log_a)
        a_ref[:, sl] = a
        u_ref[:, sl] = jnp.sqrt(1.0 - a * a) * (i * xb)

    row = lax.broadcasted_iota(jnp.int32, (SCAN_ROWS, c), 0)

    def scan_tile(k, h_prev):
        rows = pl.ds(pl.multiple_of(k * SCAN_ROWS, SCAN_ROWS), SCAN_ROWS)
        a = a_ref[rows, :]
        u = u_ref[rows, :]
        for d in (1, 2, 4):
            a_s = jnp.where(row >= d, pltpu.roll(a, d, 0), 1.0)
            u_s = jnp.where(row >= d, pltpu.roll(u, d, 0), 0.0)
            u = a * u_s + u
            a = a * a_s
        h = a * h_prev + u
        u_ref[rows, :] = h
        return jnp.broadcast_to(h[SCAN_ROWS - 1:SCAN_ROWS, :], (SCAN_ROWS, c))

    h_ref[...] = lax.fori_loop(0, t // SCAN_ROWS, scan_tile, h_ref[...])
    gate = jnp.dot(xn_ref[...], win_ref[:, :c], preferred_element_type=F32)
    o_ref[...] = (jax.nn.gelu(gate) * u_ref[...]).astype(o_ref.dtype)


def rg_lru_core(x, g, w_in, batch, seq, conv_w, conv_b, wa_bd, b_a, wx_bd, b_x, lam, t_chunk=512):
    n, d = x.shape
    c = w_in.shape[1] // 2
    nt = seq // t_chunk
    vec = lambda v: v.reshape(1, -1)
    rows = lambda width: pl.BlockSpec((t_chunk, width), lambda b, t: (b * nt + t, 0))
    const = lambda shape: pl.BlockSpec(shape, lambda b, t: (0,) * len(shape))
    return pl.pallas_call(
        _rg_kernel,
        grid=(batch, nt),
        in_specs=[rows(d), const((1, d)), const(w_in.shape), const((CONV_WIDTH, c)), const((1, c)),
                  const(wa_bd.shape), const((1, c)), const(wx_bd.shape), const((1, c)), const((1, c))],
        out_specs=rows(c),
        out_shape=jax.ShapeDtypeStruct((n, c), BF16),
        scratch_shapes=[pltpu.VMEM((t_chunk + 8, c), F32), pltpu.VMEM((SCAN_ROWS, c), F32),
                        pltpu.VMEM((t_chunk, c), F32), pltpu.VMEM((t_chunk, c), F32),
                        pltpu.VMEM((t_chunk, d), BF16)],
        compiler_params=_cparams(("parallel", "arbitrary")),
        name="rg_lru_core",
    )(x, vec(g), w_in, conv_w, vec(conv_b), wa_bd, vec(b_a), wx_bd, vec(b_x), vec(lam))


def _block_diag_chunks(w):
    nb, bw, _ = w.shape
    per = BD // bw
    w = w.reshape(nb // per, per, bw, bw)
    eye = jnp.eye(per, dtype=w.dtype)
    out = jnp.einsum('cpij,pq->cpiqj', w, eye).reshape(nb // per, BD, BD)
    return out.astype(BF16)


def _ple(x, g_ref, wg_ref, p_ref, wp_ref):
    xn = _rms(x, g_ref[...]).astype(BF16)
    gate = jax.nn.sigmoid(jnp.dot(xn, wg_ref[...], preferred_element_type=F32))
    proj = jnp.dot(p_ref[...].astype(BF16), wp_ref[...], preferred_element_type=F32)
    return x + gate * proj


SEL_ROWS = 8


def _route_kernel(a_ref, wo_ref, x_ref, g_ref, wt_ref, xo_ref, sel_ref, w_ref, cnt_ref, carry_ref):
    @pl.when(pl.program_id(0) == 0)
    def _():
        carry_ref[...] = jnp.zeros_like(carry_ref)

    x = x_ref[...] + jnp.dot(a_ref[...], wo_ref[...], preferred_element_type=F32)
    xo_ref[...] = x
    xn = _rms(x, g_ref[...])
    logits = lax.dot_general(wt_ref[...], xn, (((1,), (1,)), ((), ())), preferred_element_type=F32,
                             precision=lax.Precision.HIGHEST)
    ne, tm = logits.shape
    row = lax.broadcasted_iota(jnp.int32, (ne, tm), 0).astype(F32)
    m1 = jnp.max(logits, axis=0, keepdims=True)
    i1 = jnp.min(jnp.where(logits == m1, row, float(ne)), axis=0, keepdims=True)
    rest = jnp.where(row == i1, -jnp.inf, logits)
    m2 = jnp.max(rest, axis=0, keepdims=True)
    i2 = jnp.min(jnp.where(rest == m2, row, float(ne)), axis=0, keepdims=True)
    e2 = jnp.exp(m2 - m1)
    denom = 1.0 + e2
    ind = jnp.where((row == i1) | (row == i2), 1.0, 0.0)
    earlier = (lax.broadcasted_iota(jnp.int32, (tm, tm), 0) < lax.broadcasted_iota(jnp.int32, (tm, tm), 1))
    rank = jnp.dot(ind.astype(BF16), jnp.where(earlier, 1.0, 0.0).astype(BF16),
                   preferred_element_type=F32) + carry_ref[:, 0:1]
    r1 = jnp.sum(jnp.where(row == i1, rank, 0.0), axis=0, keepdims=True)
    r2 = jnp.sum(jnp.where(row == i2, rank, 0.0), axis=0, keepdims=True)
    carry_ref[...] = carry_ref[...] + jnp.sum(ind, axis=1, keepdims=True)
    pick = lambda vals: functools.reduce(
        lambda acc, kv: jnp.where(row == float(kv[0]), kv[1], acc), enumerate(vals), jnp.zeros((ne, tm), F32))
    sel_ref[...] = pick([i1, i2, r1, r2]).astype(jnp.int32)
    w_ref[...] = pick([1.0 / denom, e2 / denom])
    cnt_ref[...] = carry_ref[...].astype(jnp.int32)


def moe_route(a, w_o, x, g, w_router, tm=ROW_TILE):
    n, d = x.shape
    k = a.shape[1]
    ne = w_router.shape[1]
    assert ne == SEL_ROWS
    const = lambda shape: pl.BlockSpec(shape, lambda i: (0, 0))
    return pl.pallas_call(
        _route_kernel,
        grid=(n // tm,),
        in_specs=[pl.BlockSpec((tm, k), lambda i: (i, 0)), const((k, d)), pl.BlockSpec((tm, d), lambda i: (i, 0)),
                  const((1, d)), const((ne, d))],
        out_specs=[pl.BlockSpec((tm, d), lambda i: (i, 0)),
                   pl.BlockSpec((ne, tm), lambda i: (0, i)), pl.BlockSpec((ne, tm), lambda i: (0, i)),
                   pl.BlockSpec((ne, 128), lambda i: (0, 0))],
        out_shape=[jax.ShapeDtypeStruct((n, d), F32),
                   jax.ShapeDtypeStruct((ne, n), jnp.int32), jax.ShapeDtypeStruct((ne, n), F32),
                   jax.ShapeDtypeStruct((ne, 128), jnp.int32)],
        scratch_shapes=[pltpu.VMEM((ne, 128), F32)],
        compiler_params=_cparams(("arbitrary",)),
        name="moe_route",
    )(a, w_o, x, g.reshape(1, d), w_router.T)


def _slot_kernel(starts_ref, sel_ref, pos_ref):
    sel = sel_ref[...]
    start_of = lambda e_row: functools.reduce(
        lambda acc, e: jnp.where(e_row == e, starts_ref[e], acc), range(starts_ref.shape[0]),
        jnp.zeros_like(e_row))
    row = lax.broadcasted_iota(jnp.int32, sel.shape, 0)
    p1 = start_of(sel[0:1]) + sel[2:3]
    p2 = start_of(sel[1:2]) + sel[3:4]
    pos_ref[...] = jnp.where(row == 0, p1, jnp.where(row == 1, p2, 0))


def moe_slots(starts, sel, tm=2048):
    ne, n = sel.shape
    return pl.pallas_call(
        _slot_kernel,
        grid_spec=pltpu.PrefetchScalarGridSpec(
            num_scalar_prefetch=1, grid=(n // tm,),
            in_specs=[pl.BlockSpec((ne, tm), lambda i, s: (0, i))],
            out_specs=pl.BlockSpec((ne, tm), lambda i, s: (0, i))),
        out_shape=jax.ShapeDtypeStruct((ne, n), jnp.int32),
        compiler_params=_cparams(("parallel",)),
        name="moe_slots",
    )(starts, sel)


def _row_copy(src, src_row, dst, dst_row, sem):
    return pltpu.make_async_copy(src.at[pl.ds(src_row, 1)], dst.at[pl.ds(dst_row, 1)], sem)


def _inverse_kernel(pos_ref, pad_ref, inv_ref, *, n):
    def clear(s, c):
        inv_ref[s] = 0
        return c

    def fill(t, c):
        for k in range(TOP_K):
            inv_ref[pos_ref[k * n + t]] = t
        return c

    n_ranges = pad_ref.shape[0] // 2
    for e in range(n_ranges):
        lax.fori_loop(pad_ref[e], pad_ref[n_ranges + e], clear, 0)
    lax.fori_loop(0, n, fill, 0, unroll=8)


def moe_inverse(pos_flat, pad_bounds, n, n_rows):
    return pl.pallas_call(
        functools.partial(_inverse_kernel, n=n),
        in_specs=[pl.BlockSpec(memory_space=pltpu.SMEM), pl.BlockSpec(memory_space=pltpu.SMEM)],
        out_specs=pl.BlockSpec(memory_space=pltpu.SMEM),
        out_shape=jax.ShapeDtypeStruct((n_rows,), jnp.int32),
        name="moe_inverse",
    )(pos_flat, pad_bounds)


def _experts_kernel(te_ref, nu_ref, inv_ref, x_hbm, g_ref, wgu_hbm, wd_hbm, o_ref, xbuf_ref, act_ref, wgu_ref,
                    wd_ref, stage_gu, stage_d, sems, wsems, dsems, *, tm, tn):
    i = pl.program_id(0)
    n_used = nu_ref[0]
    ff = wd_ref.shape[0]
    n_chunks = ff // tn

    def gu_chunk(j):
        col = (j % 2) * ff + (j // 2) * tn
        return slice(col, col + tn)

    def gu_copy(e, j):
        slot = j % stage_gu.shape[0]
        return pltpu.make_async_copy(wgu_hbm.at[e, :, gu_chunk(j)], stage_gu.at[slot], wsems.at[slot])

    def down_copy(e, c):
        slot = c % stage_d.shape[0]
        return pltpu.make_async_copy(wd_hbm.at[e, c * tn:(c + 1) * tn, :], stage_d.at[slot], dsems.at[slot])

    def start_gather(tile, slot):
        def body(t, c):
            _row_copy(x_hbm, inv_ref[tile * tm + t], xbuf_ref.at[slot], t, sems.at[slot]).start()
            return c
        lax.fori_loop(0, tm, body, 0, unroll=8)

    def wait_gather(slot):
        pltpu.make_async_copy(x_hbm.at[pl.ds(0, tm)], xbuf_ref.at[slot], sems.at[slot]).wait()

    last = pl.num_programs(0) - 1
    slot = lax.rem(i, 2)

    @pl.when(i == 0)
    def _():
        start_gather(0, 0)

    def tile(stream):
        e = te_ref[i]
        if stream:
            for j in range(min(stage_gu.shape[0], 2 * n_chunks)):
                gu_copy(e, j).start()
            for c in range(min(stage_d.shape[0], n_chunks)):
                down_copy(e, c).start()
        wait_gather(slot)
        xn = _rms(xbuf_ref[slot], g_ref[...]).astype(BF16)
        nxt = jnp.minimum(i + 1, last) * tm
        for c in range(n_chunks):
            if stream:
                for j in (2 * c, 2 * c + 1):
                    gu_copy(e, j).wait()
                    wgu_ref[:, gu_chunk(j)] = stage_gu[j % stage_gu.shape[0]].astype(BF16)
                    if j + stage_gu.shape[0] < 2 * n_chunks:
                        gu_copy(e, j + stage_gu.shape[0]).start()
            for t in range(c * tm // n_chunks, (c + 1) * tm // n_chunks):
                _row_copy(x_hbm, inv_ref[nxt + t], xbuf_ref.at[1 - slot], t, sems.at[1 - slot]).start()
            rows = slice(c * tn, (c + 1) * tn)
            gate = jnp.dot(xn, wgu_ref[:, rows], preferred_element_type=F32)
            up = jnp.dot(xn, wgu_ref[:, ff + c * tn:ff + (c + 1) * tn], preferred_element_type=F32)
            act_ref[:, rows] = (_silu(gate) * up).astype(BF16)
            if stream:
                down_copy(e, c).wait()
                wd_ref[rows, :] = stage_d[c % stage_d.shape[0]].astype(BF16)
                if c + stage_d.shape[0] < n_chunks:
                    down_copy(e, c + stage_d.shape[0]).start()
                part = jnp.dot(act_ref[:, rows], wd_ref[rows, :], preferred_element_type=F32)
                o_ref[...] = part if c == 0 else o_ref[...] + part
        if not stream:
            o_ref[...] = jnp.dot(act_ref[...], wd_ref[...], preferred_element_type=F32)

        @pl.when(i == last)
        def _():
            wait_gather(1 - slot)

    new_expert = (i == 0) | (te_ref[i] != te_ref[jnp.maximum(i - 1, 0)])

    @pl.when((i < n_used) & new_expert)
    def _():
        tile(stream=True)

    @pl.when((i < n_used) & jnp.logical_not(new_expert))
    def _():
        tile(stream=False)

    @pl.when(i >= n_used)
    def _():
        @pl.when(i == n_used)
        def _():
            wait_gather(slot)

        o_ref[...] = jnp.zeros_like(o_ref)


def moe_experts(tile_expert, n_used, inv, x, g, w_gu, w_down, tm=ROW_TILE, tn=512):
    n, d = x.shape
    p = inv.shape[0]
    ff = w_down.shape[1]
    tn = min(tn, ff)
    hbm = pl.BlockSpec(memory_space=pl.ANY)
    return pl.pallas_call(
        functools.partial(_experts_kernel, tm=tm, tn=tn),
        grid_spec=pltpu.PrefetchScalarGridSpec(
            num_scalar_prefetch=3, grid=(p // tm,),
            in_specs=[hbm, pl.BlockSpec((1, d), lambda i, te, nu, inv: (0, 0)), hbm, hbm],
            out_specs=pl.BlockSpec((tm, d), lambda i, te, nu, inv: (i, 0)),
            scratch_shapes=[pltpu.VMEM((2, tm, d), F32), pltpu.VMEM((tm, ff), BF16),
                            pltpu.VMEM((d, 2 * ff), BF16), pltpu.VMEM((ff, d), BF16),
                            pltpu.VMEM((4, d, tn), w_gu.dtype), pltpu.VMEM((2, tn, d), w_down.dtype),
                            pltpu.SemaphoreType.DMA((2,)), pltpu.SemaphoreType.DMA((4,)),
                            pltpu.SemaphoreType.DMA((2,))]),
        out_shape=jax.ShapeDtypeStruct((p, d), F32),
        compiler_params=_cparams(("arbitrary",)),
        name="moe_experts",
    )(tile_expert, n_used, inv, x, g.reshape(1, d), w_gu, w_down)


def _combine_kernel(pos_ref, w_ref, x_ref, y_hbm, gp_ref, wg_ref, p_ref, wp_ref, *refs, n, tm, final_norm):
    if final_norm:
        gf_ref, o_ref, buf_ref, sems = refs
    else:
        o_ref, buf_ref, sems = refs
    i = pl.program_id(0)
    last = pl.num_programs(0) - 1
    slot = lax.rem(i, 2)

    def row_copies(tile_base, t, into):
        return [_row_copy(y_hbm, pos_ref[k * n + tile_base + t], buf_ref.at[into, k], t, sems.at[into])
                for k in range(TOP_K)]

    def wait_rows(of):
        for k in range(TOP_K):
            pltpu.make_async_copy(y_hbm.at[pl.ds(0, tm)], buf_ref.at[of, k], sems.at[of]).wait()

    @pl.when(i == 0)
    def _():
        def start(t, c):
            for cp in row_copies(0, t, 0):
                cp.start()
            return c
        lax.fori_loop(0, tm, start, 0, unroll=8)

    next_base = jnp.minimum(i + 1, last) * tm
    pieces = 4

    def issue_piece(q):
        for t in range(q * tm // pieces, (q + 1) * tm // pieces):
            for cp in row_copies(next_base, t, 1 - slot):
                cp.start()

    issue_piece(0)
    diag = (lax.broadcasted_iota(jnp.int32, (tm, tm), 0) == lax.broadcasted_iota(jnp.int32, (tm, tm), 1))
    wcol = [jnp.sum(jnp.where(diag, w_ref[k:k + 1, :], 0.0), axis=1, keepdims=True) for k in range(TOP_K)]
    wait_rows(slot)
    issue_piece(1)
    x = x_ref[...] + sum(wcol[k] * buf_ref[slot, k] for k in range(TOP_K))
    issue_piece(2)
    y = _ple(x, gp_ref, wg_ref, p_ref, wp_ref)
    issue_piece(3)
    o_ref[...] = _rms(y, gf_ref[...]) if final_norm else y

    @pl.when(i == last)
    def _():
        wait_rows(1 - slot)


def moe_combine(pos_flat, w, x, y_sorted, g_ple, wg_ple, p, wp_ple, g_final=None, tm=ROW_TILE):
    n, d = x.shape
    dp = p.shape[1]
    const = lambda shape: pl.BlockSpec(shape, lambda i, pos: (0, 0), pipeline_mode=pl.Buffered(1))
    in_specs = [pl.BlockSpec((w.shape[0], tm), lambda i, pos: (0, i)),
                pl.BlockSpec((tm, d), lambda i, pos: (i, 0)),
                pl.BlockSpec(memory_space=pl.ANY),
                const((1, d)), const((d, d)), pl.BlockSpec((tm, dp), lambda i, pos: (i, 0)), const((dp, d))]
    args = [pos_flat, w, x, y_sorted, g_ple.reshape(1, d), wg_ple, p, wp_ple]
    if g_final is not None:
        in_specs.append(const((1, d)))
        args.append(g_final.reshape(1, d))
    return pl.pallas_call(
        functools.partial(_combine_kernel, n=n, tm=tm, final_norm=g_final is not None),
        grid_spec=pltpu.PrefetchScalarGridSpec(
            num_scalar_prefetch=1, grid=(n // tm,),
            in_specs=in_specs,
            out_specs=pl.BlockSpec((tm, d), lambda i, pos: (i, 0)),
            scratch_shapes=[pltpu.VMEM((2, TOP_K, tm, d), F32), pltpu.SemaphoreType.DMA((2,))]),
        out_shape=jax.ShapeDtypeStruct((n, d), F32),
        compiler_params=_cparams(("arbitrary",)),
        name="moe_combine",
    )(*args)


def _compress_kernel(h_ref, pos_ref, w1_ref, w2_ref, o_ref):
    half = h_ref.shape[1]
    h = h_ref[...].astype(BF16)
    p_lo = jnp.dot(h, w1_ref[0:half, :], preferred_element_type=F32)
    p_hi = jnp.dot(h, w1_ref[half:, :], preferred_element_type=F32)
    nh = p_hi.shape[0]
    posb = jnp.dot(pos_ref[...].astype(BF16), w1_ref[...], preferred_element_type=F32)
    hid = p_lo + pltpu.roll(p_hi, nh - 1, 0) + posb
    o_ref[...] = jnp.dot(jax.nn.gelu(hid).astype(BF16), w2_ref[...],
                         preferred_element_type=F32).astype(o_ref.dtype)


def nsa_compress(halves, pos, w1, w2):
    bg, nh, hw = halves.shape
    hid = w1.shape[1]
    dh = w2.shape[1]
    return pl.pallas_call(
        _compress_kernel,
        grid=(bg,),
        in_specs=[pl.BlockSpec((None, nh, hw), lambda i: (i, 0, 0)),
                  pl.BlockSpec((1, 2 * hw), lambda i: (0, 0)),
                  pl.BlockSpec((2 * hw, hid), lambda i: (0, 0)),
                  pl.BlockSpec((hid, dh), lambda i: (0, 0))],
        out_specs=pl.BlockSpec((None, nh, dh), lambda i: (i, 0, 0)),
        out_shape=jax.ShapeDtypeStruct((bg, nh, dh), BF16),
        compiler_params=_cparams(("parallel",)),
    )(halves, pos.reshape(1, 2 * hw), w1, w2)


def _rel_bucket_np(dist):
    n = np.maximum(dist, 0)
    max_exact = N_BUCKETS // 2
    nf = np.maximum(n, 1).astype(np.float32)
    large = max_exact + (np.log(nf / max_exact) / np.float32(math.log(MAX_DISTANCE / max_exact))
                         * (N_BUCKETS - max_exact)).astype(np.int32)
    return np.where(n < max_exact, n, np.minimum(large, N_BUCKETS - 1)).astype(np.int32)


def _nsa_kernel(q_ref, gt_ref, kc_ref, vct_ref, bct_ref, ks_ref, kw_ref, vst_ref, vwt_ref, tzw_ref, tzc_ref,
                ovlt_ref, expt_ref, o_ref, *, n_cmp, n_sel):
    tq = q_ref.shape[0]
    dh = HEAD_DIM
    r = q_ref.shape[1] // dh
    near = tzw_ref.shape[0]
    n_slc = ovlt_ref.shape[0]
    ncp = kc_ref.shape[0]
    g = pl.program_id(1)
    t0 = pl.multiple_of(pl.program_id(2) * tq, tq)
    near_keys = pl.ds(t0, near)
    heads = lambda blocks: jnp.concatenate(blocks, axis=1)
    per_head = lambda x, h: x[:, h * tq:(h + 1) * tq]

    qt = q_ref[...].astype(F32).T.astype(BF16)
    zero = jnp.zeros((dh, r * tq), BF16)
    qh = heads([qt[h * dh:(h + 1) * dh, :] for h in range(r)])
    qz = jnp.where(g == 0, jnp.concatenate([qh, zero], axis=0), jnp.concatenate([zero, qh], axis=0))
    gate_t = gt_ref[...].T

    def gate(branch):
        rows = [gate_t[branch * N_HEADS + gi * r:branch * N_HEADS + (gi + 1) * r, :] for gi in range(N_KV_GROUPS)]
        mine = jnp.where(g == 0, rows[0], rows[1])
        return heads([mine[h:h + 1, :] for h in range(r)])

    cblk = lax.broadcasted_iota(jnp.int32, (ncp, r * tq), 0)
    qpos = t0 + (lax.broadcasted_iota(jnp.int32, (ncp, r * tq), 1) & (tq - 1))
    mask_c = (qpos - (cblk * CMP_STRIDE + CMP_BLOCK - 1) >= 0) & (cblk < n_cmp)
    s = jnp.dot(kc_ref[...], qz, preferred_element_type=F32) + bct_ref[...]
    l = jnp.where(mask_c, s, NEG)
    e = jnp.where(mask_c, jnp.exp2(l - jnp.max(l, axis=0, keepdims=True)), 0.0)
    p = e / jnp.maximum(jnp.sum(e, axis=0, keepdims=True), 1e-30)
    out = gate(0) * jnp.dot(vct_ref[...], p.astype(BF16), preferred_element_type=F32)
    psum = functools.reduce(lambda acc, h: acc + per_head(p, h), range(1, r), per_head(p, 0))

    def pv_and_sum(vt_ref, keys, e):
        ones = jnp.ones((PACKED_ROWS_BF16, keys.size), BF16)
        return jnp.dot(jnp.concatenate([vt_ref[:, keys], ones], axis=0), e.astype(BF16),
                       preferred_element_type=F32)

    key_exists = t0 + lax.broadcasted_iota(jnp.int32, (near, r * tq), 0) >= WINDOW
    s = jnp.dot(kw_ref[near_keys, :], qz, preferred_element_type=F32)
    l = jnp.where(key_exists, s + tzw_ref[...], NEG)
    e = jnp.exp2(l - jnp.max(l, axis=0, keepdims=True))
    o_w = pv_and_sum(vwt_ref, near_keys, e)
    out = out + gate(2) * (o_w[:dh] / o_w[dh:dh + 1])

    imp = jnp.dot(ovlt_ref[...], psum, preferred_element_type=F32, precision=lax.Precision.HIGHEST)
    jblk = lax.broadcasted_iota(jnp.int32, (n_slc, tq), 0).astype(F32)
    qblk = ((t0 + lax.broadcasted_iota(jnp.int32, (n_slc, tq), 1)) // SLC_BLOCK).astype(F32)
    forced = (jblk == 0) | (jblk == qblk) | (jblk == qblk - 1)
    score = jnp.where(forced, 1e9, jnp.where(jblk <= qblk, imp, -1e9))
    sel = jnp.zeros((n_slc, tq), F32)
    for _ in range(n_sel):
        m = jnp.max(score, axis=0, keepdims=True)
        first = jnp.min(jnp.where(score == m, jblk, float(n_slc)), axis=0, keepdims=True)
        pick = jblk == first
        sel = jnp.where(pick, 1.0, sel)
        score = jnp.where(pick, -jnp.inf, score)
    sel16 = sel.astype(BF16)
    all_heads = lambda x: heads([x] * r)

    def member_mask(keys, also=True):
        member = jnp.dot(expt_ref[keys, :], sel16, preferred_element_type=F32) > 0.5
        return all_heads(jnp.where(member & also, 0.0, NEG))

    l = jnp.dot(ks_ref[near_keys, :], qz, preferred_element_type=F32) + tzc_ref[...] + member_mask(near_keys)
    m = jnp.max(l, axis=0, keepdims=True)
    acc = pv_and_sum(vst_ref, near_keys, jnp.exp2(l - m))
    far_bias = tzc_ref[0:1, :]

    def far_chunk(f, ma):
        m, acc = ma
        first = pl.multiple_of(WINDOW + f * FAR_CHUNK, FAR_CHUNK)
        keys = pl.ds(first, FAR_CHUNK)
        before_near = first + lax.broadcasted_iota(jnp.int32, (FAR_CHUNK, tq), 0) < t0
        l = jnp.dot(ks_ref[keys, :], qz, preferred_element_type=F32) + far_bias + member_mask(keys, before_near)
        m_new = jnp.maximum(m, jnp.max(l, axis=0, keepdims=True))
        return m_new, jnp.exp2(m - m_new) * acc + pv_and_sum(vst_ref, keys, jnp.exp2(l - m_new))

    n_far = (jnp.maximum(t0, WINDOW) - 1) // FAR_CHUNK
    m, acc = lax.fori_loop(0, n_far, far_chunk, (m, acc))
    out = out + gate(1) * (acc[:dh] / acc[dh:dh + 1])

    o_ref[...] = jnp.concatenate([per_head(out, h) for h in range(r)], axis=0).T.astype(o_ref.dtype)


def nsa_attention_core(q, gates_pad, kc2, vct, bct, kv_pad, kvt_pad, tzw, tzc, ovlt, expt, batch, seq, n_cmp):
    n, hd = q.shape
    gq, dh, tq = N_KV_GROUPS, HEAD_DIM, Q_BLOCK
    r = N_HEADS // gq
    nqt = seq // tq
    spad = kv_pad.shape[1]
    ncp = kc2.shape[1]
    n_slc = seq // SLC_BLOCK
    near = tzw.shape[1]
    row_tile = lambda b, g, t: (b * nqt + t, g)
    key_cols = lambda branch: pl.BlockSpec((None, spad, gq * dh), lambda b, g, t: (b, 0, branch))
    val_rows = lambda branch: pl.BlockSpec((dh, spad), lambda b, g, t: (branch * gq + g, b))
    table = pl.BlockSpec((None, near, r * tq), lambda b, g, t: (g, 0, 0))
    const = lambda a: pl.BlockSpec(a.shape, lambda b, g, t: (0,) * a.ndim)
    return pl.pallas_call(
        functools.partial(_nsa_kernel, n_cmp=n_cmp, n_sel=min(N_SELECT, n_slc)),
        grid=(batch, gq, nqt),
        in_specs=[pl.BlockSpec((tq, r * dh), row_tile),
                  pl.BlockSpec((tq, gates_pad.shape[1]), lambda b, g, t: (b * nqt + t, 0)),
                  pl.BlockSpec((None, ncp, gq * dh), lambda b, g, t: (b, 0, 0)),
                  pl.BlockSpec((None, None, dh, ncp), lambda b, g, t: (b, g, 0, 0)),
                  pl.BlockSpec((None, ncp, r * tq), lambda b, g, t: (g, t, 0)),
                  key_cols(2), key_cols(4), val_rows(3), val_rows(5), table, table,
                  const(ovlt), const(expt)],
        out_specs=pl.BlockSpec((tq, r * dh), row_tile),
        out_shape=jax.ShapeDtypeStruct((n, hd), BF16),
        compiler_params=_cparams(("parallel", "parallel", "parallel")),
        name="nsa_attention",
    )(q, gates_pad, kc2, vct, bct, kv_pad, kv_pad, kvt_pad, kvt_pad, tzw, tzc, ovlt, expt)


def _bias_table_kernel(rb_ref, bkt_win_ref, bkt_causal_ref, bkt_cmp_ref, tzw_ref, tzc_ref, bct_ref, *, rows):
    h = pl.program_id(0)

    def lookup(bkt):
        acc = jnp.full(bkt.shape, NEG, F32)
        for k in range(N_BUCKETS):
            acc = jnp.where(bkt == k, rb_ref[k * N_HEADS + h] * LOG2_E, acc)
        return acc

    for src, dst in ((bkt_win_ref, tzw_ref), (bkt_causal_ref, tzc_ref), (bkt_cmp_ref, bct_ref)):
        def body(c, carry, src=src, dst=dst):
            sl = pl.ds(pl.multiple_of(c * rows, rows), rows)
            dst[sl, :] = lookup(src[sl, :])
            return carry
        lax.fori_loop(0, src.shape[0] // rows, body, 0)


def _nsa_tables(rel_bias, seq, n_cmp, ncp):
    tq = Q_BLOCK
    r = N_HEADS // N_KV_GROUPS
    dist_near = np.arange(tq)[None, :] + WINDOW - np.arange(NEAR)[:, None]
    bkt_near = _rel_bucket_np(dist_near)
    bkt_win = np.where((dist_near >= 0) & (dist_near < WINDOW), bkt_near, N_BUCKETS).astype(np.int32)
    bkt_causal = np.where(dist_near >= 0, bkt_near, N_BUCKETS).astype(np.int32)
    dist_c = np.arange(seq)[None, :] - (np.arange(ncp)[:, None] * CMP_STRIDE + CMP_BLOCK - 1)
    bkt_c = _rel_bucket_np(dist_c).reshape(ncp, seq // tq, tq).transpose(1, 0, 2).reshape(-1, tq)
    maps = [jnp.asarray(m) for m in (bkt_win, bkt_causal, bkt_c)]
    whole = lambda m: pl.BlockSpec(m.shape, lambda h, rb: (0, 0))
    per_head = lambda m: pl.BlockSpec((None, m.shape[0], tq), lambda h, rb: (h // r, 0, h % r))
    tzw, tzc, bct = pl.pallas_call(
        functools.partial(_bias_table_kernel, rows=64),
        grid_spec=pltpu.PrefetchScalarGridSpec(
            num_scalar_prefetch=1, grid=(N_HEADS,),
            in_specs=[whole(m) for m in maps], out_specs=[per_head(m) for m in maps]),
        out_shape=[jax.ShapeDtypeStruct((N_KV_GROUPS, m.shape[0], r * tq), F32) for m in maps],
        compiler_params=_cparams(("parallel",)),
        name="nsa_bias_tables",
    )(rel_bias.reshape(N_BUCKETS * N_HEADS), *maps)
    n_slc = seq // SLC_BLOCK
    cs = np.arange(ncp)[None, :] * CMP_STRIDE
    ss = np.arange(n_slc)[:, None] * SLC_BLOCK
    overlap_t = (cs < ss + SLC_BLOCK) & (cs + CMP_BLOCK > ss) & (np.arange(ncp)[None, :] < n_cmp)
    kpos = np.arange(WINDOW + seq)[:, None] - WINDOW
    expand_t = (kpos // SLC_BLOCK == np.arange(n_slc)[None, :]) & (kpos >= 0)
    return tzw, tzc, bct, jnp.asarray(overlap_t, F32), jnp.asarray(expand_t, BF16)


def _kv_kernel(x_ref, g_ref, w_ref, wt_ref, kv_ref, kvt_ref):
    @pl.when(pl.program_id(1) == 0)
    def _():
        kv_ref[...] = jnp.zeros_like(kv_ref)
        kvt_ref[...] = jnp.zeros_like(kvt_ref)

    @pl.when(pl.program_id(1) > 0)
    def _():
        xn = _rms(x_ref[...], g_ref[...]).astype(BF16)
        kv_ref[...] = jnp.dot(xn, w_ref[...], preferred_element_type=F32).astype(kv_ref.dtype)
        kvt_ref[...] = lax.dot_general(wt_ref[...], xn, (((1,), (1,)), ((), ())),
                                       preferred_element_type=F32).astype(kvt_ref.dtype)


def kv_projection(x, g, w_kv, batch, seq):
    n, d = x.shape
    f = w_kv.shape[1]
    tm = WINDOW
    nt = seq // tm
    rows = batch * (nt + 1) * tm
    w16 = w_kv.astype(BF16)
    const = lambda shape: pl.BlockSpec(shape, lambda b, t: (0, 0))
    return pl.pallas_call(
        _kv_kernel,
        grid=(batch, nt + 1),
        in_specs=[pl.BlockSpec((tm, d), lambda b, t: (b * nt + jnp.maximum(t - 1, 0), 0)), const((1, d)),
                  const((d, f)), const((f, d))],
        out_specs=[pl.BlockSpec((tm, f), lambda b, t: (b * (nt + 1) + t, 0)),
                   pl.BlockSpec((f, tm), lambda b, t: (0, b * (nt + 1) + t))],
        out_shape=[jax.ShapeDtypeStruct((rows, f), BF16), jax.ShapeDtypeStruct((f, rows), BF16)],
        compiler_params=_cparams(("parallel", "arbitrary")),
        name="kv_projection",
    )(x, g.reshape(1, d), w16, w16.T)


def _nsa_shared_kv(x, batch, seq, kv_norm, w_kv, cmp_pos_k, cmp_pos_v, k_w1, k_w2, v_w1, v_w2):
    n, d = x.shape
    gq, dh = N_KV_GROUPS, HEAD_DIM
    f = w_kv.shape[1]
    kv_pad, kvt_pad = kv_projection(x, kv_norm, w_kv, batch, seq)
    kv_pad = kv_pad.reshape(batch, WINDOW + seq, f)
    kv5 = kv_pad[:, WINDOW:].reshape(batch, seq, N_KV_BRANCH, gq, dh)
    per_group = lambda j: jnp.transpose(kv5[:, :, j], (0, 2, 1, 3))
    n_cmp = (seq - CMP_BLOCK) // CMP_STRIDE + 1
    nhalf = seq // CMP_STRIDE

    def compress(t, pos, w1, w2):
        halves = t.reshape(batch * gq, nhalf, CMP_STRIDE * dh)
        return nsa_compress(halves, pos, w1.astype(BF16), w2.astype(BF16)).reshape(batch, gq, nhalf, dh)

    k_cmp = compress(per_group(0), cmp_pos_k, k_w1, k_w2)
    v_cmp = compress(per_group(1), cmp_pos_v, v_w1, v_w2)
    kc2 = jnp.transpose(k_cmp, (0, 2, 1, 3)).reshape(batch, nhalf, gq * dh)
    vct = jnp.transpose(v_cmp, (0, 1, 3, 2))
    return kc2, vct, kv_pad, kvt_pad, n_cmp


def _q_kernel(x_ref, g_ref, wq_ref, wg_ref, q_ref, gates_ref):
    xn = _rms(x_ref[...], g_ref[...]).astype(BF16)
    q = jnp.dot(xn, wq_ref[...], preferred_element_type=F32) * (HEAD_DIM ** -0.5 * LOG2_E)
    q_ref[...] = q.astype(q_ref.dtype)
    gates_ref[...] = jax.nn.sigmoid(jnp.dot(xn, wg_ref[...], preferred_element_type=F32))


def q_projection(x, g, w_q, tm=ROW_TILE):
    n, d = x.shape
    hd = N_HEADS * HEAD_DIM
    w16 = w_q.astype(BF16)
    w_gate = jnp.pad(w16[:, hd:], ((0, 0), (0, 128 - (w_q.shape[1] - hd))))
    return pl.pallas_call(
        _q_kernel,
        grid=(n // tm,),
        in_specs=[pl.BlockSpec((tm, d), lambda i: (i, 0)), pl.BlockSpec((1, d), lambda i: (0, 0)),
                  pl.BlockSpec((d, hd), lambda i: (0, 0)), pl.BlockSpec((d, 128), lambda i: (0, 0))],
        out_specs=[pl.BlockSpec((tm, hd), lambda i: (i, 0)), pl.BlockSpec((tm, 128), lambda i: (i, 0))],
        out_shape=[jax.ShapeDtypeStruct((n, hd), BF16), jax.ShapeDtypeStruct((n, 128), F32)],
        compiler_params=_cparams(("parallel",)),
        name="q_projection",
    )(x, g.reshape(1, d), w16[:, :hd], w_gate)


def _nsa_mix(x, batch, seq, g_mix, w_q, shared, tables):
    kc2, vct, kv_pad, kvt_pad, n_cmp = shared
    tzw, tzc, bct, ovlt, expt = tables
    q, gates_pad = q_projection(x, g_mix, w_q)
    return nsa_attention_core(q, gates_pad, kc2, vct, bct, kv_pad, kvt_pad, tzw, tzc, ovlt, expt,
                              batch, seq, n_cmp)


def _dense_tail_kernel(a_ref, wo_ref, x_ref, g_ref, wgu_ref, wd_ref, gp_ref, wg_ref, p_ref, wp_ref, o_ref,
                       act_ref, *, tn):
    x = x_ref[...] + jnp.dot(a_ref[...], wo_ref[...], preferred_element_type=F32)
    xn = _rms(x, g_ref[...]).astype(BF16)
    ff = wd_ref.shape[0]
    for c in range(ff // tn):
        gate = jnp.dot(xn, wgu_ref[:, c * tn:(c + 1) * tn], preferred_element_type=F32)
        up = jnp.dot(xn, wgu_ref[:, ff + c * tn:ff + (c + 1) * tn], preferred_element_type=F32)
        act_ref[:, c * tn:(c + 1) * tn] = (_silu(gate) * up).astype(BF16)
    x = x + jnp.dot(act_ref[...], wd_ref[...], preferred_element_type=F32)
    o_ref[...] = _ple(x, gp_ref, wg_ref, p_ref, wp_ref)


def dense_layer_tail(a, w_o, x, g_ffn, w_gu, w_down, g_ple, wg_ple, p, wp_ple, tm=ROW_TILE, tn=512):
    n, d = x.shape
    k = a.shape[1]
    ff = w_down.shape[0]
    dp = p.shape[1]
    whole = lambda shape: pl.BlockSpec(shape, lambda i: (0, 0), pipeline_mode=pl.Buffered(1))
    rows = lambda width: pl.BlockSpec((tm, width), lambda i: (i, 0))
    return pl.pallas_call(
        functools.partial(_dense_tail_kernel, tn=min(tn, ff)),
        grid=(n // tm,),
        in_specs=[rows(k), whole((k, d)), rows(d), whole((1, d)), whole((d, 2 * ff)), whole((ff, d)),
                  whole((1, d)), whole((d, d)), rows(dp), whole((dp, d))],
        out_specs=rows(d),
        out_shape=jax.ShapeDtypeStruct((n, d), F32),
        scratch_shapes=[pltpu.VMEM((tm, ff), BF16)],
        compiler_params=_cparams(("parallel",)),
        name="dense_layer_tail",
    )(a, w_o, x, g_ffn.reshape(1, d), w_gu, w_down, g_ple.reshape(1, d), wg_ple, p, wp_ple)


def _moe_layer_tail(mix, w_o, x, g, w_router, w_gu_all, w_down_all, layer, ple, tm=ROW_TILE):
    n, d = x.shape
    ne = w_router.shape[1]
    x, sel, w, counts = moe_route(mix, w_o, x, g, w_router)
    padded = (counts[:, 0] + tm - 1) // tm * tm
    ends = jnp.cumsum(padded)
    n_tiles = (TOP_K * n) // tm + ne
    n_used = (ends[-1] // tm).astype(jnp.int32).reshape(1)
    tile_first_row = jnp.arange(n_tiles, dtype=jnp.int32) * tm
    tile_expert = jnp.sum(ends[None, :] <= tile_first_row[:, None], axis=1)
    tile_expert = jnp.minimum(tile_expert, tile_expert[jnp.maximum(n_used[0] - 1, 0)]).astype(jnp.int32)
    starts = (ends - padded).astype(jnp.int32)
    pos = moe_slots(starts, sel)
    pos_flat = pos[:TOP_K].reshape(TOP_K * n)
    total = jnp.full((1,), n_tiles * tm, jnp.int32)
    unrouted = jnp.concatenate([starts + counts[:, 0], ends[-1:], ends, total]).astype(jnp.int32)
    inv = moe_inverse(pos_flat, unrouted, n, n_tiles * tm)
    flat = lambda w_all: w_all.reshape((-1,) + w_all.shape[2:])
    y_sorted = moe_experts(tile_expert + layer * ne, n_used, inv, x, g, flat(w_gu_all), flat(w_down_all))
    return moe_combine(pos_flat, w, x, y_sorted, *ple)


def kernel(x, p, g_mix, g_ffn, g_ple, g_final, rg_w_in, rg_conv_w, rg_conv_b, rg_w_a, rg_b_a, rg_w_x,
           rg_b_x, rg_lambda, rg_w_out, kv_norm, w_kv, cmp_pos_k, cmp_pos_v, cmp_k_w1, cmp_k_w2,
           cmp_v_w1, cmp_v_w2, rel_bias, nsa_w_q, nsa_w_o, ffn_w_gu, ffn_w_down, moe_w_router,
           moe_w_gu, moe_w_down, ple_w_proj, ple_w_gate):
    batch, seq, d = x.shape
    depth = p.shape[0]
    n_a = rg_w_in.shape[0]
    n = batch * seq
    x = x.reshape(n, d)
    p = p.reshape(depth, n, p.shape[-1])
    n_cmp = (seq - CMP_BLOCK) // CMP_STRIDE + 1
    tables = _nsa_tables(rel_bias, seq, n_cmp, seq // CMP_STRIDE)
    shared = None
    for i in range(depth):
        if i < n_a:
            mix = rg_lru_core(x, g_mix[i], rg_w_in[i].astype(BF16), batch, seq, rg_conv_w[i], rg_conv_b[i],
                              _block_diag_chunks(rg_w_a[i]), rg_b_a[i], _block_diag_chunks(rg_w_x[i]),
                              rg_b_x[i], rg_lambda[i])
            w_o = rg_w_out[i].astype(BF16)
        else:
            mix = _nsa_mix(x, batch, seq, g_mix[i], nsa_w_q[i - n_a], shared, tables)
            w_o = nsa_w_o[i - n_a].astype(BF16)
        ple = (g_ple[i], ple_w_gate[i].astype(BF16), p[i], ple_w_proj[i].astype(BF16))
        if i % 2 == 0:
            assert i < depth - 1
            x = dense_layer_tail(mix, w_o, x, g_ffn[i], ffn_w_gu[i // 2].astype(BF16),
                                 ffn_w_down[i // 2].astype(BF16), *ple)
        else:
            x = _moe_layer_tail(mix, w_o, x, g_ffn[i], moe_w_router[i // 2], moe_w_gu, moe_w_down, i // 2,
                                ple + ((g_final,) if i == depth - 1 else ()))
        if i == n_a - 1:
            shared = _nsa_shared_kv(x, batch, seq, kv_norm, w_kv, cmp_pos_k, cmp_pos_v,
                                    cmp_k_w1, cmp_k_w2, cmp_v_w1, cmp_v_w2)
    return x.reshape(batch, seq, d)
```

```python
import functools
import math

import numpy as np
import jax
import jax.numpy as jnp
from jax import lax
from jax.experimental import pallas as pl
from jax.experimental.pallas import tpu as pltpu

F32 = jnp.float32
BF16 = jnp.bfloat16

EPS = 1e-6
CONV_WIDTH = 4
LRU_BLOCKS = 16
LRU_C = 8.0
N_HEADS = 16
N_KV_GROUPS = 2
GROUP = N_HEADS // N_KV_GROUPS
HEAD_DIM = 64
N_KV_BRANCH = 6
CMP_BLOCK = 32
CMP_STRIDE = 16
SLC_BLOCK = 64
N_SELECT = 8
WINDOW = 512
Q_BLOCK = 128
N_BUCKETS = 32
MAX_DISTANCE = 128
TOP_K = 2
NEAR = WINDOW + Q_BLOCK
FAR_CHUNK = 512
NEG = -1e30
LOG2_E = math.log2(math.e)
PACKED_ROWS_BF16 = 16

VMEM_LIMIT_V7X = 56 * 1024 * 1024
ROW_TILE = 512


def _cparams(sem):
    return pltpu.CompilerParams(dimension_semantics=sem, vmem_limit_bytes=VMEM_LIMIT_V7X)


def _rms(x, g):
    return x * lax.rsqrt(jnp.mean(x * x, axis=-1, keepdims=True) + EPS) * g


def _silu(x):
    return x * jax.nn.sigmoid(x)


BD = 256
SCAN_ROWS = 8


def _rg_kernel(x_ref, g_ref, win_ref, cw_ref, cb_ref, wa_ref, ba_ref, wx_ref, bx_ref, lam_ref,
               o_ref, xe_ref, h_ref, a_ref, u_ref, xn_ref):
    t, c = o_ref.shape

    @pl.when(pl.program_id(1) == 0)
    def _():
        xe_ref[0:8, :] = jnp.zeros((8, c), F32)
        h_ref[...] = jnp.zeros_like(h_ref)

    xn_ref[...] = _rms(x_ref[...], g_ref[...]).astype(BF16)
    xe_ref[8:, :] = jnp.dot(xn_ref[...], win_ref[:, c:], preferred_element_type=F32)
    xe = xe_ref[...]
    xc = cb_ref[...] + sum(
        cw_ref[CONV_WIDTH - 1 - d:CONV_WIDTH - d, :] * (pltpu.roll(xe, d, 0) if d else xe)[8:, :]
        for d in range(CONV_WIDTH))
    xe_ref[0:8, :] = xe_ref[t:t + 8, :]

    lam = -lam_ref[...]
    softplus_neg_lam = jnp.maximum(lam, 0.0) + jnp.log1p(jnp.exp(-jnp.abs(lam)))
    for cblk in range(c // BD):
        sl = slice(cblk * BD, (cblk + 1) * BD)
        xb = xc[:, sl]
        xb16 = xb.astype(BF16)
        r = jax.nn.sigmoid(jnp.dot(xb16, wa_ref[cblk], preferred_element_type=F32) + ba_ref[:, sl])
        i = jax.nn.sigmoid(jnp.dot(xb16, wx_ref[cblk], preferred_element_type=F32) + bx_ref[:, sl])
        log_a = -LRU_C * r * softplus_neg_lam[:, sl]
        a = jnp.exp(log_a)
        a_ref[:, sl] = a
        u_ref[:, sl] = jnp.sqrt(1.0 - a * a) * (i * xb)

    row = lax.broadcasted_iota(jnp.int32, (SCAN_ROWS, c), 0)

    def scan_tile(k, h_prev):
        rows = pl.ds(pl.multiple_of(k * SCAN_ROWS, SCAN_ROWS), SCAN_ROWS)
        a = a_ref[rows, :]
        u = u_ref[rows, :]
        for d in (1, 2, 4):
            a_s = jnp.where(row >= d, pltpu.roll(a, d, 0), 1.0)
            u_s = jnp.where(row >= d, pltpu.roll(u, d, 0), 0.0)
            u = a * u_s + u
            a = a * a_s
        h = a * h_prev + u
        u_ref[rows, :] = h
        return jnp.broadcast_to(h[SCAN_ROWS - 1:SCAN_ROWS, :], (SCAN_ROWS, c))

    h_ref[...] = lax.fori_loop(0, t // SCAN_ROWS, scan_tile, h_ref[...])
    gate = jnp.dot(xn_ref[...], win_ref[:, :c], preferred_element_type=F32)
    o_ref[...] = (jax.nn.gelu(gate) * u_ref[...]).astype(o_ref.dtype)


def rg_lru_core(x, g, w_in, batch, seq, conv_w, conv_b, wa_bd, b_a, wx_bd, b_x, lam, t_chunk=512):
    n, d = x.shape
    c = w_in.shape[1] // 2
    nt = seq // t_chunk
    vec = lambda v: v.reshape(1, -1)
    rows = lambda width: pl.BlockSpec((t_chunk, width), lambda b, t: (b * nt + t, 0))
    const = lambda shape: pl.BlockSpec(shape, lambda b, t: (0,) * len(shape))
    return pl.pallas_call(
        _rg_kernel,
        grid=(batch, nt),
        in_specs=[rows(d), const((1, d)), const(w_in.shape), const((CONV_WIDTH, c)), const((1, c)),
                  const(wa_bd.shape), const((1, c)), const(wx_bd.shape), const((1, c)), const((1, c))],
        out_specs=rows(c),
        out_shape=jax.ShapeDtypeStruct((n, c), BF16),
        scratch_shapes=[pltpu.VMEM((t_chunk + 8, c), F32), pltpu.VMEM((SCAN_ROWS, c), F32),
                        pltpu.VMEM((t_chunk, c), F32), pltpu.VMEM((t_chunk, c), F32),
                        pltpu.VMEM((t_chunk, d), BF16)],
        compiler_params=_cparams(("parallel", "arbitrary")),
        name="rg_lru_core",
    )(x, vec(g), w_in, conv_w, vec(conv_b), wa_bd, vec(b_a), wx_bd, vec(b_x), vec(lam))


def _block_diag_chunks(w):
    nb, bw, _ = w.shape
    per = BD // bw
    w = w.reshape(nb // per, per, bw, bw)
    eye = jnp.eye(per, dtype=w.dtype)
    out = jnp.einsum('cpij,pq->cpiqj', w, eye).reshape(nb // per, BD, BD)
    return out.astype(BF16)


def _ple(x, g_ref, wg_ref, p_ref, wp_ref):
    xn = _rms(x, g_ref[...]).astype(BF16)
    gate = jax.nn.sigmoid(jnp.dot(xn, wg_ref[...], preferred_element_type=F32))
    proj = jnp.dot(p_ref[...].astype(BF16), wp_ref[...], preferred_element_type=F32)
    return x + gate * proj


SEL_ROWS = 8


def _route_kernel(a_ref, wo_ref, x_ref, g_ref, wt_ref, xo_ref, sel_ref, w_ref, cnt_ref, carry_ref):
    @pl.when(pl.program_id(0) == 0)
    def _():
        carry_ref[...] = jnp.zeros_like(carry_ref)

    x = x_ref[...] + jnp.dot(a_ref[...], wo_ref[...], preferred_element_type=F32)
    xo_ref[...] = x
    xn = _rms(x, g_ref[...])
    logits = lax.dot_general(wt_ref[...], xn, (((1,), (1,)), ((), ())), preferred_element_type=F32,
                             precision=lax.Precision.HIGHEST)
    ne, tm = logits.shape
    row = lax.broadcasted_iota(jnp.int32, (ne, tm), 0).astype(F32)
    m1 = jnp.max(logits, axis=0, keepdims=True)
    i1 = jnp.min(jnp.where(logits == m1, row, float(ne)), axis=0, keepdims=True)
    rest = jnp.where(row == i1, -jnp.inf, logits)
    m2 = jnp.max(rest, axis=0, keepdims=True)
    i2 = jnp.min(jnp.where(rest == m2, row, float(ne)), axis=0, keepdims=True)
    e2 = jnp.exp(m2 - m1)
    denom = 1.0 + e2
    ind = jnp.where((row == i1) | (row == i2), 1.0, 0.0)
    earlier = (lax.broadcasted_iota(jnp.int32, (tm, tm), 0) < lax.broadcasted_iota(jnp.int32, (tm, tm), 1))
    rank = jnp.dot(ind.astype(BF16), jnp.where(earlier, 1.0, 0.0).astype(BF16),
                   preferred_element_type=F32) + carry_ref[:, 0:1]
    r1 = jnp.sum(jnp.where(row == i1, rank, 0.0), axis=0, keepdims=True)
    r2 = jnp.sum(jnp.where(row == i2, rank, 0.0), axis=0, keepdims=True)
    carry_ref[...] = carry_ref[...] + jnp.sum(ind, axis=1, keepdims=True)
    pick = lambda vals: functools.reduce(
        lambda acc, kv: jnp.where(row == float(kv[0]), kv[1], acc), enumerate(vals), jnp.zeros((ne, tm), F32))
    sel_ref[...] = pick([i1, i2, r1, r2]).astype(jnp.int32)
    w_ref[...] = pick([1.0 / denom, e2 / denom])
    cnt_ref[...] = carry_ref[...].astype(jnp.int32)


def moe_route(a, w_o, x, g, w_router, tm=ROW_TILE):
    n, d = x.shape
    k = a.shape[1]
    ne = w_router.shape[1]
    assert ne == SEL_ROWS
    const = lambda shape: pl.BlockSpec(shape, lambda i: (0, 0))
    return pl.pallas_call(
        _route_kernel,
        grid=(n // tm,),
        in_specs=[pl.BlockSpec((tm, k), lambda i: (i, 0)), const((k, d)), pl.BlockSpec((tm, d), lambda i: (i, 0)),
                  const((1, d)), const((ne, d))],
        out_specs=[pl.BlockSpec((tm, d), lambda i: (i, 0)),
                   pl.BlockSpec((ne, tm), lambda i: (0, i)), pl.BlockSpec((ne, tm), lambda i: (0, i)),
                   pl.BlockSpec((ne, 128), lambda i: (0, 0))],
        out_shape=[jax.ShapeDtypeStruct((n, d), F32),
                   jax.ShapeDtypeStruct((ne, n), jnp.int32), jax.ShapeDtypeStruct((ne, n), F32),
                   jax.ShapeDtypeStruct((ne, 128), jnp.int32)],
        scratch_shapes=[pltpu.VMEM((ne, 128), F32)],
        compiler_params=_cparams(("arbitrary",)),
        name="moe_route",
    )(a, w_o, x, g.reshape(1, d), w_router.T)


def _slot_kernel(starts_ref, sel_ref, pos_ref):
    sel = sel_ref[...]
    start_of = lambda e_row: functools.reduce(
        lambda acc, e: jnp.where(e_row == e, starts_ref[e], acc), range(starts_ref.shape[0]),
        jnp.zeros_like(e_row))
    row = lax.broadcasted_iota(jnp.int32, sel.shape, 0)
    p1 = start_of(sel[0:1]) + sel[2:3]
    p2 = start_of(sel[1:2]) + sel[3:4]
    pos_ref[...] = jnp.where(row == 0, p1, jnp.where(row == 1, p2, 0))


def moe_slots(starts, sel, tm=2048):
    ne, n = sel.shape
    return pl.pallas_call(
        _slot_kernel,
        grid_spec=pltpu.PrefetchScalarGridSpec(
            num_scalar_prefetch=1, grid=(n // tm,),
            in_specs=[pl.BlockSpec((ne, tm), lambda i, s: (0, i))],
            out_specs=pl.BlockSpec((ne, tm), lambda i, s: (0, i))),
        out_shape=jax.ShapeDtypeStruct((ne, n), jnp.int32),
        compiler_params=_cparams(("parallel",)),
        name="moe_slots",
    )(starts, sel)


def _row_copy(src, src_row, dst, dst_row, sem):
    return pltpu.make_async_copy(src.at[pl.ds(src_row, 1)], dst.at[pl.ds(dst_row, 1)], sem)


def _inverse_kernel(pos_ref, pad_ref, inv_ref, *, n):
    def clear(s, c):
        inv_ref[s] = 0
        return c

    def fill(t, c):
        for k in range(TOP_K):
            inv_ref[pos_ref[k * n + t]] = t
        return c

    n_ranges = pad_ref.shape[0] // 2
    for e in range(n_ranges):
        lax.fori_loop(pad_ref[e], pad_ref[n_ranges + e], clear, 0)
    lax.fori_loop(0, n, fill, 0, unroll=8)


def moe_inverse(pos_flat, pad_bounds, n, n_rows):
    return pl.pallas_call(
        functools.partial(_inverse_kernel, n=n),
        in_specs=[pl.BlockSpec(memory_space=pltpu.SMEM), pl.BlockSpec(memory_space=pltpu.SMEM)],
        out_specs=pl.BlockSpec(memory_space=pltpu.SMEM),
        out_shape=jax.ShapeDtypeStruct((n_rows,), jnp.int32),
        name="moe_inverse",
    )(pos_flat, pad_bounds)


def _experts_kernel(te_ref, nu_ref, inv_ref, x_hbm, g_ref, wgu_hbm, wd_hbm, o_ref, xbuf_ref, act_ref, wgu_ref,
                    wd_ref, stage_gu, stage_d, sems, wsems, dsems, *, tm, tn):
    i = pl.program_id(0)
    n_used = nu_ref[0]
    ff = wd_ref.shape[0]
    n_chunks = ff // tn

    def gu_chunk(j):
        col = (j % 2) * ff + (j // 2) * tn
        return slice(col, col + tn)

    def gu_copy(e, j):
        slot = j % stage_gu.shape[0]
        return pltpu.make_async_copy(wgu_hbm.at[e, :, gu_chunk(j)], stage_gu.at[slot], wsems.at[slot])

    def down_copy(e, c):
        slot = c % stage_d.shape[0]
        return pltpu.make_async_copy(wd_hbm.at[e, c * tn:(c + 1) * tn, :], stage_d.at[slot], dsems.at[slot])

    def start_gather(tile, slot):
        def body(t, c):
            _row_copy(x_hbm, inv_ref[tile * tm + t], xbuf_ref.at[slot], t, sems.at[slot]).start()
            return c
        lax.fori_loop(0, tm, body, 0, unroll=8)

    def wait_gather(slot):
        pltpu.make_async_copy(x_hbm.at[pl.ds(0, tm)], xbuf_ref.at[slot], sems.at[slot]).wait()

    last = pl.num_programs(0) - 1
    slot = lax.rem(i, 2)

    @pl.when(i == 0)
    def _():
        start_gather(0, 0)

    def tile(stream):
        e = te_ref[i]
        if stream:
            for j in range(min(stage_gu.shape[0], 2 * n_chunks)):
                gu_copy(e, j).start()
            for c in range(min(stage_d.shape[0], n_chunks)):
                down_copy(e, c).start()
        wait_gather(slot)
        xn = _rms(xbuf_ref[slot], g_ref[...]).astype(BF16)
        nxt = jnp.minimum(i + 1, last) * tm
        for c in range(n_chunks):
            if stream:
                for j in (2 * c, 2 * c + 1):
                    gu_copy(e, j).wait()
                    wgu_ref[:, gu_chunk(j)] = stage_gu[j % stage_gu.shape[0]].astype(BF16)
                    if j + stage_gu.shape[0] < 2 * n_chunks:
                        gu_copy(e, j + stage_gu.shape[0]).start()
            for t in range(c * tm // n_chunks, (c + 1) * tm // n_chunks):
                _row_copy(x_hbm, inv_ref[nxt + t], xbuf_ref.at[1 - slot], t, sems.at[1 - slot]).start()
            rows = slice(c * tn, (c + 1) * tn)
            gate = jnp.dot(xn, wgu_ref[:, rows], preferred_element_type=F32)
            up = jnp.dot(xn, wgu_ref[:, ff + c * tn:ff + (c + 1) * tn], preferred_element_type=F32)
            act_ref[:, rows] = (_silu(gate) * up).astype(BF16)
            if stream:
                down_copy(e, c).wait()
                wd_ref[rows, :] = stage_d[c % stage_d.shape[0]].astype(BF16)
                if c + stage_d.shape[0] < n_chunks:
                    down_copy(e, c + stage_d.shape[0]).start()
                part = jnp.dot(act_ref[:, rows], wd_ref[rows, :], preferred_element_type=F32)
                o_ref[...] = part if c == 0 else o_ref[...] + part
        if not stream:
            o_ref[...] = jnp.dot(act_ref[...], wd_ref[...], preferred_element_type=F32)

        @pl.when(i == last)
        def _():
            wait_gather(1 - slot)

    new_expert = (i == 0) | (te_ref[i] != te_ref[jnp.maximum(i - 1, 0)])

    @pl.when((i < n_used) & new_expert)
    def _():
        tile(stream=True)

    @pl.when((i < n_used) & jnp.logical_not(new_expert))
    def _():
        tile(stream=False)

    @pl.when(i >= n_used)
    def _():
        @pl.when(i == n_used)
        def _():
            wait_gather(slot)

        o_ref[...] = jnp.zeros_like(o_ref)


def moe_experts(tile_expert, n_used, inv, x, g, w_gu, w_down, tm=ROW_TILE, tn=512):
    n, d = x.shape
    p = inv.shape[0]
    ff = w_down.shape[1]
    tn = min(tn, ff)
    hbm = pl.BlockSpec(memory_space=pl.ANY)
    return pl.pallas_call(
        functools.partial(_experts_kernel, tm=tm, tn=tn),
        grid_spec=pltpu.PrefetchScalarGridSpec(
            num_scalar_prefetch=3, grid=(p // tm,),
            in_specs=[hbm, pl.BlockSpec((1, d), lambda i, te, nu, inv: (0, 0)), hbm, hbm],
            out_specs=pl.BlockSpec((tm, d), lambda i, te, nu, inv: (i, 0)),
            scratch_shapes=[pltpu.VMEM((2, tm, d), F32), pltpu.VMEM((tm, ff), BF16),
                            pltpu.VMEM((d, 2 * ff), BF16), pltpu.VMEM((ff, d), BF16),
                            pltpu.VMEM((4, d, tn), w_gu.dtype), pltpu.VMEM((2, tn, d), w_down.dtype),
                            pltpu.SemaphoreType.DMA((2,)), pltpu.SemaphoreType.DMA((4,)),
                            pltpu.SemaphoreType.DMA((2,))]),
        out_shape=jax.ShapeDtypeStruct((p, d), F32),
        compiler_params=_cparams(("arbitrary",)),
        name="moe_experts",
    )(tile_expert, n_used, inv, x, g.reshape(1, d), w_gu, w_down)


def _combine_kernel(pos_ref, w_ref, x_ref, y_hbm, gp_ref, wg_ref, p_ref, wp_ref, *refs, n, tm, final_norm):
    if final_norm:
        gf_ref, o_ref, buf_ref, sems = refs
    else:
        o_ref, buf_ref, sems = refs
    i = pl.program_id(0)
    last = pl.num_programs(0) - 1
    slot = lax.rem(i, 2)

    def row_copies(tile_base, t, into):
        return [_row_copy(y_hbm, pos_ref[k * n + tile_base + t], buf_ref.at[into, k], t, sems.at[into])
                for k in range(TOP_K)]

    def wait_rows(of):
        for k in range(TOP_K):
            pltpu.make_async_copy(y_hbm.at[pl.ds(0, tm)], buf_ref.at[of, k], sems.at[of]).wait()

    @pl.when(i == 0)
    def _():
        def start(t, c):
            for cp in row_copies(0, t, 0):
                cp.start()
            return c
        lax.fori_loop(0, tm, start, 0, unroll=8)

    next_base = jnp.minimum(i + 1, last) * tm
    pieces = 4

    def issue_piece(q):
        for t in range(q * tm // pieces, (q + 1) * tm // pieces):
            for cp in row_copies(next_base, t, 1 - slot):
                cp.start()

    issue_piece(0)
    diag = (lax.broadcasted_iota(jnp.int32, (tm, tm), 0) == lax.broadcasted_iota(jnp.int32, (tm, tm), 1))
    wcol = [jnp.sum(jnp.where(diag, w_ref[k:k + 1, :], 0.0), axis=1, keepdims=True) for k in range(TOP_K)]
    wait_rows(slot)
    issue_piece(1)
    x = x_ref[...] + sum(wcol[k] * buf_ref[slot, k] for k in range(TOP_K))
    issue_piece(2)
    y = _ple(x, gp_ref, wg_ref, p_ref, wp_ref)
    issue_piece(3)
    o_ref[...] = _rms(y, gf_ref[...]) if final_norm else y

    @pl.when(i == last)
    def _():
        wait_rows(1 - slot)


def moe_combine(pos_flat, w, x, y_sorted, g_ple, wg_ple, p, wp_ple, g_final=None, tm=ROW_TILE):
    n, d = x.shape
    dp = p.shape[1]
    const = lambda shape: pl.BlockSpec(shape, lambda i, pos: (0, 0), pipeline_mode=pl.Buffered(1))
    in_specs = [pl.BlockSpec((w.shape[0], tm), lambda i, pos: (0, i)),
                pl.BlockSpec((tm, d), lambda i, pos: (i, 0)),
                pl.BlockSpec(memory_space=pl.ANY),
                const((1, d)), const((d, d)), pl.BlockSpec((tm, dp), lambda i, pos: (i, 0)), const((dp, d))]
    args = [pos_flat, w, x, y_sorted, g_ple.reshape(1, d), wg_ple, p, wp_ple]
    if g_final is not None:
        in_specs.append(const((1, d)))
        args.append(g_final.reshape(1, d))
    return pl.pallas_call(
        functools.partial(_combine_kernel, n=n, tm=tm, final_norm=g_final is not None),
        grid_spec=pltpu.PrefetchScalarGridSpec(
            num_scalar_prefetch=1, grid=(n // tm,),
            in_specs=in_specs,
            out_specs=pl.BlockSpec((tm, d), lambda i, pos: (i, 0)),
            scratch_shapes=[pltpu.VMEM((2, TOP_K, tm, d), F32), pltpu.SemaphoreType.DMA((2,))]),
        out_shape=jax.ShapeDtypeStruct((n, d), F32),
        compiler_params=_cparams(("arbitrary",)),
        name="moe_combine",
    )(*args)


def _compress_kernel(h_ref, pos_ref, w1_ref, w2_ref, o_ref):
    half = h_ref.shape[1]
    h = h_ref[...].astype(BF16)
    p_lo = jnp.dot(h, w1_ref[0:half, :], preferred_element_type=F32)
    p_hi = jnp.dot(h, w1_ref[half:, :], preferred_element_type=F32)
    nh = p_hi.shape[0]
    posb = jnp.dot(pos_ref[...].astype(BF16), w1_ref[...], preferred_element_type=F32)
    hid = p_lo + pltpu.roll(p_hi, nh - 1, 0) + posb
    o_ref[...] = jnp.dot(jax.nn.gelu(hid).astype(BF16), w2_ref[...],
                         preferred_element_type=F32).astype(o_ref.dtype)


def nsa_compress(halves, pos, w1, w2):
    bg, nh, hw = halves.shape
    hid = w1.shape[1]
    dh = w2.shape[1]
    return pl.pallas_call(
        _compress_kernel,
        grid=(bg,),
        in_specs=[pl.BlockSpec((None, nh, hw), lambda i: (i, 0, 0)),
                  pl.BlockSpec((1, 2 * hw), lambda i: (0, 0)),
                  pl.BlockSpec((2 * hw, hid), lambda i: (0, 0)),
                  pl.BlockSpec((hid, dh), lambda i: (0, 0))],
        out_specs=pl.BlockSpec((None, nh, dh), lambda i: (i, 0, 0)),
        out_shape=jax.ShapeDtypeStruct((bg, nh, dh), BF16),
        compiler_params=_cparams(("parallel",)),
    )(halves, pos.reshape(1, 2 * hw), w1, w2)


def _rel_bucket_np(dist):
    n = np.maximum(dist, 0)
    max_exact = N_BUCKETS // 2
    nf = np.maximum(n, 1).astype(np.float32)
    large = max_exact + (np.log(nf / max_exact) / np.float32(math.log(MAX_DISTANCE / max_exact))
                         * (N_BUCKETS - max_exact)).astype(np.int32)
    return np.where(n < max_exact, n, np.minimum(large, N_BUCKETS - 1)).astype(np.int32)


def _nsa_kernel(q_ref, gt_ref, kc_ref, vct_ref, bct_ref, ks_ref, kw_ref, vst_ref, vwt_ref, tzw_ref, tzc_ref,
                ovlt_ref, expt_ref, o_ref, *, n_cmp, n_sel):
    tq = q_ref.shape[0]
    dh = HEAD_DIM
    r = q_ref.shape[1] // dh
    near = tzw_ref.shape[0]
    n_slc = ovlt_ref.shape[0]
    ncp = kc_ref.shape[0]
    g = pl.program_id(1)
    t0 = pl.multiple_of(pl.program_id(2) * tq, tq)
    near_keys = pl.ds(t0, near)
    heads = lambda blocks: jnp.concatenate(blocks, axis=1)
    per_head = lambda x, h: x[:, h * tq:(h + 1) * tq]

    qt = q_ref[...].astype(F32).T.astype(BF16)
    zero = jnp.zeros((dh, r * tq), BF16)
    qh = heads([qt[h * dh:(h + 1) * dh, :] for h in range(r)])
    qz = jnp.where(g == 0, jnp.concatenate([qh, zero], axis=0), jnp.concatenate([zero, qh], axis=0))
    gate_t = gt_ref[...].T

    def gate(branch):
        rows = [gate_t[branch * N_HEADS + gi * r:branch * N_HEADS + (gi + 1) * r, :] for gi in range(N_KV_GROUPS)]
        mine = jnp.where(g == 0, rows[0], rows[1])
        return heads([mine[h:h + 1, :] for h in range(r)])

    cblk = lax.broadcasted_iota(jnp.int32, (ncp, r * tq), 0)
    qpos = t0 + (lax.broadcasted_iota(jnp.int32, (ncp, r * tq), 1) & (tq - 1))
    mask_c = (qpos - (cblk * CMP_STRIDE + CMP_BLOCK - 1) >= 0) & (cblk < n_cmp)
    s = jnp.dot(kc_ref[...], qz, preferred_element_type=F32) + bct_ref[...]
    l = jnp.where(mask_c, s, NEG)
    e = jnp.where(mask_c, jnp.exp2(l - jnp.max(l, axis=0, keepdims=True)), 0.0)
    p = e / jnp.maximum(jnp.sum(e, axis=0, keepdims=True), 1e-30)
    out = gate(0) * jnp.dot(vct_ref[...], p.astype(BF16), preferred_element_type=F32)
    psum = functools.reduce(lambda acc, h: acc + per_head(p, h), range(1, r), per_head(p, 0))

    def pv_and_sum(vt_ref, keys, e):
        ones = jnp.ones((PACKED_ROWS_BF16, keys.size), BF16)
        return jnp.dot(jnp.concatenate([vt_ref[:, keys], ones], axis=0), e.astype(BF16),
                       preferred_element_type=F32)

    key_exists = t0 + lax.broadcasted_iota(jnp.int32, (near, r * tq), 0) >= WINDOW
    s = jnp.dot(kw_ref[near_keys, :], qz, preferred_element_type=F32)
    l = jnp.where(key_exists, s + tzw_ref[...], NEG)
    e = jnp.exp2(l - jnp.max(l, axis=0, keepdims=True))
    o_w = pv_and_sum(vwt_ref, near_keys, e)
    out = out + gate(2) * (o_w[:dh] / o_w[dh:dh + 1])

    imp = jnp.dot(ovlt_ref[...], psum, preferred_element_type=F32, precision=lax.Precision.HIGHEST)
    jblk = lax.broadcasted_iota(jnp.int32, (n_slc, tq), 0).astype(F32)
    qblk = ((t0 + lax.broadcasted_iota(jnp.int32, (n_slc, tq), 1)) // SLC_BLOCK).astype(F32)
    forced = (jblk == 0) | (jblk == qblk) | (jblk == qblk - 1)
    score = jnp.where(forced, 1e9, jnp.where(jblk <= qblk, imp, -1e9))
    sel = jnp.zeros((n_slc, tq), F32)
    for _ in range(n_sel):
        m = jnp.max(score, axis=0, keepdims=True)
        first = jnp.min(jnp.where(score == m, jblk, float(n_slc)), axis=0, keepdims=True)
        pick = jblk == first
        sel = jnp.where(pick, 1.0, sel)
        score = jnp.where(pick, -jnp.inf, score)
    def masked_queries(selected):
        extra = expt_ref.shape[1] - n_slc
        block_rows = jnp.concatenate([jnp.where(selected, 0.0, NEG), jnp.full((extra, tq), NEG, F32)], axis=0)
        return jnp.concatenate([qz, heads([block_rows.astype(BF16)] * r)], axis=0)

    def scores(keys, queries):
        k_aug = jnp.concatenate([ks_ref[keys, :], expt_ref[keys, :]], axis=1)
        return jnp.dot(k_aug, queries, preferred_element_type=F32)

    l = scores(near_keys, masked_queries(sel > 0.5)) + tzc_ref[...]
    m = jnp.max(l, axis=0, keepdims=True)
    acc = pv_and_sum(vst_ref, near_keys, jnp.exp2(l - m))
    far_bias = tzc_ref[0:1, :]
    q_far = masked_queries((sel > 0.5) & (jblk * SLC_BLOCK < (t0 - WINDOW).astype(F32)))

    def far_chunk(f, ma):
        m, acc = ma
        keys = pl.ds(pl.multiple_of(WINDOW + f * FAR_CHUNK, FAR_CHUNK), FAR_CHUNK)
        l = scores(keys, q_far) + far_bias
        m_new = jnp.maximum(m, jnp.max(l, axis=0, keepdims=True))
        return m_new, jnp.exp2(m - m_new) * acc + pv_and_sum(vst_ref, keys, jnp.exp2(l - m_new))

    n_far = (jnp.maximum(t0, WINDOW) - 1) // FAR_CHUNK
    m, acc = lax.fori_loop(0, n_far, far_chunk, (m, acc))
    out = out + gate(1) * (acc[:dh] / acc[dh:dh + 1])

    o_ref[...] = jnp.concatenate([per_head(out, h) for h in range(r)], axis=0).T.astype(o_ref.dtype)


def nsa_attention_core(q, gates_pad, kc2, vct, bct, kv_pad, kvt_pad, tzw, tzc, ovlt, expt, batch, seq, n_cmp):
    n, hd = q.shape
    gq, dh, tq = N_KV_GROUPS, HEAD_DIM, Q_BLOCK
    r = N_HEADS // gq
    nqt = seq // tq
    spad = kv_pad.shape[1]
    ncp = kc2.shape[1]
    n_slc = seq // SLC_BLOCK
    near = tzw.shape[1]
    row_tile = lambda b, g, t: (b * nqt + t, g)
    key_cols = lambda branch: pl.BlockSpec((None, spad, gq * dh), lambda b, g, t: (b, 0, branch))
    val_rows = lambda branch: pl.BlockSpec((dh, spad), lambda b, g, t: (branch * gq + g, b))
    table = pl.BlockSpec((None, near, r * tq), lambda b, g, t: (g, 0, 0))
    const = lambda a: pl.BlockSpec(a.shape, lambda b, g, t: (0,) * a.ndim)
    return pl.pallas_call(
        functools.partial(_nsa_kernel, n_cmp=n_cmp, n_sel=min(N_SELECT, n_slc)),
        grid=(batch, gq, nqt),
        in_specs=[pl.BlockSpec((tq, r * dh), row_tile),
                  pl.BlockSpec((tq, gates_pad.shape[1]), lambda b, g, t: (b * nqt + t, 0)),
                  pl.BlockSpec((None, ncp, gq * dh), lambda b, g, t: (b, 0, 0)),
                  pl.BlockSpec((None, None, dh, ncp), lambda b, g, t: (b, g, 0, 0)),
                  pl.BlockSpec((None, ncp, r * tq), lambda b, g, t: (g, t, 0)),
                  key_cols(2), key_cols(4), val_rows(3), val_rows(5), table, table,
                  const(ovlt), const(expt)],
        out_specs=pl.BlockSpec((tq, r * dh), row_tile),
        out_shape=jax.ShapeDtypeStruct((n, hd), BF16),
        compiler_params=_cparams(("parallel", "parallel", "parallel")),
        name="nsa_attention",
    )(q, gates_pad, kc2, vct, bct, kv_pad, kv_pad, kvt_pad, kvt_pad, tzw, tzc, ovlt, expt)


def _bias_table_kernel(rb_ref, bkt_win_ref, bkt_causal_ref, bkt_cmp_ref, tzw_ref, tzc_ref, bct_ref, *, rows):
    h = pl.program_id(0)

    def lookup(bkt):
        acc = jnp.full(bkt.shape, NEG, F32)
        for k in range(N_BUCKETS):
            acc = jnp.where(bkt == k, rb_ref[k * N_HEADS + h] * LOG2_E, acc)
        return acc

    for src, dst in ((bkt_win_ref, tzw_ref), (bkt_causal_ref, tzc_ref), (bkt_cmp_ref, bct_ref)):
        def body(c, carry, src=src, dst=dst):
            sl = pl.ds(pl.multiple_of(c * rows, rows), rows)
            dst[sl, :] = lookup(src[sl, :])
            return carry
        lax.fori_loop(0, src.shape[0] // rows, body, 0)


def _nsa_tables(rel_bias, seq, n_cmp, ncp):
    tq = Q_BLOCK
    r = N_HEADS // N_KV_GROUPS
    dist_near = np.arange(tq)[None, :] + WINDOW - np.arange(NEAR)[:, None]
    bkt_near = _rel_bucket_np(dist_near)
    bkt_win = np.where((dist_near >= 0) & (dist_near < WINDOW), bkt_near, N_BUCKETS).astype(np.int32)
    bkt_causal = np.where(dist_near >= 0, bkt_near, N_BUCKETS).astype(np.int32)
    dist_c = np.arange(seq)[None, :] - (np.arange(ncp)[:, None] * CMP_STRIDE + CMP_BLOCK - 1)
    bkt_c = _rel_bucket_np(dist_c).reshape(ncp, seq // tq, tq).transpose(1, 0, 2).reshape(-1, tq)
    maps = [jnp.asarray(m) for m in (bkt_win, bkt_causal, bkt_c)]
    whole = lambda m: pl.BlockSpec(m.shape, lambda h, rb: (0, 0))
    per_head = lambda m: pl.BlockSpec((None, m.shape[0], tq), lambda h, rb: (h // r, 0, h % r))
    tzw, tzc, bct = pl.pallas_call(
        functools.partial(_bias_table_kernel, rows=64),
        grid_spec=pltpu.PrefetchScalarGridSpec(
            num_scalar_prefetch=1, grid=(N_HEADS,),
            in_specs=[whole(m) for m in maps], out_specs=[per_head(m) for m in maps]),
        out_shape=[jax.ShapeDtypeStruct((N_KV_GROUPS, m.shape[0], r * tq), F32) for m in maps],
        compiler_params=_cparams(("parallel",)),
        name="nsa_bias_tables",
    )(rel_bias.reshape(N_BUCKETS * N_HEADS), *maps)
    n_slc = seq // SLC_BLOCK
    cs = np.arange(ncp)[None, :] * CMP_STRIDE
    ss = np.arange(n_slc)[:, None] * SLC_BLOCK
    overlap_t = (cs < ss + SLC_BLOCK) & (cs + CMP_BLOCK > ss) & (np.arange(ncp)[None, :] < n_cmp)
    kpos = np.arange(WINDOW + seq)[:, None] - WINDOW
    expand_t = np.zeros((WINDOW + seq, n_slc + PACKED_ROWS_BF16), bool)
    expand_t[:, :n_slc] = (kpos // SLC_BLOCK == np.arange(n_slc)[None, :]) & (kpos >= 0)
    expand_t[:, n_slc] = kpos[:, 0] < 0
    return tzw, tzc, bct, jnp.asarray(overlap_t, F32), jnp.asarray(expand_t, BF16)


def _kv_kernel(x_ref, g_ref, w_ref, wt_ref, kv_ref, kvt_ref):
    @pl.when(pl.program_id(1) == 0)
    def _():
        kv_ref[...] = jnp.zeros_like(kv_ref)
        kvt_ref[...] = jnp.zeros_like(kvt_ref)

    @pl.when(pl.program_id(1) > 0)
    def _():
        xn = _rms(x_ref[...], g_ref[...]).astype(BF16)
        kv_ref[...] = jnp.dot(xn, w_ref[...], preferred_element_type=F32).astype(kv_ref.dtype)
        kvt_ref[...] = lax.dot_general(wt_ref[...], xn, (((1,), (1,)), ((), ())),
                                       preferred_element_type=F32).astype(kvt_ref.dtype)


def kv_projection(x, g, w_kv, batch, seq):
    n, d = x.shape
    f = w_kv.shape[1]
    tm = WINDOW
    nt = seq // tm
    rows = batch * (nt + 1) * tm
    w16 = w_kv.astype(BF16)
    const = lambda shape: pl.BlockSpec(shape, lambda b, t: (0, 0))
    return pl.pallas_call(
        _kv_kernel,
        grid=(batch, nt + 1),
        in_specs=[pl.BlockSpec((tm, d), lambda b, t: (b * nt + jnp.maximum(t - 1, 0), 0)), const((1, d)),
                  const((d, f)), const((f, d))],
        out_specs=[pl.BlockSpec((tm, f), lambda b, t: (b * (nt + 1) + t, 0)),
                   pl.BlockSpec((f, tm), lambda b, t: (0, b * (nt + 1) + t))],
        out_shape=[jax.ShapeDtypeStruct((rows, f), BF16), jax.ShapeDtypeStruct((f, rows), BF16)],
        compiler_params=_cparams(("parallel", "arbitrary")),
        name="kv_projection",
    )(x, g.reshape(1, d), w16, w16.T)


def _nsa_shared_kv(x, batch, seq, kv_norm, w_kv, cmp_pos_k, cmp_pos_v, k_w1, k_w2, v_w1, v_w2):
    n, d = x.shape
    gq, dh = N_KV_GROUPS, HEAD_DIM
    f = w_kv.shape[1]
    kv_pad, kvt_pad = kv_projection(x, kv_norm, w_kv, batch, seq)
    kv_pad = kv_pad.reshape(batch, WINDOW + seq, f)
    kv5 = kv_pad[:, WINDOW:].reshape(batch, seq, N_KV_BRANCH, gq, dh)
    per_group = lambda j: jnp.transpose(kv5[:, :, j], (0, 2, 1, 3))
    n_cmp = (seq - CMP_BLOCK) // CMP_STRIDE + 1
    nhalf = seq // CMP_STRIDE

    def compress(t, pos, w1, w2):
        halves = t.reshape(batch * gq, nhalf, CMP_STRIDE * dh)
        return nsa_compress(halves, pos, w1.astype(BF16), w2.astype(BF16)).reshape(batch, gq, nhalf, dh)

    k_cmp = compress(per_group(0), cmp_pos_k, k_w1, k_w2)
    v_cmp = compress(per_group(1), cmp_pos_v, v_w1, v_w2)
    kc2 = jnp.transpose(k_cmp, (0, 2, 1, 3)).reshape(batch, nhalf, gq * dh)
    vct = jnp.transpose(v_cmp, (0, 1, 3, 2))
    return kc2, vct, kv_pad, kvt_pad, n_cmp


def _q_kernel(x_ref, g_ref, wq_ref, wg_ref, q_ref, gates_ref):
    xn = _rms(x_ref[...], g_ref[...]).astype(BF16)
    q = jnp.dot(xn, wq_ref[...], preferred_element_type=F32) * (HEAD_DIM ** -0.5 * LOG2_E)
    q_ref[...] = q.astype(q_ref.dtype)
    gates_ref[...] = jax.nn.sigmoid(jnp.dot(xn, wg_ref[...], preferred_element_type=F32))


def q_projection(x, g, w_q, tm=ROW_TILE):
    n, d = x.shape
    hd = N_HEADS * HEAD_DIM
    w16 = w_q.astype(BF16)
    w_gate = jnp.pad(w16[:, hd:], ((0, 0), (0, 128 - (w_q.shape[1] - hd))))
    return pl.pallas_call(
        _q_kernel,
        grid=(n // tm,),
        in_specs=[pl.BlockSpec((tm, d), lambda i: (i, 0)), pl.BlockSpec((1, d), lambda i: (0, 0)),
                  pl.BlockSpec((d, hd), lambda i: (0, 0)), pl.BlockSpec((d, 128), lambda i: (0, 0))],
        out_specs=[pl.BlockSpec((tm, hd), lambda i: (i, 0)), pl.BlockSpec((tm, 128), lambda i: (i, 0))],
        out_shape=[jax.ShapeDtypeStruct((n, hd), BF16), jax.ShapeDtypeStruct((n, 128), F32)],
        compiler_params=_cparams(("parallel",)),
        name="q_projection",
    )(x, g.reshape(1, d), w16[:, :hd], w_gate)


def _nsa_mix(x, batch, seq, g_mix, w_q, shared, tables):
    kc2, vct, kv_pad, kvt_pad, n_cmp = shared
    tzw, tzc, bct, ovlt, expt = tables
    q, gates_pad = q_projection(x, g_mix, w_q)
    return nsa_attention_core(q, gates_pad, kc2, vct, bct, kv_pad, kvt_pad, tzw, tzc, ovlt, expt,
                              batch, seq, n_cmp)


def _dense_tail_kernel(a_ref, wo_ref, x_ref, g_ref, wgu_ref, wd_ref, gp_ref, wg_ref, p_ref, wp_ref, o_ref,
                       act_ref, *, tn):
    x = x_ref[...] + jnp.dot(a_ref[...], wo_ref[...], preferred_element_type=F32)
    xn = _rms(x, g_ref[...]).astype(BF16)
    ff = wd_ref.shape[0]
    for c in range(ff // tn):
        gate = jnp.dot(xn, wgu_ref[:, c * tn:(c + 1) * tn], preferred_element_type=F32)
        up = jnp.dot(xn, wgu_ref[:, ff + c * tn:ff + (c + 1) * tn], preferred_element_type=F32)
        act_ref[:, c * tn:(c + 1) * tn] = (_silu(gate) * up).astype(BF16)
    x = x + jnp.dot(act_ref[...], wd_ref[...], preferred_element_type=F32)
    o_ref[...] = _ple(x, gp_ref, wg_ref, p_ref, wp_ref)


def dense_layer_tail(a, w_o, x, g_ffn, w_gu, w_down, g_ple, wg_ple, p, wp_ple, tm=ROW_TILE, tn=512):
    n, d = x.shape
    k = a.shape[1]
    ff = w_down.shape[0]
    dp = p.shape[1]
    whole = lambda shape: pl.BlockSpec(shape, lambda i: (0, 0), pipeline_mode=pl.Buffered(1))
    rows = lambda width: pl.BlockSpec((tm, width), lambda i: (i, 0))
    return pl.pallas_call(
        functools.partial(_dense_tail_kernel, tn=min(tn, ff)),
        grid=(n // tm,),
        in_specs=[rows(k), whole((k, d)), rows(d), whole((1, d)), whole((d, 2 * ff)), whole((ff, d)),
                  whole((1, d)), whole((d, d)), rows(dp), whole((dp, d))],
        out_specs=rows(d),
        out_shape=jax.ShapeDtypeStruct((n, d), F32),
        scratch_shapes=[pltpu.VMEM((tm, ff), BF16)],
        compiler_params=_cparams(("parallel",)),
        name="dense_layer_tail",
    )(a, w_o, x, g_ffn.reshape(1, d), w_gu, w_down, g_ple.reshape(1, d), wg_ple, p, wp_ple)


def _moe_layer_tail(mix, w_o, x, g, w_router, w_gu_all, w_down_all, layer, ple, tm=ROW_TILE):
    n, d = x.shape
    ne = w_router.shape[1]
    x, sel, w, counts = moe_route(mix, w_o, x, g, w_router)
    padded = (counts[:, 0] + tm - 1) // tm * tm
    ends = jnp.cumsum(padded)
    n_tiles = (TOP_K * n) // tm + ne
    n_used = (ends[-1] // tm).astype(jnp.int32).reshape(1)
    tile_first_row = jnp.arange(n_tiles, dtype=jnp.int32) * tm
    tile_expert = jnp.sum(ends[None, :] <= tile_first_row[:, None], axis=1)
    tile_expert = jnp.minimum(tile_expert, tile_expert[jnp.maximum(n_used[0] - 1, 0)]).astype(jnp.int32)
    starts = (ends - padded).astype(jnp.int32)
    pos = moe_slots(starts, sel)
    pos_flat = pos[:TOP_K].reshape(TOP_K * n)
    total = jnp.full((1,), n_tiles * tm, jnp.int32)
    unrouted = jnp.concatenate([starts + counts[:, 0], ends[-1:], ends, total]).astype(jnp.int32)
    inv = moe_inverse(pos_flat, unrouted, n, n_tiles * tm)
    flat = lambda w_all: w_all.reshape((-1,) + w_all.shape[2:])
    y_sorted = moe_experts(tile_expert + layer * ne, n_used, inv, x, g, flat(w_gu_all), flat(w_down_all))
    return moe_combine(pos_flat, w, x, y_sorted, *ple)


def kernel(x, p, g_mix, g_ffn, g_ple, g_final, rg_w_in, rg_conv_w, rg_conv_b, rg_w_a, rg_b_a, rg_w_x,
           rg_b_x, rg_lambda, rg_w_out, kv_norm, w_kv, cmp_pos_k, cmp_pos_v, cmp_k_w1, cmp_k_w2,
           cmp_v_w1, cmp_v_w2, rel_bias, nsa_w_q, nsa_w_o, ffn_w_gu, ffn_w_down, moe_w_router,
           moe_w_gu, moe_w_down, ple_w_proj, ple_w_gate):
    batch, seq, d = x.shape
    depth = p.shape[0]
    n_a = rg_w_in.shape[0]
    n = batch * seq
    x = x.reshape(n, d)
    p = p.reshape(depth, n, p.shape[-1])
    n_cmp = (seq - CMP_BLOCK) // CMP_STRIDE + 1
    tables = _nsa_tables(rel_bias, seq, n_cmp, seq // CMP_STRIDE)
    shared = None
    for i in range(depth):
        if i < n_a:
            mix = rg_lru_core(x, g_mix[i], rg_w_in[i].astype(BF16), batch, seq, rg_conv_w[i], rg_conv_b[i],
                              _block_diag_chunks(rg_w_a[i]), rg_b_a[i], _block_diag_chunks(rg_w_x[i]),
                              rg_b_x[i], rg_lambda[i])
            w_o = rg_w_out[i].astype(BF16)
        else:
            mix = _nsa_mix(x, batch, seq, g_mix[i], nsa_w_q[i - n_a], shared, tables)
            w_o = nsa_w_o[i - n_a].astype(BF16)
        ple = (g_ple[i], ple_w_gate[i].astype(BF16), p[i], ple_w_proj[i].astype(BF16))
        if i % 2 == 0:
            assert i < depth - 1
            x = dense_layer_tail(mix, w_o, x, g_ffn[i], ffn_w_gu[i // 2].astype(BF16),
                                 ffn_w_down[i // 2].astype(BF16), *ple)
        else:
            x = _moe_layer_tail(mix, w_o, x, g_ffn[i], moe_w_router[i // 2], moe_w_gu, moe_w_down, i // 2,
                                ple + ((g_final,) if i == depth - 1 else ()))
        if i == n_a - 1:
            shared = _nsa_shared_kv(x, batch, seq, kv_norm, w_kv, cmp_pos_k, cmp_pos_v,
                                    cmp_k_w1, cmp_k_w2, cmp_v_w1, cmp_v_w2)
    return x.reshape(batch, seq, d)
```

```python
import functools
import math

import numpy as np
import jax
import jax.numpy as jnp
from jax import lax
from jax.experimental import pallas as pl
from jax.experimental.pallas import tpu as pltpu

F32 = jnp.float32
BF16 = jnp.bfloat16

EPS = 1e-6
CONV_WIDTH = 4
LRU_BLOCKS = 16
LRU_C = 8.0
N_HEADS = 16
N_KV_GROUPS = 2
GROUP = N_HEADS // N_KV_GROUPS
HEAD_DIM = 64
N_KV_BRANCH = 6
CMP_BLOCK = 32
CMP_STRIDE = 16
SLC_BLOCK = 64
N_SELECT = 8
WINDOW = 512
Q_BLOCK = 128
N_BUCKETS = 32
MAX_DISTANCE = 128
TOP_K = 2
NEAR = WINDOW + Q_BLOCK
FAR_CHUNK = 512
NEG = -1e30
LOG2_E = math.log2(math.e)
PACKED_ROWS_BF16 = 16

VMEM_LIMIT_V7X = 56 * 1024 * 1024
ROW_TILE = 512


def _cparams(sem):
    return pltpu.CompilerParams(dimension_semantics=sem, vmem_limit_bytes=VMEM_LIMIT_V7X)


def _rms(x, g):
    return x * lax.rsqrt(jnp.mean(x * x, axis=-1, keepdims=True) + EPS) * g


def _silu(x):
    return x * jax.nn.sigmoid(x)


BD = 256
SCAN_ROWS = 8


def _rg_kernel(x_ref, g_ref, win_ref, cw_ref, cb_ref, wa_ref, ba_ref, wx_ref, bx_ref, lam_ref,
               o_ref, xe_ref, h_ref, a_ref, u_ref, xn_ref):
    t, c = o_ref.shape

    @pl.when(pl.program_id(1) == 0)
    def _():
        xe_ref[0:8, :] = jnp.zeros((8, c), F32)
        h_ref[...] = jnp.zeros_like(h_ref)

    xn_ref[...] = _rms(x_ref[...], g_ref[...]).astype(BF16)
    xe_ref[8:, :] = jnp.dot(xn_ref[...], win_ref[:, c:], preferred_element_type=F32)
    xe = xe_ref[...]
    xc = cb_ref[...] + sum(
        cw_ref[CONV_WIDTH - 1 - d:CONV_WIDTH - d, :] * (pltpu.roll(xe, d, 0) if d else xe)[8:, :]
        for d in range(CONV_WIDTH))
    xe_ref[0:8, :] = xe_ref[t:t + 8, :]

    lam = -lam_ref[...]
    softplus_neg_lam = jnp.maximum(lam, 0.0) + jnp.log1p(jnp.exp(-jnp.abs(lam)))
    for cblk in range(c // BD):
        sl = slice(cblk * BD, (cblk + 1) * BD)
        xb = xc[:, sl]
        xb16 = xb.astype(BF16)
        r = jax.nn.sigmoid(jnp.dot(xb16, wa_ref[cblk], preferred_element_type=F32) + ba_ref[:, sl])
        i = jax.nn.sigmoid(jnp.dot(xb16, wx_ref[cblk], preferred_element_type=F32) + bx_ref[:, sl])
        log_a = -LRU_C * r * softplus_neg_lam[:, sl]
        a = jnp.exp(log_a)
        a_ref[:, sl] = a
        u_ref[:, sl] = jnp.sqrt(1.0 - a * a) * (i * xb)

    row = lax.broadcasted_iota(jnp.int32, (SCAN_ROWS, c), 0)

    def scan_tile(k, h_prev):
        rows = pl.ds(pl.multiple_of(k * SCAN_ROWS, SCAN_ROWS), SCAN_ROWS)
        a = a_ref[rows, :]
        u = u_ref[rows, :]
        for d in (1, 2, 4):
            a_s = jnp.where(row >= d, pltpu.roll(a, d, 0), 1.0)
            u_s = jnp.where(row >= d, pltpu.roll(u, d, 0), 0.0)
            u = a * u_s + u
            a = a * a_s
        h = a * h_prev + u
        u_ref[rows, :] = h
        return jnp.broadcast_to(h[SCAN_ROWS - 1:SCAN_ROWS, :], (SCAN_ROWS, c))

    h_ref[...] = lax.fori_loop(0, t // SCAN_ROWS, scan_tile, h_ref[...])
    gate = jnp.dot(xn_ref[...], win_ref[:, :c], preferred_element_type=F32)
    o_ref[...] = (jax.nn.gelu(gate) * u_ref[...]).astype(o_ref.dtype)


def rg_lru_core(x, g, w_in, batch, seq, conv_w, conv_b, wa_bd, b_a, wx_bd, b_x, lam, t_chunk=512):
    n, d = x.shape
    c = w_in.shape[1] // 2
    nt = seq // t_chunk
    vec = lambda v: v.reshape(1, -1)
    rows = lambda width: pl.BlockSpec((t_chunk, width), lambda b, t: (b * nt + t, 0))
    const = lambda shape: pl.BlockSpec(shape, lambda b, t: (0,) * len(shape))
    return pl.pallas_call(
        _rg_kernel,
        grid=(batch, nt),
        in_specs=[rows(d), const((1, d)), const(w_in.shape), const((CONV_WIDTH, c)), const((1, c)),
                  const(wa_bd.shape), const((1, c)), const(wx_bd.shape), const((1, c)), const((1, c))],
        out_specs=rows(c),
        out_shape=jax.ShapeDtypeStruct((n, c), BF16),
        scratch_shapes=[pltpu.VMEM((t_chunk + 8, c), F32), pltpu.VMEM((SCAN_ROWS, c), F32),
                        pltpu.VMEM((t_chunk, c), F32), pltpu.VMEM((t_chunk, c), F32),
                        pltpu.VMEM((t_chunk, d), BF16)],
        compiler_params=_cparams(("parallel", "arbitrary")),
        name="rg_lru_core",
    )(x, vec(g), w_in, conv_w, vec(conv_b), wa_bd, vec(b_a), wx_bd, vec(b_x), vec(lam))


def _block_diag_chunks(w):
    nb, bw, _ = w.shape
    per = BD // bw
    w = w.reshape(nb // per, per, bw, bw)
    eye = jnp.eye(per, dtype=w.dtype)
    out = jnp.einsum('cpij,pq->cpiqj', w, eye).reshape(nb // per, BD, BD)
    return out.astype(BF16)


def _ple(x, g_ref, wg_ref, p_ref, wp_ref):
    xn = _rms(x, g_ref[...]).astype(BF16)
    gate = jax.nn.sigmoid(jnp.dot(xn, wg_ref[...], preferred_element_type=F32))
    proj = jnp.dot(p_ref[...].astype(BF16), wp_ref[...], preferred_element_type=F32)
    return x + gate * proj


SEL_ROWS = 8


def _route_kernel(a_ref, wo_ref, x_ref, g_ref, wt_ref, xo_ref, sel_ref, w_ref, cnt_ref, carry_ref):
    @pl.when(pl.program_id(0) == 0)
    def _():
        carry_ref[...] = jnp.zeros_like(carry_ref)

    x = x_ref[...] + jnp.dot(a_ref[...], wo_ref[...], preferred_element_type=F32)
    xo_ref[...] = x
    xn = _rms(x, g_ref[...])
    logits = lax.dot_general(wt_ref[...], xn, (((1,), (1,)), ((), ())), preferred_element_type=F32,
                             precision=lax.Precision.HIGHEST)
    ne, tm = logits.shape
    row = lax.broadcasted_iota(jnp.int32, (ne, tm), 0).astype(F32)
    m1 = jnp.max(logits, axis=0, keepdims=True)
    i1 = jnp.min(jnp.where(logits == m1, row, float(ne)), axis=0, keepdims=True)
    rest = jnp.where(row == i1, -jnp.inf, logits)
    m2 = jnp.max(rest, axis=0, keepdims=True)
    i2 = jnp.min(jnp.where(rest == m2, row, float(ne)), axis=0, keepdims=True)
    e2 = jnp.exp(m2 - m1)
    denom = 1.0 + e2
    ind = jnp.where((row == i1) | (row == i2), 1.0, 0.0)
    earlier = (lax.broadcasted_iota(jnp.int32, (tm, tm), 0) < lax.broadcasted_iota(jnp.int32, (tm, tm), 1))
    rank = jnp.dot(ind.astype(BF16), jnp.where(earlier, 1.0, 0.0).astype(BF16),
                   preferred_element_type=F32) + carry_ref[:, 0:1]
    r1 = jnp.sum(jnp.where(row == i1, rank, 0.0), axis=0, keepdims=True)
    r2 = jnp.sum(jnp.where(row == i2, rank, 0.0), axis=0, keepdims=True)
    carry_ref[...] = carry_ref[...] + jnp.sum(ind, axis=1, keepdims=True)
    pick = lambda vals: functools.reduce(
        lambda acc, kv: jnp.where(row == float(kv[0]), kv[1], acc), enumerate(vals), jnp.zeros((ne, tm), F32))
    sel_ref[...] = pick([i1, i2, r1, r2]).astype(jnp.int32)
    w_ref[...] = pick([1.0 / denom, e2 / denom])
    cnt_ref[...] = carry_ref[...].astype(jnp.int32)


def moe_route(a, w_o, x, g, w_router, tm=ROW_TILE):
    n, d = x.shape
    k = a.shape[1]
    ne = w_router.shape[1]
    assert ne == SEL_ROWS
    const = lambda shape: pl.BlockSpec(shape, lambda i: (0, 0))
    return pl.pallas_call(
        _route_kernel,
        grid=(n // tm,),
        in_specs=[pl.BlockSpec((tm, k), lambda i: (i, 0)), const((k, d)), pl.BlockSpec((tm, d), lambda i: (i, 0)),
                  const((1, d)), const((ne, d))],
        out_specs=[pl.BlockSpec((tm, d), lambda i: (i, 0)),
                   pl.BlockSpec((ne, tm), lambda i: (0, i)), pl.BlockSpec((ne, tm), lambda i: (0, i)),
                   pl.BlockSpec((ne, 128), lambda i: (0, 0))],
        out_shape=[jax.ShapeDtypeStruct((n, d), F32),
                   jax.ShapeDtypeStruct((ne, n), jnp.int32), jax.ShapeDtypeStruct((ne, n), F32),
                   jax.ShapeDtypeStruct((ne, 128), jnp.int32)],
        scratch_shapes=[pltpu.VMEM((ne, 128), F32)],
        compiler_params=_cparams(("arbitrary",)),
        name="moe_route",
    )(a, w_o, x, g.reshape(1, d), w_router.T)


def _slot_kernel(starts_ref, sel_ref, pos_ref):
    sel = sel_ref[...]
    start_of = lambda e_row: functools.reduce(
        lambda acc, e: jnp.where(e_row == e, starts_ref[e], acc), range(starts_ref.shape[0]),
        jnp.zeros_like(e_row))
    row = lax.broadcasted_iota(jnp.int32, sel.shape, 0)
    p1 = start_of(sel[0:1]) + sel[2:3]
    p2 = start_of(sel[1:2]) + sel[3:4]
    pos_ref[...] = jnp.where(row == 0, p1, jnp.where(row == 1, p2, 0))


def moe_slots(starts, sel, tm=2048):
    ne, n = sel.shape
    return pl.pallas_call(
        _slot_kernel,
        grid_spec=pltpu.PrefetchScalarGridSpec(
            num_scalar_prefetch=1, grid=(n // tm,),
            in_specs=[pl.BlockSpec((ne, tm), lambda i, s: (0, i))],
            out_specs=pl.BlockSpec((ne, tm), lambda i, s: (0, i))),
        out_shape=jax.ShapeDtypeStruct((ne, n), jnp.int32),
        compiler_params=_cparams(("parallel",)),
        name="moe_slots",
    )(starts, sel)


def _row_copy(src, src_row, dst, dst_row, sem):
    return pltpu.make_async_copy(src.at[pl.ds(src_row, 1)], dst.at[pl.ds(dst_row, 1)], sem)


def _inverse_kernel(pos_ref, pad_ref, inv_ref, *, n):
    def clear(s, c):
        inv_ref[s] = 0
        return c

    def fill(t, c):
        for k in range(TOP_K):
            inv_ref[pos_ref[k * n + t]] = t
        return c

    n_ranges = pad_ref.shape[0] // 2
    for e in range(n_ranges):
        lax.fori_loop(pad_ref[e], pad_ref[n_ranges + e], clear, 0)
    lax.fori_loop(0, n, fill, 0, unroll=8)


def moe_inverse(pos_flat, pad_bounds, n, n_rows):
    return pl.pallas_call(
        functools.partial(_inverse_kernel, n=n),
        in_specs=[pl.BlockSpec(memory_space=pltpu.SMEM), pl.BlockSpec(memory_space=pltpu.SMEM)],
        out_specs=pl.BlockSpec(memory_space=pltpu.SMEM),
        out_shape=jax.ShapeDtypeStruct((n_rows,), jnp.int32),
        name="moe_inverse",
    )(pos_flat, pad_bounds)


def _experts_kernel(te_ref, nu_ref, inv_ref, x_hbm, g_ref, wgu_hbm, wd_hbm, o_ref, xbuf_ref, act_ref, wgu_ref,
                    wd_ref, stage_gu, stage_d, sems, wsems, dsems, *, tm, tn):
    i = pl.program_id(0)
    n_used = nu_ref[0]
    ff = wd_ref.shape[0]
    n_chunks = ff // tn

    def gu_chunk(j):
        col = (j % 2) * ff + (j // 2) * tn
        return slice(col, col + tn)

    def gu_copy(e, j):
        slot = j % stage_gu.shape[0]
        return pltpu.make_async_copy(wgu_hbm.at[e, :, gu_chunk(j)], stage_gu.at[slot], wsems.at[slot])

    def down_copy(e, c):
        slot = c % stage_d.shape[0]
        return pltpu.make_async_copy(wd_hbm.at[e, c * tn:(c + 1) * tn, :], stage_d.at[slot], dsems.at[slot])

    def start_gather(tile, slot):
        def body(t, c):
            _row_copy(x_hbm, inv_ref[tile * tm + t], xbuf_ref.at[slot], t, sems.at[slot]).start()
            return c
        lax.fori_loop(0, tm, body, 0, unroll=8)

    def wait_gather(slot):
        pltpu.make_async_copy(x_hbm.at[pl.ds(0, tm)], xbuf_ref.at[slot], sems.at[slot]).wait()

    last = pl.num_programs(0) - 1
    slot = lax.rem(i, 2)

    @pl.when(i == 0)
    def _():
        start_gather(0, 0)

    def tile(stream):
        e = te_ref[i]
        if stream:
            for j in range(min(stage_gu.shape[0], 2 * n_chunks)):
                gu_copy(e, j).start()
            for c in range(min(stage_d.shape[0], n_chunks)):
                down_copy(e, c).start()
        wait_gather(slot)
        xn = _rms(xbuf_ref[slot], g_ref[...]).astype(BF16)
        nxt = jnp.minimum(i + 1, last) * tm
        for c in range(n_chunks):
            if stream:
                for j in (2 * c, 2 * c + 1):
                    gu_copy(e, j).wait()
                    wgu_ref[:, gu_chunk(j)] = stage_gu[j % stage_gu.shape[0]].astype(BF16)
                    if j + stage_gu.shape[0] < 2 * n_chunks:
                        gu_copy(e, j + stage_gu.shape[0]).start()
            for t in range(c * tm // n_chunks, (c + 1) * tm // n_chunks):
                _row_copy(x_hbm, inv_ref[nxt + t], xbuf_ref.at[1 - slot], t, sems.at[1 - slot]).start()
            rows = slice(c * tn, (c + 1) * tn)
            gate = jnp.dot(xn, wgu_ref[:, rows], preferred_element_type=F32)
            up = jnp.dot(xn, wgu_ref[:, ff + c * tn:ff + (c + 1) * tn], preferred_element_type=F32)
            act_ref[:, rows] = (_silu(gate) * up).astype(BF16)
            if stream:
                down_copy(e, c).wait()
                wd_ref[rows, :] = stage_d[c % stage_d.shape[0]].astype(BF16)
                if c + stage_d.shape[0] < n_chunks:
                    down_copy(e, c + stage_d.shape[0]).start()
                part = jnp.dot(act_ref[:, rows], wd_ref[rows, :], preferred_element_type=F32)
                o_ref[...] = part if c == 0 else o_ref[...] + part
        if not stream:
            o_ref[...] = jnp.dot(act_ref[...], wd_ref[...], preferred_element_type=F32)

        @pl.when(i == last)
        def _():
            wait_gather(1 - slot)

    new_expert = (i == 0) | (te_ref[i] != te_ref[jnp.maximum(i - 1, 0)])

    @pl.when((i < n_used) & new_expert)
    def _():
        tile(stream=True)

    @pl.when((i < n_used) & jnp.logical_not(new_expert))
    def _():
        tile(stream=False)

    @pl.when(i >= n_used)
    def _():
        @pl.when(i == n_used)
        def _():
            wait_gather(slot)

        o_ref[...] = jnp.zeros_like(o_ref)


def moe_experts(tile_expert, n_used, inv, x, g, w_gu, w_down, tm=ROW_TILE, tn=512):
    n, d = x.shape
    p = inv.shape[0]
    ff = w_down.shape[1]
    tn = min(tn, ff)
    hbm = pl.BlockSpec(memory_space=pl.ANY)
    return pl.pallas_call(
        functools.partial(_experts_kernel, tm=tm, tn=tn),
        grid_spec=pltpu.PrefetchScalarGridSpec(
            num_scalar_prefetch=3, grid=(p // tm,),
            in_specs=[hbm, pl.BlockSpec((1, d), lambda i, te, nu, inv: (0, 0)), hbm, hbm],
            out_specs=pl.BlockSpec((tm, d), lambda i, te, nu, inv: (i, 0)),
            scratch_shapes=[pltpu.VMEM((2, tm, d), F32), pltpu.VMEM((tm, ff), BF16),
                            pltpu.VMEM((d, 2 * ff), BF16), pltpu.VMEM((ff, d), BF16),
                            pltpu.VMEM((4, d, tn), w_gu.dtype), pltpu.VMEM((2, tn, d), w_down.dtype),
                            pltpu.SemaphoreType.DMA((2,)), pltpu.SemaphoreType.DMA((4,)),
                            pltpu.SemaphoreType.DMA((2,))]),
        out_shape=jax.ShapeDtypeStruct((p, d), F32),
        compiler_params=_cparams(("arbitrary",)),
        name="moe_experts",
    )(tile_expert, n_used, inv, x, g.reshape(1, d), w_gu, w_down)


def _combine_kernel(pos_ref, w_ref, x_ref, y_hbm, gp_ref, wg_ref, p_ref, wp_ref, *refs, n, tm, final_norm):
    if final_norm:
        gf_ref, o_ref, buf_ref, sems = refs
    else:
        o_ref, buf_ref, sems = refs
    i = pl.program_id(0)
    last = pl.num_programs(0) - 1
    slot = lax.rem(i, 2)

    def row_copies(tile_base, t, into):
        return [_row_copy(y_hbm, pos_ref[k * n + tile_base + t], buf_ref.at[into, k], t, sems.at[into])
                for k in range(TOP_K)]

    def wait_rows(of):
        for k in range(TOP_K):
            pltpu.make_async_copy(y_hbm.at[pl.ds(0, tm)], buf_ref.at[of, k], sems.at[of]).wait()

    @pl.when(i == 0)
    def _():
        def start(t, c):
            for cp in row_copies(0, t, 0):
                cp.start()
            return c
        lax.fori_loop(0, tm, start, 0, unroll=8)

    next_base = jnp.minimum(i + 1, last) * tm
    pieces = 4

    def issue_piece(q):
        for t in range(q * tm // pieces, (q + 1) * tm // pieces):
            for cp in row_copies(next_base, t, 1 - slot):
                cp.start()

    issue_piece(0)
    diag = (lax.broadcasted_iota(jnp.int32, (tm, tm), 0) == lax.broadcasted_iota(jnp.int32, (tm, tm), 1))
    wcol = [jnp.sum(jnp.where(diag, w_ref[k:k + 1, :], 0.0), axis=1, keepdims=True) for k in range(TOP_K)]
    wait_rows(slot)
    issue_piece(1)
    x = x_ref[...] + sum(wcol[k] * buf_ref[slot, k] for k in range(TOP_K))
    issue_piece(2)
    y = _ple(x, gp_ref, wg_ref, p_ref, wp_ref)
    issue_piece(3)
    o_ref[...] = _rms(y, gf_ref[...]) if final_norm else y

    @pl.when(i == last)
    def _():
        wait_rows(1 - slot)


def moe_combine(pos_flat, w, x, y_sorted, g_ple, wg_ple, p, wp_ple, g_final=None, tm=ROW_TILE):
    n, d = x.shape
    dp = p.shape[1]
    const = lambda shape: pl.BlockSpec(shape, lambda i, pos: (0, 0), pipeline_mode=pl.Buffered(1))
    in_specs = [pl.BlockSpec((w.shape[0], tm), lambda i, pos: (0, i)),
                pl.BlockSpec((tm, d), lambda i, pos: (i, 0)),
                pl.BlockSpec(memory_space=pl.ANY),
                const((1, d)), const((d, d)), pl.BlockSpec((tm, dp), lambda i, pos: (i, 0)), const((dp, d))]
    args = [pos_flat, w, x, y_sorted, g_ple.reshape(1, d), wg_ple, p, wp_ple]
    if g_final is not None:
        in_specs.append(const((1, d)))
        args.append(g_final.reshape(1, d))
    return pl.pallas_call(
        functools.partial(_combine_kernel, n=n, tm=tm, final_norm=g_final is not None),
        grid_spec=pltpu.PrefetchScalarGridSpec(
            num_scalar_prefetch=1, grid=(n // tm,),
            in_specs=in_specs,
            out_specs=pl.BlockSpec((tm, d), lambda i, pos: (i, 0)),
            scratch_shapes=[pltpu.VMEM((2, TOP_K, tm, d), F32), pltpu.SemaphoreType.DMA((2,))]),
        out_shape=jax.ShapeDtypeStruct((n, d), F32),
        compiler_params=_cparams(("arbitrary",)),
        name="moe_combine",
    )(*args)


def _compress_kernel(h_ref, pos_ref, w1_ref, w2_ref, o_ref):
    half = h_ref.shape[1]
    h = h_ref[...].astype(BF16)
    p_lo = jnp.dot(h, w1_ref[0:half, :], preferred_element_type=F32)
    p_hi = jnp.dot(h, w1_ref[half:, :], preferred_element_type=F32)
    nh = p_hi.shape[0]
    posb = jnp.dot(pos_ref[...].astype(BF16), w1_ref[...], preferred_element_type=F32)
    hid = p_lo + pltpu.roll(p_hi, nh - 1, 0) + posb
    o_ref[...] = jnp.dot(jax.nn.gelu(hid).astype(BF16), w2_ref[...],
                         preferred_element_type=F32).astype(o_ref.dtype)


def nsa_compress(halves, pos, w1, w2):
    bg, nh, hw = halves.shape
    hid = w1.shape[1]
    dh = w2.shape[1]
    return pl.pallas_call(
        _compress_kernel,
        grid=(bg,),
        in_specs=[pl.BlockSpec((None, nh, hw), lambda i: (i, 0, 0)),
                  pl.BlockSpec((1, 2 * hw), lambda i: (0, 0)),
                  pl.BlockSpec((2 * hw, hid), lambda i: (0, 0)),
                  pl.BlockSpec((hid, dh), lambda i: (0, 0))],
        out_specs=pl.BlockSpec((None, nh, dh), lambda i: (i, 0, 0)),
        out_shape=jax.ShapeDtypeStruct((bg, nh, dh), BF16),
        compiler_params=_cparams(("parallel",)),
    )(halves, pos.reshape(1, 2 * hw), w1, w2)


def _rel_bucket_np(dist):
    n = np.maximum(dist, 0)
    max_exact = N_BUCKETS // 2
    nf = np.maximum(n, 1).astype(np.float32)
    large = max_exact + (np.log(nf / max_exact) / np.float32(math.log(MAX_DISTANCE / max_exact))
                         * (N_BUCKETS - max_exact)).astype(np.int32)
    return np.where(n < max_exact, n, np.minimum(large, N_BUCKETS - 1)).astype(np.int32)


def _nsa_kernel(q_ref, gt_ref, kc_ref, vct_ref, bct_ref, ks_ref, kw_ref, vst_ref, vwt_ref, tzw_ref, tzc_ref,
                ovlt_ref, expt_ref, o_ref, *, n_cmp, n_sel):
    tq = q_ref.shape[0]
    dh = HEAD_DIM
    r = q_ref.shape[1] // dh
    near = tzw_ref.shape[0]
    n_slc = ovlt_ref.shape[0]
    ncp = kc_ref.shape[0]
    g = pl.program_id(1)
    t0 = pl.multiple_of(pl.program_id(2) * tq, tq)
    near_keys = pl.ds(t0, near)
    heads = lambda blocks: jnp.concatenate(blocks, axis=1)
    per_head = lambda x, h: x[:, h * tq:(h + 1) * tq]

    qt = q_ref[...].astype(F32).T.astype(BF16)
    zero = jnp.zeros((dh, r * tq), BF16)
    qh = heads([qt[h * dh:(h + 1) * dh, :] for h in range(r)])
    qz = jnp.where(g == 0, jnp.concatenate([qh, zero], axis=0), jnp.concatenate([zero, qh], axis=0))
    gate_t = gt_ref[...].T

    def gate(branch):
        rows = [gate_t[branch * N_HEADS + gi * r:branch * N_HEADS + (gi + 1) * r, :] for gi in range(N_KV_GROUPS)]
        mine = jnp.where(g == 0, rows[0], rows[1])
        return heads([mine[h:h + 1, :] for h in range(r)])

    cblk = lax.broadcasted_iota(jnp.int32, (ncp, r * tq), 0)
    qpos = t0 + (lax.broadcasted_iota(jnp.int32, (ncp, r * tq), 1) & (tq - 1))
    mask_c = (qpos - (cblk * CMP_STRIDE + CMP_BLOCK - 1) >= 0) & (cblk < n_cmp)
    s = jnp.dot(kc_ref[...], qz, preferred_element_type=F32) + bct_ref[...]
    l = jnp.where(mask_c, s, NEG)
    e = jnp.where(mask_c, jnp.exp2(l - jnp.max(l, axis=0, keepdims=True)), 0.0)
    p = e / jnp.maximum(jnp.sum(e, axis=0, keepdims=True), 1e-30)
    out = gate(0) * jnp.dot(vct_ref[...], p.astype(BF16), preferred_element_type=F32)
    psum = functools.reduce(lambda acc, h: acc + per_head(p, h), range(1, r), per_head(p, 0))

    def pv_and_sum(vt_ref, keys, e):
        ones = jnp.ones((PACKED_ROWS_BF16, keys.size), BF16)
        return jnp.dot(jnp.concatenate([vt_ref[:, keys], ones], axis=0), e.astype(BF16),
                       preferred_element_type=F32)

    flags = expt_ref.shape[1] - n_slc
    kw_aug = jnp.concatenate([kw_ref[near_keys, :], expt_ref[near_keys, n_slc:]], axis=1)
    q_aug = jnp.concatenate([qz, jnp.full((flags, r * tq), NEG, BF16)], axis=0)
    l = jnp.dot(kw_aug, q_aug, preferred_element_type=F32) + tzw_ref[...]
    e = jnp.exp2(l - jnp.max(l, axis=0, keepdims=True))
    o_w = pv_and_sum(vwt_ref, near_keys, e)
    out = out + gate(2) * (o_w[:dh] / o_w[dh:dh + 1])

    imp = jnp.dot(ovlt_ref[...], psum, preferred_element_type=F32, precision=lax.Precision.HIGHEST)
    jblk = lax.broadcasted_iota(jnp.int32, (n_slc, tq), 0).astype(F32)
    qblk = ((t0 + lax.broadcasted_iota(jnp.int32, (n_slc, tq), 1)) // SLC_BLOCK).astype(F32)
    forced = (jblk == 0) | (jblk == qblk) | (jblk == qblk - 1)
    score = jnp.where(forced, 1e9, jnp.where(jblk <= qblk, imp, -1e9))
    sel = jnp.zeros((n_slc, tq), F32)
    for _ in range(n_sel):
        m = jnp.max(score, axis=0, keepdims=True)
        first = jnp.min(jnp.where(score == m, jblk, float(n_slc)), axis=0, keepdims=True)
        pick = jblk == first
        sel = jnp.where(pick, 1.0, sel)
        score = jnp.where(pick, -jnp.inf, score)
    def masked_queries(selected):
        extra = expt_ref.shape[1] - n_slc
        block_rows = jnp.concatenate([jnp.where(selected, 0.0, NEG), jnp.full((extra, tq), NEG, F32)], axis=0)
        return jnp.concatenate([qz, heads([block_rows.astype(BF16)] * r)], axis=0)

    def scores(keys, queries):
        k_aug = jnp.concatenate([ks_ref[keys, :], expt_ref[keys, :]], axis=1)
        return jnp.dot(k_aug, queries, preferred_element_type=F32)

    l = scores(near_keys, masked_queries(sel > 0.5)) + tzc_ref[...]
    m = jnp.max(l, axis=0, keepdims=True)
    acc = pv_and_sum(vst_ref, near_keys, jnp.exp2(l - m))
    far_bias = tzc_ref[0:1, :]
    q_far = masked_queries((sel > 0.5) & (jblk * SLC_BLOCK < (t0 - WINDOW).astype(F32)))

    def far_chunk(f, ma):
        m, acc = ma
        keys = pl.ds(pl.multiple_of(WINDOW + f * FAR_CHUNK, FAR_CHUNK), FAR_CHUNK)
        l = scores(keys, q_far) + far_bias
        m_new = jnp.maximum(m, jnp.max(l, axis=0, keepdims=True))
        return m_new, jnp.exp2(m - m_new) * acc + pv_and_sum(vst_ref, keys, jnp.exp2(l - m_new))

    n_far = (jnp.maximum(t0, WINDOW) - 1) // FAR_CHUNK
    m, acc = lax.fori_loop(0, n_far, far_chunk, (m, acc))
    out = out + gate(1) * (acc[:dh] / acc[dh:dh + 1])

    o_ref[...] = jnp.concatenate([per_head(out, h) for h in range(r)], axis=0).T.astype(o_ref.dtype)


def nsa_attention_core(q, gates_pad, kc2, vct, bct, kv_pad, kvt_pad, tzw, tzc, ovlt, expt, batch, seq, n_cmp):
    n, hd = q.shape
    gq, dh, tq = N_KV_GROUPS, HEAD_DIM, Q_BLOCK
    r = N_HEADS // gq
    nqt = seq // tq
    spad = kv_pad.shape[1]
    ncp = kc2.shape[1]
    n_slc = seq // SLC_BLOCK
    near = tzw.shape[1]
    row_tile = lambda b, g, t: (b * nqt + t, g)
    key_cols = lambda branch: pl.BlockSpec((None, spad, gq * dh), lambda b, g, t: (b, 0, branch))
    val_rows = lambda branch: pl.BlockSpec((dh, spad), lambda b, g, t: (branch * gq + g, b))
    table = pl.BlockSpec((None, near, r * tq), lambda b, g, t: (g, 0, 0))
    const = lambda a: pl.BlockSpec(a.shape, lambda b, g, t: (0,) * a.ndim)
    return pl.pallas_call(
        functools.partial(_nsa_kernel, n_cmp=n_cmp, n_sel=min(N_SELECT, n_slc)),
        grid=(batch, gq, nqt),
        in_specs=[pl.BlockSpec((tq, r * dh), row_tile),
                  pl.BlockSpec((tq, gates_pad.shape[1]), lambda b, g, t: (b * nqt + t, 0)),
                  pl.BlockSpec((None, ncp, gq * dh), lambda b, g, t: (b, 0, 0)),
                  pl.BlockSpec((None, None, dh, ncp), lambda b, g, t: (b, g, 0, 0)),
                  pl.BlockSpec((None, ncp, r * tq), lambda b, g, t: (g, t, 0)),
                  key_cols(2), key_cols(4), val_rows(3), val_rows(5), table, table,
                  const(ovlt), const(expt)],
        out_specs=pl.BlockSpec((tq, r * dh), row_tile),
        out_shape=jax.ShapeDtypeStruct((n, hd), BF16),
        compiler_params=_cparams(("parallel", "parallel", "parallel")),
        name="nsa_attention",
    )(q, gates_pad, kc2, vct, bct, kv_pad, kv_pad, kvt_pad, kvt_pad, tzw, tzc, ovlt, expt)


def _bias_table_kernel(rb_ref, bkt_win_ref, bkt_causal_ref, bkt_cmp_ref, tzw_ref, tzc_ref, bct_ref, *, rows):
    h = pl.program_id(0)

    def lookup(bkt):
        acc = jnp.full(bkt.shape, NEG, F32)
        for k in range(N_BUCKETS):
            acc = jnp.where(bkt == k, rb_ref[k * N_HEADS + h] * LOG2_E, acc)
        return acc

    for src, dst in ((bkt_win_ref, tzw_ref), (bkt_causal_ref, tzc_ref), (bkt_cmp_ref, bct_ref)):
        def body(c, carry, src=src, dst=dst):
            sl = pl.ds(pl.multiple_of(c * rows, rows), rows)
            dst[sl, :] = lookup(src[sl, :])
            return carry
        lax.fori_loop(0, src.shape[0] // rows, body, 0)


def _nsa_tables(rel_bias, seq, n_cmp, ncp):
    tq = Q_BLOCK
    r = N_HEADS // N_KV_GROUPS
    dist_near = np.arange(tq)[None, :] + WINDOW - np.arange(NEAR)[:, None]
    bkt_near = _rel_bucket_np(dist_near)
    bkt_win = np.where((dist_near >= 0) & (dist_near < WINDOW), bkt_near, N_BUCKETS).astype(np.int32)
    bkt_causal = np.where(dist_near >= 0, bkt_near, N_BUCKETS).astype(np.int32)
    dist_c = np.arange(seq)[None, :] - (np.arange(ncp)[:, None] * CMP_STRIDE + CMP_BLOCK - 1)
    bkt_c = _rel_bucket_np(dist_c).reshape(ncp, seq // tq, tq).transpose(1, 0, 2).reshape(-1, tq)
    maps = [jnp.asarray(m) for m in (bkt_win, bkt_causal, bkt_c)]
    whole = lambda m: pl.BlockSpec(m.shape, lambda h, rb: (0, 0))
    per_head = lambda m: pl.BlockSpec((None, m.shape[0], tq), lambda h, rb: (h // r, 0, h % r))
    tzw, tzc, bct = pl.pallas_call(
        functools.partial(_bias_table_kernel, rows=64),
        grid_spec=pltpu.PrefetchScalarGridSpec(
            num_scalar_prefetch=1, grid=(N_HEADS,),
            in_specs=[whole(m) for m in maps], out_specs=[per_head(m) for m in maps]),
        out_shape=[jax.ShapeDtypeStruct((N_KV_GROUPS, m.shape[0], r * tq), F32) for m in maps],
        compiler_params=_cparams(("parallel",)),
        name="nsa_bias_tables",
    )(rel_bias.reshape(N_BUCKETS * N_HEADS), *maps)
    n_slc = seq // SLC_BLOCK
    cs = np.arange(ncp)[None, :] * CMP_STRIDE
    ss = np.arange(n_slc)[:, None] * SLC_BLOCK
    overlap_t = (cs < ss + SLC_BLOCK) & (cs + CMP_BLOCK > ss) & (np.arange(ncp)[None, :] < n_cmp)
    kpos = np.arange(WINDOW + seq)[:, None] - WINDOW
    expand_t = np.zeros((WINDOW + seq, n_slc + PACKED_ROWS_BF16), bool)
    expand_t[:, :n_slc] = (kpos // SLC_BLOCK == np.arange(n_slc)[None, :]) & (kpos >= 0)
    expand_t[:, n_slc] = kpos[:, 0] < 0
    return tzw, tzc, bct, jnp.asarray(overlap_t, F32), jnp.asarray(expand_t, BF16)


def _kv_kernel(x_ref, g_ref, w_ref, wt_ref, kv_ref, kvt_ref):
    @pl.when(pl.program_id(1) == 0)
    def _():
        kv_ref[...] = jnp.zeros_like(kv_ref)
        kvt_ref[...] = jnp.zeros_like(kvt_ref)

    @pl.when(pl.program_id(1) > 0)
    def _():
        xn = _rms(x_ref[...], g_ref[...]).astype(BF16)
        kv_ref[...] = jnp.dot(xn, w_ref[...], preferred_element_type=F32).astype(kv_ref.dtype)
        kvt_ref[...] = lax.dot_general(wt_ref[...], xn, (((1,), (1,)), ((), ())),
                                       preferred_element_type=F32).astype(kvt_ref.dtype)


def kv_projection(x, g, w_kv, batch, seq):
    n, d = x.shape
    f = w_kv.shape[1]
    tm = WINDOW
    nt = seq // tm
    rows = batch * (nt + 1) * tm
    w16 = w_kv.astype(BF16)
    const = lambda shape: pl.BlockSpec(shape, lambda b, t: (0, 0))
    return pl.pallas_call(
        _kv_kernel,
        grid=(batch, nt + 1),
        in_specs=[pl.BlockSpec((tm, d), lambda b, t: (b * nt + jnp.maximum(t - 1, 0), 0)), const((1, d)),
                  const((d, f)), const((f, d))],
        out_specs=[pl.BlockSpec((tm, f), lambda b, t: (b * (nt + 1) + t, 0)),
                   pl.BlockSpec((f, tm), lambda b, t: (0, b * (nt + 1) + t))],
        out_shape=[jax.ShapeDtypeStruct((rows, f), BF16), jax.ShapeDtypeStruct((f, rows), BF16)],
        compiler_params=_cparams(("parallel", "arbitrary")),
        name="kv_projection",
    )(x, g.reshape(1, d), w16, w16.T)


def _nsa_shared_kv(x, batch, seq, kv_norm, w_kv, cmp_pos_k, cmp_pos_v, k_w1, k_w2, v_w1, v_w2):
    n, d = x.shape
    gq, dh = N_KV_GROUPS, HEAD_DIM
    f = w_kv.shape[1]
    kv_pad, kvt_pad = kv_projection(x, kv_norm, w_kv, batch, seq)
    kv_pad = kv_pad.reshape(batch, WINDOW + seq, f)
    kv5 = kv_pad[:, WINDOW:].reshape(batch, seq, N_KV_BRANCH, gq, dh)
    per_group = lambda j: jnp.transpose(kv5[:, :, j], (0, 2, 1, 3))
    n_cmp = (seq - CMP_BLOCK) // CMP_STRIDE + 1
    nhalf = seq // CMP_STRIDE

    def compress(t, pos, w1, w2):
        halves = t.reshape(batch * gq, nhalf, CMP_STRIDE * dh)
        return nsa_compress(halves, pos, w1.astype(BF16), w2.astype(BF16)).reshape(batch, gq, nhalf, dh)

    k_cmp = compress(per_group(0), cmp_pos_k, k_w1, k_w2)
    v_cmp = compress(per_group(1), cmp_pos_v, v_w1, v_w2)
    kc2 = jnp.transpose(k_cmp, (0, 2, 1, 3)).reshape(batch, nhalf, gq * dh)
    vct = jnp.transpose(v_cmp, (0, 1, 3, 2))
    return kc2, vct, kv_pad, kvt_pad, n_cmp


def _q_kernel(x_ref, g_ref, wq_ref, wg_ref, q_ref, gates_ref):
    xn = _rms(x_ref[...], g_ref[...]).astype(BF16)
    q = jnp.dot(xn, wq_ref[...], preferred_element_type=F32) * (HEAD_DIM ** -0.5 * LOG2_E)
    q_ref[...] = q.astype(q_ref.dtype)
    gates_ref[...] = jax.nn.sigmoid(jnp.dot(xn, wg_ref[...], preferred_element_type=F32))


def q_projection(x, g, w_q, tm=ROW_TILE):
    n, d = x.shape
    hd = N_HEADS * HEAD_DIM
    w16 = w_q.astype(BF16)
    w_gate = jnp.pad(w16[:, hd:], ((0, 0), (0, 128 - (w_q.shape[1] - hd))))
    return pl.pallas_call(
        _q_kernel,
        grid=(n // tm,),
        in_specs=[pl.BlockSpec((tm, d), lambda i: (i, 0)), pl.BlockSpec((1, d), lambda i: (0, 0)),
                  pl.BlockSpec((d, hd), lambda i: (0, 0)), pl.BlockSpec((d, 128), lambda i: (0, 0))],
        out_specs=[pl.BlockSpec((tm, hd), lambda i: (i, 0)), pl.BlockSpec((tm, 128), lambda i: (i, 0))],
        out_shape=[jax.ShapeDtypeStruct((n, hd), BF16), jax.ShapeDtypeStruct((n, 128), F32)],
        compiler_params=_cparams(("parallel",)),
        name="q_projection",
    )(x, g.reshape(1, d), w16[:, :hd], w_gate)


def _nsa_mix(x, batch, seq, g_mix, w_q, shared, tables):
    kc2, vct, kv_pad, kvt_pad, n_cmp = shared
    tzw, tzc, bct, ovlt, expt = tables
    q, gates_pad = q_projection(x, g_mix, w_q)
    return nsa_attention_core(q, gates_pad, kc2, vct, bct, kv_pad, kvt_pad, tzw, tzc, ovlt, expt,
                              batch, seq, n_cmp)


def _dense_tail_kernel(a_ref, wo_ref, x_ref, g_ref, wgu_ref, wd_ref, gp_ref, wg_ref, p_ref, wp_ref, o_ref,
                       act_ref, *, tn):
    x = x_ref[...] + jnp.dot(a_ref[...], wo_ref[...], preferred_element_type=F32)
    xn = _rms(x, g_ref[...]).astype(BF16)
    ff = wd_ref.shape[0]
    for c in range(ff // tn):
        gate = jnp.dot(xn, wgu_ref[:, c * tn:(c + 1) * tn], preferred_element_type=F32)
        up = jnp.dot(xn, wgu_ref[:, ff + c * tn:ff + (c + 1) * tn], preferred_element_type=F32)
        act_ref[:, c * tn:(c + 1) * tn] = (_silu(gate) * up).astype(BF16)
    x = x + jnp.dot(act_ref[...], wd_ref[...], preferred_element_type=F32)
    o_ref[...] = _ple(x, gp_ref, wg_ref, p_ref, wp_ref)


def dense_layer_tail(a, w_o, x, g_ffn, w_gu, w_down, g_ple, wg_ple, p, wp_ple, tm=ROW_TILE, tn=512):
    n, d = x.shape
    k = a.shape[1]
    ff = w_down.shape[0]
    dp = p.shape[1]
    whole = lambda shape: pl.BlockSpec(shape, lambda i: (0, 0), pipeline_mode=pl.Buffered(1))
    rows = lambda width: pl.BlockSpec((tm, width), lambda i: (i, 0))
    return pl.pallas_call(
        functools.partial(_dense_tail_kernel, tn=min(tn, ff)),
        grid=(n // tm,),
        in_specs=[rows(k), whole((k, d)), rows(d), whole((1, d)), whole((d, 2 * ff)), whole((ff, d)),
                  whole((1, d)), whole((d, d)), rows(dp), whole((dp, d))],
        out_specs=rows(d),
        out_shape=jax.ShapeDtypeStruct((n, d), F32),
        scratch_shapes=[pltpu.VMEM((tm, ff), BF16)],
        compiler_params=_cparams(("parallel",)),
        name="dense_layer_tail",
    )(a, w_o, x, g_ffn.reshape(1, d), w_gu, w_down, g_ple.reshape(1, d), wg_ple, p, wp_ple)


def _moe_layer_tail(mix, w_o, x, g, w_router, w_gu_all, w_down_all, layer, ple, tm=ROW_TILE):
    n, d = x.shape
    ne = w_router.shape[1]
    x, sel, w, counts = moe_route(mix, w_o, x, g, w_router)
    padded = (counts[:, 0] + tm - 1) // tm * tm
    ends = jnp.cumsum(padded)
    n_tiles = (TOP_K * n) // tm + ne
    n_used = (ends[-1] // tm).astype(jnp.int32).reshape(1)
    tile_first_row = jnp.arange(n_tiles, dtype=jnp.int32) * tm
    tile_expert = jnp.sum(ends[None, :] <= tile_first_row[:, None], axis=1)
    tile_expert = jnp.minimum(tile_expert, tile_expert[jnp.maximum(n_used[0] - 1, 0)]).astype(jnp.int32)
    starts = (ends - padded).astype(jnp.int32)
    pos = moe_slots(starts, sel)
    pos_flat = pos[:TOP_K].reshape(TOP_K * n)
    total = jnp.full((1,), n_tiles * tm, jnp.int32)
    unrouted = jnp.concatenate([starts + counts[:, 0], ends[-1:], ends, total]).astype(jnp.int32)
    inv = moe_inverse(pos_flat, unrouted, n, n_tiles * tm)
    flat = lambda w_all: w_all.reshape((-1,) + w_all.shape[2:])
    y_sorted = moe_experts(tile_expert + layer * ne, n_used, inv, x, g, flat(w_gu_all), flat(w_down_all))
    return moe_combine(pos_flat, w, x, y_sorted, *ple)


def kernel(x, p, g_mix, g_ffn, g_ple, g_final, rg_w_in, rg_conv_w, rg_conv_b, rg_w_a, rg_b_a, rg_w_x,
           rg_b_x, rg_lambda, rg_w_out, kv_norm, w_kv, cmp_pos_k, cmp_pos_v, cmp_k_w1, cmp_k_w2,
           cmp_v_w1, cmp_v_w2, rel_bias, nsa_w_q, nsa_w_o, ffn_w_gu, ffn_w_down, moe_w_router,
           moe_w_gu, moe_w_down, ple_w_proj, ple_w_gate):
    batch, seq, d = x.shape
    depth = p.shape[0]
    n_a = rg_w_in.shape[0]
    n = batch * seq
    x = x.reshape(n, d)
    p = p.reshape(depth, n, p.shape[-1])
    n_cmp = (seq - CMP_BLOCK) // CMP_STRIDE + 1
    tables = _nsa_tables(rel_bias, seq, n_cmp, seq // CMP_STRIDE)
    shared = None
    for i in range(depth):
        if i < n_a:
            mix = rg_lru_core(x, g_mix[i], rg_w_in[i].astype(BF16), batch, seq, rg_conv_w[i], rg_conv_b[i],
                              _block_diag_chunks(rg_w_a[i]), rg_b_a[i], _block_diag_chunks(rg_w_x[i]),
                              rg_b_x[i], rg_lambda[i])
            w_o = rg_w_out[i].astype(BF16)
        else:
            mix = _nsa_mix(x, batch, seq, g_mix[i], nsa_w_q[i - n_a], shared, tables)
            w_o = nsa_w_o[i - n_a].astype(BF16)
        ple = (g_ple[i], ple_w_gate[i].astype(BF16), p[i], ple_w_proj[i].astype(BF16))
        if i % 2 == 0:
            assert i < depth - 1
            x = dense_layer_tail(mix, w_o, x, g_ffn[i], ffn_w_gu[i // 2].astype(BF16),
                                 ffn_w_down[i // 2].astype(BF16), *ple)
        else:
            x = _moe_layer_tail(mix, w_o, x, g_ffn[i], moe_w_router[i // 2], moe_w_gu, moe_w_down, i // 2,
                                ple + ((g_final,) if i == depth - 1 else ()))
        if i == n_a - 1:
            shared = _nsa_shared_kv(x, batch, seq, kv_norm, w_kv, cmp_pos_k, cmp_pos_v,
                                    cmp_k_w1, cmp_k_w2, cmp_v_w1, cmp_v_w2)
    return x.reshape(batch, seq, d)
```

```python
import functools
import math

import numpy as np
import jax
import jax.numpy as jnp
from jax import lax
from jax.experimental import pallas as pl
from jax.experimental.pallas import tpu as pltpu

F32 = jnp.float32
BF16 = jnp.bfloat16

EPS = 1e-6
CONV_WIDTH = 4
LRU_BLOCKS = 16
LRU_C = 8.0
N_HEADS = 16
N_KV_GROUPS = 2
GROUP = N_HEADS // N_KV_GROUPS
HEAD_DIM = 64
N_KV_BRANCH = 6
CMP_BLOCK = 32
CMP_STRIDE = 16
SLC_BLOCK = 64
N_SELECT = 8
WINDOW = 512
Q_BLOCK = 128
N_BUCKETS = 32
MAX_DISTANCE = 128
TOP_K = 2
NEAR = WINDOW + Q_BLOCK
FAR_CHUNK = 512
NEG = -1e30
LOG2_E = math.log2(math.e)
PACKED_ROWS_BF16 = 16
LANES = 128

VMEM_LIMIT_V7X = 56 * 1024 * 1024
ROW_TILE = 512


def _cparams(sem):
    return pltpu.CompilerParams(dimension_semantics=sem, vmem_limit_bytes=VMEM_LIMIT_V7X)


def _rms(x, g):
    return x * lax.rsqrt(jnp.mean(x * x, axis=-1, keepdims=True) + EPS) * g


def _silu(x):
    return x * jax.nn.sigmoid(x)


BD = 256
SCAN_ROWS = 8


def _rg_kernel(x_ref, g_ref, win_ref, cw_ref, cb_ref, wa_ref, ba_ref, wx_ref, bx_ref, lam_ref,
               o_ref, xe_ref, h_ref, a_ref, u_ref, xn_ref):
    t, c = o_ref.shape

    @pl.when(pl.program_id(1) == 0)
    def _():
        xe_ref[0:8, :] = jnp.zeros((8, c), F32)
        h_ref[...] = jnp.zeros_like(h_ref)

    xn_ref[...] = _rms(x_ref[...], g_ref[...]).astype(BF16)
    xe_ref[8:, :] = jnp.dot(xn_ref[...], win_ref[:, c:], preferred_element_type=F32)
    xe = xe_ref[...]
    xc = cb_ref[...] + sum(
        cw_ref[CONV_WIDTH - 1 - d:CONV_WIDTH - d, :] * (pltpu.roll(xe, d, 0) if d else xe)[8:, :]
        for d in range(CONV_WIDTH))
    xe_ref[0:8, :] = xe_ref[t:t + 8, :]

    lam = -lam_ref[...]
    softplus_neg_lam = jnp.maximum(lam, 0.0) + jnp.log1p(jnp.exp(-jnp.abs(lam)))
    for cblk in range(c // BD):
        sl = slice(cblk * BD, (cblk + 1) * BD)
        xb = xc[:, sl]
        xb16 = xb.astype(BF16)
        r = jax.nn.sigmoid(jnp.dot(xb16, wa_ref[cblk], preferred_element_type=F32) + ba_ref[:, sl])
        i = jax.nn.sigmoid(jnp.dot(xb16, wx_ref[cblk], preferred_element_type=F32) + bx_ref[:, sl])
        log_a = -LRU_C * r * softplus_neg_lam[:, sl]
        a = jnp.exp(log_a)
        a_ref[:, sl] = a
        u_ref[:, sl] = jnp.sqrt(1.0 - a * a) * (i * xb)

    row = lax.broadcasted_iota(jnp.int32, (SCAN_ROWS, c), 0)

    def scan_tile(k, h_prev):
        rows = pl.ds(pl.multiple_of(k * SCAN_ROWS, SCAN_ROWS), SCAN_ROWS)
        a = a_ref[rows, :]
        u = u_ref[rows, :]
        for d in (1, 2, 4):
            a_s = jnp.where(row >= d, pltpu.roll(a, d, 0), 1.0)
            u_s = jnp.where(row >= d, pltpu.roll(u, d, 0), 0.0)
            u = a * u_s + u
            a = a * a_s
        h = a * h_prev + u
        u_ref[rows, :] = h
        return jnp.broadcast_to(h[SCAN_ROWS - 1:SCAN_ROWS, :], (SCAN_ROWS, c))

    h_ref[...] = lax.fori_loop(0, t // SCAN_ROWS, scan_tile, h_ref[...])
    gate = jnp.dot(xn_ref[...], win_ref[:, :c], preferred_element_type=F32)
    o_ref[...] = (jax.nn.gelu(gate) * u_ref[...]).astype(o_ref.dtype)


def rg_lru_core(x, g, w_in, batch, seq, conv_w, conv_b, wa_bd, b_a, wx_bd, b_x, lam, t_chunk=512):
    n, d = x.shape
    c = w_in.shape[1] // 2
    nt = seq // t_chunk
    vec = lambda v: v.reshape(1, -1)
    rows = lambda width: pl.BlockSpec((t_chunk, width), lambda b, t: (b * nt + t, 0))
    const = lambda shape: pl.BlockSpec(shape, lambda b, t: (0,) * len(shape))
    return pl.pallas_call(
        _rg_kernel,
        grid=(batch, nt),
        in_specs=[rows(d), const((1, d)), const(w_in.shape), const((CONV_WIDTH, c)), const((1, c)),
                  const(wa_bd.shape), const((1, c)), const(wx_bd.shape), const((1, c)), const((1, c))],
        out_specs=rows(c),
        out_shape=jax.ShapeDtypeStruct((n, c), BF16),
        scratch_shapes=[pltpu.VMEM((t_chunk + 8, c), F32), pltpu.VMEM((SCAN_ROWS, c), F32),
                        pltpu.VMEM((t_chunk, c), F32), pltpu.VMEM((t_chunk, c), F32),
                        pltpu.VMEM((t_chunk, d), BF16)],
        compiler_params=_cparams(("parallel", "arbitrary")),
        name="rg_lru_core",
    )(x, vec(g), w_in, conv_w, vec(conv_b), wa_bd, vec(b_a), wx_bd, vec(b_x), vec(lam))


def _block_diag_chunks(w):
    nb, bw, _ = w.shape
    per = BD // bw
    w = w.reshape(nb // per, per, bw, bw)
    eye = jnp.eye(per, dtype=w.dtype)
    out = jnp.einsum('cpij,pq->cpiqj', w, eye).reshape(nb // per, BD, BD)
    return out.astype(BF16)


def _ple(x, g_ref, wg_ref, p_ref, wp_ref):
    xn = _rms(x, g_ref[...]).astype(BF16)
    gate = jax.nn.sigmoid(jnp.dot(xn, wg_ref[...], preferred_element_type=F32))
    proj = jnp.dot(p_ref[...].astype(BF16), wp_ref[...], preferred_element_type=F32)
    return x + gate * proj


SEL_ROWS = 8


def _route_kernel(a_ref, wo_ref, x_ref, g_ref, wt_ref, xo_ref, sel_ref, w_ref, cnt_ref, carry_ref):
    @pl.when(pl.program_id(0) == 0)
    def _():
        carry_ref[...] = jnp.zeros_like(carry_ref)

    x = x_ref[...] + jnp.dot(a_ref[...], wo_ref[...], preferred_element_type=F32)
    xo_ref[...] = x
    xn = _rms(x, g_ref[...])
    logits = lax.dot_general(wt_ref[...], xn, (((1,), (1,)), ((), ())), preferred_element_type=F32,
                             precision=lax.Precision.HIGHEST)
    ne, tm = logits.shape
    row = lax.broadcasted_iota(jnp.int32, (ne, tm), 0).astype(F32)
    m1 = jnp.max(logits, axis=0, keepdims=True)
    i1 = jnp.min(jnp.where(logits == m1, row, float(ne)), axis=0, keepdims=True)
    rest = jnp.where(row == i1, -jnp.inf, logits)
    m2 = jnp.max(rest, axis=0, keepdims=True)
    i2 = jnp.min(jnp.where(rest == m2, row, float(ne)), axis=0, keepdims=True)
    e2 = jnp.exp(m2 - m1)
    denom = 1.0 + e2
    ind = jnp.where((row == i1) | (row == i2), 1.0, 0.0)
    earlier = (lax.broadcasted_iota(jnp.int32, (tm, tm), 0) < lax.broadcasted_iota(jnp.int32, (tm, tm), 1))
    rank = jnp.dot(ind.astype(BF16), jnp.where(earlier, 1.0, 0.0).astype(BF16),
                   preferred_element_type=F32) + carry_ref[:, 0:1]
    r1 = jnp.sum(jnp.where(row == i1, rank, 0.0), axis=0, keepdims=True)
    r2 = jnp.sum(jnp.where(row == i2, rank, 0.0), axis=0, keepdims=True)
    carry_ref[...] = carry_ref[...] + jnp.sum(ind, axis=1, keepdims=True)
    pick = lambda vals: functools.reduce(
        lambda acc, kv: jnp.where(row == float(kv[0]), kv[1], acc), enumerate(vals), jnp.zeros((ne, tm), F32))
    sel_ref[...] = pick([i1, i2, r1, r2]).astype(jnp.int32)
    w_ref[...] = pick([1.0 / denom, e2 / denom])
    cnt_ref[...] = carry_ref[...].astype(jnp.int32)


def moe_route(a, w_o, x, g, w_router, tm=ROW_TILE):
    n, d = x.shape
    k = a.shape[1]
    ne = w_router.shape[1]
    assert ne == SEL_ROWS
    const = lambda shape: pl.BlockSpec(shape, lambda i: (0, 0))
    return pl.pallas_call(
        _route_kernel,
        grid=(n // tm,),
        in_specs=[pl.BlockSpec((tm, k), lambda i: (i, 0)), const((k, d)), pl.BlockSpec((tm, d), lambda i: (i, 0)),
                  const((1, d)), const((ne, d))],
        out_specs=[pl.BlockSpec((tm, d), lambda i: (i, 0)),
                   pl.BlockSpec((ne, tm), lambda i: (0, i)), pl.BlockSpec((ne, tm), lambda i: (0, i)),
                   pl.BlockSpec((ne, LANES), lambda i: (0, 0))],
        out_shape=[jax.ShapeDtypeStruct((n, d), F32),
                   jax.ShapeDtypeStruct((ne, n), jnp.int32), jax.ShapeDtypeStruct((ne, n), F32),
                   jax.ShapeDtypeStruct((ne, LANES), jnp.int32)],
        scratch_shapes=[pltpu.VMEM((ne, LANES), F32)],
        compiler_params=_cparams(("arbitrary",)),
        name="moe_route",
    )(a, w_o, x, g.reshape(1, d), w_router.T)


def _slot_kernel(starts_ref, sel_ref, pos_ref):
    sel = sel_ref[...]
    start_of = lambda e_row: functools.reduce(
        lambda acc, e: jnp.where(e_row == e, starts_ref[e], acc), range(starts_ref.shape[0]),
        jnp.zeros_like(e_row))
    row = lax.broadcasted_iota(jnp.int32, sel.shape, 0)
    p1 = start_of(sel[0:1]) + sel[2:3]
    p2 = start_of(sel[1:2]) + sel[3:4]
    pos_ref[...] = jnp.where(row == 0, p1, jnp.where(row == 1, p2, 0))


def moe_slots(starts, sel, tm=2048):
    ne, n = sel.shape
    return pl.pallas_call(
        _slot_kernel,
        grid_spec=pltpu.PrefetchScalarGridSpec(
            num_scalar_prefetch=1, grid=(n // tm,),
            in_specs=[pl.BlockSpec((ne, tm), lambda i, s: (0, i))],
            out_specs=pl.BlockSpec((ne, tm), lambda i, s: (0, i))),
        out_shape=jax.ShapeDtypeStruct((ne, n), jnp.int32),
        compiler_params=_cparams(("parallel",)),
        name="moe_slots",
    )(starts, sel)


def _row_copy(src, src_row, dst, dst_row, sem):
    return pltpu.make_async_copy(src.at[pl.ds(src_row, 1)], dst.at[pl.ds(dst_row, 1)], sem)


def _inverse_kernel(pos_ref, pad_ref, inv_ref, *, n):
    def clear(s, c):
        inv_ref[s] = 0
        return c

    def fill(t, c):
        for k in range(TOP_K):
            inv_ref[pos_ref[k * n + t]] = t
        return c

    n_ranges = pad_ref.shape[0] // 2
    for e in range(n_ranges):
        lax.fori_loop(pad_ref[e], pad_ref[n_ranges + e], clear, 0)
    lax.fori_loop(0, n, fill, 0, unroll=8)


def moe_inverse(pos_flat, pad_bounds, n, n_rows):
    return pl.pallas_call(
        functools.partial(_inverse_kernel, n=n),
        in_specs=[pl.BlockSpec(memory_space=pltpu.SMEM), pl.BlockSpec(memory_space=pltpu.SMEM)],
        out_specs=pl.BlockSpec(memory_space=pltpu.SMEM),
        out_shape=jax.ShapeDtypeStruct((n_rows,), jnp.int32),
        name="moe_inverse",
    )(pos_flat, pad_bounds)


def _experts_kernel(te_ref, nu_ref, inv_ref, x_hbm, g_ref, wgu_hbm, wd_hbm, o_ref, xbuf_ref, act_ref, wgu_ref,
                    wd_ref, stage_gu, stage_d, sems, wsems, dsems, *, tm, tn):
    i = pl.program_id(0)
    n_used = nu_ref[0]
    ff = wd_ref.shape[0]
    n_chunks = ff // tn

    def gu_chunk(j):
        col = (j % 2) * ff + (j // 2) * tn
        return slice(col, col + tn)

    def gu_copy(e, j):
        slot = j % stage_gu.shape[0]
        return pltpu.make_async_copy(wgu_hbm.at[e, :, gu_chunk(j)], stage_gu.at[slot], wsems.at[slot])

    def down_copy(e, c):
        slot = c % stage_d.shape[0]
        return pltpu.make_async_copy(wd_hbm.at[e, c * tn:(c + 1) * tn, :], stage_d.at[slot], dsems.at[slot])

    def start_gather(tile, slot):
        def body(t, c):
            _row_copy(x_hbm, inv_ref[tile * tm + t], xbuf_ref.at[slot], t, sems.at[slot]).start()
            return c
        lax.fori_loop(0, tm, body, 0, unroll=8)

    def wait_gather(slot):
        pltpu.make_async_copy(x_hbm.at[pl.ds(0, tm)], xbuf_ref.at[slot], sems.at[slot]).wait()

    last = pl.num_programs(0) - 1
    slot = lax.rem(i, 2)

    @pl.when(i == 0)
    def _():
        start_gather(0, 0)

    def tile(stream):
        e = te_ref[i]
        if stream:
            for j in range(min(stage_gu.shape[0], 2 * n_chunks)):
                gu_copy(e, j).start()
            for c in range(min(stage_d.shape[0], n_chunks)):
                down_copy(e, c).start()
        wait_gather(slot)
        xn = _rms(xbuf_ref[slot], g_ref[...]).astype(BF16)
        nxt = jnp.minimum(i + 1, last) * tm
        for c in range(n_chunks):
            if stream:
                for j in (2 * c, 2 * c + 1):
                    gu_copy(e, j).wait()
                    wgu_ref[:, gu_chunk(j)] = stage_gu[j % stage_gu.shape[0]].astype(BF16)
                    if j + stage_gu.shape[0] < 2 * n_chunks:
                        gu_copy(e, j + stage_gu.shape[0]).start()
            for t in range(c * tm // n_chunks, (c + 1) * tm // n_chunks):
                _row_copy(x_hbm, inv_ref[nxt + t], xbuf_ref.at[1 - slot], t, sems.at[1 - slot]).start()
            rows = slice(c * tn, (c + 1) * tn)
            gate = jnp.dot(xn, wgu_ref[:, rows], preferred_element_type=F32)
            up = jnp.dot(xn, wgu_ref[:, ff + c * tn:ff + (c + 1) * tn], preferred_element_type=F32)
            act_ref[:, rows] = (_silu(gate) * up).astype(BF16)
            if stream:
                down_copy(e, c).wait()
                wd_ref[rows, :] = stage_d[c % stage_d.shape[0]].astype(BF16)
                if c + stage_d.shape[0] < n_chunks:
                    down_copy(e, c + stage_d.shape[0]).start()
                part = jnp.dot(act_ref[:, rows], wd_ref[rows, :], preferred_element_type=F32)
                o_ref[...] = part if c == 0 else o_ref[...] + part
        if not stream:
            o_ref[...] = jnp.dot(act_ref[...], wd_ref[...], preferred_element_type=F32)

        @pl.when(i == last)
        def _():
            wait_gather(1 - slot)

    new_expert = (i == 0) | (te_ref[i] != te_ref[jnp.maximum(i - 1, 0)])

    @pl.when((i < n_used) & new_expert)
    def _():
        tile(stream=True)

    @pl.when((i < n_used) & jnp.logical_not(new_expert))
    def _():
        tile(stream=False)

    @pl.when(i >= n_used)
    def _():
        @pl.when(i == n_used)
        def _():
            wait_gather(slot)

        o_ref[...] = jnp.zeros_like(o_ref)


def moe_experts(tile_expert, n_used, inv, x, g, w_gu, w_down, tm=ROW_TILE, tn=512):
    n, d = x.shape
    p = inv.shape[0]
    ff = w_down.shape[1]
    tn = min(tn, ff)
    hbm = pl.BlockSpec(memory_space=pl.ANY)
    return pl.pallas_call(
        functools.partial(_experts_kernel, tm=tm, tn=tn),
        grid_spec=pltpu.PrefetchScalarGridSpec(
            num_scalar_prefetch=3, grid=(p // tm,),
            in_specs=[hbm, pl.BlockSpec((1, d), lambda i, te, nu, inv: (0, 0)), hbm, hbm],
            out_specs=pl.BlockSpec((tm, d), lambda i, te, nu, inv: (i, 0)),
            scratch_shapes=[pltpu.VMEM((2, tm, d), F32), pltpu.VMEM((tm, ff), BF16),
                            pltpu.VMEM((d, 2 * ff), BF16), pltpu.VMEM((ff, d), BF16),
                            pltpu.VMEM((4, d, tn), w_gu.dtype), pltpu.VMEM((2, tn, d), w_down.dtype),
                            pltpu.SemaphoreType.DMA((2,)), pltpu.SemaphoreType.DMA((4,)),
                            pltpu.SemaphoreType.DMA((2,))]),
        out_shape=jax.ShapeDtypeStruct((p, d), F32),
        compiler_params=_cparams(("arbitrary",)),
        name="moe_experts",
    )(tile_expert, n_used, inv, x, g.reshape(1, d), w_gu, w_down)


def _combine_kernel(pos_ref, w_ref, x_ref, y_hbm, gp_ref, wg_ref, p_ref, wp_ref, *refs, n, tm, final_norm):
    if final_norm:
        gf_ref, o_ref, buf_ref, sems = refs
    else:
        o_ref, buf_ref, sems = refs
    i = pl.program_id(0)
    last = pl.num_programs(0) - 1
    slot = lax.rem(i, 2)

    def row_copies(tile_base, t, into):
        return [_row_copy(y_hbm, pos_ref[k * n + tile_base + t], buf_ref.at[into, k], t, sems.at[into])
                for k in range(TOP_K)]

    def wait_rows(of):
        for k in range(TOP_K):
            pltpu.make_async_copy(y_hbm.at[pl.ds(0, tm)], buf_ref.at[of, k], sems.at[of]).wait()

    @pl.when(i == 0)
    def _():
        def start(t, c):
            for cp in row_copies(0, t, 0):
                cp.start()
            return c
        lax.fori_loop(0, tm, start, 0, unroll=8)

    next_base = jnp.minimum(i + 1, last) * tm
    pieces = 4

    def issue_piece(q):
        for t in range(q * tm // pieces, (q + 1) * tm // pieces):
            for cp in row_copies(next_base, t, 1 - slot):
                cp.start()

    issue_piece(0)
    diag = (lax.broadcasted_iota(jnp.int32, (tm, tm), 0) == lax.broadcasted_iota(jnp.int32, (tm, tm), 1))
    wcol = [jnp.sum(jnp.where(diag, w_ref[k:k + 1, :], 0.0), axis=1, keepdims=True) for k in range(TOP_K)]
    wait_rows(slot)
    issue_piece(1)
    x = x_ref[...] + sum(wcol[k] * buf_ref[slot, k] for k in range(TOP_K))
    issue_piece(2)
    y = _ple(x, gp_ref, wg_ref, p_ref, wp_ref)
    issue_piece(3)
    o_ref[...] = _rms(y, gf_ref[...]) if final_norm else y

    @pl.when(i == last)
    def _():
        wait_rows(1 - slot)


def moe_combine(pos_flat, w, x, y_sorted, g_ple, wg_ple, p, wp_ple, g_final=None, tm=ROW_TILE):
    n, d = x.shape
    dp = p.shape[1]
    const = lambda shape: pl.BlockSpec(shape, lambda i, pos: (0, 0), pipeline_mode=pl.Buffered(1))
    in_specs = [pl.BlockSpec((w.shape[0], tm), lambda i, pos: (0, i)),
                pl.BlockSpec((tm, d), lambda i, pos: (i, 0)),
                pl.BlockSpec(memory_space=pl.ANY),
                const((1, d)), const((d, d)), pl.BlockSpec((tm, dp), lambda i, pos: (i, 0)), const((dp, d))]
    args = [pos_flat, w, x, y_sorted, g_ple.reshape(1, d), wg_ple, p, wp_ple]
    if g_final is not None:
        in_specs.append(const((1, d)))
        args.append(g_final.reshape(1, d))
    return pl.pallas_call(
        functools.partial(_combine_kernel, n=n, tm=tm, final_norm=g_final is not None),
        grid_spec=pltpu.PrefetchScalarGridSpec(
            num_scalar_prefetch=1, grid=(n // tm,),
            in_specs=in_specs,
            out_specs=pl.BlockSpec((tm, d), lambda i, pos: (i, 0)),
            scratch_shapes=[pltpu.VMEM((2, TOP_K, tm, d), F32), pltpu.SemaphoreType.DMA((2,))]),
        out_shape=jax.ShapeDtypeStruct((n, d), F32),
        compiler_params=_cparams(("arbitrary",)),
        name="moe_combine",
    )(*args)


def _compress_kernel(h_ref, pos_ref, w1_ref, w2_ref, o_ref):
    half = h_ref.shape[1]
    h = h_ref[...].astype(BF16)
    p_lo = jnp.dot(h, w1_ref[0:half, :], preferred_element_type=F32)
    p_hi = jnp.dot(h, w1_ref[half:, :], preferred_element_type=F32)
    nh = p_hi.shape[0]
    posb = jnp.dot(pos_ref[...].astype(BF16), w1_ref[...], preferred_element_type=F32)
    hid = p_lo + pltpu.roll(p_hi, nh - 1, 0) + posb
    o_ref[...] = jnp.dot(jax.nn.gelu(hid).astype(BF16), w2_ref[...],
                         preferred_element_type=F32).astype(o_ref.dtype)


def nsa_compress(halves, pos, w1, w2):
    bg, nh, hw = halves.shape
    hid = w1.shape[1]
    dh = w2.shape[1]
    return pl.pallas_call(
        _compress_kernel,
        grid=(bg,),
        in_specs=[pl.BlockSpec((None, nh, hw), lambda i: (i, 0, 0)),
                  pl.BlockSpec((1, 2 * hw), lambda i: (0, 0)),
                  pl.BlockSpec((2 * hw, hid), lambda i: (0, 0)),
                  pl.BlockSpec((hid, dh), lambda i: (0, 0))],
        out_specs=pl.BlockSpec((None, nh, dh), lambda i: (i, 0, 0)),
        out_shape=jax.ShapeDtypeStruct((bg, nh, dh), BF16),
        compiler_params=_cparams(("parallel",)),
    )(halves, pos.reshape(1, 2 * hw), w1, w2)


def _rel_bucket_np(dist):
    n = np.maximum(dist, 0)
    max_exact = N_BUCKETS // 2
    nf = np.maximum(n, 1).astype(np.float32)
    large = max_exact + (np.log(nf / max_exact) / np.float32(math.log(MAX_DISTANCE / max_exact))
                         * (N_BUCKETS - max_exact)).astype(np.int32)
    return np.where(n < max_exact, n, np.minimum(large, N_BUCKETS - 1)).astype(np.int32)


def _nsa_kernel(q_ref, gt_ref, kc_ref, vct_ref, bct_ref, ks_ref, kw_ref, vst_ref, vwt_ref, tzw_ref, tzc_ref,
                ovlt_ref, expt_ref, o_ref, *, n_sel):
    tq = q_ref.shape[0]
    dh = HEAD_DIM
    r = q_ref.shape[1] // dh
    near = tzw_ref.shape[0]
    n_slc = ovlt_ref.shape[0]
    ncp = kc_ref.shape[0]
    g = pl.program_id(1)
    t0 = pl.multiple_of(pl.program_id(2) * tq, tq)
    near_keys = pl.ds(t0, near)
    heads = lambda blocks: jnp.concatenate(blocks, axis=1)
    per_head = lambda x, h: x[:, h * tq:(h + 1) * tq]

    qt = q_ref[...].astype(F32).T.astype(BF16)
    zero = jnp.zeros((dh, r * tq), BF16)
    qh = heads([qt[h * dh:(h + 1) * dh, :] for h in range(r)])
    qz = jnp.where(g == 0, jnp.concatenate([qh, zero], axis=0), jnp.concatenate([zero, qh], axis=0))
    gate_t = gt_ref[...].T

    def gate(branch):
        rows = [gate_t[branch * N_HEADS + gi * r:branch * N_HEADS + (gi + 1) * r, :] for gi in range(N_KV_GROUPS)]
        mine = jnp.where(g == 0, rows[0], rows[1])
        return heads([mine[h:h + 1, :] for h in range(r)])

    sees_any = t0 + (lax.broadcasted_iota(jnp.int32, (1, r * tq), 1) & (tq - 1)) >= CMP_BLOCK - 1
    l = jnp.dot(kc_ref[...], qz, preferred_element_type=F32) + bct_ref[...]
    e = jnp.exp2(l - jnp.max(l, axis=0, keepdims=True))
    p = e * jnp.where(sees_any, 1.0 / jnp.sum(e, axis=0, keepdims=True), 0.0)
    out = gate(0) * jnp.dot(vct_ref[...], p.astype(BF16), preferred_element_type=F32)
    psum = functools.reduce(lambda acc, h: acc + per_head(p, h), range(1, r), per_head(p, 0))

    def pv_and_sum(vt_ref, keys, e):
        ones = jnp.ones((PACKED_ROWS_BF16, keys.size), BF16)
        return jnp.dot(jnp.concatenate([vt_ref[:, keys], ones], axis=0), e.astype(BF16),
                       preferred_element_type=F32)

    flags = expt_ref.shape[1] - n_slc
    kw_aug = jnp.concatenate([kw_ref[near_keys, :], expt_ref[near_keys, n_slc:]], axis=1)
    q_aug = jnp.concatenate([qz, jnp.full((flags, r * tq), NEG, BF16)], axis=0)
    l = jnp.dot(kw_aug, q_aug, preferred_element_type=F32) + tzw_ref[...]
    e = jnp.exp2(l - jnp.max(l, axis=0, keepdims=True))
    o_w = pv_and_sum(vwt_ref, near_keys, e)
    out = out + gate(2) * (o_w[:dh] / o_w[dh:dh + 1])

    imp = jnp.dot(ovlt_ref[...], psum, preferred_element_type=F32, precision=lax.Precision.HIGHEST)
    jblk = lax.broadcasted_iota(jnp.int32, (n_slc, tq), 0).astype(F32)
    qblk = ((t0 + lax.broadcasted_iota(jnp.int32, (n_slc, tq), 1)) // SLC_BLOCK).astype(F32)
    forced = (jblk == 0) | (jblk == qblk) | (jblk == qblk - 1)
    score = jnp.where(forced, 1e9, jnp.where(jblk <= qblk, imp, -1e9))
    sel = jnp.zeros((n_slc, tq), F32)
    for _ in range(n_sel):
        m = jnp.max(score, axis=0, keepdims=True)
        first = jnp.min(jnp.where(score == m, jblk, float(n_slc)), axis=0, keepdims=True)
        pick = jblk == first
        sel = jnp.where(pick, 1.0, sel)
        score = jnp.where(pick, -jnp.inf, score)
    def masked_queries(selected):
        extra = expt_ref.shape[1] - n_slc
        block_rows = jnp.concatenate([jnp.where(selected, 0.0, NEG), jnp.full((extra, tq), NEG, F32)], axis=0)
        return jnp.concatenate([qz, heads([block_rows.astype(BF16)] * r)], axis=0)

    def scores(keys, queries):
        k_aug = jnp.concatenate([ks_ref[keys, :], expt_ref[keys, :]], axis=1)
        return jnp.dot(k_aug, queries, preferred_element_type=F32)

    l = scores(near_keys, masked_queries(sel > 0.5)) + tzc_ref[...]
    m = jnp.max(l, axis=0, keepdims=True)
    acc = pv_and_sum(vst_ref, near_keys, jnp.exp2(l - m))
    far_bias = tzc_ref[0:1, :]
    q_far = masked_queries((sel > 0.5) & (jblk * SLC_BLOCK < (t0 - WINDOW).astype(F32)))

    def far_chunk(f, ma):
        m, acc = ma
        keys = pl.ds(pl.multiple_of(WINDOW + f * FAR_CHUNK, FAR_CHUNK), FAR_CHUNK)
        l = scores(keys, q_far)
        m_new = jnp.maximum(m, jnp.max(l, axis=0, keepdims=True))
        return m_new, jnp.exp2(m - m_new) * acc + pv_and_sum(vst_ref, keys, jnp.exp2(l - m_new))

    n_far = (jnp.maximum(t0, WINDOW) - 1) // FAR_CHUNK
    _, acc = lax.fori_loop(0, n_far, far_chunk, (m - far_bias, acc))
    out = out + gate(1) * (acc[:dh] / acc[dh:dh + 1])

    o_ref[...] = jnp.concatenate([per_head(out, h) for h in range(r)], axis=0).T.astype(o_ref.dtype)


def nsa_attention_core(q, gates_pad, kc2, vct, bct, kv_pad, kvt_pad, tzw, tzc, ovlt, expt, batch, seq):
    n, hd = q.shape
    gq, dh, tq = N_KV_GROUPS, HEAD_DIM, Q_BLOCK
    r = N_HEADS // gq
    nqt = seq // tq
    spad = kv_pad.shape[1]
    ncp = kc2.shape[1]
    n_slc = seq // SLC_BLOCK
    near = tzw.shape[1]
    row_tile = lambda b, g, t: (b * nqt + t, g)
    key_cols = lambda branch: pl.BlockSpec((None, spad, gq * dh), lambda b, g, t: (b, 0, branch))
    val_rows = lambda branch: pl.BlockSpec((dh, spad), lambda b, g, t: (branch * gq + g, b))
    table = pl.BlockSpec((None, near, r * tq), lambda b, g, t: (g, 0, 0))
    const = lambda a: pl.BlockSpec(a.shape, lambda b, g, t: (0,) * a.ndim)
    return pl.pallas_call(
        functools.partial(_nsa_kernel, n_sel=min(N_SELECT, n_slc)),
        grid=(batch, gq, nqt),
        in_specs=[pl.BlockSpec((tq, r * dh), row_tile),
                  pl.BlockSpec((tq, gates_pad.shape[1]), lambda b, g, t: (b * nqt + t, 0)),
                  pl.BlockSpec((None, ncp, gq * dh), lambda b, g, t: (b, 0, 0)),
                  pl.BlockSpec((None, None, dh, ncp), lambda b, g, t: (b, g, 0, 0)),
                  pl.BlockSpec((None, ncp, r * tq), lambda b, g, t: (g, t, 0)),
                  key_cols(2), key_cols(4), val_rows(3), val_rows(5), table, table,
                  const(ovlt), const(expt)],
        out_specs=pl.BlockSpec((tq, r * dh), row_tile),
        out_shape=jax.ShapeDtypeStruct((n, hd), BF16),
        compiler_params=_cparams(("parallel", "parallel", "parallel")),
        name="nsa_attention",
    )(q, gates_pad, kc2, vct, bct, kv_pad, kv_pad, kvt_pad, kvt_pad, tzw, tzc, ovlt, expt)


def _bias_table_kernel(rb_ref, bkt_win_ref, bkt_causal_ref, bkt_cmp_ref, tzw_ref, tzc_ref, bct_ref, *, rows):
    h = pl.program_id(0)

    def lookup(bkt):
        acc = jnp.full(bkt.shape, NEG, F32)
        for k in range(N_BUCKETS):
            acc = jnp.where(bkt == k, rb_ref[k * N_HEADS + h] * LOG2_E, acc)
        return acc

    for src, dst in ((bkt_win_ref, tzw_ref), (bkt_causal_ref, tzc_ref), (bkt_cmp_ref, bct_ref)):
        def body(c, carry, src=src, dst=dst):
            sl = pl.ds(pl.multiple_of(c * rows, rows), rows)
            dst[sl, :] = lookup(src[sl, :])
            return carry
        lax.fori_loop(0, src.shape[0] // rows, body, 0)


def _nsa_tables(rel_bias, seq, n_cmp, ncp):
    tq = Q_BLOCK
    r = N_HEADS // N_KV_GROUPS
    dist_near = np.arange(tq)[None, :] + WINDOW - np.arange(NEAR)[:, None]
    bkt_near = _rel_bucket_np(dist_near)
    bkt_win = np.where((dist_near >= 0) & (dist_near < WINDOW), bkt_near, N_BUCKETS).astype(np.int32)
    bkt_causal = np.where(dist_near >= 0, bkt_near, N_BUCKETS).astype(np.int32)
    dist_c = np.arange(seq)[None, :] - (np.arange(ncp)[:, None] * CMP_STRIDE + CMP_BLOCK - 1)
    bkt_c = np.where((dist_c >= 0) & (np.arange(ncp)[:, None] < n_cmp), _rel_bucket_np(dist_c), N_BUCKETS)
    bkt_c = bkt_c.astype(np.int32).reshape(ncp, seq // tq, tq).transpose(1, 0, 2).reshape(-1, tq)
    maps = [jnp.asarray(m) for m in (bkt_win, bkt_causal, bkt_c)]
    whole = lambda m: pl.BlockSpec(m.shape, lambda h, rb: (0, 0))
    per_head = lambda m: pl.BlockSpec((None, m.shape[0], tq), lambda h, rb: (h // r, 0, h % r))
    tzw, tzc, bct = pl.pallas_call(
        functools.partial(_bias_table_kernel, rows=64),
        grid_spec=pltpu.PrefetchScalarGridSpec(
            num_scalar_prefetch=1, grid=(N_HEADS,),
            in_specs=[whole(m) for m in maps], out_specs=[per_head(m) for m in maps]),
        out_shape=[jax.ShapeDtypeStruct((N_KV_GROUPS, m.shape[0], r * tq), F32) for m in maps],
        compiler_params=_cparams(("parallel",)),
        name="nsa_bias_tables",
    )(rel_bias.reshape(N_BUCKETS * N_HEADS), *maps)
    n_slc = seq // SLC_BLOCK
    cs = np.arange(ncp)[None, :] * CMP_STRIDE
    ss = np.arange(n_slc)[:, None] * SLC_BLOCK
    overlap_t = (cs < ss + SLC_BLOCK) & (cs + CMP_BLOCK > ss) & (np.arange(ncp)[None, :] < n_cmp)
    kpos = np.arange(WINDOW + seq)[:, None] - WINDOW
    expand_t = np.zeros((WINDOW + seq, n_slc + PACKED_ROWS_BF16), bool)
    expand_t[:, :n_slc] = (kpos // SLC_BLOCK == np.arange(n_slc)[None, :]) & (kpos >= 0)
    expand_t[:, n_slc] = kpos[:, 0] < 0
    return tzw, tzc, bct, jnp.asarray(overlap_t, F32), jnp.asarray(expand_t, BF16)


def _kv_kernel(x_ref, g_ref, w_ref, wt_ref, kv_ref, kvt_ref):
    @pl.when(pl.program_id(1) == 0)
    def _():
        kv_ref[...] = jnp.zeros_like(kv_ref)
        kvt_ref[...] = jnp.zeros_like(kvt_ref)

    @pl.when(pl.program_id(1) > 0)
    def _():
        xn = _rms(x_ref[...], g_ref[...]).astype(BF16)
        kv_ref[...] = jnp.dot(xn, w_ref[...], preferred_element_type=F32).astype(kv_ref.dtype)
        kvt_ref[...] = lax.dot_general(wt_ref[...], xn, (((1,), (1,)), ((), ())),
                                       preferred_element_type=F32).astype(kvt_ref.dtype)


def kv_projection(x, g, w_kv, batch, seq):
    n, d = x.shape
    f = w_kv.shape[1]
    tm = WINDOW
    nt = seq // tm
    rows = batch * (nt + 1) * tm
    w16 = w_kv.astype(BF16)
    const = lambda shape: pl.BlockSpec(shape, lambda b, t: (0, 0))
    return pl.pallas_call(
        _kv_kernel,
        grid=(batch, nt + 1),
        in_specs=[pl.BlockSpec((tm, d), lambda b, t: (b * nt + jnp.maximum(t - 1, 0), 0)), const((1, d)),
                  const((d, f)), const((f, d))],
        out_specs=[pl.BlockSpec((tm, f), lambda b, t: (b * (nt + 1) + t, 0)),
                   pl.BlockSpec((f, tm), lambda b, t: (0, b * (nt + 1) + t))],
        out_shape=[jax.ShapeDtypeStruct((rows, f), BF16), jax.ShapeDtypeStruct((f, rows), BF16)],
        compiler_params=_cparams(("parallel", "arbitrary")),
        name="kv_projection",
    )(x, g.reshape(1, d), w16, w16.T)


def _nsa_shared_kv(x, batch, seq, kv_norm, w_kv, cmp_pos_k, cmp_pos_v, k_w1, k_w2, v_w1, v_w2):
    n, d = x.shape
    gq, dh = N_KV_GROUPS, HEAD_DIM
    f = w_kv.shape[1]
    kv_pad, kvt_pad = kv_projection(x, kv_norm, w_kv, batch, seq)
    kv_pad = kv_pad.reshape(batch, WINDOW + seq, f)
    kv5 = kv_pad[:, WINDOW:].reshape(batch, seq, N_KV_BRANCH, gq, dh)
    per_group = lambda j: jnp.transpose(kv5[:, :, j], (0, 2, 1, 3))
    n_cmp = (seq - CMP_BLOCK) // CMP_STRIDE + 1
    nhalf = seq // CMP_STRIDE

    def compress(t, pos, w1, w2):
        halves = t.reshape(batch * gq, nhalf, CMP_STRIDE * dh)
        return nsa_compress(halves, pos, w1.astype(BF16), w2.astype(BF16)).reshape(batch, gq, nhalf, dh)

    k_cmp = compress(per_group(0), cmp_pos_k, k_w1, k_w2)
    v_cmp = compress(per_group(1), cmp_pos_v, v_w1, v_w2)
    kc2 = jnp.transpose(k_cmp, (0, 2, 1, 3)).reshape(batch, nhalf, gq * dh)
    vct = jnp.transpose(v_cmp, (0, 1, 3, 2))
    return kc2, vct, kv_pad, kvt_pad, n_cmp


def _q_kernel(x_ref, g_ref, wq_ref, wg_ref, q_ref, gates_ref):
    xn = _rms(x_ref[...], g_ref[...]).astype(BF16)
    q = jnp.dot(xn, wq_ref[...], preferred_element_type=F32) * (HEAD_DIM ** -0.5 * LOG2_E)
    q_ref[...] = q.astype(q_ref.dtype)
    gates_ref[...] = jax.nn.sigmoid(jnp.dot(xn, wg_ref[...], preferred_element_type=F32))


def q_projection(x, g, w_q, tm=ROW_TILE):
    n, d = x.shape
    hd = N_HEADS * HEAD_DIM
    w16 = w_q.astype(BF16)
    w_gate = jnp.pad(w16[:, hd:], ((0, 0), (0, LANES - (w_q.shape[1] - hd))))
    return pl.pallas_call(
        _q_kernel,
        grid=(n // tm,),
        in_specs=[pl.BlockSpec((tm, d), lambda i: (i, 0)), pl.BlockSpec((1, d), lambda i: (0, 0)),
                  pl.BlockSpec((d, hd), lambda i: (0, 0)), pl.BlockSpec((d, LANES), lambda i: (0, 0))],
        out_specs=[pl.BlockSpec((tm, hd), lambda i: (i, 0)), pl.BlockSpec((tm, LANES), lambda i: (i, 0))],
        out_shape=[jax.ShapeDtypeStruct((n, hd), BF16), jax.ShapeDtypeStruct((n, LANES), F32)],
        compiler_params=_cparams(("parallel",)),
        name="q_projection",
    )(x, g.reshape(1, d), w16[:, :hd], w_gate)


def _nsa_mix(x, batch, seq, g_mix, w_q, shared, tables):
    kc2, vct, kv_pad, kvt_pad, n_cmp = shared
    tzw, tzc, bct, ovlt, expt = tables
    q, gates_pad = q_projection(x, g_mix, w_q)
    return nsa_attention_core(q, gates_pad, kc2, vct, bct, kv_pad, kvt_pad, tzw, tzc, ovlt, expt,
                              batch, seq)


def _dense_tail_kernel(a_ref, wo_ref, x_ref, g_ref, wgu_ref, wd_ref, gp_ref, wg_ref, p_ref, wp_ref, o_ref,
                       act_ref, *, tn):
    x = x_ref[...] + jnp.dot(a_ref[...], wo_ref[...], preferred_element_type=F32)
    xn = _rms(x, g_ref[...]).astype(BF16)
    ff = wd_ref.shape[0]
    for c in range(ff // tn):
        gate = jnp.dot(xn, wgu_ref[:, c * tn:(c + 1) * tn], preferred_element_type=F32)
        up = jnp.dot(xn, wgu_ref[:, ff + c * tn:ff + (c + 1) * tn], preferred_element_type=F32)
        act_ref[:, c * tn:(c + 1) * tn] = (_silu(gate) * up).astype(BF16)
    x = x + jnp.dot(act_ref[...], wd_ref[...], preferred_element_type=F32)
    o_ref[...] = _ple(x, gp_ref, wg_ref, p_ref, wp_ref)


def dense_layer_tail(a, w_o, x, g_ffn, w_gu, w_down, g_ple, wg_ple, p, wp_ple, tm=ROW_TILE, tn=512):
    n, d = x.shape
    k = a.shape[1]
    ff = w_down.shape[0]
    dp = p.shape[1]
    whole = lambda shape: pl.BlockSpec(shape, lambda i: (0, 0), pipeline_mode=pl.Buffered(1))
    rows = lambda width: pl.BlockSpec((tm, width), lambda i: (i, 0))
    return pl.pallas_call(
        functools.partial(_dense_tail_kernel, tn=min(tn, ff)),
        grid=(n // tm,),
        in_specs=[rows(k), whole((k, d)), rows(d), whole((1, d)), whole((d, 2 * ff)), whole((ff, d)),
                  whole((1, d)), whole((d, d)), rows(dp), whole((dp, d))],
        out_specs=rows(d),
        out_shape=jax.ShapeDtypeStruct((n, d), F32),
        scratch_shapes=[pltpu.VMEM((tm, ff), BF16)],
        compiler_params=_cparams(("parallel",)),
        name="dense_layer_tail",
    )(a, w_o, x, g_ffn.reshape(1, d), w_gu, w_down, g_ple.reshape(1, d), wg_ple, p, wp_ple)


def _moe_layer_tail(mix, w_o, x, g, w_router, w_gu_all, w_down_all, layer, ple, tm=ROW_TILE):
    n, d = x.shape
    ne = w_router.shape[1]
    x, sel, w, counts = moe_route(mix, w_o, x, g, w_router)
    padded = (counts[:, 0] + tm - 1) // tm * tm
    ends = jnp.cumsum(padded)
    n_tiles = (TOP_K * n) // tm + ne
    n_used = (ends[-1] // tm).astype(jnp.int32).reshape(1)
    tile_first_row = jnp.arange(n_tiles, dtype=jnp.int32) * tm
    tile_expert = jnp.sum(ends[None, :] <= tile_first_row[:, None], axis=1)
    tile_expert = jnp.minimum(tile_expert, tile_expert[jnp.maximum(n_used[0] - 1, 0)]).astype(jnp.int32)
    starts = (ends - padded).astype(jnp.int32)
    pos = moe_slots(starts, sel)
    pos_flat = pos[:TOP_K].reshape(TOP_K * n)
    total = jnp.full((1,), n_tiles * tm, jnp.int32)
    unrouted = jnp.concatenate([starts + counts[:, 0], ends[-1:], ends, total]).astype(jnp.int32)
    inv = moe_inverse(pos_flat, unrouted, n, n_tiles * tm)
    flat = lambda w_all: w_all.reshape((-1,) + w_all.shape[2:])
    y_sorted = moe_experts(tile_expert + layer * ne, n_used, inv, x, g, flat(w_gu_all), flat(w_down_all))
    return moe_combine(pos_flat, w, x, y_sorted, *ple)


def kernel(x, p, g_mix, g_ffn, g_ple, g_final, rg_w_in, rg_conv_w, rg_conv_b, rg_w_a, rg_b_a, rg_w_x,
           rg_b_x, rg_lambda, rg_w_out, kv_norm, w_kv, cmp_pos_k, cmp_pos_v, cmp_k_w1, cmp_k_w2,
           cmp_v_w1, cmp_v_w2, rel_bias, nsa_w_q, nsa_w_o, ffn_w_gu, ffn_w_down, moe_w_router,
           moe_w_gu, moe_w_down, ple_w_proj, ple_w_gate):
    batch, seq, d = x.shape
    depth = p.shape[0]
    n_a = rg_w_in.shape[0]
    n = batch * seq
    x = x.reshape(n, d)
    p = p.reshape(depth, n, p.shape[-1])
    n_cmp = (seq - CMP_BLOCK) // CMP_STRIDE + 1
    tables = _nsa_tables(rel_bias, seq, n_cmp, seq // CMP_STRIDE)
    shared = None
    for i in range(depth):
        if i < n_a:
            mix = rg_lru_core(x, g_mix[i], rg_w_in[i].astype(BF16), batch, seq, rg_conv_w[i], rg_conv_b[i],
                              _block_diag_chunks(rg_w_a[i]), rg_b_a[i], _block_diag_chunks(rg_w_x[i]),
                              rg_b_x[i], rg_lambda[i])
            w_o = rg_w_out[i].astype(BF16)
        else:
            mix = _nsa_mix(x, batch, seq, g_mix[i], nsa_w_q[i - n_a], shared, tables)
            w_o = nsa_w_o[i - n_a].astype(BF16)
        ple = (g_ple[i], ple_w_gate[i].astype(BF16), p[i], ple_w_proj[i].astype(BF16))
        if i % 2 == 0:
            assert i < depth - 1
            x = dense_layer_tail(mix, w_o, x, g_ffn[i], ffn_w_gu[i // 2].astype(BF16),
                                 ffn_w_down[i // 2].astype(BF16), *ple)
        else:
            x = _moe_layer_tail(mix, w_o, x, g_ffn[i], moe_w_router[i // 2], moe_w_gu, moe_w_down, i // 2,
                                ple + ((g_final,) if i == depth - 1 else ()))
        if i == n_a - 1:
            shared = _nsa_shared_kv(x, batch, seq, kv_norm, w_kv, cmp_pos_k, cmp_pos_v,
                                    cmp_k_w1, cmp_k_w2, cmp_v_w1, cmp_v_w2)
    return x.reshape(batch, seq, d)
```

```python
import functools
import math

import numpy as np
import jax
import jax.numpy as jnp
from jax import lax
from jax.experimental import pallas as pl
from jax.experimental.pallas import tpu as pltpu

F32 = jnp.float32
BF16 = jnp.bfloat16

EPS = 1e-6
CONV_WIDTH = 4
LRU_BLOCKS = 16
LRU_C = 8.0
N_HEADS = 16
N_KV_GROUPS = 2
GROUP = N_HEADS // N_KV_GROUPS
HEAD_DIM = 64
N_KV_BRANCH = 6
CMP_BLOCK = 32
CMP_STRIDE = 16
SLC_BLOCK = 64
N_SELECT = 8
WINDOW = 512
Q_BLOCK = 128
N_BUCKETS = 32
MAX_DISTANCE = 128
TOP_K = 2
NEAR = WINDOW + Q_BLOCK
FAR_CHUNK = 512
NEG = -1e30
LOG2_E = math.log2(math.e)
PACKED_ROWS_BF16 = 16
LANES = 128

VMEM_LIMIT_V7X = 56 * 1024 * 1024
ROW_TILE = 512


def _cparams(sem):
    return pltpu.CompilerParams(dimension_semantics=sem, vmem_limit_bytes=VMEM_LIMIT_V7X)


def _rms(x, g):
    return x * lax.rsqrt(jnp.mean(x * x, axis=-1, keepdims=True) + EPS) * g


def _silu(x):
    return x * jax.nn.sigmoid(x)


BD = 256
SCAN_ROWS = 8


def _rg_kernel(x_ref, g_ref, win_ref, cw_ref, cb_ref, wa_ref, ba_ref, wx_ref, bx_ref, lam_ref,
               o_ref, xe_ref, h_ref, a_ref, u_ref, xn_ref):
    t, c = o_ref.shape

    @pl.when(pl.program_id(1) == 0)
    def _():
        xe_ref[0:8, :] = jnp.zeros((8, c), F32)
        h_ref[...] = jnp.zeros_like(h_ref)

    xn_ref[...] = _rms(x_ref[...], g_ref[...]).astype(BF16)
    xe_ref[8:, :] = jnp.dot(xn_ref[...], win_ref[:, c:], preferred_element_type=F32)
    xe = xe_ref[...]
    xc = cb_ref[...] + sum(
        cw_ref[CONV_WIDTH - 1 - d:CONV_WIDTH - d, :] * (pltpu.roll(xe, d, 0) if d else xe)[8:, :]
        for d in range(CONV_WIDTH))
    xe_ref[0:8, :] = xe_ref[t:t + 8, :]

    lam = -lam_ref[...]
    softplus_neg_lam = jnp.maximum(lam, 0.0) + jnp.log1p(jnp.exp(-jnp.abs(lam)))
    for cblk in range(c // BD):
        sl = slice(cblk * BD, (cblk + 1) * BD)
        xb = xc[:, sl]
        xb16 = xb.astype(BF16)
        r = jax.nn.sigmoid(jnp.dot(xb16, wa_ref[cblk], preferred_element_type=F32) + ba_ref[:, sl])
        i = jax.nn.sigmoid(jnp.dot(xb16, wx_ref[cblk], preferred_element_type=F32) + bx_ref[:, sl])
        log_a = -LRU_C * r * softplus_neg_lam[:, sl]
        a = jnp.exp(log_a)
        a_ref[:, sl] = a
        u_ref[:, sl] = jnp.sqrt(1.0 - a * a) * (i * xb)

    row = lax.broadcasted_iota(jnp.int32, (SCAN_ROWS, c), 0)

    def scan_tile(k, h_prev):
        rows = pl.ds(pl.multiple_of(k * SCAN_ROWS, SCAN_ROWS), SCAN_ROWS)
        a = a_ref[rows, :]
        u = u_ref[rows, :]
        for d in (1, 2, 4):
            a_s = jnp.where(row >= d, pltpu.roll(a, d, 0), 1.0)
            u_s = jnp.where(row >= d, pltpu.roll(u, d, 0), 0.0)
            u = a * u_s + u
            a = a * a_s
        h = a * h_prev + u
        u_ref[rows, :] = h
        return jnp.broadcast_to(h[SCAN_ROWS - 1:SCAN_ROWS, :], (SCAN_ROWS, c))

    h_ref[...] = lax.fori_loop(0, t // SCAN_ROWS, scan_tile, h_ref[...])
    gate = jnp.dot(xn_ref[...], win_ref[:, :c], preferred_element_type=F32)
    o_ref[...] = (jax.nn.gelu(gate) * u_ref[...]).astype(o_ref.dtype)


def rg_lru_core(x, g, w_in, batch, seq, conv_w, conv_b, wa_bd, b_a, wx_bd, b_x, lam, t_chunk=512):
    n, d = x.shape
    c = w_in.shape[1] // 2
    nt = seq // t_chunk
    vec = lambda v: v.reshape(1, -1)
    rows = lambda width: pl.BlockSpec((t_chunk, width), lambda b, t: (b * nt + t, 0))
    const = lambda shape: pl.BlockSpec(shape, lambda b, t: (0,) * len(shape))
    return pl.pallas_call(
        _rg_kernel,
        grid=(batch, nt),
        in_specs=[rows(d), const((1, d)), const(w_in.shape), const((CONV_WIDTH, c)), const((1, c)),
                  const(wa_bd.shape), const((1, c)), const(wx_bd.shape), const((1, c)), const((1, c))],
        out_specs=rows(c),
        out_shape=jax.ShapeDtypeStruct((n, c), BF16),
        scratch_shapes=[pltpu.VMEM((t_chunk + 8, c), F32), pltpu.VMEM((SCAN_ROWS, c), F32),
                        pltpu.VMEM((t_chunk, c), F32), pltpu.VMEM((t_chunk, c), F32),
                        pltpu.VMEM((t_chunk, d), BF16)],
        compiler_params=_cparams(("parallel", "arbitrary")),
        name="rg_lru_core",
    )(x, vec(g), w_in, conv_w, vec(conv_b), wa_bd, vec(b_a), wx_bd, vec(b_x), vec(lam))


def _block_diag_chunks(w):
    nb, bw, _ = w.shape
    per = BD // bw
    w = w.reshape(nb // per, per, bw, bw)
    eye = jnp.eye(per, dtype=w.dtype)
    out = jnp.einsum('cpij,pq->cpiqj', w, eye).reshape(nb // per, BD, BD)
    return out.astype(BF16)


def _ple(x, g_ref, wg_ref, p_ref, wp_ref):
    xn = _rms(x, g_ref[...]).astype(BF16)
    gate = jax.nn.sigmoid(jnp.dot(xn, wg_ref[...], preferred_element_type=F32))
    proj = jnp.dot(p_ref[...].astype(BF16), wp_ref[...], preferred_element_type=F32)
    return x + gate * proj


SEL_ROWS = 8
WEIGHT_DMA_PRIORITY = 1


def _route_kernel(a_ref, wo_ref, x_ref, g_ref, wt_ref, xo_ref, sel_ref, w_ref, cnt_ref, carry_ref):
    @pl.when(pl.program_id(0) == 0)
    def _():
        carry_ref[...] = jnp.zeros_like(carry_ref)

    x = x_ref[...] + jnp.dot(a_ref[...], wo_ref[...], preferred_element_type=F32)
    xo_ref[...] = x
    xn = _rms(x, g_ref[...])
    logits = lax.dot_general(wt_ref[...], xn, (((1,), (1,)), ((), ())), preferred_element_type=F32,
                             precision=lax.Precision.HIGHEST)
    ne, tm = logits.shape
    row = lax.broadcasted_iota(jnp.int32, (ne, tm), 0).astype(F32)
    m1 = jnp.max(logits, axis=0, keepdims=True)
    i1 = jnp.min(jnp.where(logits == m1, row, float(ne)), axis=0, keepdims=True)
    rest = jnp.where(row == i1, -jnp.inf, logits)
    m2 = jnp.max(rest, axis=0, keepdims=True)
    i2 = jnp.min(jnp.where(rest == m2, row, float(ne)), axis=0, keepdims=True)
    e2 = jnp.exp(m2 - m1)
    denom = 1.0 + e2
    ind = jnp.where((row == i1) | (row == i2), 1.0, 0.0)
    earlier = (lax.broadcasted_iota(jnp.int32, (tm, tm), 0) < lax.broadcasted_iota(jnp.int32, (tm, tm), 1))
    rank = jnp.dot(ind.astype(BF16), jnp.where(earlier, 1.0, 0.0).astype(BF16),
                   preferred_element_type=F32) + carry_ref[:, 0:1]
    r1 = jnp.sum(jnp.where(row == i1, rank, 0.0), axis=0, keepdims=True)
    r2 = jnp.sum(jnp.where(row == i2, rank, 0.0), axis=0, keepdims=True)
    carry_ref[...] = carry_ref[...] + jnp.sum(ind, axis=1, keepdims=True)
    pick = lambda vals: functools.reduce(
        lambda acc, kv: jnp.where(row == float(kv[0]), kv[1], acc), enumerate(vals), jnp.zeros((ne, tm), F32))
    sel_ref[...] = pick([i1, i2, r1, r2]).astype(jnp.int32)
    w_ref[...] = pick([1.0 / denom, e2 / denom])
    cnt_ref[...] = carry_ref[...].astype(jnp.int32)


def moe_route(a, w_o, x, g, w_router, tm=ROW_TILE):
    n, d = x.shape
    k = a.shape[1]
    ne = w_router.shape[1]
    assert ne == SEL_ROWS
    const = lambda shape: pl.BlockSpec(shape, lambda i: (0, 0))
    return pl.pallas_call(
        _route_kernel,
        grid=(n // tm,),
        in_specs=[pl.BlockSpec((tm, k), lambda i: (i, 0)), const((k, d)), pl.BlockSpec((tm, d), lambda i: (i, 0)),
                  const((1, d)), const((ne, d))],
        out_specs=[pl.BlockSpec((tm, d), lambda i: (i, 0)),
                   pl.BlockSpec((ne, tm), lambda i: (0, i)), pl.BlockSpec((ne, tm), lambda i: (0, i)),
                   pl.BlockSpec((ne, LANES), lambda i: (0, 0))],
        out_shape=[jax.ShapeDtypeStruct((n, d), F32),
                   jax.ShapeDtypeStruct((ne, n), jnp.int32), jax.ShapeDtypeStruct((ne, n), F32),
                   jax.ShapeDtypeStruct((ne, LANES), jnp.int32)],
        scratch_shapes=[pltpu.VMEM((ne, LANES), F32)],
        compiler_params=_cparams(("arbitrary",)),
        name="moe_route",
    )(a, w_o, x, g.reshape(1, d), w_router.T)


def _slot_kernel(starts_ref, sel_ref, pos_ref):
    sel = sel_ref[...]
    start_of = lambda e_row: functools.reduce(
        lambda acc, e: jnp.where(e_row == e, starts_ref[e], acc), range(starts_ref.shape[0]),
        jnp.zeros_like(e_row))
    row = lax.broadcasted_iota(jnp.int32, sel.shape, 0)
    p1 = start_of(sel[0:1]) + sel[2:3]
    p2 = start_of(sel[1:2]) + sel[3:4]
    pos_ref[...] = jnp.where(row == 0, p1, jnp.where(row == 1, p2, 0))


def moe_slots(starts, sel, tm=2048):
    ne, n = sel.shape
    return pl.pallas_call(
        _slot_kernel,
        grid_spec=pltpu.PrefetchScalarGridSpec(
            num_scalar_prefetch=1, grid=(n // tm,),
            in_specs=[pl.BlockSpec((ne, tm), lambda i, s: (0, i))],
            out_specs=pl.BlockSpec((ne, tm), lambda i, s: (0, i))),
        out_shape=jax.ShapeDtypeStruct((ne, n), jnp.int32),
        compiler_params=_cparams(("parallel",)),
        name="moe_slots",
    )(starts, sel)


def _row_copy(src, src_row, dst, dst_row, sem):
    return pltpu.make_async_copy(src.at[pl.ds(src_row, 1)], dst.at[pl.ds(dst_row, 1)], sem)


def _inverse_kernel(pos_ref, pad_ref, inv_ref, *, n):
    def clear(s, c):
        inv_ref[s] = 0
        return c

    def fill(t, c):
        for k in range(TOP_K):
            inv_ref[pos_ref[k * n + t]] = t
        return c

    n_ranges = pad_ref.shape[0] // 2
    for e in range(n_ranges):
        lax.fori_loop(pad_ref[e], pad_ref[n_ranges + e], clear, 0)
    lax.fori_loop(0, n, fill, 0, unroll=8)


def moe_inverse(pos_flat, pad_bounds, n, n_rows):
    return pl.pallas_call(
        functools.partial(_inverse_kernel, n=n),
        in_specs=[pl.BlockSpec(memory_space=pltpu.SMEM), pl.BlockSpec(memory_space=pltpu.SMEM)],
        out_specs=pl.BlockSpec(memory_space=pltpu.SMEM),
        out_shape=jax.ShapeDtypeStruct((n_rows,), jnp.int32),
        name="moe_inverse",
    )(pos_flat, pad_bounds)


def _experts_kernel(te_ref, nu_ref, inv_ref, x_hbm, g_ref, wgu_hbm, wd_hbm, o_ref, xbuf_ref, act_ref, wgu_ref,
                    wd_ref, stage_gu, stage_d, sems, wsems, dsems, *, tm, tn):
    i = pl.program_id(0)
    n_used = nu_ref[0]
    ff = wd_ref.shape[0]
    n_chunks = ff // tn

    def gu_chunk(j):
        col = (j % 2) * ff + (j // 2) * tn
        return slice(col, col + tn)

    def gu_copy(e, j):
        slot = j % stage_gu.shape[0]
        return pltpu.make_async_copy(wgu_hbm.at[e, :, gu_chunk(j)], stage_gu.at[slot], wsems.at[slot])

    def down_copy(e, c):
        slot = c % stage_d.shape[0]
        return pltpu.make_async_copy(wd_hbm.at[e, c * tn:(c + 1) * tn, :], stage_d.at[slot], dsems.at[slot])

    def start_gather(tile, slot):
        def body(t, c):
            _row_copy(x_hbm, inv_ref[tile * tm + t], xbuf_ref.at[slot], t, sems.at[slot]).start()
            return c
        lax.fori_loop(0, tm, body, 0, unroll=8)

    def wait_gather(slot):
        pltpu.make_async_copy(x_hbm.at[pl.ds(0, tm)], xbuf_ref.at[slot], sems.at[slot]).wait()

    last = pl.num_programs(0) - 1
    slot = lax.rem(i, 2)

    @pl.when(i == 0)
    def _():
        start_gather(0, 0)

    def tile(stream):
        e = te_ref[i]
        if stream:
            for j in range(min(stage_gu.shape[0], 2 * n_chunks)):
                gu_copy(e, j).start(priority=WEIGHT_DMA_PRIORITY)
            for c in range(min(stage_d.shape[0], n_chunks)):
                down_copy(e, c).start(priority=WEIGHT_DMA_PRIORITY)
        wait_gather(slot)
        xn = _rms(xbuf_ref[slot], g_ref[...]).astype(BF16)
        nxt = jnp.minimum(i + 1, last) * tm
        for c in range(n_chunks):
            if stream:
                for j in (2 * c, 2 * c + 1):
                    gu_copy(e, j).wait()
                    wgu_ref[:, gu_chunk(j)] = stage_gu[j % stage_gu.shape[0]].astype(BF16)
                    if j + stage_gu.shape[0] < 2 * n_chunks:
                        gu_copy(e, j + stage_gu.shape[0]).start(priority=WEIGHT_DMA_PRIORITY)
            for t in range(c * tm // n_chunks, (c + 1) * tm // n_chunks):
                _row_copy(x_hbm, inv_ref[nxt + t], xbuf_ref.at[1 - slot], t, sems.at[1 - slot]).start()
            rows = slice(c * tn, (c + 1) * tn)
            gate = jnp.dot(xn, wgu_ref[:, rows], preferred_element_type=F32)
            up = jnp.dot(xn, wgu_ref[:, ff + c * tn:ff + (c + 1) * tn], preferred_element_type=F32)
            act_ref[:, rows] = (_silu(gate) * up).astype(BF16)
            if stream:
                down_copy(e, c).wait()
                wd_ref[rows, :] = stage_d[c % stage_d.shape[0]].astype(BF16)
                if c + stage_d.shape[0] < n_chunks:
                    down_copy(e, c + stage_d.shape[0]).start(priority=WEIGHT_DMA_PRIORITY)
                part = jnp.dot(act_ref[:, rows], wd_ref[rows, :], preferred_element_type=F32)
                o_ref[...] = part if c == 0 else o_ref[...] + part
        if not stream:
            o_ref[...] = jnp.dot(act_ref[...], wd_ref[...], preferred_element_type=F32)

        @pl.when(i == last)
        def _():
            wait_gather(1 - slot)

    new_expert = (i == 0) | (te_ref[i] != te_ref[jnp.maximum(i - 1, 0)])

    @pl.when((i < n_used) & new_expert)
    def _():
        tile(stream=True)

    @pl.when((i < n_used) & jnp.logical_not(new_expert))
    def _():
        tile(stream=False)

    @pl.when(i >= n_used)
    def _():
        @pl.when(i == n_used)
        def _():
            wait_gather(slot)

        o_ref[...] = jnp.zeros_like(o_ref)


def moe_experts(tile_expert, n_used, inv, x, g, w_gu, w_down, tm=ROW_TILE, tn=512):
    n, d = x.shape
    p = inv.shape[0]
    ff = w_down.shape[1]
    tn = min(tn, ff)
    hbm = pl.BlockSpec(memory_space=pl.ANY)
    return pl.pallas_call(
        functools.partial(_experts_kernel, tm=tm, tn=tn),
        grid_spec=pltpu.PrefetchScalarGridSpec(
            num_scalar_prefetch=3, grid=(p // tm,),
            in_specs=[hbm, pl.BlockSpec((1, d), lambda i, te, nu, inv: (0, 0)), hbm, hbm],
            out_specs=pl.BlockSpec((tm, d), lambda i, te, nu, inv: (i, 0)),
            scratch_shapes=[pltpu.VMEM((2, tm, d), F32), pltpu.VMEM((tm, ff), BF16),
                            pltpu.VMEM((d, 2 * ff), BF16), pltpu.VMEM((ff, d), BF16),
                            pltpu.VMEM((4, d, tn), w_gu.dtype), pltpu.VMEM((2, tn, d), w_down.dtype),
                            pltpu.SemaphoreType.DMA((2,)), pltpu.SemaphoreType.DMA((4,)),
                            pltpu.SemaphoreType.DMA((2,))]),
        out_shape=jax.ShapeDtypeStruct((p, d), F32),
        compiler_params=_cparams(("arbitrary",)),
        name="moe_experts",
    )(tile_expert, n_used, inv, x, g.reshape(1, d), w_gu, w_down)


def _combine_kernel(pos_ref, w_ref, x_ref, y_hbm, gp_ref, wg_ref, p_ref, wp_ref, *refs, n, tm, final_norm):
    if final_norm:
        gf_ref, o_ref, buf_ref, sems = refs
    else:
        o_ref, buf_ref, sems = refs
    i = pl.program_id(0)
    last = pl.num_programs(0) - 1
    slot = lax.rem(i, 2)

    def row_copies(tile_base, t, into):
        return [_row_copy(y_hbm, pos_ref[k * n + tile_base + t], buf_ref.at[into, k], t, sems.at[into])
                for k in range(TOP_K)]

    def wait_rows(of):
        for k in range(TOP_K):
            pltpu.make_async_copy(y_hbm.at[pl.ds(0, tm)], buf_ref.at[of, k], sems.at[of]).wait()

    @pl.when(i == 0)
    def _():
        def start(t, c):
            for k, cp in enumerate(row_copies(0, t, 0)):
                cp.start(priority=k)
            return c
        lax.fori_loop(0, tm, start, 0, unroll=8)

    next_base = jnp.minimum(i + 1, last) * tm
    pieces = 4

    def issue_piece(q):
        for t in range(q * tm // pieces, (q + 1) * tm // pieces):
            for k, cp in enumerate(row_copies(next_base, t, 1 - slot)):
                cp.start(priority=k)

    issue_piece(0)
    diag = (lax.broadcasted_iota(jnp.int32, (tm, tm), 0) == lax.broadcasted_iota(jnp.int32, (tm, tm), 1))
    wcol = [jnp.sum(jnp.where(diag, w_ref[k:k + 1, :], 0.0), axis=1, keepdims=True) for k in range(TOP_K)]
    wait_rows(slot)
    issue_piece(1)
    x = x_ref[...] + sum(wcol[k] * buf_ref[slot, k] for k in range(TOP_K))
    issue_piece(2)
    y = _ple(x, gp_ref, wg_ref, p_ref, wp_ref)
    issue_piece(3)
    o_ref[...] = _rms(y, gf_ref[...]) if final_norm else y

    @pl.when(i == last)
    def _():
        wait_rows(1 - slot)


def moe_combine(pos_flat, w, x, y_sorted, g_ple, wg_ple, p, wp_ple, g_final=None, tm=ROW_TILE):
    n, d = x.shape
    dp = p.shape[1]
    const = lambda shape: pl.BlockSpec(shape, lambda i, pos: (0, 0), pipeline_mode=pl.Buffered(1))
    in_specs = [pl.BlockSpec((w.shape[0], tm), lambda i, pos: (0, i)),
                pl.BlockSpec((tm, d), lambda i, pos: (i, 0)),
                pl.BlockSpec(memory_space=pl.ANY),
                const((1, d)), const((d, d)), pl.BlockSpec((tm, dp), lambda i, pos: (i, 0)), const((dp, d))]
    args = [pos_flat, w, x, y_sorted, g_ple.reshape(1, d), wg_ple, p, wp_ple]
    if g_final is not None:
        in_specs.append(const((1, d)))
        args.append(g_final.reshape(1, d))
    return pl.pallas_call(
        functools.partial(_combine_kernel, n=n, tm=tm, final_norm=g_final is not None),
        grid_spec=pltpu.PrefetchScalarGridSpec(
            num_scalar_prefetch=1, grid=(n // tm,),
            in_specs=in_specs,
            out_specs=pl.BlockSpec((tm, d), lambda i, pos: (i, 0)),
            scratch_shapes=[pltpu.VMEM((2, TOP_K, tm, d), F32), pltpu.SemaphoreType.DMA((2,))]),
        out_shape=jax.ShapeDtypeStruct((n, d), F32),
        compiler_params=_cparams(("arbitrary",)),
        name="moe_combine",
    )(*args)


def _compress_kernel(h_ref, pos_ref, w1_ref, w2_ref, o_ref):
    half = h_ref.shape[1]
    h = h_ref[...].astype(BF16)
    p_lo = jnp.dot(h, w1_ref[0:half, :], preferred_element_type=F32)
    p_hi = jnp.dot(h, w1_ref[half:, :], preferred_element_type=F32)
    nh = p_hi.shape[0]
    posb = jnp.dot(pos_ref[...].astype(BF16), w1_ref[...], preferred_element_type=F32)
    hid = p_lo + pltpu.roll(p_hi, nh - 1, 0) + posb
    o_ref[...] = jnp.dot(jax.nn.gelu(hid).astype(BF16), w2_ref[...],
                         preferred_element_type=F32).astype(o_ref.dtype)


def nsa_compress(halves, pos, w1, w2):
    bg, nh, hw = halves.shape
    hid = w1.shape[1]
    dh = w2.shape[1]
    return pl.pallas_call(
        _compress_kernel,
        grid=(bg,),
        in_specs=[pl.BlockSpec((None, nh, hw), lambda i: (i, 0, 0)),
                  pl.BlockSpec((1, 2 * hw), lambda i: (0, 0)),
                  pl.BlockSpec((2 * hw, hid), lambda i: (0, 0)),
                  pl.BlockSpec((hid, dh), lambda i: (0, 0))],
        out_specs=pl.BlockSpec((None, nh, dh), lambda i: (i, 0, 0)),
        out_shape=jax.ShapeDtypeStruct((bg, nh, dh), BF16),
        compiler_params=_cparams(("parallel",)),
    )(halves, pos.reshape(1, 2 * hw), w1, w2)


def _rel_bucket_np(dist):
    n = np.maximum(dist, 0)
    max_exact = N_BUCKETS // 2
    nf = np.maximum(n, 1).astype(np.float32)
    large = max_exact + (np.log(nf / max_exact) / np.float32(math.log(MAX_DISTANCE / max_exact))
                         * (N_BUCKETS - max_exact)).astype(np.int32)
    return np.where(n < max_exact, n, np.minimum(large, N_BUCKETS - 1)).astype(np.int32)


def _nsa_kernel(q_ref, gt_ref, kc_ref, vct_ref, bct_ref, ks_ref, kw_ref, vst_ref, vwt_ref, tzw_ref, tzc_ref,
                ovlt_ref, expt_ref, o_ref, *, n_sel):
    tq = q_ref.shape[0]
    dh = HEAD_DIM
    r = q_ref.shape[1] // dh
    near = tzw_ref.shape[0]
    n_slc = ovlt_ref.shape[0]
    ncp = kc_ref.shape[0]
    g = pl.program_id(1)
    t0 = pl.multiple_of(pl.program_id(2) * tq, tq)
    near_keys = pl.ds(t0, near)
    heads = lambda blocks: jnp.concatenate(blocks, axis=1)
    per_head = lambda x, h: x[:, h * tq:(h + 1) * tq]

    qt = q_ref[...].astype(F32).T.astype(BF16)
    zero = jnp.zeros((dh, r * tq), BF16)
    qh = heads([qt[h * dh:(h + 1) * dh, :] for h in range(r)])
    qz = jnp.where(g == 0, jnp.concatenate([qh, zero], axis=0), jnp.concatenate([zero, qh], axis=0))
    gate_t = gt_ref[...].T

    def gate(branch):
        rows = [gate_t[branch * N_HEADS + gi * r:branch * N_HEADS + (gi + 1) * r, :] for gi in range(N_KV_GROUPS)]
        mine = jnp.where(g == 0, rows[0], rows[1])
        return heads([mine[h:h + 1, :] for h in range(r)])

    sees_any = t0 + (lax.broadcasted_iota(jnp.int32, (1, r * tq), 1) & (tq - 1)) >= CMP_BLOCK - 1
    l = jnp.dot(kc_ref[...], qz, preferred_element_type=F32) + bct_ref[...]
    e = jnp.exp2(l - jnp.max(l, axis=0, keepdims=True))
    p = e * jnp.where(sees_any, 1.0 / jnp.sum(e, axis=0, keepdims=True), 0.0)
    out = gate(0) * jnp.dot(vct_ref[...], p.astype(BF16), preferred_element_type=F32)
    psum = functools.reduce(lambda acc, h: acc + per_head(p, h), range(1, r), per_head(p, 0))

    def pv_and_sum(vt_ref, keys, e):
        ones = jnp.ones((PACKED_ROWS_BF16, keys.size), BF16)
        return jnp.dot(jnp.concatenate([vt_ref[:, keys], ones], axis=0), e.astype(BF16),
                       preferred_element_type=F32)

    flags = expt_ref.shape[1] - n_slc
    kw_aug = jnp.concatenate([kw_ref[near_keys, :], expt_ref[near_keys, n_slc:]], axis=1)
    q_aug = jnp.concatenate([qz, jnp.full((flags, r * tq), NEG, BF16)], axis=0)
    l = jnp.dot(kw_aug, q_aug, preferred_element_type=F32) + tzw_ref[...]
    e = jnp.exp2(l - jnp.max(l, axis=0, keepdims=True))
    o_w = pv_and_sum(vwt_ref, near_keys, e)
    out = out + gate(2) * (o_w[:dh] / o_w[dh:dh + 1])

    imp = jnp.dot(ovlt_ref[...], psum, preferred_element_type=F32, precision=lax.Precision.HIGHEST)
    jblk = lax.broadcasted_iota(jnp.int32, (n_slc, tq), 0).astype(F32)
    qblk = ((t0 + lax.broadcasted_iota(jnp.int32, (n_slc, tq), 1)) // SLC_BLOCK).astype(F32)
    forced = (jblk == 0) | (jblk == qblk) | (jblk == qblk - 1)
    score = jnp.where(forced, 1e9, jnp.where(jblk <= qblk, imp, -1e9))
    sel = jnp.zeros((n_slc, tq), F32)
    for _ in range(n_sel):
        m = jnp.max(score, axis=0, keepdims=True)
        first = jnp.min(jnp.where(score == m, jblk, float(n_slc)), axis=0, keepdims=True)
        pick = jblk == first
        sel = jnp.where(pick, 1.0, sel)
        score = jnp.where(pick, -jnp.inf, score)
    def masked_queries(selected):
        extra = expt_ref.shape[1] - n_slc
        block_rows = jnp.concatenate([jnp.where(selected, 0.0, NEG), jnp.full((extra, tq), NEG, F32)], axis=0)
        return jnp.concatenate([qz, heads([block_rows.astype(BF16)] * r)], axis=0)

    def scores(keys, queries):
        k_aug = jnp.concatenate([ks_ref[keys, :], expt_ref[keys, :]], axis=1)
        return jnp.dot(k_aug, queries, preferred_element_type=F32)

    l = scores(near_keys, masked_queries(sel > 0.5)) + tzc_ref[...]
    m = jnp.max(l, axis=0, keepdims=True)
    acc = pv_and_sum(vst_ref, near_keys, jnp.exp2(l - m))
    far_bias = tzc_ref[0:1, :]
    q_far = masked_queries((sel > 0.5) & (jblk * SLC_BLOCK < (t0 - WINDOW).astype(F32)))

    def far_chunk(f, ma):
        m, acc = ma
        keys = pl.ds(pl.multiple_of(WINDOW + f * FAR_CHUNK, FAR_CHUNK), FAR_CHUNK)
        l = scores(keys, q_far)
        m_new = jnp.maximum(m, jnp.max(l, axis=0, keepdims=True))
        return m_new, jnp.exp2(m - m_new) * acc + pv_and_sum(vst_ref, keys, jnp.exp2(l - m_new))

    n_far = (jnp.maximum(t0, WINDOW) - 1) // FAR_CHUNK
    _, acc = lax.fori_loop(0, n_far, far_chunk, (m - far_bias, acc))
    out = out + gate(1) * (acc[:dh] / acc[dh:dh + 1])

    o_ref[...] = jnp.concatenate([per_head(out, h) for h in range(r)], axis=0).T.astype(o_ref.dtype)


def nsa_attention_core(q, gates_pad, kc2, vct, bct, kv_pad, kvt_pad, tzw, tzc, ovlt, expt, batch, seq):
    n, hd = q.shape
    gq, dh, tq = N_KV_GROUPS, HEAD_DIM, Q_BLOCK
    r = N_HEADS // gq
    nqt = seq // tq
    spad = kv_pad.shape[1]
    ncp = kc2.shape[1]
    n_slc = seq // SLC_BLOCK
    near = tzw.shape[1]
    row_tile = lambda b, g, t: (b * nqt + t, g)
    key_cols = lambda branch: pl.BlockSpec((None, spad, gq * dh), lambda b, g, t: (b, 0, branch))
    val_rows = lambda branch: pl.BlockSpec((dh, spad), lambda b, g, t: (branch * gq + g, b))
    table = pl.BlockSpec((None, near, r * tq), lambda b, g, t: (g, 0, 0))
    const = lambda a: pl.BlockSpec(a.shape, lambda b, g, t: (0,) * a.ndim)
    return pl.pallas_call(
        functools.partial(_nsa_kernel, n_sel=min(N_SELECT, n_slc)),
        grid=(batch, gq, nqt),
        in_specs=[pl.BlockSpec((tq, r * dh), row_tile),
                  pl.BlockSpec((tq, gates_pad.shape[1]), lambda b, g, t: (b * nqt + t, 0)),
                  pl.BlockSpec((None, ncp, gq * dh), lambda b, g, t: (b, 0, 0)),
                  pl.BlockSpec((None, None, dh, ncp), lambda b, g, t: (b, g, 0, 0)),
                  pl.BlockSpec((None, ncp, r * tq), lambda b, g, t: (g, t, 0)),
                  key_cols(2), key_cols(4), val_rows(3), val_rows(5), table, table,
                  const(ovlt), const(expt)],
        out_specs=pl.BlockSpec((tq, r * dh), row_tile),
        out_shape=jax.ShapeDtypeStruct((n, hd), BF16),
        compiler_params=_cparams(("parallel", "parallel", "parallel")),
        name="nsa_attention",
    )(q, gates_pad, kc2, vct, bct, kv_pad, kv_pad, kvt_pad, kvt_pad, tzw, tzc, ovlt, expt)


def _bias_table_kernel(rb_ref, bkt_win_ref, bkt_causal_ref, bkt_cmp_ref, tzw_ref, tzc_ref, bct_ref, *, rows):
    h = pl.program_id(0)

    def lookup(bkt):
        acc = jnp.full(bkt.shape, NEG, F32)
        for k in range(N_BUCKETS):
            acc = jnp.where(bkt == k, rb_ref[k * N_HEADS + h] * LOG2_E, acc)
        return acc

    for src, dst in ((bkt_win_ref, tzw_ref), (bkt_causal_ref, tzc_ref), (bkt_cmp_ref, bct_ref)):
        def body(c, carry, src=src, dst=dst):
            sl = pl.ds(pl.multiple_of(c * rows, rows), rows)
            dst[sl, :] = lookup(src[sl, :])
            return carry
        lax.fori_loop(0, src.shape[0] // rows, body, 0)


def _nsa_tables(rel_bias, seq, n_cmp, ncp):
    tq = Q_BLOCK
    r = N_HEADS // N_KV_GROUPS
    dist_near = np.arange(tq)[None, :] + WINDOW - np.arange(NEAR)[:, None]
    bkt_near = _rel_bucket_np(dist_near)
    bkt_win = np.where((dist_near >= 0) & (dist_near < WINDOW), bkt_near, N_BUCKETS).astype(np.int32)
    bkt_causal = np.where(dist_near >= 0, bkt_near, N_BUCKETS).astype(np.int32)
    dist_c = np.arange(seq)[None, :] - (np.arange(ncp)[:, None] * CMP_STRIDE + CMP_BLOCK - 1)
    bkt_c = np.where((dist_c >= 0) & (np.arange(ncp)[:, None] < n_cmp), _rel_bucket_np(dist_c), N_BUCKETS)
    bkt_c = bkt_c.astype(np.int32).reshape(ncp, seq // tq, tq).transpose(1, 0, 2).reshape(-1, tq)
    maps = [jnp.asarray(m) for m in (bkt_win, bkt_causal, bkt_c)]
    whole = lambda m: pl.BlockSpec(m.shape, lambda h, rb: (0, 0))
    per_head = lambda m: pl.BlockSpec((None, m.shape[0], tq), lambda h, rb: (h // r, 0, h % r))
    tzw, tzc, bct = pl.pallas_call(
        functools.partial(_bias_table_kernel, rows=64),
        grid_spec=pltpu.PrefetchScalarGridSpec(
            num_scalar_prefetch=1, grid=(N_HEADS,),
            in_specs=[whole(m) for m in maps], out_specs=[per_head(m) for m in maps]),
        out_shape=[jax.ShapeDtypeStruct((N_KV_GROUPS, m.shape[0], r * tq), F32) for m in maps],
        compiler_params=_cparams(("parallel",)),
        name="nsa_bias_tables",
    )(rel_bias.reshape(N_BUCKETS * N_HEADS), *maps)
    n_slc = seq // SLC_BLOCK
    cs = np.arange(ncp)[None, :] * CMP_STRIDE
    ss = np.arange(n_slc)[:, None] * SLC_BLOCK
    overlap_t = (cs < ss + SLC_BLOCK) & (cs + CMP_BLOCK > ss) & (np.arange(ncp)[None, :] < n_cmp)
    kpos = np.arange(WINDOW + seq)[:, None] - WINDOW
    expand_t = np.zeros((WINDOW + seq, n_slc + PACKED_ROWS_BF16), bool)
    expand_t[:, :n_slc] = (kpos // SLC_BLOCK == np.arange(n_slc)[None, :]) & (kpos >= 0)
    expand_t[:, n_slc] = kpos[:, 0] < 0
    return tzw, tzc, bct, jnp.asarray(overlap_t, F32), jnp.asarray(expand_t, BF16)


def _kv_kernel(x_ref, g_ref, w_ref, wt_ref, kv_ref, kvt_ref):
    @pl.when(pl.program_id(1) == 0)
    def _():
        kv_ref[...] = jnp.zeros_like(kv_ref)
        kvt_ref[...] = jnp.zeros_like(kvt_ref)

    @pl.when(pl.program_id(1) > 0)
    def _():
        xn = _rms(x_ref[...], g_ref[...]).astype(BF16)
        kv_ref[...] = jnp.dot(xn, w_ref[...], preferred_element_type=F32).astype(kv_ref.dtype)
        kvt_ref[...] = lax.dot_general(wt_ref[...], xn, (((1,), (1,)), ((), ())),
                                       preferred_element_type=F32).astype(kvt_ref.dtype)


def kv_projection(x, g, w_kv, batch, seq):
    n, d = x.shape
    f = w_kv.shape[1]
    tm = WINDOW
    nt = seq // tm
    rows = batch * (nt + 1) * tm
    w16 = w_kv.astype(BF16)
    const = lambda shape: pl.BlockSpec(shape, lambda b, t: (0, 0))
    return pl.pallas_call(
        _kv_kernel,
        grid=(batch, nt + 1),
        in_specs=[pl.BlockSpec((tm, d), lambda b, t: (b * nt + jnp.maximum(t - 1, 0), 0)), const((1, d)),
                  const((d, f)), const((f, d))],
        out_specs=[pl.BlockSpec((tm, f), lambda b, t: (b * (nt + 1) + t, 0)),
                   pl.BlockSpec((f, tm), lambda b, t: (0, b * (nt + 1) + t))],
        out_shape=[jax.ShapeDtypeStruct((rows, f), BF16), jax.ShapeDtypeStruct((f, rows), BF16)],
        compiler_params=_cparams(("parallel", "arbitrary")),
        name="kv_projection",
    )(x, g.reshape(1, d), w16, w16.T)


def _nsa_shared_kv(x, batch, seq, kv_norm, w_kv, cmp_pos_k, cmp_pos_v, k_w1, k_w2, v_w1, v_w2):
    n, d = x.shape
    gq, dh = N_KV_GROUPS, HEAD_DIM
    f = w_kv.shape[1]
    kv_pad, kvt_pad = kv_projection(x, kv_norm, w_kv, batch, seq)
    kv_pad = kv_pad.reshape(batch, WINDOW + seq, f)
    kv5 = kv_pad[:, WINDOW:].reshape(batch, seq, N_KV_BRANCH, gq, dh)
    per_group = lambda j: jnp.transpose(kv5[:, :, j], (0, 2, 1, 3))
    n_cmp = (seq - CMP_BLOCK) // CMP_STRIDE + 1
    nhalf = seq // CMP_STRIDE

    def compress(t, pos, w1, w2):
        halves = t.reshape(batch * gq, nhalf, CMP_STRIDE * dh)
        return nsa_compress(halves, pos, w1.astype(BF16), w2.astype(BF16)).reshape(batch, gq, nhalf, dh)

    k_cmp = compress(per_group(0), cmp_pos_k, k_w1, k_w2)
    v_cmp = compress(per_group(1), cmp_pos_v, v_w1, v_w2)
    kc2 = jnp.transpose(k_cmp, (0, 2, 1, 3)).reshape(batch, nhalf, gq * dh)
    vct = jnp.transpose(v_cmp, (0, 1, 3, 2))
    return kc2, vct, kv_pad, kvt_pad, n_cmp


def _q_kernel(x_ref, g_ref, wq_ref, wg_ref, q_ref, gates_ref):
    xn = _rms(x_ref[...], g_ref[...]).astype(BF16)
    q = jnp.dot(xn, wq_ref[...], preferred_element_type=F32) * (HEAD_DIM ** -0.5 * LOG2_E)
    q_ref[...] = q.astype(q_ref.dtype)
    gates_ref[...] = jax.nn.sigmoid(jnp.dot(xn, wg_ref[...], preferred_element_type=F32))


def q_projection(x, g, w_q, tm=ROW_TILE):
    n, d = x.shape
    hd = N_HEADS * HEAD_DIM
    w16 = w_q.astype(BF16)
    w_gate = jnp.pad(w16[:, hd:], ((0, 0), (0, LANES - (w_q.shape[1] - hd))))
    return pl.pallas_call(
        _q_kernel,
        grid=(n // tm,),
        in_specs=[pl.BlockSpec((tm, d), lambda i: (i, 0)), pl.BlockSpec((1, d), lambda i: (0, 0)),
                  pl.BlockSpec((d, hd), lambda i: (0, 0)), pl.BlockSpec((d, LANES), lambda i: (0, 0))],
        out_specs=[pl.BlockSpec((tm, hd), lambda i: (i, 0)), pl.BlockSpec((tm, LANES), lambda i: (i, 0))],
        out_shape=[jax.ShapeDtypeStruct((n, hd), BF16), jax.ShapeDtypeStruct((n, LANES), F32)],
        compiler_params=_cparams(("parallel",)),
        name="q_projection",
    )(x, g.reshape(1, d), w16[:, :hd], w_gate)


def _nsa_mix(x, batch, seq, g_mix, w_q, shared, tables):
    kc2, vct, kv_pad, kvt_pad, n_cmp = shared
    tzw, tzc, bct, ovlt, expt = tables
    q, gates_pad = q_projection(x, g_mix, w_q)
    return nsa_attention_core(q, gates_pad, kc2, vct, bct, kv_pad, kvt_pad, tzw, tzc, ovlt, expt,
                              batch, seq)


def _dense_tail_kernel(a_ref, wo_ref, x_ref, g_ref, wgu_ref, wd_ref, gp_ref, wg_ref, p_ref, wp_ref, o_ref,
                       act_ref, *, tn):
    x = x_ref[...] + jnp.dot(a_ref[...], wo_ref[...], preferred_element_type=F32)
    xn = _rms(x, g_ref[...]).astype(BF16)
    ff = wd_ref.shape[0]
    for c in range(ff // tn):
        gate = jnp.dot(xn, wgu_ref[:, c * tn:(c + 1) * tn], preferred_element_type=F32)
        up = jnp.dot(xn, wgu_ref[:, ff + c * tn:ff + (c + 1) * tn], preferred_element_type=F32)
        act_ref[:, c * tn:(c + 1) * tn] = (_silu(gate) * up).astype(BF16)
    x = x + jnp.dot(act_ref[...], wd_ref[...], preferred_element_type=F32)
    o_ref[...] = _ple(x, gp_ref, wg_ref, p_ref, wp_ref)


def dense_layer_tail(a, w_o, x, g_ffn, w_gu, w_down, g_ple, wg_ple, p, wp_ple, tm=ROW_TILE, tn=512):
    n, d = x.shape
    k = a.shape[1]
    ff = w_down.shape[0]
    dp = p.shape[1]
    whole = lambda shape: pl.BlockSpec(shape, lambda i: (0, 0), pipeline_mode=pl.Buffered(1))
    rows = lambda width: pl.BlockSpec((tm, width), lambda i: (i, 0))
    return pl.pallas_call(
        functools.partial(_dense_tail_kernel, tn=min(tn, ff)),
        grid=(n // tm,),
        in_specs=[rows(k), whole((k, d)), rows(d), whole((1, d)), whole((d, 2 * ff)), whole((ff, d)),
                  whole((1, d)), whole((d, d)), rows(dp), whole((dp, d))],
        out_specs=rows(d),
        out_shape=jax.ShapeDtypeStruct((n, d), F32),
        scratch_shapes=[pltpu.VMEM((tm, ff), BF16)],
        compiler_params=_cparams(("parallel",)),
        name="dense_layer_tail",
    )(a, w_o, x, g_ffn.reshape(1, d), w_gu, w_down, g_ple.reshape(1, d), wg_ple, p, wp_ple)


def _moe_layer_tail(mix, w_o, x, g, w_router, w_gu_all, w_down_all, layer, ple, tm=ROW_TILE):
    n, d = x.shape
    ne = w_router.shape[1]
    x, sel, w, counts = moe_route(mix, w_o, x, g, w_router)
    padded = (counts[:, 0] + tm - 1) // tm * tm
    ends = jnp.cumsum(padded)
    n_tiles = (TOP_K * n) // tm + ne
    n_used = (ends[-1] // tm).astype(jnp.int32).reshape(1)
    tile_first_row = jnp.arange(n_tiles, dtype=jnp.int32) * tm
    tile_expert = jnp.sum(ends[None, :] <= tile_first_row[:, None], axis=1)
    tile_expert = jnp.minimum(tile_expert, tile_expert[jnp.maximum(n_used[0] - 1, 0)]).astype(jnp.int32)
    starts = (ends - padded).astype(jnp.int32)
    pos = moe_slots(starts, sel)
    pos_flat = pos[:TOP_K].reshape(TOP_K * n)
    total = jnp.full((1,), n_tiles * tm, jnp.int32)
    unrouted = jnp.concatenate([starts + counts[:, 0], ends[-1:], ends, total]).astype(jnp.int32)
    inv = moe_inverse(pos_flat, unrouted, n, n_tiles * tm)
    flat = lambda w_all: w_all.reshape((-1,) + w_all.shape[2:])
    y_sorted = moe_experts(tile_expert + layer * ne, n_used, inv, x, g, flat(w_gu_all), flat(w_down_all))
    return moe_combine(pos_flat, w, x, y_sorted, *ple)


def kernel(x, p, g_mix, g_ffn, g_ple, g_final, rg_w_in, rg_conv_w, rg_conv_b, rg_w_a, rg_b_a, rg_w_x,
           rg_b_x, rg_lambda, rg_w_out, kv_norm, w_kv, cmp_pos_k, cmp_pos_v, cmp_k_w1, cmp_k_w2,
           cmp_v_w1, cmp_v_w2, rel_bias, nsa_w_q, nsa_w_o, ffn_w_gu, ffn_w_down, moe_w_router,
           moe_w_gu, moe_w_down, ple_w_proj, ple_w_gate):
    batch, seq, d = x.shape
    depth = p.shape[0]
    n_a = rg_w_in.shape[0]
    n = batch * seq
    x = x.reshape(n, d)
    p = p.reshape(depth, n, p.shape[-1])
    n_cmp = (seq - CMP_BLOCK) // CMP_STRIDE + 1
    tables = _nsa_tables(rel_bias, seq, n_cmp, seq // CMP_STRIDE)
    shared = None
    for i in range(depth):
        if i < n_a:
            mix = rg_lru_core(x, g_mix[i], rg_w_in[i].astype(BF16), batch, seq, rg_conv_w[i], rg_conv_b[i],
                              _block_diag_chunks(rg_w_a[i]), rg_b_a[i], _block_diag_chunks(rg_w_x[i]),
                              rg_b_x[i], rg_lambda[i])
            w_o = rg_w_out[i].astype(BF16)
        else:
            mix = _nsa_mix(x, batch, seq, g_mix[i], nsa_w_q[i - n_a], shared, tables)
            w_o = nsa_w_o[i - n_a].astype(BF16)
        ple = (g_ple[i], ple_w_gate[i].astype(BF16), p[i], ple_w_proj[i].astype(BF16))
        if i % 2 == 0:
            assert i < depth - 1
            x = dense_layer_tail(mix, w_o, x, g_ffn[i], ffn_w_gu[i // 2].astype(BF16),
                                 ffn_w_down[i // 2].astype(BF16), *ple)
        else:
            x = _moe_layer_tail(mix, w_o, x, g_ffn[i], moe_w_router[i // 2], moe_w_gu, moe_w_down, i // 2,
                                ple + ((g_final,) if i == depth - 1 else ()))
        if i == n_a - 1:
            shared = _nsa_shared_kv(x, batch, seq, kv_norm, w_kv, cmp_pos_k, cmp_pos_v,
                                    cmp_k_w1, cmp_k_w2, cmp_v_w1, cmp_v_w2)
    return x.reshape(batch, seq, d)
```
